```python
import math
import jax, jax.numpy as jnp
from jax import lax
import numpy as np


D_MODEL = 1024
BATCH = 4
SEQ = 4096
DEPTH = 2

D_FF = 2816
MLSTM_HEADS = 4
MLSTM_HEAD_DIM = 64
MLSTM_WIDTH = MLSTM_HEADS * MLSTM_HEAD_DIM
MLSTM_CHUNK = 64
SSM_HEADS = 8
SSM_HEAD_DIM = 64
SSM_WIDTH = SSM_HEADS * SSM_HEAD_DIM
SSM_STATE = 128
SSM_GROUPS = 2
SSM_CONV = 4
SSM_CONV_DIM = SSM_WIDTH + 2 * SSM_GROUPS * SSM_STATE
SSM_CHUNK = 128
DIFF_HEADS = 4
DIFF_QK_DIM = 32
DIFF_V_DIM = 64
DIFF_WIDTH = DIFF_HEADS * DIFF_V_DIM
Q_BLOCK = 128
REL_BUCKETS = 32
REL_MAX_DIST = 128
D_MIX = MLSTM_WIDTH + SSM_WIDTH + DIFF_WIDTH
IN_SPLIT_SIZES = (MLSTM_WIDTH, MLSTM_WIDTH, MLSTM_WIDTH, MLSTM_WIDTH, MLSTM_HEADS, MLSTM_HEADS,
                  SSM_WIDTH, SSM_CONV_DIM, SSM_HEADS,
                  2 * DIFF_HEADS * DIFF_QK_DIM, 2 * DIFF_HEADS * DIFF_QK_DIM, DIFF_WIDTH)
D_IN = sum(IN_SPLIT_SIZES)
NORM_EPS = 1e-6

kernel_name = 'hymba_style_mlstm_ssd_diffattn_macaron'


def rms_norm(x, w):
    xf = x.astype(jnp.float32)
    y = xf * lax.rsqrt(jnp.mean(xf * xf, axis=-1, keepdims=True) + NORM_EPS)
    return (y * w.astype(jnp.float32)).astype(x.dtype)


def swiglu_ffn(x, w_gate, w_up, w_down):
    return (jax.nn.silu(x @ w_gate) * (x @ w_up)) @ w_down


def t5_bucket(rel):
    n = jnp.maximum(rel, 0)
    max_exact = REL_BUCKETS // 2
    nf = jnp.maximum(n, 1).astype(jnp.float32)
    large = max_exact + (jnp.log(nf / max_exact) / math.log(REL_MAX_DIST / max_exact)
                         * (REL_BUCKETS - max_exact)).astype(jnp.int32)
    large = jnp.minimum(large, REL_BUCKETS - 1)
    return jnp.where(n < max_exact, n, large)


def causal_dwconv(x, w, b):
    K, C = w.shape
    y = lax.conv_general_dilated(x, w[:, None, :], window_strides=(1,), padding=((K - 1, 0),),
                                 dimension_numbers=('NWC', 'WIO', 'NWC'), feature_group_count=C)
    return y + b


def mlstm_chunkwise(q, k, v, i_pre, log_f):
    Bsz, H, S, DK = q.shape
    DV = v.shape[-1]
    L = MLSTM_CHUNK
    nc = S // L

    def to_chunks(a):
        return jnp.moveaxis(a.reshape(a.shape[:2] + (nc, L) + a.shape[3:]), 2, 0)

    causal = jnp.tril(jnp.ones((L, L), dtype=bool))

    def step(carry, inp):
        C, n, m = carry
        qc, kc, vc, ic, fc = inp
        b = jnp.cumsum(fc, axis=-1)
        log_d = jnp.where(causal, b[..., :, None] - b[..., None, :] + ic[..., None, :], -jnp.inf)
        m_inter = b + m[..., None]
        m_t = jnp.maximum(m_inter, jnp.max(log_d, axis=-1))
        d = jnp.exp(log_d - m_t[..., None])
        inter = jnp.exp(m_inter - m_t)
        s = jnp.einsum('bhtd,bhsd->bhts', qc, kc) * d
        num = jnp.einsum('bhts,bhsv->bhtv', s, vc) + inter[..., None] * jnp.einsum('bhtd,bhdv->bhtv', qc, C)
        den = jnp.sum(s, axis=-1) + inter * jnp.einsum('bhtd,bhd->bht', qc, n)
        h = num / jnp.maximum(jnp.abs(den), jnp.exp(-m_t))[..., None]
        b_last = b[..., -1]
        log_w = b_last[..., None] - b + ic
        m_new = jnp.maximum(b_last + m, jnp.max(log_w, axis=-1))
        w = jnp.exp(log_w - m_new[..., None])
        decay = jnp.exp(b_last + m - m_new)
        C_new = decay[..., None, None] * C + jnp.einsum('bhs,bhsd,bhsv->bhdv', w, kc, vc)
        n_new = decay[..., None] * n + jnp.einsum('bhs,bhsd->bhd', w, kc)
        return (C_new, n_new, m_new), h

    init = (jnp.zeros((Bsz, H, DK, DV), jnp.float32), jnp.zeros((Bsz, H, DK), jnp.float32),
            jnp.zeros((Bsz, H), jnp.float32))
    xs = (to_chunks(q), to_chunks(k), to_chunks(v), to_chunks(i_pre), to_chunks(log_f))
    _, h = lax.scan(step, init, xs)
    return jnp.moveaxis(h, 0, 2).reshape(Bsz, H, S, DV)


def ssd_chunked(x, dt, A, Bm, Cm):
    Bsz, S, H, P = x.shape
    N = Bm.shape[-1]
    L = SSM_CHUNK
    nc = S // L
    xc = (x * dt[..., None]).reshape(Bsz, nc, L, H, P)
    Bc = Bm.reshape(Bsz, nc, L, H, N)
    Cc = Cm.reshape(Bsz, nc, L, H, N)
    a_cs = jnp.cumsum((dt * A).reshape(Bsz, nc, L, H).transpose(0, 3, 1, 2), axis=-1)
    causal = jnp.tril(jnp.ones((L, L), dtype=bool))
    seg = jnp.exp(jnp.where(causal, a_cs[..., :, None] - a_cs[..., None, :], -jnp.inf))
    scores = jnp.einsum('bclhn,bcshn->bhcls', Cc, Bc) * seg
    y_diag = jnp.einsum('bhcls,bcshp->bclhp', scores, xc)
    decay_in = jnp.exp(a_cs[..., -1:] - a_cs)
    states = jnp.einsum('bcshn,bhcs,bcshp->bchpn', Bc, decay_in, xc)
    chunk_decay = jnp.exp(a_cs[..., -1])

    def step(s, inp):
        st, dec = inp
        return dec[..., None, None] * s + st, s

    _, prev = lax.scan(step, jnp.zeros((Bsz, H, P, N), jnp.float32),
                       (jnp.moveaxis(states, 1, 0), jnp.moveaxis(chunk_decay, 2, 0)))
    prev = jnp.moveaxis(prev, 0, 1)
    y_off = jnp.einsum('bclhn,bchpn,bhcl->bclhp', Cc, prev, jnp.exp(a_cs))
    return (y_diag + y_off).reshape(Bsz, S, H, P)


def diff_attention(q, k, v, rel_bias, lam):
    Bsz, H, S = q.shape[:3]
    DV = v.shape[-1]
    nb = S // Q_BLOCK
    scale = DIFF_QK_DIM ** -0.5
    k1 = k[..., 0, :]
    k2 = k[..., 1, :]
    qb = jnp.moveaxis(q.reshape(Bsz, H, nb, Q_BLOCK, 2, DIFF_QK_DIM), 2, 0)
    k_pos = jnp.arange(S)

    def block(args):
        qblk, bi = args
        q_pos = bi * Q_BLOCK + jnp.arange(Q_BLOCK)
        rel = q_pos[:, None] - k_pos[None, :]
        mask = rel >= 0
        bias = jnp.moveaxis(rel_bias[t5_bucket(rel)], -1, 0).astype(jnp.float32)

        def probs(qi, ki):
            logits = jnp.einsum('bhqd,bhkd->bhqk', qi, ki).astype(jnp.float32) * scale + bias
            return jax.nn.softmax(jnp.where(mask, logits, -jnp.inf), axis=-1)

        a = probs(qblk[..., 0, :], k1) - lam * probs(qblk[..., 1, :], k2)
        return jnp.einsum('bhqk,bhkv->bhqv', a.astype(v.dtype), v)

    out = lax.map(block, (qb, jnp.arange(nb)))
    return jnp.moveaxis(out, 0, 2).reshape(Bsz, H, S, DV)


def hybrid_mixer(h, layer_idx, w_in, w_out, mlstm_gate_bias, mlstm_norm_w, conv_w, conv_b,
                 dt_bias, A_log, D_skip, ssm_norm_w, q_norm_w, k_norm_w, lambdas, subln_w, rel_bias):
    Bsz, S, _ = h.shape
    f32 = jnp.float32
    proj = h @ w_in
    split_idx = np.cumsum(IN_SPLIT_SIZES)[:-1].tolist()
    mq, mk, mv, mo, mi, mf, z, xbc, dt_raw, dq, dk, dv = jnp.split(proj, split_idx, axis=-1)

    def heads(a, nh):
        return a.reshape(Bsz, S, nh, -1).transpose(0, 2, 1, 3)

    i_pre = (mi + mlstm_gate_bias[0]).astype(f32).transpose(0, 2, 1)
    log_f = jax.nn.log_sigmoid((mf + mlstm_gate_bias[1]).astype(f32)).transpose(0, 2, 1)
    hm = mlstm_chunkwise(heads(mq, MLSTM_HEADS).astype(f32),
                         heads(mk, MLSTM_HEADS).astype(f32) * (MLSTM_HEAD_DIM ** -0.5),
                         heads(mv, MLSTM_HEADS).astype(f32), i_pre, log_f)
    hm = rms_norm(hm.transpose(0, 2, 1, 3), mlstm_norm_w.reshape(MLSTM_HEADS, MLSTM_HEAD_DIM))
    y_mlstm = jax.nn.sigmoid(mo) * hm.reshape(Bsz, S, MLSTM_WIDTH).astype(h.dtype)

    xbc = jax.nn.silu(causal_dwconv(xbc, conv_w, conv_b))
    xs, Bm, Cm = jnp.split(xbc, [SSM_WIDTH, SSM_WIDTH + SSM_GROUPS * SSM_STATE], axis=-1)
    xs = xs.reshape(Bsz, S, SSM_HEADS, SSM_HEAD_DIM).astype(f32)
    rep = SSM_HEADS // SSM_GROUPS
    Bm = jnp.repeat(Bm.reshape(Bsz, S, SSM_GROUPS, SSM_STATE), rep, axis=2).astype(f32)
    Cm = jnp.repeat(Cm.reshape(Bsz, S, SSM_GROUPS, SSM_STATE), rep, axis=2).astype(f32)
    dt = jax.nn.softplus((dt_raw + dt_bias).astype(f32))
    A = -jnp.exp(A_log.astype(f32))
    y = ssd_chunked(xs, dt, A, Bm, Cm) + D_skip.astype(f32)[:, None] * xs
    y = y.reshape(Bsz, S, SSM_WIDTH).astype(h.dtype) * jax.nn.silu(z)
    y_ssm = rms_norm(y.reshape(Bsz, S, SSM_GROUPS, -1),
                     ssm_norm_w.reshape(SSM_GROUPS, -1)).reshape(Bsz, S, SSM_WIDTH)

    dq = rms_norm(dq.reshape(Bsz, S, DIFF_HEADS, 2, DIFF_QK_DIM), q_norm_w).transpose(0, 2, 1, 3, 4)
    dk = rms_norm(dk.reshape(Bsz, S, DIFF_HEADS, 2, DIFF_QK_DIM), k_norm_w).transpose(0, 2, 1, 3, 4)
    dv = heads(dv, DIFF_HEADS)
    lam_init = 0.8 - 0.6 * math.exp(-0.3 * layer_idx)
    lf = lambdas.astype(f32)
    lam = jnp.exp(jnp.sum(lf[0] * lf[1])) - jnp.exp(jnp.sum(lf[2] * lf[3])) + lam_init
    o = diff_attention(dq, dk, dv, rel_bias, lam)
    o = rms_norm(o.transpose(0, 2, 1, 3), subln_w) * (1.0 - lam_init)
    y_diff = o.reshape(Bsz, S, DIFF_WIDTH)

    return jnp.concatenate([y_mlstm, y_ssm, y_diff], axis=-1) @ w_out


def setup_inputs(seed: int = 0) -> dict:
    key = jax.random.key(seed)
    ks = jax.random.split(key, 32)
    L = DEPTH
    f32 = jnp.float32

    def nrm(i, shape, scale):
        return scale * jax.random.normal(ks[i], shape, f32)

    def gain(i, shape):
        return 1.0 + 0.1 * jax.random.normal(ks[i], shape, f32)

    x = jax.random.normal(ks[0], (BATCH, SEQ, D_MODEL), f32)
    ffn1_norm_w = gain(1, (L, D_MODEL))
    ffn1_w_gate = nrm(2, (L, D_MODEL, D_FF), D_MODEL ** -0.5)
    ffn1_w_up = nrm(3, (L, D_MODEL, D_FF), D_MODEL ** -0.5)
    ffn1_w_down = nrm(4, (L, D_FF, D_MODEL), D_FF ** -0.5)
    mix_norm_w = gain(5, (L, D_MODEL))
    w_in = nrm(6, (L, D_MODEL, D_IN), D_MODEL ** -0.5)
    mlstm_gate_bias = jnp.stack([nrm(7, (L, MLSTM_HEADS), 0.1),
                                 jnp.linspace(3.0, 6.0, MLSTM_HEADS)[None, :] + nrm(8, (L, MLSTM_HEADS), 0.1)],
                                axis=1)
    mlstm_norm_w = gain(9, (L, MLSTM_WIDTH))
    ssm_conv_w = nrm(10, (L, SSM_CONV, SSM_CONV_DIM), SSM_CONV ** -0.5)
    ssm_conv_b = nrm(11, (L, SSM_CONV_DIM), 0.01)
    dt0 = jnp.exp(jax.random.uniform(ks[12], (L, SSM_HEADS), f32, math.log(1e-3), math.log(1e-1)))
    ssm_dt_bias = dt0 + jnp.log(-jnp.expm1(-dt0))
    ssm_A_log = jnp.log(jax.random.uniform(ks[13], (L, SSM_HEADS), f32, 1.0, 16.0))
    ssm_D = gain(14, (L, SSM_HEADS))
    ssm_norm_w = gain(15, (L, SSM_WIDTH))
    diff_q_norm_w = gain(16, (L, 2, DIFF_QK_DIM))
    diff_k_norm_w = gain(17, (L, 2, DIFF_QK_DIM))
    diff_lambda = nrm(18, (L, 4, DIFF_QK_DIM), 0.1)
    diff_subln_w = gain(19, (L, DIFF_V_DIM))
    rel_bias = nrm(20, (REL_BUCKETS, DIFF_HEADS), 0.5)
    w_out = nrm(21, (L, D_MIX, D_MODEL), D_MIX ** -0.5)
    ffn2_norm_w = gain(22, (L, D_MODEL))
    ffn2_w_gate = nrm(23, (L, D_MODEL, D_FF), D_MODEL ** -0.5)
    ffn2_w_up = nrm(24, (L, D_MODEL, D_FF), D_MODEL ** -0.5)
    ffn2_w_down = nrm(25, (L, D_FF, D_MODEL), D_FF ** -0.5)
    return {'x': x, 'ffn1_norm_w': ffn1_norm_w, 'ffn1_w_gate': ffn1_w_gate, 'ffn1_w_up': ffn1_w_up,
            'ffn1_w_down': ffn1_w_down, 'mix_norm_w': mix_norm_w, 'w_in': w_in,
            'mlstm_gate_bias': mlstm_gate_bias, 'mlstm_norm_w': mlstm_norm_w,
            'ssm_conv_w': ssm_conv_w, 'ssm_conv_b': ssm_conv_b, 'ssm_dt_bias': ssm_dt_bias,
            'ssm_A_log': ssm_A_log, 'ssm_D': ssm_D, 'ssm_norm_w': ssm_norm_w,
            'diff_q_norm_w': diff_q_norm_w, 'diff_k_norm_w': diff_k_norm_w, 'diff_lambda': diff_lambda,
            'diff_subln_w': diff_subln_w, 'rel_bias': rel_bias, 'w_out': w_out,
            'ffn2_norm_w': ffn2_norm_w, 'ffn2_w_gate': ffn2_w_gate, 'ffn2_w_up': ffn2_w_up,
            'ffn2_w_down': ffn2_w_down}


def reference(x, ffn1_norm_w, ffn1_w_gate, ffn1_w_up, ffn1_w_down, mix_norm_w, w_in,
              mlstm_gate_bias, mlstm_norm_w, ssm_conv_w, ssm_conv_b, ssm_dt_bias, ssm_A_log, ssm_D,
              ssm_norm_w, diff_q_norm_w, diff_k_norm_w, diff_lambda, diff_subln_w, rel_bias, w_out,
              ffn2_norm_w, ffn2_w_gate, ffn2_w_up, ffn2_w_down):
    for l in range(DEPTH):
        x = x + 0.5 * swiglu_ffn(rms_norm(x, ffn1_norm_w[l]), ffn1_w_gate[l], ffn1_w_up[l], ffn1_w_down[l])
        x = x + hybrid_mixer(rms_norm(x, mix_norm_w[l]), l, w_in[l], w_out[l],
                             mlstm_gate_bias[l], mlstm_norm_w[l], ssm_conv_w[l], ssm_conv_b[l],
                             ssm_dt_bias[l], ssm_A_log[l], ssm_D[l], ssm_norm_w[l],
                             diff_q_norm_w[l], diff_k_norm_w[l], diff_lambda[l], diff_subln_w[l],
                             rel_bias)
        x = x + 0.5 * swiglu_ffn(rms_norm(x, ffn2_norm_w[l]), ffn2_w_gate[l], ffn2_w_up[l], ffn2_w_down[l])
    return x
```

```python
import functools
import math

import jax
import jax.numpy as jnp
from jax import lax
from jax.experimental import pallas as pl
from jax.experimental.pallas import tpu as pltpu

F32 = jnp.float32
BF16 = jnp.bfloat16

D_MODEL = 1024
DEPTH = 2
D_FF = 2816
MLSTM_HEADS = 4
MLSTM_HEAD_DIM = 64
MLSTM_WIDTH = MLSTM_HEADS * MLSTM_HEAD_DIM
SSM_HEADS = 8
SSM_HEAD_DIM = 64
SSM_WIDTH = SSM_HEADS * SSM_HEAD_DIM
SSM_STATE = 128
SSM_GROUPS = 2
SSM_CONV = 4
SSM_CONV_DIM = SSM_WIDTH + 2 * SSM_GROUPS * SSM_STATE
DIFF_HEADS = 4
DIFF_QK_DIM = 32
DIFF_V_DIM = 64
DIFF_WIDTH = DIFF_HEADS * DIFF_V_DIM
REL_BUCKETS = 32
REL_MAX_DIST = 128
NORM_EPS = 1e-6

LANES = 128
ROW_TILE = 512
FFN_CHUNK = 1408
MIX_CHUNK = 256
ATT_TILE = 256
VMEM_LIMIT = 56 * 1024 * 1024
NEG_BIG = -1e30

_C_M = 0
_C_Z = _C_M + 4 * MLSTM_WIDTH
_C_XBC = _C_Z + SSM_WIDTH
_C_SMALL = _C_XBC + SSM_CONV_DIM
_C_QK = _C_SMALL + LANES
_C_V = _C_QK + 4 * DIFF_HEADS * DIFF_QK_DIM
_C_END = _C_V + DIFF_WIDTH
_LANE_I = 0
_LANE_F = MLSTM_HEADS
_LANE_DT = 2 * MLSTM_HEADS


def _dot(a, b):
    return jnp.dot(a, b, preferred_element_type=F32)


def _dot_nt(a, b):
    return lax.dot_general(a, b, (((1,), (1,)), ((), ())), preferred_element_type=F32)


def _sigmoid(x):
    return 1.0 / (1.0 + jnp.exp(-x))


def _softplus(x):
    return jnp.maximum(x, 0.0) + jnp.log(1.0 + jnp.exp(-jnp.abs(x)))


def _rms_rows(x, w_row):
    ms = jnp.mean(x * x, axis=-1, keepdims=True)
    return x * lax.rsqrt(ms + NORM_EPS) * w_row


def _group_mean_sq(x, gmat):
    sq = x * x
    hi = sq.astype(BF16)
    lo = (sq - hi.astype(F32)).astype(BF16)
    return _dot(hi, gmat) + _dot(lo, gmat)


def _cumsum_rows(x):
    n = x.shape[0]
    row = lax.broadcasted_iota(jnp.int32, x.shape, 0)
    step = 1
    while step < n:
        x = x + jnp.where(row >= step, pltpu.roll(x, step, axis=0), 0.0)
        step *= 2
    return x


def _expand_heads(cols, width):
    rows = cols[0].shape[0]
    lane = lax.broadcasted_iota(jnp.int32, (rows, len(cols) * width), 1)
    out = cols[-1]
    for h in range(len(cols) - 2, -1, -1):
        out = jnp.where(lane < (h + 1) * width, cols[h], out)
    return jnp.broadcast_to(out, (rows, len(cols) * width))


def _block_diag(n, group, value):
    r = jnp.arange(n) // group
    return jnp.where(r[:, None] == r[None, :], value, 0.0).astype(BF16)


def _const_spec(shape):
    nd = len(shape)
    return pl.BlockSpec(shape, lambda *_: (0,) * nd)


def _params(sem):
    return pltpu.CompilerParams(dimension_semantics=sem, vmem_limit_bytes=VMEM_LIMIT)


def _ffn_kernel(x_ref, nw_ref, wg_ref, wu_ref, wd_ref, o_ref):
    x = x_ref[...]
    xn = _rms_rows(x, nw_ref[...]).astype(BF16)
    acc = None
    for c in range(D_FF // FFN_CHUNK):
        cols = slice(c * FFN_CHUNK, (c + 1) * FFN_CHUNK)
        g = _dot(xn, wg_ref[:, cols])
        u = _dot(xn, wu_ref[:, cols])
        h = (g * _sigmoid(g) * u).astype(BF16)
        part = _dot(h, wd_ref[cols, :])
        acc = part if acc is None else acc + part
    o_ref[...] = x + 0.5 * acc


def _ffn(x, norm_w, wg, wu, wd):
    n = x.shape[0]
    row_spec = pl.BlockSpec((ROW_TILE, D_MODEL), lambda i: (i, 0))
    return pl.pallas_call(
        _ffn_kernel,
        grid=(n // ROW_TILE,),
        in_specs=[row_spec, _const_spec((1, D_MODEL)), _const_spec((D_MODEL, D_FF)),
                  _const_spec((D_MODEL, D_FF)), _const_spec((D_FF, D_MODEL))],
        out_specs=row_spec,
        out_shape=jax.ShapeDtypeStruct((n, D_MODEL), F32),
        compiler_params=_params(("parallel",)),
        name="ffn",
    )(x, norm_w.reshape(1, D_MODEL), wg, wu, wd)


def _inproj_kernel(x_ref, nw_ref, w_ref, qkw_ref, g32_ref,
                   m_ref, z_ref, xbc_ref, small_ref, q_ref, k_ref, v_ref):
    xn = _rms_rows(x_ref[...], nw_ref[...]).astype(BF16)
    m_ref[...] = _dot(xn, w_ref[:, _C_M:_C_Z]).astype(BF16)
    z_ref[...] = _dot(xn, w_ref[:, _C_Z:_C_XBC]).astype(BF16)
    xbc_ref[...] = _dot(xn, w_ref[:, _C_XBC:_C_SMALL]).astype(BF16)
    small_ref[...] = _dot(xn, w_ref[:, _C_SMALL:_C_QK])
    v_ref[...] = _dot(xn, w_ref[:, _C_V:_C_END]).astype(BF16)
    half = (_C_V - _C_QK) // 2
    for out_ref, lo in ((q_ref, _C_QK), (k_ref, _C_QK + half)):
        a = _dot(xn, w_ref[:, lo:lo + half])
        ms = _group_mean_sq(a, g32_ref[...])
        w_row = qkw_ref[:, lo - _C_QK:lo - _C_QK + half]
        out_ref[...] = (a * lax.rsqrt(ms + NORM_EPS) * w_row).astype(BF16)


def _inproj(x, norm_w, w_all, qkw_row, g32):
    n = x.shape[0]

    def rows(width):
        return pl.BlockSpec((ROW_TILE, width), lambda i: (i, 0))

    widths = (4 * MLSTM_WIDTH, SSM_WIDTH, SSM_CONV_DIM, LANES, DIFF_WIDTH, DIFF_WIDTH, DIFF_WIDTH)
    dtypes = (BF16, BF16, BF16, F32, BF16, BF16, BF16)
    return pl.pallas_call(
        _inproj_kernel,
        grid=(n // ROW_TILE,),
        in_specs=[rows(D_MODEL), _const_spec((1, D_MODEL)), _const_spec((D_MODEL, _C_END)),
                  _const_spec((1, 2 * DIFF_WIDTH)), _const_spec((DIFF_WIDTH, DIFF_WIDTH))],
        out_specs=[rows(w) for w in widths],
        out_shape=[jax.ShapeDtypeStruct((n, w), d) for w, d in zip(widths, dtypes)],
        compiler_params=_params(("parallel",)),
        name="inproj",
    )(x, norm_w.reshape(1, D_MODEL), w_all, qkw_row, g32)


def _mlstm_kernel(q_ref, k_ref, v_ref, o_ref, small_ref, gbias_ref, nw_ref, g64_ref,
                  y_ref, c_scr, n_scr, m_scr):
    L = MIX_CHUNK
    H = MLSTM_HEADS
    W = MLSTM_HEAD_DIM

    @pl.when(pl.program_id(1) == 0)
    def _():
        c_scr[...] = jnp.zeros_like(c_scr)
        n_scr[...] = jnp.zeros_like(n_scr)
        m_scr[...] = jnp.zeros_like(m_scr)

    q = q_ref[...]
    ks = k_ref[...] * (MLSTM_HEAD_DIM ** -0.5)
    v = v_ref[...]
    g = small_ref[...] + gbias_ref[...]
    logf = jnp.minimum(g, 0.0) - jnp.log(1.0 + jnp.exp(-jnp.abs(g)))
    b = _cumsum_rows(logf)
    bT = b.T
    gT = g.T
    m_state = m_scr[...]
    head_of_lane = lax.broadcasted_iota(jnp.int32, (L, H * W), 1) // W
    causal = (lax.broadcasted_iota(jnp.int32, (L, L), 0) >= lax.broadcasted_iota(jnp.int32, (L, L), 1))

    ps, inter, den_intra, m_t, w_col, decay, m_new = [], [], [], [], [], [], []
    for h in range(H):
        b_col = b[:, _LANE_F + h:_LANE_F + h + 1]
        i_col = g[:, _LANE_I + h:_LANE_I + h + 1]
        b_row = bT[_LANE_F + h:_LANE_F + h + 1, :]
        i_row = gT[_LANE_I + h:_LANE_I + h + 1, :]
        m_prev = m_state[:, h:h + 1]
        log_d = jnp.where(causal, b_col - b_row + i_row, -jnp.inf)
        m_inter = b_col + m_prev
        mt = jnp.maximum(m_inter, jnp.max(log_d, axis=1, keepdims=True))
        d = jnp.exp(log_d - mt)
        qm = jnp.where(head_of_lane == h, q, jnp.zeros_like(q))
        s = _dot_nt(qm, ks) * d
        den_intra.append(jnp.sum(s, axis=1, keepdims=True))
        inter.append(jnp.exp(m_inter - mt))
        m_t.append(mt)
        ps.append(s.astype(BF16))
        b_last = b[L - 1:L, _LANE_F + h:_LANE_F + h + 1]
        log_w = b_last - b_col + i_col
        mn = jnp.maximum(b_last + m_prev, jnp.max(log_w, axis=0, keepdims=True))
        w_col.append(jnp.exp(log_w - mn))
        decay.append(jnp.exp(b_last + m_prev - mn))
        m_new.append(mn)

    vm = jnp.concatenate([jnp.where(head_of_lane == h, v, jnp.zeros_like(v)) for h in range(H)], axis=0)
    num = _dot(jnp.concatenate(ps, axis=1), vm)
    num = num + _expand_heads(inter, W) * _dot(q, c_scr[...].astype(BF16))
    qn = _dot(q, n_scr[...].astype(BF16))
    denom = []
    for h in range(H):
        den = den_intra[h] + inter[h] * qn[:, h:h + 1]
        denom.append(jnp.maximum(jnp.abs(den), jnp.exp(-m_t[h])))
    hout = num / _expand_heads(denom, W)
    hn = hout * lax.rsqrt(_group_mean_sq(hout, g64_ref[...]) + NORM_EPS) * nw_ref[...]
    y_ref[...] = (_sigmoid(o_ref[...].astype(F32)) * hn).astype(BF16)

    kw_t = (ks.astype(F32) * _expand_heads(w_col, W)).T.astype(BF16)
    row_head = lax.broadcasted_iota(jnp.int32, (H * W, H * W), 0) // W
    col_head = lax.broadcasted_iota(jnp.int32, (H * W, H * W), 1) // W
    decay_rows = decay[-1]
    for h in range(H - 2, -1, -1):
        decay_rows = jnp.where(row_head[:, :1] < h + 1, decay[h], decay_rows)
    c_scr[...] = decay_rows * c_scr[...] + jnp.where(row_head == col_head, _dot(kw_t, v), 0.0)
    ksum = _dot(kw_t, jnp.ones((L, LANES), BF16))
    n_lane = lax.broadcasted_iota(jnp.int32, (H * W, LANES), 1)
    n_scr[...] = decay_rows * n_scr[...] + jnp.where(row_head[:, :LANES] == n_lane, ksum, 0.0)
    lane = lax.broadcasted_iota(jnp.int32, (1, LANES), 1)
    m_next = m_state
    for h in range(H):
        m_next = jnp.where(lane == h, m_new[h], m_next)
    m_scr[...] = m_next


def _mlstm(m_proj, small, gate_bias_row, norm_w_row, g64, batch, seq):
    nc = seq // MIX_CHUNK
    n = batch * seq

    def col_block(j):
        return pl.BlockSpec((MIX_CHUNK, MLSTM_WIDTH), lambda b, c: (b * nc + c, j))

    rows = pl.BlockSpec((MIX_CHUNK, LANES), lambda b, c: (b * nc + c, 0))
    return pl.pallas_call(
        _mlstm_kernel,
        grid=(batch, nc),
        in_specs=[col_block(0), col_block(1), col_block(2), col_block(3), rows,
                  _const_spec((1, LANES)), _const_spec((1, MLSTM_WIDTH)),
                  _const_spec((MLSTM_WIDTH, MLSTM_WIDTH))],
        out_specs=pl.BlockSpec((MIX_CHUNK, MLSTM_WIDTH), lambda b, c: (b * nc + c, 0)),
        out_shape=jax.ShapeDtypeStruct((n, MLSTM_WIDTH), BF16),
        scratch_shapes=[pltpu.VMEM((MLSTM_WIDTH, MLSTM_WIDTH), F32),
                        pltpu.VMEM((MLSTM_WIDTH, LANES), F32),
                        pltpu.VMEM((1, LANES), F32)],
        compiler_params=_params(("arbitrary", "arbitrary")),
        name="mlstm",
    )(m_proj, m_proj, m_proj, m_proj, small, gate_bias_row, norm_w_row, g64)


def _ssd_kernel(z_ref, xbc_ref, small_ref, bias_ref, alog_ref, convw_ref, convb_ref, dskip_ref, nw_ref,
                y_ref, halo_scr, t_scr):
    L = MIX_CHUNK
    P = SSM_HEAD_DIM
    HG = SSM_HEADS // SSM_GROUPS
    GW = HG * P

    @pl.when(pl.program_id(1) == 0)
    def _():
        halo_scr[...] = jnp.zeros_like(halo_scr)
        t_scr[...] = jnp.zeros_like(t_scr)

    xr = xbc_ref[...].astype(F32)
    ext = jnp.concatenate([halo_scr[...], xr], axis=0)
    halo_scr[...] = xr[L - 8:, :]
    conv = convb_ref[...] + convw_ref[SSM_CONV - 1:SSM_CONV, :] * xr
    for j in range(1, SSM_CONV):
        conv = conv + convw_ref[SSM_CONV - 1 - j:SSM_CONV - j, :] * ext[8 - j:8 - j + L, :]
    xa = conv * _sigmoid(conv)
    xs = xa[:, :SSM_WIDTH]
    bm = xa[:, SSM_WIDTH:SSM_WIDTH + SSM_GROUPS * SSM_STATE]
    cm = xa[:, SSM_WIDTH + SSM_GROUPS * SSM_STATE:]

    dt = _softplus(small_ref[...] + bias_ref[...])
    a_cs = _cumsum_rows(dt * (-jnp.exp(alog_ref[...])))
    a_t = a_cs.T
    a_last = a_cs[L - 1:L, :]
    causal = (lax.broadcasted_iota(jnp.int32, (L, L), 0) >= lax.broadcasted_iota(jnp.int32, (L, L), 1))
    head_of_lane = lax.broadcasted_iota(jnp.int32, (L, GW), 1) // P

    for grp in range(SSM_GROUPS):
        heads = [grp * HG + i for i in range(HG)]
        lanes = [_LANE_DT + h for h in heads]
        cg = cm[:, grp * SSM_STATE:(grp + 1) * SSM_STATE].astype(BF16)
        bg = bm[:, grp * SSM_STATE:(grp + 1) * SSM_STATE]
        xs_g = xs[:, grp * GW:(grp + 1) * GW]
        xc = xs_g * _expand_heads([dt[:, ln:ln + 1] for ln in lanes], P)
        scores = _dot_nt(cg, bg.astype(BF16))
        ps = []
        for ln in lanes:
            seg = jnp.exp(jnp.where(causal, a_cs[:, ln:ln + 1] - a_t[ln:ln + 1, :], -jnp.inf))
            ps.append((scores * seg).astype(BF16))
        xm = jnp.concatenate([jnp.where(head_of_lane == i, xc, 0.0).astype(BF16) for i in range(HG)], axis=0)
        y = _dot(jnp.concatenate(ps, axis=1), xm)
        t_prev = t_scr[grp]
        y = y + _dot(cg, t_prev.astype(BF16)) * _expand_heads([jnp.exp(a_cs[:, ln:ln + 1]) for ln in lanes], P)
        d_in = _expand_heads([jnp.exp(a_last[:, ln:ln + 1] - a_cs[:, ln:ln + 1]) for ln in lanes], P)
        chunk_decay = _expand_heads([jnp.exp(a_last[:, ln:ln + 1]) for ln in lanes], P)
        t_scr[grp] = chunk_decay * t_prev + _dot(bg.T.astype(BF16), (xc * d_in).astype(BF16))
        y = y + dskip_ref[:, grp * GW:(grp + 1) * GW] * xs_g
        zg = z_ref[:, grp * GW:(grp + 1) * GW].astype(F32)
        y = y * (zg * _sigmoid(zg))
        y_ref[:, grp * GW:(grp + 1) * GW] = _rms_rows(y, nw_ref[:, grp * GW:(grp + 1) * GW]).astype(BF16)


def _ssd(z, xbc, small, bias_row, alog_row, conv_w, conv_b_row, dskip_row, norm_w_row, batch, seq):
    nc = seq // MIX_CHUNK
    n = batch * seq

    def rows(width):
        return pl.BlockSpec((MIX_CHUNK, width), lambda b, c: (b * nc + c, 0))

    return pl.pallas_call(
        _ssd_kernel,
        grid=(batch, nc),
        in_specs=[rows(SSM_WIDTH), rows(SSM_CONV_DIM), rows(LANES),
                  _const_spec((1, LANES)), _const_spec((1, LANES)),
                  _const_spec((SSM_CONV, SSM_CONV_DIM)), _const_spec((1, SSM_CONV_DIM)),
                  _const_spec((1, SSM_WIDTH)), _const_spec((1, SSM_WIDTH))],
        out_specs=rows(SSM_WIDTH),
        out_shape=jax.ShapeDtypeStruct((n, SSM_WIDTH), BF16),
        scratch_shapes=[pltpu.VMEM((8, SSM_CONV_DIM), F32),
                        pltpu.VMEM((SSM_GROUPS, SSM_STATE, SSM_WIDTH // SSM_GROUPS), F32)],
        compiler_params=_params(("arbitrary", "arbitrary")),
        name="ssd",
    )(z, xbc, small, bias_row, alog_row, conv_w, conv_b_row, dskip_row, norm_w_row)


def _t5_bias_tile(rel, relb_ref, head):
    n = jnp.maximum(rel, 0)
    max_exact = REL_BUCKETS // 2
    nf = jnp.maximum(n, 1).astype(F32)
    large = max_exact + (jnp.log(nf / max_exact) / math.log(REL_MAX_DIST / max_exact)
                         * (REL_BUCKETS - max_exact)).astype(jnp.int32)
    large = jnp.minimum(large, REL_BUCKETS - 1)
    bucket = jnp.where(n < max_exact, n, large)
    far = relb_ref[REL_BUCKETS - 1, head]
    bias = jnp.zeros(rel.shape, F32)
    for bkt in range(REL_BUCKETS - 1):
        bias = jnp.where(bucket == bkt, relb_ref[bkt, head] - far, bias)
    return jnp.where(rel >= 0, bias, NEG_BIG)


def _attn_kernel(relb_ref, q_ref, k_ref, v_ref, lam_ref, subw_ref, g64_ref,
                 y_ref, bias_scr, qm_scr, m_scr, l_scr, acc_scr, *, lam_init):
    T = ATT_TILE
    H = DIFF_HEADS
    i = pl.program_id(1)

    @pl.when((pl.program_id(0) == 0) & (i == 0))
    def _():
        rel = lax.broadcasted_iota(jnp.int32, (T, T), 0) - lax.broadcasted_iota(jnp.int32, (T, T), 1)
        for h in range(H):
            bias_scr[0, h] = _t5_bias_tile(rel, relb_ref, h)
            bias_scr[1, h] = _t5_bias_tile(rel + T, relb_ref, h)

    q = q_ref[...]
    comp_of_lane = lax.broadcasted_iota(jnp.int32, (T, DIFF_WIDTH), 1) // DIFF_QK_DIM
    for hc in range(2 * H):
        qm_scr[hc] = jnp.where(comp_of_lane == hc, q, jnp.zeros_like(q))
    m_scr[...] = jnp.full(m_scr.shape, NEG_BIG, F32)
    l_scr[...] = jnp.zeros_like(l_scr)
    acc_scr[...] = jnp.zeros_like(acc_scr)
    v_head = lax.broadcasted_iota(jnp.int32, (T, DIFF_WIDTH), 1) // DIFF_V_DIM

    def tile(j, bias_idx):
        start = pl.multiple_of(j * T, T)
        kt = k_ref[pl.ds(start, T), :]
        vt = v_ref[pl.ds(start, T), :]
        vm = jnp.concatenate([jnp.where(v_head == h, vt, jnp.zeros_like(vt)) for h in range(H)], axis=0)
        for c in range(2):
            ps, alphas = [], []
            for h in range(H):
                hc = 2 * h + c
                s = _dot_nt(qm_scr[hc], kt)
                if bias_idx is not None:
                    s = s + bias_scr[bias_idx, h]
                m_prev = m_scr[hc]
                m_next = jnp.maximum(m_prev, jnp.max(s, axis=1, keepdims=True))
                alpha = jnp.exp(m_prev - m_next)
                p = jnp.exp(s - m_next)
                l_scr[hc] = alpha * l_scr[hc] + jnp.sum(p, axis=1, keepdims=True)
                m_scr[hc] = m_next
                ps.append(p.astype(BF16))
                alphas.append(alpha)
            pv = _dot(jnp.concatenate(ps, axis=1), vm)
            acc_scr[c] = _expand_heads(alphas, DIFF_V_DIM) * acc_scr[c] + pv

    def far_body(j, carry):
        tile(j, None)
        return carry

    lax.fori_loop(0, i - 1, far_body, 0)

    @pl.when(i >= 1)
    def _():
        tile(i - 1, 1)

    tile(i, 0)

    lam_p = lam_ref[...]
    lam = (jnp.exp(jnp.sum(lam_p[0:1] * lam_p[1:2], axis=1, keepdims=True))
           - jnp.exp(jnp.sum(lam_p[2:3] * lam_p[3:4], axis=1, keepdims=True)) + lam_init)
    o1 = acc_scr[0] / _expand_heads([l_scr[2 * h] for h in range(H)], DIFF_V_DIM)
    o2 = acc_scr[1] / _expand_heads([l_scr[2 * h + 1] for h in range(H)], DIFF_V_DIM)
    o = o1 - lam * o2
    ms = _group_mean_sq(o, g64_ref[...])
    y_ref[...] = (o * lax.rsqrt(ms + NORM_EPS) * subw_ref[...] * (1.0 - lam_init)).astype(BF16)


def _diff_attn(qn, kn, dv, rel_bias, lambdas, subw_row, g64, lam_init, batch, seq):
    nq = seq // ATT_TILE
    n = batch * seq
    q_spec = pl.BlockSpec((ATT_TILE, DIFF_WIDTH), lambda b, i: (b * nq + i, 0))
    kv_spec = pl.BlockSpec((seq, DIFF_WIDTH), lambda b, i: (b, 0))
    return pl.pallas_call(
        functools.partial(_attn_kernel, lam_init=lam_init),
        grid=(batch, nq),
        in_specs=[pl.BlockSpec(memory_space=pltpu.SMEM), q_spec, kv_spec, kv_spec,
                  _const_spec((4, DIFF_QK_DIM)), _const_spec((1, DIFF_WIDTH)),
                  _const_spec((DIFF_WIDTH, DIFF_WIDTH))],
        out_specs=q_spec,
        out_shape=jax.ShapeDtypeStruct((n, DIFF_WIDTH), BF16),
        scratch_shapes=[pltpu.VMEM((2, DIFF_HEADS, ATT_TILE, ATT_TILE), F32),
                        pltpu.VMEM((2 * DIFF_HEADS, ATT_TILE, DIFF_WIDTH), BF16),
                        pltpu.VMEM((2 * DIFF_HEADS, ATT_TILE, 1), F32),
                        pltpu.VMEM((2 * DIFF_HEADS, ATT_TILE, 1), F32),
                        pltpu.VMEM((2, ATT_TILE, DIFF_WIDTH), F32)],
        compiler_params=_params(("arbitrary", "arbitrary")),
        name="diff_attn",
    )(rel_bias, qn, kn, dv, lambdas, subw_row, g64)


def _outproj_kernel(x_ref, ym_ref, ys_ref, yd_ref, w_ref, o_ref):
    y = jnp.concatenate([ym_ref[...], ys_ref[...], yd_ref[...]], axis=1)
    o_ref[...] = x_ref[...] + _dot(y, w_ref[...])


def _outproj(x, y_m, y_s, y_d, w_out):
    n = x.shape[0]

    def rows(width):
        return pl.BlockSpec((ROW_TILE, width), lambda i: (i, 0))

    return pl.pallas_call(
        _outproj_kernel,
        grid=(n // ROW_TILE,),
        in_specs=[rows(D_MODEL), rows(MLSTM_WIDTH), rows(SSM_WIDTH), rows(DIFF_WIDTH),
                  _const_spec((D_MODEL, D_MODEL))],
        out_specs=rows(D_MODEL),
        out_shape=jax.ShapeDtypeStruct((n, D_MODEL), F32),
        compiler_params=_params(("parallel",)),
        name="outproj",
    )(x, y_m, y_s, y_d, w_out)


def _regroup_w_in(w_in):
    sizes = (MLSTM_WIDTH, MLSTM_WIDTH, MLSTM_WIDTH, MLSTM_WIDTH, MLSTM_HEADS, MLSTM_HEADS,
             SSM_WIDTH, SSM_CONV_DIM, SSM_HEADS, 2 * DIFF_HEADS * DIFF_QK_DIM, 2 * DIFF_HEADS * DIFF_QK_DIM,
             DIFF_WIDTH)
    offs = [0]
    for s in sizes:
        offs.append(offs[-1] + s)
    mq, mk, mv, mo, mi, mf, z, xbc, dt, dq, dk, dv = [w_in[:, offs[i]:offs[i + 1]] for i in range(len(sizes))]
    pad = jnp.zeros((w_in.shape[0], LANES - 2 * MLSTM_HEADS - SSM_HEADS), w_in.dtype)
    return jnp.concatenate([mq, mk, mv, mo, z, xbc, mi, mf, dt, pad, dq, dk, dv], axis=1).astype(BF16)


def _small_row(values_by_lane):
    row = jnp.zeros((LANES,), F32)
    for lane, vals in values_by_lane:
        row = lax.dynamic_update_slice(row, vals.astype(F32), (lane,))
    return row.reshape(1, LANES)


def kernel(x, ffn1_norm_w, ffn1_w_gate, ffn1_w_up, ffn1_w_down, mix_norm_w, w_in, mlstm_gate_bias, mlstm_norm_w, ssm_conv_w, ssm_conv_b, ssm_dt_bias, ssm_A_log, ssm_D, ssm_norm_w, diff_q_norm_w, diff_k_norm_w, diff_lambda, diff_subln_w, rel_bias, w_out, ffn2_norm_w, ffn2_w_gate, ffn2_w_up, ffn2_w_down):
    batch, seq, d = x.shape
    assert d == D_MODEL and seq % MIX_CHUNK == 0 and seq % ATT_TILE == 0 and (batch * seq) % ROW_TILE == 0
    xf = x.reshape(batch * seq, D_MODEL)
    g32 = _block_diag(DIFF_WIDTH, DIFF_QK_DIM, 1.0 / DIFF_QK_DIM)
    g64 = _block_diag(DIFF_WIDTH, DIFF_V_DIM, 1.0 / DIFF_V_DIM)
    for l in range(DEPTH):
        xf = _ffn(xf, ffn1_norm_w[l], ffn1_w_gate[l].astype(BF16), ffn1_w_up[l].astype(BF16),
                  ffn1_w_down[l].astype(BF16))
        qkw_row = jnp.concatenate([jnp.tile(diff_q_norm_w[l].reshape(-1), DIFF_HEADS) * (DIFF_QK_DIM ** -0.5),
                                   jnp.tile(diff_k_norm_w[l].reshape(-1), DIFF_HEADS)]).reshape(1, 2 * DIFF_WIDTH)
        m_proj, z, xbc, small, qn, kn, dv = _inproj(xf, mix_norm_w[l], _regroup_w_in(w_in[l]), qkw_row, g32)
        bias_row = _small_row([(_LANE_I, mlstm_gate_bias[l, 0]), (_LANE_F, mlstm_gate_bias[l, 1]),
                               (_LANE_DT, ssm_dt_bias[l])])
        y_m = _mlstm(m_proj, small, bias_row, mlstm_norm_w[l].reshape(1, MLSTM_WIDTH), g64, batch, seq)
        y_s = _ssd(z, xbc, small, bias_row, _small_row([(_LANE_DT, ssm_A_log[l])]), ssm_conv_w[l],
                   ssm_conv_b[l].reshape(1, SSM_CONV_DIM), jnp.repeat(ssm_D[l], SSM_HEAD_DIM).reshape(1, SSM_WIDTH),
                   ssm_norm_w[l].reshape(1, SSM_WIDTH), batch, seq)
        lam_init = 0.8 - 0.6 * math.exp(-0.3 * l)
        y_d = _diff_attn(qn, kn, dv, rel_bias, diff_lambda[l], jnp.tile(diff_subln_w[l], DIFF_HEADS).reshape(1, DIFF_WIDTH),
                         g64, lam_init, batch, seq)
        xf = _outproj(xf, y_m, y_s, y_d, w_out[l].astype(BF16))
        xf = _ffn(xf, ffn2_norm_w[l], ffn2_w_gate[l].astype(BF16), ffn2_w_up[l].astype(BF16),
                  ffn2_w_down[l].astype(BF16))
    return xf.reshape(batch, seq, D_MODEL)
```

```python
import functools
import math

import jax
import jax.numpy as jnp
from jax import lax
from jax.experimental import pallas as pl
from jax.experimental.pallas import tpu as pltpu

F32 = jnp.float32
BF16 = jnp.bfloat16

D_MODEL = 1024
DEPTH = 2
D_FF = 2816
MLSTM_HEADS = 4
MLSTM_HEAD_DIM = 64
MLSTM_WIDTH = MLSTM_HEADS * MLSTM_HEAD_DIM
SSM_HEADS = 8
SSM_HEAD_DIM = 64
SSM_WIDTH = SSM_HEADS * SSM_HEAD_DIM
SSM_STATE = 128
SSM_GROUPS = 2
SSM_CONV = 4
SSM_CONV_DIM = SSM_WIDTH + 2 * SSM_GROUPS * SSM_STATE
DIFF_HEADS = 4
DIFF_QK_DIM = 32
DIFF_V_DIM = 64
DIFF_WIDTH = DIFF_HEADS * DIFF_V_DIM
REL_BUCKETS = 32
REL_MAX_DIST = 128
NORM_EPS = 1e-6

LANES = 128
ROW_TILE = 512
FFN_CHUNK = 1408
MIX_CHUNK = 256
ATT_TILE = 256
VMEM_LIMIT = 56 * 1024 * 1024
NEG_BIG = -1e30

_C_M = 0
_C_Z = _C_M + 4 * MLSTM_WIDTH
_C_XBC = _C_Z + SSM_WIDTH
_C_SMALL = _C_XBC + SSM_CONV_DIM
_C_QK = _C_SMALL + LANES
_C_V = _C_QK + 4 * DIFF_HEADS * DIFF_QK_DIM
_C_END = _C_V + DIFF_WIDTH
_LANE_I = 0
_LANE_F = MLSTM_HEADS
_LANE_DT = 2 * MLSTM_HEADS


def _dot(a, b):
    return jnp.dot(a, b, preferred_element_type=F32)


def _dot_nt(a, b):
    return lax.dot_general(a, b, (((1,), (1,)), ((), ())), preferred_element_type=F32)


def _sigmoid(x):
    return 1.0 / (1.0 + jnp.exp(-x))


def _softplus(x):
    return jnp.maximum(x, 0.0) + jnp.log(1.0 + jnp.exp(-jnp.abs(x)))


def _rms_rows(x, w_row):
    ms = jnp.mean(x * x, axis=-1, keepdims=True)
    return x * lax.rsqrt(ms + NORM_EPS) * w_row


def _group_mean_sq(x, gmat):
    sq = x * x
    hi = sq.astype(BF16)
    lo = (sq - hi.astype(F32)).astype(BF16)
    return _dot(hi, gmat) + _dot(lo, gmat)


def _cumsum_rows(x):
    n = x.shape[0]
    row = lax.broadcasted_iota(jnp.int32, x.shape, 0)
    step = 1
    while step < n:
        x = x + jnp.where(row >= step, pltpu.roll(x, step, axis=0), 0.0)
        step *= 2
    return x


def _expand_heads(cols, width):
    rows = cols[0].shape[0]
    lane = lax.broadcasted_iota(jnp.int32, (rows, len(cols) * width), 1)
    out = cols[-1]
    for h in range(len(cols) - 2, -1, -1):
        out = jnp.where(lane < (h + 1) * width, cols[h], out)
    return jnp.broadcast_to(out, (rows, len(cols) * width))


def _block_diag(n, group, value):
    r = jnp.arange(n) // group
    return jnp.where(r[:, None] == r[None, :], value, 0.0).astype(BF16)


def _const_spec(shape):
    nd = len(shape)
    return pl.BlockSpec(shape, lambda *_: (0,) * nd)


def _params(sem):
    return pltpu.CompilerParams(dimension_semantics=sem, vmem_limit_bytes=VMEM_LIMIT)


def _ffn_kernel(x_ref, nw_ref, wg_ref, wu_ref, wd_ref, o_ref):
    x = x_ref[...]
    xn = _rms_rows(x, nw_ref[...]).astype(BF16)
    acc = None
    for c in range(D_FF // FFN_CHUNK):
        cols = slice(c * FFN_CHUNK, (c + 1) * FFN_CHUNK)
        g = _dot(xn, wg_ref[:, cols])
        u = _dot(xn, wu_ref[:, cols])
        h = (g * _sigmoid(g) * u).astype(BF16)
        part = _dot(h, wd_ref[cols, :])
        acc = part if acc is None else acc + part
    o_ref[...] = x + 0.5 * acc


def _ffn(x, norm_w, wg, wu, wd):
    n = x.shape[0]
    row_spec = pl.BlockSpec((ROW_TILE, D_MODEL), lambda i: (i, 0))
    return pl.pallas_call(
        _ffn_kernel,
        grid=(n // ROW_TILE,),
        in_specs=[row_spec, _const_spec((1, D_MODEL)), _const_spec((D_MODEL, D_FF)),
                  _const_spec((D_MODEL, D_FF)), _const_spec((D_FF, D_MODEL))],
        out_specs=row_spec,
        out_shape=jax.ShapeDtypeStruct((n, D_MODEL), F32),
        compiler_params=_params(("parallel",)),
        name="ffn",
    )(x, norm_w.reshape(1, D_MODEL), wg, wu, wd)


def _store_transposed_tiles(out_ref, a):
    a_t = a.T.astype(BF16)
    for t in range(ROW_TILE // ATT_TILE):
        out_ref[t] = a_t[:, t * ATT_TILE:(t + 1) * ATT_TILE]


def _inproj_kernel(x_ref, nw_ref, w_ref, qkw_ref, g32_ref,
                   m_ref, z_ref, xbc_ref, small_ref, qt_ref, k_ref, vt_ref):
    xn = _rms_rows(x_ref[...], nw_ref[...]).astype(BF16)
    m_ref[...] = _dot(xn, w_ref[:, _C_M:_C_Z]).astype(BF16)
    z_ref[...] = _dot(xn, w_ref[:, _C_Z:_C_XBC]).astype(BF16)
    xbc_ref[...] = _dot(xn, w_ref[:, _C_XBC:_C_SMALL]).astype(BF16)
    small_ref[...] = _dot(xn, w_ref[:, _C_SMALL:_C_QK])
    _store_transposed_tiles(vt_ref, _dot(xn, w_ref[:, _C_V:_C_END]))
    half = (_C_V - _C_QK) // 2
    qk = []
    for lo in (_C_QK, _C_QK + half):
        a = _dot(xn, w_ref[:, lo:lo + half])
        ms = _group_mean_sq(a, g32_ref[...])
        qk.append(a * lax.rsqrt(ms + NORM_EPS) * qkw_ref[:, lo - _C_QK:lo - _C_QK + half])
    _store_transposed_tiles(qt_ref, qk[0])
    k_ref[...] = qk[1].astype(BF16)


def _inproj(x, norm_w, w_all, qkw_row, g32):
    n = x.shape[0]

    def rows(width):
        return pl.BlockSpec((ROW_TILE, width), lambda i: (i, 0))

    t_spec = pl.BlockSpec((ROW_TILE // ATT_TILE, DIFF_WIDTH, ATT_TILE), lambda i: (i, 0, 0))
    t_shape = jax.ShapeDtypeStruct((n // ATT_TILE, DIFF_WIDTH, ATT_TILE), BF16)
    widths = (4 * MLSTM_WIDTH, SSM_WIDTH, SSM_CONV_DIM, LANES)
    dtypes = (BF16, BF16, BF16, F32)
    return pl.pallas_call(
        _inproj_kernel,
        grid=(n // ROW_TILE,),
        in_specs=[rows(D_MODEL), _const_spec((1, D_MODEL)), _const_spec((D_MODEL, _C_END)),
                  _const_spec((1, 2 * DIFF_WIDTH)), _const_spec((DIFF_WIDTH, DIFF_WIDTH))],
        out_specs=[rows(w) for w in widths] + [t_spec, rows(DIFF_WIDTH), t_spec],
        out_shape=[jax.ShapeDtypeStruct((n, w), d) for w, d in zip(widths, dtypes)]
        + [t_shape, jax.ShapeDtypeStruct((n, DIFF_WIDTH), BF16), t_shape],
        compiler_params=_params(("parallel",)),
        name="inproj",
    )(x, norm_w.reshape(1, D_MODEL), w_all, qkw_row, g32)


def _mlstm_kernel(q_ref, k_ref, v_ref, o_ref, small_ref, gbias_ref, nw_ref, g64_ref,
                  y_ref, c_scr, n_scr, m_scr):
    L = MIX_CHUNK
    H = MLSTM_HEADS
    W = MLSTM_HEAD_DIM

    @pl.when(pl.program_id(1) == 0)
    def _():
        c_scr[...] = jnp.zeros_like(c_scr)
        n_scr[...] = jnp.zeros_like(n_scr)
        m_scr[...] = jnp.zeros_like(m_scr)

    q = q_ref[...]
    ks = k_ref[...] * (MLSTM_HEAD_DIM ** -0.5)
    v = v_ref[...]
    g = small_ref[...] + gbias_ref[...]
    logf = jnp.minimum(g, 0.0) - jnp.log(1.0 + jnp.exp(-jnp.abs(g)))
    b = _cumsum_rows(logf)
    bT = b.T
    gT = g.T
    m_state = m_scr[...]
    head_of_lane = lax.broadcasted_iota(jnp.int32, (L, H * W), 1) // W
    causal = (lax.broadcasted_iota(jnp.int32, (L, L), 0) >= lax.broadcasted_iota(jnp.int32, (L, L), 1))

    ps, inter, den_intra, m_t, w_col, decay, m_new = [], [], [], [], [], [], []
    for h in range(H):
        b_col = b[:, _LANE_F + h:_LANE_F + h + 1]
        i_col = g[:, _LANE_I + h:_LANE_I + h + 1]
        b_row = bT[_LANE_F + h:_LANE_F + h + 1, :]
        i_row = gT[_LANE_I + h:_LANE_I + h + 1, :]
        m_prev = m_state[:, h:h + 1]
        log_d = jnp.where(causal, b_col - b_row + i_row, -jnp.inf)
        m_inter = b_col + m_prev
        mt = jnp.maximum(m_inter, jnp.max(log_d, axis=1, keepdims=True))
        d = jnp.exp(log_d - mt)
        qm = jnp.where(head_of_lane == h, q, jnp.zeros_like(q))
        s = _dot_nt(qm, ks) * d
        den_intra.append(jnp.sum(s, axis=1, keepdims=True))
        inter.append(jnp.exp(m_inter - mt))
        m_t.append(mt)
        ps.append(s.astype(BF16))
        b_last = b[L - 1:L, _LANE_F + h:_LANE_F + h + 1]
        log_w = b_last - b_col + i_col
        mn = jnp.maximum(b_last + m_prev, jnp.max(log_w, axis=0, keepdims=True))
        w_col.append(jnp.exp(log_w - mn))
        decay.append(jnp.exp(b_last + m_prev - mn))
        m_new.append(mn)

    vm = jnp.concatenate([jnp.where(head_of_lane == h, v, jnp.zeros_like(v)) for h in range(H)], axis=0)
    num = _dot(jnp.concatenate(ps, axis=1), vm)
    num = num + _expand_heads(inter, W) * _dot(q, c_scr[...].astype(BF16))
    qn = _dot(q, n_scr[...].astype(BF16))
    denom = []
    for h in range(H):
        den = den_intra[h] + inter[h] * qn[:, h:h + 1]
        denom.append(jnp.maximum(jnp.abs(den), jnp.exp(-m_t[h])))
    hout = num / _expand_heads(denom, W)
    hn = hout * lax.rsqrt(_group_mean_sq(hout, g64_ref[...]) + NORM_EPS) * nw_ref[...]
    y_ref[...] = (_sigmoid(o_ref[...].astype(F32)) * hn).astype(BF16)

    kw_t = (ks.astype(F32) * _expand_heads(w_col, W)).T.astype(BF16)
    row_head = lax.broadcasted_iota(jnp.int32, (H * W, H * W), 0) // W
    col_head = lax.broadcasted_iota(jnp.int32, (H * W, H * W), 1) // W
    decay_rows = decay[-1]
    for h in range(H - 2, -1, -1):
        decay_rows = jnp.where(row_head[:, :1] < h + 1, decay[h], decay_rows)
    c_scr[...] = decay_rows * c_scr[...] + jnp.where(row_head == col_head, _dot(kw_t, v), 0.0)
    ksum = _dot(kw_t, jnp.ones((L, LANES), BF16))
    n_lane = lax.broadcasted_iota(jnp.int32, (H * W, LANES), 1)
    n_scr[...] = decay_rows * n_scr[...] + jnp.where(row_head[:, :LANES] == n_lane, ksum, 0.0)
    lane = lax.broadcasted_iota(jnp.int32, (1, LANES), 1)
    m_next = m_state
    for h in range(H):
        m_next = jnp.where(lane == h, m_new[h], m_next)
    m_scr[...] = m_next


def _mlstm(m_proj, small, gate_bias_row, norm_w_row, g64, batch, seq):
    nc = seq // MIX_CHUNK
    n = batch * seq

    def col_block(j):
        return pl.BlockSpec((MIX_CHUNK, MLSTM_WIDTH), lambda b, c: (b * nc + c, j))

    rows = pl.BlockSpec((MIX_CHUNK, LANES), lambda b, c: (b * nc + c, 0))
    return pl.pallas_call(
        _mlstm_kernel,
        grid=(batch, nc),
        in_specs=[col_block(0), col_block(1), col_block(2), col_block(3), rows,
                  _const_spec((1, LANES)), _const_spec((1, MLSTM_WIDTH)),
                  _const_spec((MLSTM_WIDTH, MLSTM_WIDTH))],
        out_specs=pl.BlockSpec((MIX_CHUNK, MLSTM_WIDTH), lambda b, c: (b * nc + c, 0)),
        out_shape=jax.ShapeDtypeStruct((n, MLSTM_WIDTH), BF16),
        scratch_shapes=[pltpu.VMEM((MLSTM_WIDTH, MLSTM_WIDTH), F32),
                        pltpu.VMEM((MLSTM_WIDTH, LANES), F32),
                        pltpu.VMEM((1, LANES), F32)],
        compiler_params=_params(("arbitrary", "arbitrary")),
        name="mlstm",
    )(m_proj, m_proj, m_proj, m_proj, small, gate_bias_row, norm_w_row, g64)


def _ssd_kernel(z_ref, xbc_ref, small_ref, bias_ref, alog_ref, convw_ref, convb_ref, dskip_ref, nw_ref,
                y_ref, halo_scr, t_scr):
    L = MIX_CHUNK
    P = SSM_HEAD_DIM
    HG = SSM_HEADS // SSM_GROUPS
    GW = HG * P

    @pl.when(pl.program_id(1) == 0)
    def _():
        halo_scr[...] = jnp.zeros_like(halo_scr)
        t_scr[...] = jnp.zeros_like(t_scr)

    xr = xbc_ref[...].astype(F32)
    ext = jnp.concatenate([halo_scr[...], xr], axis=0)
    halo_scr[...] = xr[L - 8:, :]
    conv = convb_ref[...] + convw_ref[SSM_CONV - 1:SSM_CONV, :] * xr
    for j in range(1, SSM_CONV):
        conv = conv + convw_ref[SSM_CONV - 1 - j:SSM_CONV - j, :] * ext[8 - j:8 - j + L, :]
    xa = conv * _sigmoid(conv)
    xs = xa[:, :SSM_WIDTH]
    bm = xa[:, SSM_WIDTH:SSM_WIDTH + SSM_GROUPS * SSM_STATE]
    cm = xa[:, SSM_WIDTH + SSM_GROUPS * SSM_STATE:]

    dt = _softplus(small_ref[...] + bias_ref[...])
    a_cs = _cumsum_rows(dt * (-jnp.exp(alog_ref[...])))
    a_t = a_cs.T
    a_last = a_cs[L - 1:L, :]
    causal = (lax.broadcasted_iota(jnp.int32, (L, L), 0) >= lax.broadcasted_iota(jnp.int32, (L, L), 1))
    head_of_lane = lax.broadcasted_iota(jnp.int32, (L, GW), 1) // P

    for grp in range(SSM_GROUPS):
        heads = [grp * HG + i for i in range(HG)]
        lanes = [_LANE_DT + h for h in heads]
        cg = cm[:, grp * SSM_STATE:(grp + 1) * SSM_STATE].astype(BF16)
        bg = bm[:, grp * SSM_STATE:(grp + 1) * SSM_STATE]
        xs_g = xs[:, grp * GW:(grp + 1) * GW]
        xc = xs_g * _expand_heads([dt[:, ln:ln + 1] for ln in lanes], P)
        scores = _dot_nt(cg, bg.astype(BF16))
        ps = []
        for ln in lanes:
            seg = jnp.exp(jnp.where(causal, a_cs[:, ln:ln + 1] - a_t[ln:ln + 1, :], -jnp.inf))
            ps.append((scores * seg).astype(BF16))
        xm = jnp.concatenate([jnp.where(head_of_lane == i, xc, 0.0).astype(BF16) for i in range(HG)], axis=0)
        y = _dot(jnp.concatenate(ps, axis=1), xm)
        t_prev = t_scr[grp]
        y = y + _dot(cg, t_prev.astype(BF16)) * _expand_heads([jnp.exp(a_cs[:, ln:ln + 1]) for ln in lanes], P)
        d_in = _expand_heads([jnp.exp(a_last[:, ln:ln + 1] - a_cs[:, ln:ln + 1]) for ln in lanes], P)
        chunk_decay = _expand_heads([jnp.exp(a_last[:, ln:ln + 1]) for ln in lanes], P)
        t_scr[grp] = chunk_decay * t_prev + _dot(bg.T.astype(BF16), (xc * d_in).astype(BF16))
        y = y + dskip_ref[:, grp * GW:(grp + 1) * GW] * xs_g
        zg = z_ref[:, grp * GW:(grp + 1) * GW].astype(F32)
        y = y * (zg * _sigmoid(zg))
        y_ref[:, grp * GW:(grp + 1) * GW] = _rms_rows(y, nw_ref[:, grp * GW:(grp + 1) * GW]).astype(BF16)


def _ssd(z, xbc, small, bias_row, alog_row, conv_w, conv_b_row, dskip_row, norm_w_row, batch, seq):
    nc = seq // MIX_CHUNK
    n = batch * seq

    def rows(width):
        return pl.BlockSpec((MIX_CHUNK, width), lambda b, c: (b * nc + c, 0))

    return pl.pallas_call(
        _ssd_kernel,
        grid=(batch, nc),
        in_specs=[rows(SSM_WIDTH), rows(SSM_CONV_DIM), rows(LANES),
                  _const_spec((1, LANES)), _const_spec((1, LANES)),
                  _const_spec((SSM_CONV, SSM_CONV_DIM)), _const_spec((1, SSM_CONV_DIM)),
                  _const_spec((1, SSM_WIDTH)), _const_spec((1, SSM_WIDTH))],
        out_specs=rows(SSM_WIDTH),
        out_shape=jax.ShapeDtypeStruct((n, SSM_WIDTH), BF16),
        scratch_shapes=[pltpu.VMEM((8, SSM_CONV_DIM), F32),
                        pltpu.VMEM((SSM_GROUPS, SSM_STATE, SSM_WIDTH // SSM_GROUPS), F32)],
        compiler_params=_params(("arbitrary", "arbitrary")),
        name="ssd",
    )(z, xbc, small, bias_row, alog_row, conv_w, conv_b_row, dskip_row, norm_w_row)


def _t5_bias_tile(rel, relb_ref, head):
    n = jnp.maximum(rel, 0)
    max_exact = REL_BUCKETS // 2
    nf = jnp.maximum(n, 1).astype(F32)
    large = max_exact + (jnp.log(nf / max_exact) / math.log(REL_MAX_DIST / max_exact)
                         * (REL_BUCKETS - max_exact)).astype(jnp.int32)
    large = jnp.minimum(large, REL_BUCKETS - 1)
    bucket = jnp.where(n < max_exact, n, large)
    far = relb_ref[REL_BUCKETS - 1, head]
    bias = jnp.zeros(rel.shape, F32)
    for bkt in range(REL_BUCKETS - 1):
        bias = jnp.where(bucket == bkt, relb_ref[bkt, head] - far, bias)
    return jnp.where(rel >= 0, bias, NEG_BIG)


def _attn_kernel(relb_ref, qt_ref, k_ref, vt_ref, lam_ref, subw_ref, g64_ref,
                 y_ref, bias_scr, qm_scr, m_scr, l_scr, acc_scr, *, lam_init):
    T = ATT_TILE
    H = DIFF_HEADS
    DV = DIFF_V_DIM
    i = pl.program_id(1)

    @pl.when((pl.program_id(0) == 0) & (i == 0))
    def _():
        rel = lax.broadcasted_iota(jnp.int32, (T, T), 1) - lax.broadcasted_iota(jnp.int32, (T, T), 0)
        for h in range(H):
            bias_scr[0, h] = _t5_bias_tile(rel, relb_ref, h)
            bias_scr[1, h] = _t5_bias_tile(rel + T, relb_ref, h)

    qt = qt_ref[0]
    comp_of_row = lax.broadcasted_iota(jnp.int32, (DIFF_WIDTH, T), 0) // DIFF_QK_DIM
    for hc in range(2 * H):
        qm_scr[hc] = jnp.where(comp_of_row == hc, qt, jnp.zeros_like(qt))
    m_scr[...] = jnp.full(m_scr.shape, NEG_BIG, F32)
    l_scr[...] = jnp.zeros_like(l_scr)
    acc_scr[...] = jnp.zeros_like(acc_scr)

    def tile(j, bias_idx):
        kt = k_ref[pl.ds(pl.multiple_of(j * T, T), T), :]
        vt = vt_ref[j]
        logits = [_dot(kt, qm_scr[hc]) for hc in range(2 * H)]
        for hc in range(2 * H):
            h = hc // 2
            s = logits[hc]
            if bias_idx is not None:
                s = bias_scr[bias_idx, h] + s
            m_prev = m_scr[hc]
            m_next = jnp.maximum(m_prev, jnp.max(s, axis=0, keepdims=True))
            alpha = jnp.exp(m_prev - m_next)
            p = jnp.exp(s - m_next)
            l_scr[hc] = alpha * l_scr[hc] + jnp.sum(p, axis=0, keepdims=True)
            m_scr[hc] = m_next
            acc_scr[hc] = alpha * acc_scr[hc] + _dot(vt[h * DV:(h + 1) * DV, :], p.astype(BF16))

    def far_body(j, carry):
        tile(j, None)
        return carry

    lax.fori_loop(0, i - 1, far_body, 0)

    @pl.when(i >= 1)
    def _():
        tile(i - 1, 1)

    tile(i, 0)

    lam_p = lam_ref[...]
    lam = (jnp.exp(jnp.sum(lam_p[0:1] * lam_p[1:2], axis=1, keepdims=True))
           - jnp.exp(jnp.sum(lam_p[2:3] * lam_p[3:4], axis=1, keepdims=True)) + lam_init)
    o_t = jnp.concatenate([acc_scr[2 * h] / l_scr[2 * h] - lam * (acc_scr[2 * h + 1] / l_scr[2 * h + 1])
                           for h in range(H)], axis=0)
    o = o_t.T
    ms = _group_mean_sq(o, g64_ref[...])
    y_ref[...] = (o * lax.rsqrt(ms + NORM_EPS) * subw_ref[...] * (1.0 - lam_init)).astype(BF16)


def _diff_attn(qt, kn, vt, rel_bias, lambdas, subw_row, g64, lam_init, batch, seq):
    nq = seq // ATT_TILE
    n = batch * seq
    return pl.pallas_call(
        functools.partial(_attn_kernel, lam_init=lam_init),
        grid=(batch, nq),
        in_specs=[pl.BlockSpec(memory_space=pltpu.SMEM),
                  pl.BlockSpec((1, DIFF_WIDTH, ATT_TILE), lambda b, i: (b * nq + i, 0, 0)),
                  pl.BlockSpec((seq, DIFF_WIDTH), lambda b, i: (b, 0)),
                  pl.BlockSpec((nq, DIFF_WIDTH, ATT_TILE), lambda b, i: (b, 0, 0)),
                  _const_spec((4, DIFF_QK_DIM)), _const_spec((1, DIFF_WIDTH)),
                  _const_spec((DIFF_WIDTH, DIFF_WIDTH))],
        out_specs=pl.BlockSpec((ATT_TILE, DIFF_WIDTH), lambda b, i: (b * nq + i, 0)),
        out_shape=jax.ShapeDtypeStruct((n, DIFF_WIDTH), BF16),
        scratch_shapes=[pltpu.VMEM((2, DIFF_HEADS, ATT_TILE, ATT_TILE), F32),
                        pltpu.VMEM((2 * DIFF_HEADS, DIFF_WIDTH, ATT_TILE), BF16),
                        pltpu.VMEM((2 * DIFF_HEADS, 1, ATT_TILE), F32),
                        pltpu.VMEM((2 * DIFF_HEADS, 1, ATT_TILE), F32),
                        pltpu.VMEM((2 * DIFF_HEADS, DIFF_V_DIM, ATT_TILE), F32)],
        compiler_params=_params(("arbitrary", "arbitrary")),
        name="diff_attn",
    )(rel_bias, qt, kn, vt, lambdas, subw_row, g64)


def _outproj_kernel(x_ref, ym_ref, ys_ref, yd_ref, w_ref, o_ref):
    y = jnp.concatenate([ym_ref[...], ys_ref[...], yd_ref[...]], axis=1)
    o_ref[...] = x_ref[...] + _dot(y, w_ref[...])


def _outproj(x, y_m, y_s, y_d, w_out):
    n = x.shape[0]

    def rows(width):
        return pl.BlockSpec((ROW_TILE, width), lambda i: (i, 0))

    return pl.pallas_call(
        _outproj_kernel,
        grid=(n // ROW_TILE,),
        in_specs=[rows(D_MODEL), rows(MLSTM_WIDTH), rows(SSM_WIDTH), rows(DIFF_WIDTH),
                  _const_spec((D_MODEL, D_MODEL))],
        out_specs=rows(D_MODEL),
        out_shape=jax.ShapeDtypeStruct((n, D_MODEL), F32),
        compiler_params=_params(("parallel",)),
        name="outproj",
    )(x, y_m, y_s, y_d, w_out)


def _regroup_w_in(w_in):
    sizes = (MLSTM_WIDTH, MLSTM_WIDTH, MLSTM_WIDTH, MLSTM_WIDTH, MLSTM_HEADS, MLSTM_HEADS,
             SSM_WIDTH, SSM_CONV_DIM, SSM_HEADS, 2 * DIFF_HEADS * DIFF_QK_DIM, 2 * DIFF_HEADS * DIFF_QK_DIM,
             DIFF_WIDTH)
    offs = [0]
    for s in sizes:
        offs.append(offs[-1] + s)
    mq, mk, mv, mo, mi, mf, z, xbc, dt, dq, dk, dv = [w_in[:, offs[i]:offs[i + 1]] for i in range(len(sizes))]
    pad = jnp.zeros((w_in.shape[0], LANES - 2 * MLSTM_HEADS - SSM_HEADS), w_in.dtype)
    return jnp.concatenate([mq, mk, mv, mo, z, xbc, mi, mf, dt, pad, dq, dk, dv], axis=1).astype(BF16)


def _small_row(values_by_lane):
    row = jnp.zeros((LANES,), F32)
    for lane, vals in values_by_lane:
        row = lax.dynamic_update_slice(row, vals.astype(F32), (lane,))
    return row.reshape(1, LANES)


def kernel(x, ffn1_norm_w, ffn1_w_gate, ffn1_w_up, ffn1_w_down, mix_norm_w, w_in, mlstm_gate_bias, mlstm_norm_w, ssm_conv_w, ssm_conv_b, ssm_dt_bias, ssm_A_log, ssm_D, ssm_norm_w, diff_q_norm_w, diff_k_norm_w, diff_lambda, diff_subln_w, rel_bias, w_out, ffn2_norm_w, ffn2_w_gate, ffn2_w_up, ffn2_w_down):
    batch, seq, d = x.shape
    assert d == D_MODEL and seq % MIX_CHUNK == 0 and seq % ATT_TILE == 0 and (batch * seq) % ROW_TILE == 0
    xf = x.reshape(batch * seq, D_MODEL)
    g32 = _block_diag(DIFF_WIDTH, DIFF_QK_DIM, 1.0 / DIFF_QK_DIM)
    g64 = _block_diag(DIFF_WIDTH, DIFF_V_DIM, 1.0 / DIFF_V_DIM)
    for l in range(DEPTH):
        xf = _ffn(xf, ffn1_norm_w[l], ffn1_w_gate[l].astype(BF16), ffn1_w_up[l].astype(BF16),
                  ffn1_w_down[l].astype(BF16))
        qkw_row = jnp.concatenate([jnp.tile(diff_q_norm_w[l].reshape(-1), DIFF_HEADS) * (DIFF_QK_DIM ** -0.5),
                                   jnp.tile(diff_k_norm_w[l].reshape(-1), DIFF_HEADS)]).reshape(1, 2 * DIFF_WIDTH)
        m_proj, z, xbc, small, qt, kn, vt = _inproj(xf, mix_norm_w[l], _regroup_w_in(w_in[l]), qkw_row, g32)
        bias_row = _small_row([(_LANE_I, mlstm_gate_bias[l, 0]), (_LANE_F, mlstm_gate_bias[l, 1]),
                               (_LANE_DT, ssm_dt_bias[l])])
        y_m = _mlstm(m_proj, small, bias_row, mlstm_norm_w[l].reshape(1, MLSTM_WIDTH), g64, batch, seq)
        y_s = _ssd(z, xbc, small, bias_row, _small_row([(_LANE_DT, ssm_A_log[l])]), ssm_conv_w[l],
                   ssm_conv_b[l].reshape(1, SSM_CONV_DIM), jnp.repeat(ssm_D[l], SSM_HEAD_DIM).reshape(1, SSM_WIDTH),
                   ssm_norm_w[l].reshape(1, SSM_WIDTH), batch, seq)
        lam_init = 0.8 - 0.6 * math.exp(-0.3 * l)
        y_d = _diff_attn(qt, kn, vt, rel_bias, diff_lambda[l], jnp.tile(diff_subln_w[l], DIFF_HEADS).reshape(1, DIFF_WIDTH),
                         g64, lam_init, batch, seq)
        xf = _outproj(xf, y_m, y_s, y_d, w_out[l].astype(BF16))
        xf = _ffn(xf, ffn2_norm_w[l], ffn2_w_gate[l].astype(BF16), ffn2_w_up[l].astype(BF16),
                  ffn2_w_down[l].astype(BF16))
    return xf.reshape(batch, seq, D_MODEL)
```

```python
import functools
import math

import jax
import jax.numpy as jnp
from jax import lax
from jax.experimental import pallas as pl
from jax.experimental.pallas import tpu as pltpu

F32 = jnp.float32
BF16 = jnp.bfloat16

D_MODEL = 1024
DEPTH = 2
D_FF = 2816
MLSTM_HEADS = 4
MLSTM_HEAD_DIM = 64
MLSTM_WIDTH = MLSTM_HEADS * MLSTM_HEAD_DIM
SSM_HEADS = 8
SSM_HEAD_DIM = 64
SSM_WIDTH = SSM_HEADS * SSM_HEAD_DIM
SSM_STATE = 128
SSM_GROUPS = 2
SSM_CONV = 4
SSM_CONV_DIM = SSM_WIDTH + 2 * SSM_GROUPS * SSM_STATE
DIFF_HEADS = 4
DIFF_QK_DIM = 32
DIFF_V_DIM = 64
DIFF_WIDTH = DIFF_HEADS * DIFF_V_DIM
REL_BUCKETS = 32
REL_MAX_DIST = 128
NORM_EPS = 1e-6

LANES = 128
ROW_TILE = 512
FFN_CHUNK = 1408
MIX_CHUNK = 256
ATT_TILE = 256
VMEM_LIMIT = 56 * 1024 * 1024
NEG_BIG = -1e30
LOG2E = math.log2(math.e)
ONES_ROWS = 16

_C_M = 0
_C_Z = _C_M + 4 * MLSTM_WIDTH
_C_XBC = _C_Z + SSM_WIDTH
_C_SMALL = _C_XBC + SSM_CONV_DIM
_C_QK = _C_SMALL + LANES
_C_V = _C_QK + 4 * DIFF_HEADS * DIFF_QK_DIM
_C_END = _C_V + DIFF_WIDTH
_LANE_I = 0
_LANE_F = MLSTM_HEADS
_LANE_DT = 2 * MLSTM_HEADS


def _dot(a, b):
    return jnp.dot(a, b, preferred_element_type=F32)


def _dot_nt(a, b):
    return lax.dot_general(a, b, (((1,), (1,)), ((), ())), preferred_element_type=F32)


def _sigmoid(x):
    return 1.0 / (1.0 + jnp.exp(-x))


def _softplus(x):
    return jnp.maximum(x, 0.0) + jnp.log(1.0 + jnp.exp(-jnp.abs(x)))


def _rms_rows(x, w_row):
    ms = jnp.mean(x * x, axis=-1, keepdims=True)
    return x * lax.rsqrt(ms + NORM_EPS) * w_row


def _group_mean_sq(x, gmat):
    sq = x * x
    hi = sq.astype(BF16)
    lo = (sq - hi.astype(F32)).astype(BF16)
    return _dot(hi, gmat) + _dot(lo, gmat)


def _cumsum_rows(x):
    n = x.shape[0]
    row = lax.broadcasted_iota(jnp.int32, x.shape, 0)
    step = 1
    while step < n:
        x = x + jnp.where(row >= step, pltpu.roll(x, step, axis=0), 0.0)
        step *= 2
    return x


def _expand_heads(cols, width):
    rows = cols[0].shape[0]
    lane = lax.broadcasted_iota(jnp.int32, (rows, len(cols) * width), 1)
    out = cols[-1]
    for h in range(len(cols) - 2, -1, -1):
        out = jnp.where(lane < (h + 1) * width, cols[h], out)
    return jnp.broadcast_to(out, (rows, len(cols) * width))


def _block_diag(n, group, value):
    r = jnp.arange(n) // group
    return jnp.where(r[:, None] == r[None, :], value, 0.0).astype(BF16)


def _const_spec(shape):
    nd = len(shape)
    return pl.BlockSpec(shape, lambda *_: (0,) * nd)


def _params(sem):
    return pltpu.CompilerParams(dimension_semantics=sem, vmem_limit_bytes=VMEM_LIMIT)


def _ffn_kernel(x_ref, nw_ref, wg_ref, wu_ref, wd_ref, o_ref):
    x = x_ref[...]
    xn = _rms_rows(x, nw_ref[...]).astype(BF16)
    acc = None
    for c in range(D_FF // FFN_CHUNK):
        cols = slice(c * FFN_CHUNK, (c + 1) * FFN_CHUNK)
        g = _dot(xn, wg_ref[:, cols])
        u = _dot(xn, wu_ref[:, cols])
        h = (g * _sigmoid(g) * u).astype(BF16)
        part = _dot(h, wd_ref[cols, :])
        acc = part if acc is None else acc + part
    o_ref[...] = x + 0.5 * acc


def _ffn(x, norm_w, wg, wu, wd):
    n = x.shape[0]
    row_spec = pl.BlockSpec((ROW_TILE, D_MODEL), lambda i: (i, 0))
    return pl.pallas_call(
        _ffn_kernel,
        grid=(n // ROW_TILE,),
        in_specs=[row_spec, _const_spec((1, D_MODEL)), _const_spec((D_MODEL, D_FF)),
                  _const_spec((D_MODEL, D_FF)), _const_spec((D_FF, D_MODEL))],
        out_specs=row_spec,
        out_shape=jax.ShapeDtypeStruct((n, D_MODEL), F32),
        compiler_params=_params(("parallel",)),
        name="ffn",
    )(x, norm_w.reshape(1, D_MODEL), wg, wu, wd)


def _store_transposed_tiles(out_ref, a):
    a_t = a.T.astype(BF16)
    for t in range(ROW_TILE // ATT_TILE):
        out_ref[t] = a_t[:, t * ATT_TILE:(t + 1) * ATT_TILE]


def _inproj_kernel(x_ref, nw_ref, w_ref, qkw_ref, g32_ref,
                   m_ref, z_ref, xbc_ref, small_ref, qt_ref, k_ref, vt_ref):
    xn = _rms_rows(x_ref[...], nw_ref[...]).astype(BF16)
    m_ref[...] = _dot(xn, w_ref[:, _C_M:_C_Z]).astype(BF16)
    z_ref[...] = _dot(xn, w_ref[:, _C_Z:_C_XBC]).astype(BF16)
    xbc_ref[...] = _dot(xn, w_ref[:, _C_XBC:_C_SMALL]).astype(BF16)
    small_ref[...] = _dot(xn, w_ref[:, _C_SMALL:_C_QK])
    _store_transposed_tiles(vt_ref, _dot(xn, w_ref[:, _C_V:_C_END]))
    half = (_C_V - _C_QK) // 2
    qk = []
    for lo in (_C_QK, _C_QK + half):
        a = _dot(xn, w_ref[:, lo:lo + half])
        ms = _group_mean_sq(a, g32_ref[...])
        qk.append(a * lax.rsqrt(ms + NORM_EPS) * qkw_ref[:, lo - _C_QK:lo - _C_QK + half])
    _store_transposed_tiles(qt_ref, qk[0])
    k_ref[...] = qk[1].astype(BF16)


def _inproj(x, norm_w, w_all, qkw_row, g32):
    n = x.shape[0]

    def rows(width):
        return pl.BlockSpec((ROW_TILE, width), lambda i: (i, 0))

    t_spec = pl.BlockSpec((ROW_TILE // ATT_TILE, DIFF_WIDTH, ATT_TILE), lambda i: (i, 0, 0))
    t_shape = jax.ShapeDtypeStruct((n // ATT_TILE, DIFF_WIDTH, ATT_TILE), BF16)
    widths = (4 * MLSTM_WIDTH, SSM_WIDTH, SSM_CONV_DIM, LANES)
    dtypes = (BF16, BF16, BF16, F32)
    return pl.pallas_call(
        _inproj_kernel,
        grid=(n // ROW_TILE,),
        in_specs=[rows(D_MODEL), _const_spec((1, D_MODEL)), _const_spec((D_MODEL, _C_END)),
                  _const_spec((1, 2 * DIFF_WIDTH)), _const_spec((DIFF_WIDTH, DIFF_WIDTH))],
        out_specs=[rows(w) for w in widths] + [t_spec, rows(DIFF_WIDTH), t_spec],
        out_shape=[jax.ShapeDtypeStruct((n, w), d) for w, d in zip(widths, dtypes)]
        + [t_shape, jax.ShapeDtypeStruct((n, DIFF_WIDTH), BF16), t_shape],
        compiler_params=_params(("parallel",)),
        name="inproj",
    )(x, norm_w.reshape(1, D_MODEL), w_all, qkw_row, g32)


def _mlstm_kernel(q_ref, k_ref, v_ref, o_ref, small_ref, gbias_ref, nw_ref, g64_ref,
                  y_ref, c_scr, n_scr, m_scr):
    L = MIX_CHUNK
    H = MLSTM_HEADS
    W = MLSTM_HEAD_DIM

    @pl.when(pl.program_id(1) == 0)
    def _():
        c_scr[...] = jnp.zeros_like(c_scr)
        n_scr[...] = jnp.zeros_like(n_scr)
        m_scr[...] = jnp.zeros_like(m_scr)

    q = q_ref[...]
    ks = k_ref[...] * (MLSTM_HEAD_DIM ** -0.5)
    v = v_ref[...]
    g = small_ref[...] + gbias_ref[...]
    logf = jnp.minimum(g, 0.0) - jnp.log(1.0 + jnp.exp(-jnp.abs(g)))
    b = _cumsum_rows(logf)
    bT = b.T
    gT = g.T
    m_state = m_scr[...]
    head_of_lane = lax.broadcasted_iota(jnp.int32, (L, H * W), 1) // W
    causal = (lax.broadcasted_iota(jnp.int32, (L, L), 0) >= lax.broadcasted_iota(jnp.int32, (L, L), 1))

    ps, inter, den_intra, m_t, w_col, decay, m_new = [], [], [], [], [], [], []
    for h in range(H):
        b_col = b[:, _LANE_F + h:_LANE_F + h + 1]
        i_col = g[:, _LANE_I + h:_LANE_I + h + 1]
        b_row = bT[_LANE_F + h:_LANE_F + h + 1, :]
        i_row = gT[_LANE_I + h:_LANE_I + h + 1, :]
        m_prev = m_state[:, h:h + 1]
        log_d = jnp.where(causal, b_col - b_row + i_row, -jnp.inf)
        m_inter = b_col + m_prev
        mt = jnp.maximum(m_inter, jnp.max(log_d, axis=1, keepdims=True))
        d = jnp.exp(log_d - mt)
        qm = jnp.where(head_of_lane == h, q, jnp.zeros_like(q))
        s = _dot_nt(qm, ks) * d
        den_intra.append(jnp.sum(s, axis=1, keepdims=True))
        inter.append(jnp.exp(m_inter - mt))
        m_t.append(mt)
        ps.append(s.astype(BF16))
        b_last = b[L - 1:L, _LANE_F + h:_LANE_F + h + 1]
        log_w = b_last - b_col + i_col
        mn = jnp.maximum(b_last + m_prev, jnp.max(log_w, axis=0, keepdims=True))
        w_col.append(jnp.exp(log_w - mn))
        decay.append(jnp.exp(b_last + m_prev - mn))
        m_new.append(mn)

    vm = jnp.concatenate([jnp.where(head_of_lane == h, v, jnp.zeros_like(v)) for h in range(H)], axis=0)
    num = _dot(jnp.concatenate(ps, axis=1), vm)
    num = num + _expand_heads(inter, W) * _dot(q, c_scr[...].astype(BF16))
    qn = _dot(q, n_scr[...].astype(BF16))
    denom = []
    for h in range(H):
        den = den_intra[h] + inter[h] * qn[:, h:h + 1]
        denom.append(jnp.maximum(jnp.abs(den), jnp.exp(-m_t[h])))
    hout = num / _expand_heads(denom, W)
    hn = hout * lax.rsqrt(_group_mean_sq(hout, g64_ref[...]) + NORM_EPS) * nw_ref[...]
    y_ref[...] = (_sigmoid(o_ref[...].astype(F32)) * hn).astype(BF16)

    kw_t = (ks.astype(F32) * _expand_heads(w_col, W)).T.astype(BF16)
    row_head = lax.broadcasted_iota(jnp.int32, (H * W, H * W), 0) // W
    col_head = lax.broadcasted_iota(jnp.int32, (H * W, H * W), 1) // W
    decay_rows = decay[-1]
    for h in range(H - 2, -1, -1):
        decay_rows = jnp.where(row_head[:, :1] < h + 1, decay[h], decay_rows)
    c_scr[...] = decay_rows * c_scr[...] + jnp.where(row_head == col_head, _dot(kw_t, v), 0.0)
    ksum = _dot(kw_t, jnp.ones((L, LANES), BF16))
    n_lane = lax.broadcasted_iota(jnp.int32, (H * W, LANES), 1)
    n_scr[...] = decay_rows * n_scr[...] + jnp.where(row_head[:, :LANES] == n_lane, ksum, 0.0)
    lane = lax.broadcasted_iota(jnp.int32, (1, LANES), 1)
    m_next = m_state
    for h in range(H):
        m_next = jnp.where(lane == h, m_new[h], m_next)
    m_scr[...] = m_next


def _mlstm(m_proj, small, gate_bias_row, norm_w_row, g64, batch, seq):
    nc = seq // MIX_CHUNK
    n = batch * seq

    def col_block(j):
        return pl.BlockSpec((MIX_CHUNK, MLSTM_WIDTH), lambda b, c: (b * nc + c, j))

    rows = pl.BlockSpec((MIX_CHUNK, LANES), lambda b, c: (b * nc + c, 0))
    return pl.pallas_call(
        _mlstm_kernel,
        grid=(batch, nc),
        in_specs=[col_block(0), col_block(1), col_block(2), col_block(3), rows,
                  _const_spec((1, LANES)), _const_spec((1, MLSTM_WIDTH)),
                  _const_spec((MLSTM_WIDTH, MLSTM_WIDTH))],
        out_specs=pl.BlockSpec((MIX_CHUNK, MLSTM_WIDTH), lambda b, c: (b * nc + c, 0)),
        out_shape=jax.ShapeDtypeStruct((n, MLSTM_WIDTH), BF16),
        scratch_shapes=[pltpu.VMEM((MLSTM_WIDTH, MLSTM_WIDTH), F32),
                        pltpu.VMEM((MLSTM_WIDTH, LANES), F32),
                        pltpu.VMEM((1, LANES), F32)],
        compiler_params=_params(("arbitrary", "arbitrary")),
        name="mlstm",
    )(m_proj, m_proj, m_proj, m_proj, small, gate_bias_row, norm_w_row, g64)


def _ssd_kernel(z_ref, xbc_ref, small_ref, bias_ref, alog_ref, convw_ref, convb_ref, dskip_ref, nw_ref,
                y_ref, halo_scr, t_scr):
    L = MIX_CHUNK
    P = SSM_HEAD_DIM
    HG = SSM_HEADS // SSM_GROUPS
    GW = HG * P

    @pl.when(pl.program_id(1) == 0)
    def _():
        halo_scr[...] = jnp.zeros_like(halo_scr)
        t_scr[...] = jnp.zeros_like(t_scr)

    xr = xbc_ref[...].astype(F32)
    ext = jnp.concatenate([halo_scr[...], xr], axis=0)
    halo_scr[...] = xr[L - 8:, :]
    conv = convb_ref[...] + convw_ref[SSM_CONV - 1:SSM_CONV, :] * xr
    for j in range(1, SSM_CONV):
        conv = conv + convw_ref[SSM_CONV - 1 - j:SSM_CONV - j, :] * ext[8 - j:8 - j + L, :]
    xa = conv * _sigmoid(conv)
    xs = xa[:, :SSM_WIDTH]
    bm = xa[:, SSM_WIDTH:SSM_WIDTH + SSM_GROUPS * SSM_STATE]
    cm = xa[:, SSM_WIDTH + SSM_GROUPS * SSM_STATE:]

    dt = _softplus(small_ref[...] + bias_ref[...])
    a_cs = _cumsum_rows(dt * (-jnp.exp(alog_ref[...])))
    a_t = a_cs.T
    a_last = a_cs[L - 1:L, :]
    causal = (lax.broadcasted_iota(jnp.int32, (L, L), 0) >= lax.broadcasted_iota(jnp.int32, (L, L), 1))
    head_of_lane = lax.broadcasted_iota(jnp.int32, (L, GW), 1) // P

    for grp in range(SSM_GROUPS):
        heads = [grp * HG + i for i in range(HG)]
        lanes = [_LANE_DT + h for h in heads]
        cg = cm[:, grp * SSM_STATE:(grp + 1) * SSM_STATE].astype(BF16)
        bg = bm[:, grp * SSM_STATE:(grp + 1) * SSM_STATE]
        xs_g = xs[:, grp * GW:(grp + 1) * GW]
        xc = xs_g * _expand_heads([dt[:, ln:ln + 1] for ln in lanes], P)
        scores = _dot_nt(cg, bg.astype(BF16))
        ps = []
        for ln in lanes:
            seg = jnp.exp(jnp.where(causal, a_cs[:, ln:ln + 1] - a_t[ln:ln + 1, :], -jnp.inf))
            ps.append((scores * seg).astype(BF16))
        xm = jnp.concatenate([jnp.where(head_of_lane == i, xc, 0.0).astype(BF16) for i in range(HG)], axis=0)
        y = _dot(jnp.concatenate(ps, axis=1), xm)
        t_prev = t_scr[grp]
        y = y + _dot(cg, t_prev.astype(BF16)) * _expand_heads([jnp.exp(a_cs[:, ln:ln + 1]) for ln in lanes], P)
        d_in = _expand_heads([jnp.exp(a_last[:, ln:ln + 1] - a_cs[:, ln:ln + 1]) for ln in lanes], P)
        chunk_decay = _expand_heads([jnp.exp(a_last[:, ln:ln + 1]) for ln in lanes], P)
        t_scr[grp] = chunk_decay * t_prev + _dot(bg.T.astype(BF16), (xc * d_in).astype(BF16))
        y = y + dskip_ref[:, grp * GW:(grp + 1) * GW] * xs_g
        zg = z_ref[:, grp * GW:(grp + 1) * GW].astype(F32)
        y = y * (zg * _sigmoid(zg))
        y_ref[:, grp * GW:(grp + 1) * GW] = _rms_rows(y, nw_ref[:, grp * GW:(grp + 1) * GW]).astype(BF16)


def _ssd(z, xbc, small, bias_row, alog_row, conv_w, conv_b_row, dskip_row, norm_w_row, batch, seq):
    nc = seq // MIX_CHUNK
    n = batch * seq

    def rows(width):
        return pl.BlockSpec((MIX_CHUNK, width), lambda b, c: (b * nc + c, 0))

    return pl.pallas_call(
        _ssd_kernel,
        grid=(batch, nc),
        in_specs=[rows(SSM_WIDTH), rows(SSM_CONV_DIM), rows(LANES),
                  _const_spec((1, LANES)), _const_spec((1, LANES)),
                  _const_spec((SSM_CONV, SSM_CONV_DIM)), _const_spec((1, SSM_CONV_DIM)),
                  _const_spec((1, SSM_WIDTH)), _const_spec((1, SSM_WIDTH))],
        out_specs=rows(SSM_WIDTH),
        out_shape=jax.ShapeDtypeStruct((n, SSM_WIDTH), BF16),
        scratch_shapes=[pltpu.VMEM((8, SSM_CONV_DIM), F32),
                        pltpu.VMEM((SSM_GROUPS, SSM_STATE, SSM_WIDTH // SSM_GROUPS), F32)],
        compiler_params=_params(("arbitrary", "arbitrary")),
        name="ssd",
    )(z, xbc, small, bias_row, alog_row, conv_w, conv_b_row, dskip_row, norm_w_row)


def _t5_bias_tile(rel, relb_ref, head):
    n = jnp.maximum(rel, 0)
    max_exact = REL_BUCKETS // 2
    nf = jnp.maximum(n, 1).astype(F32)
    large = max_exact + (jnp.log(nf / max_exact) / math.log(REL_MAX_DIST / max_exact)
                         * (REL_BUCKETS - max_exact)).astype(jnp.int32)
    large = jnp.minimum(large, REL_BUCKETS - 1)
    bucket = jnp.where(n < max_exact, n, large)
    far = relb_ref[REL_BUCKETS - 1, head]
    bias = jnp.zeros(rel.shape, F32)
    for bkt in range(REL_BUCKETS - 1):
        bias = jnp.where(bucket == bkt, (relb_ref[bkt, head] - far) * LOG2E, bias)
    return jnp.where(rel >= 0, bias, NEG_BIG)


def _attn_kernel(relb_ref, qt_ref, k_ref, vt_ref, lam_ref, subw_ref, g64_ref,
                 y_ref, bias_scr, qm_scr, m_scr, acc_scr, sa_scr, sb_scr, *, lam_init):
    T = ATT_TILE
    H = DIFF_HEADS
    DV = DIFF_V_DIM
    i = pl.program_id(1)

    @pl.when((pl.program_id(0) == 0) & (i == 0))
    def _():
        rel = lax.broadcasted_iota(jnp.int32, (T, T), 1) - lax.broadcasted_iota(jnp.int32, (T, T), 0)
        for h in range(H):
            bias_scr[0, h] = _t5_bias_tile(rel, relb_ref, h)
            bias_scr[1, h] = _t5_bias_tile(rel + T, relb_ref, h)

    qt = qt_ref[0]
    comp_of_row = lax.broadcasted_iota(jnp.int32, (DIFF_WIDTH, T), 0) // DIFF_QK_DIM
    for hc in range(2 * H):
        qm_scr[hc] = jnp.where(comp_of_row == hc, qt, jnp.zeros_like(qt))
    m_scr[...] = jnp.full(m_scr.shape, NEG_BIG, F32)
    acc_scr[...] = jnp.zeros_like(acc_scr)
    ones_rows = jnp.ones((ONES_ROWS, T), BF16)

    def logits_into(s_ref, j):
        kt = k_ref[pl.ds(pl.multiple_of(j * T, T), T), :]
        for hc in range(2 * H):
            s_ref[hc] = _dot(kt, qm_scr[hc])

    def consume(s_ref, j, bias_idx):
        vt = vt_ref[j]
        for hc in range(2 * H):
            h = hc // 2
            s = s_ref[hc]
            if bias_idx is not None:
                s = bias_scr[bias_idx, h] + s
            m_prev = m_scr[hc]
            m_next = jnp.maximum(m_prev, jnp.max(s, axis=0, keepdims=True))
            alpha = jnp.exp2(m_prev - m_next)
            p = jnp.exp2(s - m_next).astype(BF16)
            m_scr[hc] = m_next
            v_aug = jnp.concatenate([vt[h * DV:(h + 1) * DV, :], ones_rows], axis=0)
            acc_scr[hc] = alpha * acc_scr[hc] + _dot(v_aug, p)

    DIAG, NEAR = 0, 1
    n_far = jnp.maximum(i - 1, 0)
    logits_into(sa_scr, 0)

    def far_pair(t, carry):
        j = 2 * t
        logits_into(sb_scr, j + 1)
        consume(sa_scr, j, None)
        logits_into(sa_scr, j + 2)
        consume(sb_scr, j + 1, None)
        return carry

    lax.fori_loop(0, n_far // 2, far_pair, 0)
    u = 2 * (n_far // 2)
    odd = n_far % 2 == 1

    @pl.when(odd)
    def _():
        logits_into(sb_scr, u + 1)
        consume(sa_scr, u, None)
        logits_into(sa_scr, u + 2)
        consume(sb_scr, u + 1, NEAR)
        consume(sa_scr, u + 2, DIAG)

    @pl.when(jnp.logical_not(odd) & (i >= 1))
    def _():
        logits_into(sb_scr, u + 1)
        consume(sa_scr, u, NEAR)
        consume(sb_scr, u + 1, DIAG)

    @pl.when(i == 0)
    def _():
        consume(sa_scr, 0, DIAG)

    lam_p = lam_ref[...]
    lam = (jnp.exp(jnp.sum(lam_p[0:1] * lam_p[1:2], axis=1, keepdims=True))
           - jnp.exp(jnp.sum(lam_p[2:3] * lam_p[3:4], axis=1, keepdims=True)) + lam_init)

    def normalised(hc):
        return acc_scr[hc, :DV, :] / acc_scr[hc, DV:DV + 1, :]

    o_t = jnp.concatenate([normalised(2 * h) - lam * normalised(2 * h + 1) for h in range(H)], axis=0)
    o = o_t.T
    ms = _group_mean_sq(o, g64_ref[...])
    y_ref[...] = (o * lax.rsqrt(ms + NORM_EPS) * subw_ref[...] * (1.0 - lam_init)).astype(BF16)


def _diff_attn(qt, kn, vt, rel_bias, lambdas, subw_row, g64, lam_init, batch, seq):
    nq = seq // ATT_TILE
    n = batch * seq
    return pl.pallas_call(
        functools.partial(_attn_kernel, lam_init=lam_init),
        grid=(batch, nq),
        in_specs=[pl.BlockSpec(memory_space=pltpu.SMEM),
                  pl.BlockSpec((1, DIFF_WIDTH, ATT_TILE), lambda b, i: (b * nq + i, 0, 0)),
                  pl.BlockSpec((seq, DIFF_WIDTH), lambda b, i: (b, 0)),
                  pl.BlockSpec((nq, DIFF_WIDTH, ATT_TILE), lambda b, i: (b, 0, 0)),
                  _const_spec((4, DIFF_QK_DIM)), _const_spec((1, DIFF_WIDTH)),
                  _const_spec((DIFF_WIDTH, DIFF_WIDTH))],
        out_specs=pl.BlockSpec((ATT_TILE, DIFF_WIDTH), lambda b, i: (b * nq + i, 0)),
        out_shape=jax.ShapeDtypeStruct((n, DIFF_WIDTH), BF16),
        scratch_shapes=[pltpu.VMEM((2, DIFF_HEADS, ATT_TILE, ATT_TILE), F32),
                        pltpu.VMEM((2 * DIFF_HEADS, DIFF_WIDTH, ATT_TILE), BF16),
                        pltpu.VMEM((2 * DIFF_HEADS, 1, ATT_TILE), F32),
                        pltpu.VMEM((2 * DIFF_HEADS, DIFF_V_DIM + ONES_ROWS, ATT_TILE), F32),
                        pltpu.VMEM((2 * DIFF_HEADS, ATT_TILE, ATT_TILE), F32),
                        pltpu.VMEM((2 * DIFF_HEADS, ATT_TILE, ATT_TILE), F32)],
        compiler_params=_params(("arbitrary", "arbitrary")),
        name="diff_attn",
    )(rel_bias, qt, kn, vt, lambdas, subw_row, g64)


def _outproj_kernel(x_ref, ym_ref, ys_ref, yd_ref, w_ref, o_ref):
    y = jnp.concatenate([ym_ref[...], ys_ref[...], yd_ref[...]], axis=1)
    o_ref[...] = x_ref[...] + _dot(y, w_ref[...])


def _outproj(x, y_m, y_s, y_d, w_out):
    n = x.shape[0]

    def rows(width):
        return pl.BlockSpec((ROW_TILE, width), lambda i: (i, 0))

    return pl.pallas_call(
        _outproj_kernel,
        grid=(n // ROW_TILE,),
        in_specs=[rows(D_MODEL), rows(MLSTM_WIDTH), rows(SSM_WIDTH), rows(DIFF_WIDTH),
                  _const_spec((D_MODEL, D_MODEL))],
        out_specs=rows(D_MODEL),
        out_shape=jax.ShapeDtypeStruct((n, D_MODEL), F32),
        compiler_params=_params(("parallel",)),
        name="outproj",
    )(x, y_m, y_s, y_d, w_out)


def _regroup_w_in(w_in):
    sizes = (MLSTM_WIDTH, MLSTM_WIDTH, MLSTM_WIDTH, MLSTM_WIDTH, MLSTM_HEADS, MLSTM_HEADS,
             SSM_WIDTH, SSM_CONV_DIM, SSM_HEADS, 2 * DIFF_HEADS * DIFF_QK_DIM, 2 * DIFF_HEADS * DIFF_QK_DIM,
             DIFF_WIDTH)
    offs = [0]
    for s in sizes:
        offs.append(offs[-1] + s)
    mq, mk, mv, mo, mi, mf, z, xbc, dt, dq, dk, dv = [w_in[:, offs[i]:offs[i + 1]] for i in range(len(sizes))]
    pad = jnp.zeros((w_in.shape[0], LANES - 2 * MLSTM_HEADS - SSM_HEADS), w_in.dtype)
    return jnp.concatenate([mq, mk, mv, mo, z, xbc, mi, mf, dt, pad, dq, dk, dv], axis=1).astype(BF16)


def _small_row(values_by_lane):
    row = jnp.zeros((LANES,), F32)
    for lane, vals in values_by_lane:
        row = lax.dynamic_update_slice(row, vals.astype(F32), (lane,))
    return row.reshape(1, LANES)


def kernel(x, ffn1_norm_w, ffn1_w_gate, ffn1_w_up, ffn1_w_down, mix_norm_w, w_in, mlstm_gate_bias, mlstm_norm_w, ssm_conv_w, ssm_conv_b, ssm_dt_bias, ssm_A_log, ssm_D, ssm_norm_w, diff_q_norm_w, diff_k_norm_w, diff_lambda, diff_subln_w, rel_bias, w_out, ffn2_norm_w, ffn2_w_gate, ffn2_w_up, ffn2_w_down):
    batch, seq, d = x.shape
    assert d == D_MODEL and seq % MIX_CHUNK == 0 and seq % ATT_TILE == 0 and (batch * seq) % ROW_TILE == 0
    xf = x.reshape(batch * seq, D_MODEL)
    g32 = _block_diag(DIFF_WIDTH, DIFF_QK_DIM, 1.0 / DIFF_QK_DIM)
    g64 = _block_diag(DIFF_WIDTH, DIFF_V_DIM, 1.0 / DIFF_V_DIM)
    for l in range(DEPTH):
        xf = _ffn(xf, ffn1_norm_w[l], ffn1_w_gate[l].astype(BF16), ffn1_w_up[l].astype(BF16),
                  ffn1_w_down[l].astype(BF16))
        qkw_row = jnp.concatenate([jnp.tile(diff_q_norm_w[l].reshape(-1), DIFF_HEADS) * (DIFF_QK_DIM ** -0.5 * LOG2E),
                                   jnp.tile(diff_k_norm_w[l].reshape(-1), DIFF_HEADS)]).reshape(1, 2 * DIFF_WIDTH)
        m_proj, z, xbc, small, qt, kn, vt = _inproj(xf, mix_norm_w[l], _regroup_w_in(w_in[l]), qkw_row, g32)
        bias_row = _small_row([(_LANE_I, mlstm_gate_bias[l, 0]), (_LANE_F, mlstm_gate_bias[l, 1]),
                               (_LANE_DT, ssm_dt_bias[l])])
        y_m = _mlstm(m_proj, small, bias_row, mlstm_norm_w[l].reshape(1, MLSTM_WIDTH), g64, batch, seq)
        y_s = _ssd(z, xbc, small, bias_row, _small_row([(_LANE_DT, ssm_A_log[l])]), ssm_conv_w[l],
                   ssm_conv_b[l].reshape(1, SSM_CONV_DIM), jnp.repeat(ssm_D[l], SSM_HEAD_DIM).reshape(1, SSM_WIDTH),
                   ssm_norm_w[l].reshape(1, SSM_WIDTH), batch, seq)
        lam_init = 0.8 - 0.6 * math.exp(-0.3 * l)
        y_d = _diff_attn(qt, kn, vt, rel_bias, diff_lambda[l], jnp.tile(diff_subln_w[l], DIFF_HEADS).reshape(1, DIFF_WIDTH),
                         g64, lam_init, batch, seq)
        xf = _outproj(xf, y_m, y_s, y_d, w_out[l].astype(BF16))
        xf = _ffn(xf, ffn2_norm_w[l], ffn2_w_gate[l].astype(BF16), ffn2_w_up[l].astype(BF16),
                  ffn2_w_down[l].astype(BF16))
    return xf.reshape(batch, seq, D_MODEL)
```

```python
import functools
import math

import jax
import jax.numpy as jnp
from jax import lax
from jax.experimental import pallas as pl
from jax.experimental.pallas import tpu as pltpu

F32 = jnp.float32
BF16 = jnp.bfloat16

D_MODEL = 1024
DEPTH = 2
D_FF = 2816
MLSTM_HEADS = 4
MLSTM_HEAD_DIM = 64
MLSTM_WIDTH = MLSTM_HEADS * MLSTM_HEAD_DIM
SSM_HEADS = 8
SSM_HEAD_DIM = 64
SSM_WIDTH = SSM_HEADS * SSM_HEAD_DIM
SSM_STATE = 128
SSM_GROUPS = 2
SSM_CONV = 4
SSM_CONV_DIM = SSM_WIDTH + 2 * SSM_GROUPS * SSM_STATE
DIFF_HEADS = 4
DIFF_QK_DIM = 32
DIFF_V_DIM = 64
DIFF_WIDTH = DIFF_HEADS * DIFF_V_DIM
REL_BUCKETS = 32
REL_MAX_DIST = 128
NORM_EPS = 1e-6

LANES = 128
ROW_TILE = 512
FFN_CHUNK = 1408
MIX_CHUNK = 256
ATT_TILE = 256
VMEM_LIMIT = 56 * 1024 * 1024
NEG_BIG = -1e30
LOG2E = math.log2(math.e)
ONES_ROWS = 16

_C_M = 0
_C_Z = _C_M + 4 * MLSTM_WIDTH
_C_XBC = _C_Z + SSM_WIDTH
_C_SMALL = _C_XBC + SSM_CONV_DIM
_C_QK = _C_SMALL + LANES
_C_V = _C_QK + 4 * DIFF_HEADS * DIFF_QK_DIM
_C_END = _C_V + DIFF_WIDTH
_LANE_I = 0
_LANE_F = MLSTM_HEADS
_LANE_DT = 2 * MLSTM_HEADS


def _dot(a, b):
    return jnp.dot(a, b, preferred_element_type=F32)


def _dot_nt(a, b):
    return lax.dot_general(a, b, (((1,), (1,)), ((), ())), preferred_element_type=F32)


def _sigmoid(x):
    return 1.0 / (1.0 + jnp.exp(-x))


def _softplus(x):
    return jnp.maximum(x, 0.0) + jnp.log(1.0 + jnp.exp(-jnp.abs(x)))


def _rms_rows(x, w_row):
    ms = jnp.mean(x * x, axis=-1, keepdims=True)
    return x * lax.rsqrt(ms + NORM_EPS) * w_row


def _group_mean_sq(x, gmat):
    sq = x * x
    hi = sq.astype(BF16)
    lo = (sq - hi.astype(F32)).astype(BF16)
    return _dot(hi, gmat) + _dot(lo, gmat)


def _cumsum_rows(x):
    n = x.shape[0]
    row = lax.broadcasted_iota(jnp.int32, x.shape, 0)
    step = 1
    while step < n:
        x = x + jnp.where(row >= step, pltpu.roll(x, step, axis=0), 0.0)
        step *= 2
    return x


def _expand_heads(cols, width):
    rows = cols[0].shape[0]
    lane = lax.broadcasted_iota(jnp.int32, (rows, len(cols) * width), 1)
    out = cols[-1]
    for h in range(len(cols) - 2, -1, -1):
        out = jnp.where(lane < (h + 1) * width, cols[h], out)
    return jnp.broadcast_to(out, (rows, len(cols) * width))


def _block_diag(n, group, value):
    r = jnp.arange(n) // group
    return jnp.where(r[:, None] == r[None, :], value, 0.0).astype(BF16)


def _const_spec(shape):
    nd = len(shape)
    return pl.BlockSpec(shape, lambda *_: (0,) * nd)


def _layer_spec(shape, layer):
    nd = len(shape)
    return pl.BlockSpec((None,) + tuple(shape), lambda *_: (layer,) + (0,) * nd, pipeline_mode=pl.Buffered(1))


def _params(sem):
    return pltpu.CompilerParams(dimension_semantics=sem, vmem_limit_bytes=VMEM_LIMIT)


def _ffn_block(x, nw_ref, wg_ref, wu_ref, wd_ref):
    xn = _rms_rows(x, nw_ref[...]).astype(BF16)
    acc = None
    for c in range(D_FF // FFN_CHUNK):
        cols = slice(c * FFN_CHUNK, (c + 1) * FFN_CHUNK)
        g = _dot(xn, wg_ref[:, cols])
        u = _dot(xn, wu_ref[:, cols])
        h = (g * _sigmoid(g) * u).astype(BF16)
        part = _dot(h, wd_ref[cols, :])
        acc = part if acc is None else acc + part
    return x + 0.5 * acc


def _row_spec(width):
    return pl.BlockSpec((ROW_TILE, width), lambda i: (i, 0))


def _ffn_weight_specs(layer):
    return [_const_spec((1, D_MODEL)), _layer_spec((D_MODEL, D_FF), layer),
            _layer_spec((D_MODEL, D_FF), layer), _layer_spec((D_FF, D_MODEL), layer)]


def _store_transposed_tiles(out_ref, a):
    a_t = a.T.astype(BF16)
    for t in range(ROW_TILE // ATT_TILE):
        out_ref[t] = a_t[:, t * ATT_TILE:(t + 1) * ATT_TILE]


def _ffn_inproj_kernel(x_ref, fnw_ref, wg_ref, wu_ref, wd_ref, nw_ref, w_ref, qkw_ref, g32_ref,
                       x_out_ref, m_ref, z_ref, xbc_ref, small_ref, qt_ref, k_ref, vt_ref):
    x = _ffn_block(x_ref[...], fnw_ref, wg_ref, wu_ref, wd_ref)
    x_out_ref[...] = x
    xn = _rms_rows(x, nw_ref[...]).astype(BF16)
    m_ref[...] = _dot(xn, w_ref[:, _C_M:_C_Z]).astype(BF16)
    z_ref[...] = _dot(xn, w_ref[:, _C_Z:_C_XBC]).astype(BF16)
    xbc_ref[...] = _dot(xn, w_ref[:, _C_XBC:_C_SMALL]).astype(BF16)
    small_ref[...] = _dot(xn, w_ref[:, _C_SMALL:_C_QK])
    _store_transposed_tiles(vt_ref, _dot(xn, w_ref[:, _C_V:_C_END]))
    half = (_C_V - _C_QK) // 2
    qk = []
    for lo in (_C_QK, _C_QK + half):
        a = _dot(xn, w_ref[:, lo:lo + half])
        ms = _group_mean_sq(a, g32_ref[...])
        qk.append(a * lax.rsqrt(ms + NORM_EPS) * qkw_ref[:, lo - _C_QK:lo - _C_QK + half])
    _store_transposed_tiles(qt_ref, qk[0])
    k_ref[...] = qk[1].astype(BF16)


def _ffn_inproj(x, ffn_norm_w, wg, wu, wd, norm_w, w_all, qkw_row, g32, layer):
    n = x.shape[0]
    t_spec = pl.BlockSpec((ROW_TILE // ATT_TILE, DIFF_WIDTH, ATT_TILE), lambda i: (i, 0, 0))
    t_shape = jax.ShapeDtypeStruct((n // ATT_TILE, DIFF_WIDTH, ATT_TILE), BF16)
    widths = (D_MODEL, 4 * MLSTM_WIDTH, SSM_WIDTH, SSM_CONV_DIM, LANES)
    dtypes = (F32, BF16, BF16, BF16, F32)
    return pl.pallas_call(
        _ffn_inproj_kernel,
        grid=(n // ROW_TILE,),
        in_specs=[_row_spec(D_MODEL)] + _ffn_weight_specs(layer)
        + [_const_spec((1, D_MODEL)), _layer_spec((D_MODEL, _C_END), layer),
           _const_spec((1, 2 * DIFF_WIDTH)), _const_spec((DIFF_WIDTH, DIFF_WIDTH))],
        out_specs=[_row_spec(w) for w in widths] + [t_spec, _row_spec(DIFF_WIDTH), t_spec],
        out_shape=[jax.ShapeDtypeStruct((n, w), d) for w, d in zip(widths, dtypes)]
        + [t_shape, jax.ShapeDtypeStruct((n, DIFF_WIDTH), BF16), t_shape],
        compiler_params=_params(("parallel",)),
        name="ffn_inproj",
    )(x, ffn_norm_w.reshape(1, D_MODEL), wg, wu, wd, norm_w.reshape(1, D_MODEL), w_all, qkw_row, g32)


def _mlstm_kernel(q_ref, k_ref, v_ref, o_ref, small_ref, gbias_ref, nw_ref, g64_ref,
                  y_ref, c_scr, n_scr, m_scr):
    L = MIX_CHUNK
    H = MLSTM_HEADS
    W = MLSTM_HEAD_DIM

    @pl.when(pl.program_id(1) == 0)
    def _():
        c_scr[...] = jnp.zeros_like(c_scr)
        n_scr[...] = jnp.zeros_like(n_scr)
        m_scr[...] = jnp.zeros_like(m_scr)

    q = q_ref[...]
    ks = k_ref[...] * (MLSTM_HEAD_DIM ** -0.5)
    v = v_ref[...]
    g = small_ref[...] + gbias_ref[...]
    logf = jnp.minimum(g, 0.0) - jnp.log(1.0 + jnp.exp(-jnp.abs(g)))
    b = _cumsum_rows(logf)
    bT = b.T
    gT = g.T
    m_state = m_scr[...]
    head_of_lane = lax.broadcasted_iota(jnp.int32, (L, H * W), 1) // W
    causal = (lax.broadcasted_iota(jnp.int32, (L, L), 0) >= lax.broadcasted_iota(jnp.int32, (L, L), 1))

    ps, inter, den_intra, m_t, w_col, decay, m_new = [], [], [], [], [], [], []
    for h in range(H):
        b_col = b[:, _LANE_F + h:_LANE_F + h + 1]
        i_col = g[:, _LANE_I + h:_LANE_I + h + 1]
        b_row = bT[_LANE_F + h:_LANE_F + h + 1, :]
        i_row = gT[_LANE_I + h:_LANE_I + h + 1, :]
        m_prev = m_state[:, h:h + 1]
        log_d = jnp.where(causal, b_col - b_row + i_row, -jnp.inf)
        m_inter = b_col + m_prev
        mt = jnp.maximum(m_inter, jnp.max(log_d, axis=1, keepdims=True))
        d = jnp.exp(log_d - mt)
        qm = jnp.where(head_of_lane == h, q, jnp.zeros_like(q))
        s = _dot_nt(qm, ks) * d
        den_intra.append(jnp.sum(s, axis=1, keepdims=True))
        inter.append(jnp.exp(m_inter - mt))
        m_t.append(mt)
        ps.append(s.astype(BF16))
        b_last = b[L - 1:L, _LANE_F + h:_LANE_F + h + 1]
        log_w = b_last - b_col + i_col
        mn = jnp.maximum(b_last + m_prev, jnp.max(log_w, axis=0, keepdims=True))
        w_col.append(jnp.exp(log_w - mn))
        decay.append(jnp.exp(b_last + m_prev - mn))
        m_new.append(mn)

    vm = jnp.concatenate([jnp.where(head_of_lane == h, v, jnp.zeros_like(v)) for h in range(H)], axis=0)
    num = _dot(jnp.concatenate(ps, axis=1), vm)
    num = num + _expand_heads(inter, W) * _dot(q, c_scr[...].astype(BF16))
    qn = _dot(q, n_scr[...].astype(BF16))
    denom = []
    for h in range(H):
        den = den_intra[h] + inter[h] * qn[:, h:h + 1]
        denom.append(jnp.maximum(jnp.abs(den), jnp.exp(-m_t[h])))
    hout = num / _expand_heads(denom, W)
    hn = hout * lax.rsqrt(_group_mean_sq(hout, g64_ref[...]) + NORM_EPS) * nw_ref[...]
    y_ref[...] = (_sigmoid(o_ref[...].astype(F32)) * hn).astype(BF16)

    kw_t = (ks.astype(F32) * _expand_heads(w_col, W)).T.astype(BF16)
    row_head = lax.broadcasted_iota(jnp.int32, (H * W, H * W), 0) // W
    col_head = lax.broadcasted_iota(jnp.int32, (H * W, H * W), 1) // W
    decay_rows = decay[-1]
    for h in range(H - 2, -1, -1):
        decay_rows = jnp.where(row_head[:, :1] < h + 1, decay[h], decay_rows)
    c_scr[...] = decay_rows * c_scr[...] + jnp.where(row_head == col_head, _dot(kw_t, v), 0.0)
    ksum = _dot(kw_t, jnp.ones((L, LANES), BF16))
    n_lane = lax.broadcasted_iota(jnp.int32, (H * W, LANES), 1)
    n_scr[...] = decay_rows * n_scr[...] + jnp.where(row_head[:, :LANES] == n_lane, ksum, 0.0)
    lane = lax.broadcasted_iota(jnp.int32, (1, LANES), 1)
    m_next = m_state
    for h in range(H):
        m_next = jnp.where(lane == h, m_new[h], m_next)
    m_scr[...] = m_next


def _mlstm(m_proj, small, gate_bias_row, norm_w_row, g64, batch, seq):
    nc = seq // MIX_CHUNK
    n = batch * seq

    def col_block(j):
        return pl.BlockSpec((MIX_CHUNK, MLSTM_WIDTH), lambda b, c: (b * nc + c, j))

    rows = pl.BlockSpec((MIX_CHUNK, LANES), lambda b, c: (b * nc + c, 0))
    return pl.pallas_call(
        _mlstm_kernel,
        grid=(batch, nc),
        in_specs=[col_block(0), col_block(1), col_block(2), col_block(3), rows,
                  _const_spec((1, LANES)), _const_spec((1, MLSTM_WIDTH)),
                  _const_spec((MLSTM_WIDTH, MLSTM_WIDTH))],
        out_specs=pl.BlockSpec((MIX_CHUNK, MLSTM_WIDTH), lambda b, c: (b * nc + c, 0)),
        out_shape=jax.ShapeDtypeStruct((n, MLSTM_WIDTH), BF16),
        scratch_shapes=[pltpu.VMEM((MLSTM_WIDTH, MLSTM_WIDTH), F32),
                        pltpu.VMEM((MLSTM_WIDTH, LANES), F32),
                        pltpu.VMEM((1, LANES), F32)],
        compiler_params=_params(("arbitrary", "arbitrary")),
        name="mlstm",
    )(m_proj, m_proj, m_proj, m_proj, small, gate_bias_row, norm_w_row, g64)


def _ssd_kernel(z_ref, xbc_ref, small_ref, bias_ref, alog_ref, convw_ref, convb_ref, dskip_ref, nw_ref,
                y_ref, halo_scr, t_scr):
    L = MIX_CHUNK
    P = SSM_HEAD_DIM
    HG = SSM_HEADS // SSM_GROUPS
    GW = HG * P

    @pl.when(pl.program_id(1) == 0)
    def _():
        halo_scr[...] = jnp.zeros_like(halo_scr)
        t_scr[...] = jnp.zeros_like(t_scr)

    xr = xbc_ref[...].astype(F32)
    ext = jnp.concatenate([halo_scr[...], xr], axis=0)
    halo_scr[...] = xr[L - 8:, :]
    conv = convb_ref[...] + convw_ref[SSM_CONV - 1:SSM_CONV, :] * xr
    for j in range(1, SSM_CONV):
        conv = conv + convw_ref[SSM_CONV - 1 - j:SSM_CONV - j, :] * ext[8 - j:8 - j + L, :]
    xa = conv * _sigmoid(conv)
    xs = xa[:, :SSM_WIDTH]
    bm = xa[:, SSM_WIDTH:SSM_WIDTH + SSM_GROUPS * SSM_STATE]
    cm = xa[:, SSM_WIDTH + SSM_GROUPS * SSM_STATE:]

    dt = _softplus(small_ref[...] + bias_ref[...])
    a_cs = _cumsum_rows(dt * (-jnp.exp(alog_ref[...])))
    a_t = a_cs.T
    a_last = a_cs[L - 1:L, :]
    causal = (lax.broadcasted_iota(jnp.int32, (L, L), 0) >= lax.broadcasted_iota(jnp.int32, (L, L), 1))
    head_of_lane = lax.broadcasted_iota(jnp.int32, (L, GW), 1) // P

    for grp in range(SSM_GROUPS):
        heads = [grp * HG + i for i in range(HG)]
        lanes = [_LANE_DT + h for h in heads]
        cg = cm[:, grp * SSM_STATE:(grp + 1) * SSM_STATE].astype(BF16)
        bg = bm[:, grp * SSM_STATE:(grp + 1) * SSM_STATE]
        xs_g = xs[:, grp * GW:(grp + 1) * GW]
        xc = xs_g * _expand_heads([dt[:, ln:ln + 1] for ln in lanes], P)
        scores = _dot_nt(cg, bg.astype(BF16))
        ps = []
        for ln in lanes:
            seg = jnp.exp(jnp.where(causal, a_cs[:, ln:ln + 1] - a_t[ln:ln + 1, :], -jnp.inf))
            ps.append((scores * seg).astype(BF16))
        xm = jnp.concatenate([jnp.where(head_of_lane == i, xc, 0.0).astype(BF16) for i in range(HG)], axis=0)
        y = _dot(jnp.concatenate(ps, axis=1), xm)
        t_prev = t_scr[grp]
        y = y + _dot(cg, t_prev.astype(BF16)) * _expand_heads([jnp.exp(a_cs[:, ln:ln + 1]) for ln in lanes], P)
        d_in = _expand_heads([jnp.exp(a_last[:, ln:ln + 1] - a_cs[:, ln:ln + 1]) for ln in lanes], P)
        chunk_decay = _expand_heads([jnp.exp(a_last[:, ln:ln + 1]) for ln in lanes], P)
        t_scr[grp] = chunk_decay * t_prev + _dot(bg.T.astype(BF16), (xc * d_in).astype(BF16))
        y = y + dskip_ref[:, grp * GW:(grp + 1) * GW] * xs_g
        zg = z_ref[:, grp * GW:(grp + 1) * GW].astype(F32)
        y = y * (zg * _sigmoid(zg))
        y_ref[:, grp * GW:(grp + 1) * GW] = _rms_rows(y, nw_ref[:, grp * GW:(grp + 1) * GW]).astype(BF16)


def _ssd(z, xbc, small, bias_row, alog_row, conv_w, conv_b_row, dskip_row, norm_w_row, batch, seq):
    nc = seq // MIX_CHUNK
    n = batch * seq

    def rows(width):
        return pl.BlockSpec((MIX_CHUNK, width), lambda b, c: (b * nc + c, 0))

    return pl.pallas_call(
        _ssd_kernel,
        grid=(batch, nc),
        in_specs=[rows(SSM_WIDTH), rows(SSM_CONV_DIM), rows(LANES),
                  _const_spec((1, LANES)), _const_spec((1, LANES)),
                  _const_spec((SSM_CONV, SSM_CONV_DIM)), _const_spec((1, SSM_CONV_DIM)),
                  _const_spec((1, SSM_WIDTH)), _const_spec((1, SSM_WIDTH))],
        out_specs=rows(SSM_WIDTH),
        out_shape=jax.ShapeDtypeStruct((n, SSM_WIDTH), BF16),
        scratch_shapes=[pltpu.VMEM((8, SSM_CONV_DIM), F32),
                        pltpu.VMEM((SSM_GROUPS, SSM_STATE, SSM_WIDTH // SSM_GROUPS), F32)],
        compiler_params=_params(("arbitrary", "arbitrary")),
        name="ssd",
    )(z, xbc, small, bias_row, alog_row, conv_w, conv_b_row, dskip_row, norm_w_row)


def _t5_bias_tile(rel, relb_ref, head):
    n = jnp.maximum(rel, 0)
    max_exact = REL_BUCKETS // 2
    nf = jnp.maximum(n, 1).astype(F32)
    large = max_exact + (jnp.log(nf / max_exact) / math.log(REL_MAX_DIST / max_exact)
                         * (REL_BUCKETS - max_exact)).astype(jnp.int32)
    large = jnp.minimum(large, REL_BUCKETS - 1)
    bucket = jnp.where(n < max_exact, n, large)
    far = relb_ref[REL_BUCKETS - 1, head]
    bias = jnp.zeros(rel.shape, F32)
    for bkt in range(REL_BUCKETS - 1):
        bias = jnp.where(bucket == bkt, (relb_ref[bkt, head] - far) * LOG2E, bias)
    return jnp.where(rel >= 0, bias, NEG_BIG)


def _attn_kernel(relb_ref, qt_ref, k_ref, vt_ref, lam_ref, subw_ref, g64_ref,
                 y_ref, bias_scr, qm_scr, m_scr, acc_scr, sa_scr, sb_scr, *, lam_init):
    T = ATT_TILE
    H = DIFF_HEADS
    DV = DIFF_V_DIM
    i = pl.program_id(1)

    @pl.when((pl.program_id(0) == 0) & (i == 0))
    def _():
        rel = lax.broadcasted_iota(jnp.int32, (T, T), 1) - lax.broadcasted_iota(jnp.int32, (T, T), 0)
        for h in range(H):
            bias_scr[0, h] = _t5_bias_tile(rel, relb_ref, h)
            bias_scr[1, h] = _t5_bias_tile(rel + T, relb_ref, h)

    qt = qt_ref[0]
    comp_of_row = lax.broadcasted_iota(jnp.int32, (DIFF_WIDTH, T), 0) // DIFF_QK_DIM
    for hc in range(2 * H):
        qm_scr[hc] = jnp.where(comp_of_row == hc, qt, jnp.zeros_like(qt))
    m_scr[...] = jnp.full(m_scr.shape, NEG_BIG, F32)
    acc_scr[...] = jnp.zeros_like(acc_scr)
    ones_rows = jnp.ones((ONES_ROWS, T), BF16)

    def logits_into(s_ref, j):
        kt = k_ref[pl.ds(pl.multiple_of(j * T, T), T), :]
        for hc in range(2 * H):
            s_ref[hc] = _dot(kt, qm_scr[hc])

    def consume(s_ref, j, bias_idx):
        vt = vt_ref[j]
        for hc in range(2 * H):
            h = hc // 2
            s = s_ref[hc]
            if bias_idx is not None:
                s = bias_scr[bias_idx, h] + s
            m_prev = m_scr[hc]
            m_next = jnp.maximum(m_prev, jnp.max(s, axis=0, keepdims=True))
            alpha = jnp.exp2(m_prev - m_next)
            p = jnp.exp2(s - m_next).astype(BF16)
            m_scr[hc] = m_next
            v_aug = jnp.concatenate([vt[h * DV:(h + 1) * DV, :], ones_rows], axis=0)
            acc_scr[hc] = alpha * acc_scr[hc] + _dot(v_aug, p)

    DIAG, NEAR = 0, 1
    n_far = jnp.maximum(i - 1, 0)
    logits_into(sa_scr, 0)

    def far_pair(t, carry):
        j = 2 * t
        logits_into(sb_scr, j + 1)
        consume(sa_scr, j, None)
        logits_into(sa_scr, j + 2)
        consume(sb_scr, j + 1, None)
        return carry

    lax.fori_loop(0, n_far // 2, far_pair, 0)
    u = 2 * (n_far // 2)
    odd = n_far % 2 == 1

    @pl.when(odd)
    def _():
        logits_into(sb_scr, u + 1)
        consume(sa_scr, u, None)
        logits_into(sa_scr, u + 2)
        consume(sb_scr, u + 1, NEAR)
        consume(sa_scr, u + 2, DIAG)

    @pl.when(jnp.logical_not(odd) & (i >= 1))
    def _():
        logits_into(sb_scr, u + 1)
        consume(sa_scr, u, NEAR)
        consume(sb_scr, u + 1, DIAG)

    @pl.when(i == 0)
    def _():
        consume(sa_scr, 0, DIAG)

    lam_p = lam_ref[...]
    lam = (jnp.exp(jnp.sum(lam_p[0:1] * lam_p[1:2], axis=1, keepdims=True))
           - jnp.exp(jnp.sum(lam_p[2:3] * lam_p[3:4], axis=1, keepdims=True)) + lam_init)

    def normalised(hc):
        return acc_scr[hc, :DV, :] / acc_scr[hc, DV:DV + 1, :]

    o_t = jnp.concatenate([normalised(2 * h) - lam * normalised(2 * h + 1) for h in range(H)], axis=0)
    o = o_t.T
    ms = _group_mean_sq(o, g64_ref[...])
    y_ref[...] = (o * lax.rsqrt(ms + NORM_EPS) * subw_ref[...] * (1.0 - lam_init)).astype(BF16)


def _diff_attn(qt, kn, vt, rel_bias, lambdas, subw_row, g64, lam_init, batch, seq):
    nq = seq // ATT_TILE
    n = batch * seq
    return pl.pallas_call(
        functools.partial(_attn_kernel, lam_init=lam_init),
        grid=(batch, nq),
        in_specs=[pl.BlockSpec(memory_space=pltpu.SMEM),
                  pl.BlockSpec((1, DIFF_WIDTH, ATT_TILE), lambda b, i: (b * nq + i, 0, 0)),
                  pl.BlockSpec((seq, DIFF_WIDTH), lambda b, i: (b, 0)),
                  pl.BlockSpec((nq, DIFF_WIDTH, ATT_TILE), lambda b, i: (b, 0, 0)),
                  _const_spec((4, DIFF_QK_DIM)), _const_spec((1, DIFF_WIDTH)),
                  _const_spec((DIFF_WIDTH, DIFF_WIDTH))],
        out_specs=pl.BlockSpec((ATT_TILE, DIFF_WIDTH), lambda b, i: (b * nq + i, 0)),
        out_shape=jax.ShapeDtypeStruct((n, DIFF_WIDTH), BF16),
        scratch_shapes=[pltpu.VMEM((2, DIFF_HEADS, ATT_TILE, ATT_TILE), F32),
                        pltpu.VMEM((2 * DIFF_HEADS, DIFF_WIDTH, ATT_TILE), BF16),
                        pltpu.VMEM((2 * DIFF_HEADS, 1, ATT_TILE), F32),
                        pltpu.VMEM((2 * DIFF_HEADS, DIFF_V_DIM + ONES_ROWS, ATT_TILE), F32),
                        pltpu.VMEM((2 * DIFF_HEADS, ATT_TILE, ATT_TILE), F32),
                        pltpu.VMEM((2 * DIFF_HEADS, ATT_TILE, ATT_TILE), F32)],
        compiler_params=_params(("arbitrary", "arbitrary")),
        name="diff_attn",
    )(rel_bias, qt, kn, vt, lambdas, subw_row, g64)


def _outproj_ffn_kernel(x_ref, ym_ref, ys_ref, yd_ref, w_ref, fnw_ref, wg_ref, wu_ref, wd_ref, o_ref):
    y = jnp.concatenate([ym_ref[...], ys_ref[...], yd_ref[...]], axis=1)
    x = x_ref[...] + _dot(y, w_ref[...])
    o_ref[...] = _ffn_block(x, fnw_ref, wg_ref, wu_ref, wd_ref)


def _outproj_ffn(x, y_m, y_s, y_d, w_out, ffn_norm_w, wg, wu, wd, layer):
    n = x.shape[0]
    return pl.pallas_call(
        _outproj_ffn_kernel,
        grid=(n // ROW_TILE,),
        in_specs=[_row_spec(D_MODEL), _row_spec(MLSTM_WIDTH), _row_spec(SSM_WIDTH), _row_spec(DIFF_WIDTH),
                  _layer_spec((D_MODEL, D_MODEL), layer)] + _ffn_weight_specs(layer),
        out_specs=_row_spec(D_MODEL),
        out_shape=jax.ShapeDtypeStruct((n, D_MODEL), F32),
        compiler_params=_params(("parallel",)),
        name="outproj_ffn",
    )(x, y_m, y_s, y_d, w_out, ffn_norm_w.reshape(1, D_MODEL), wg, wu, wd)


def _regroup_w_in(w_in):
    sizes = (MLSTM_WIDTH, MLSTM_WIDTH, MLSTM_WIDTH, MLSTM_WIDTH, MLSTM_HEADS, MLSTM_HEADS,
             SSM_WIDTH, SSM_CONV_DIM, SSM_HEADS, 2 * DIFF_HEADS * DIFF_QK_DIM, 2 * DIFF_HEADS * DIFF_QK_DIM,
             DIFF_WIDTH)
    offs = [0]
    for s in sizes:
        offs.append(offs[-1] + s)
    w_in = w_in.astype(BF16)
    mq, mk, mv, mo, mi, mf, z, xbc, dt, dq, dk, dv = [w_in[..., offs[i]:offs[i + 1]] for i in range(len(sizes))]
    pad = jnp.zeros(w_in.shape[:-1] + (LANES - 2 * MLSTM_HEADS - SSM_HEADS,), BF16)
    return jnp.concatenate([mq, mk, mv, mo, z, xbc, mi, mf, dt, pad, dq, dk, dv], axis=-1)


def _small_row(values_by_lane):
    row = jnp.zeros((LANES,), F32)
    for lane, vals in values_by_lane:
        row = lax.dynamic_update_slice(row, vals.astype(F32), (lane,))
    return row.reshape(1, LANES)


def kernel(x, ffn1_norm_w, ffn1_w_gate, ffn1_w_up, ffn1_w_down, mix_norm_w, w_in, mlstm_gate_bias, mlstm_norm_w, ssm_conv_w, ssm_conv_b, ssm_dt_bias, ssm_A_log, ssm_D, ssm_norm_w, diff_q_norm_w, diff_k_norm_w, diff_lambda, diff_subln_w, rel_bias, w_out, ffn2_norm_w, ffn2_w_gate, ffn2_w_up, ffn2_w_down):
    batch, seq, d = x.shape
    assert d == D_MODEL and seq % MIX_CHUNK == 0 and seq % ATT_TILE == 0 and (batch * seq) % ROW_TILE == 0
    xf = x.reshape(batch * seq, D_MODEL)
    g32 = _block_diag(DIFF_WIDTH, DIFF_QK_DIM, 1.0 / DIFF_QK_DIM)
    g64 = _block_diag(DIFF_WIDTH, DIFF_V_DIM, 1.0 / DIFF_V_DIM)
    ffn1 = [w.astype(BF16) for w in (ffn1_w_gate, ffn1_w_up, ffn1_w_down)]
    ffn2 = [w.astype(BF16) for w in (ffn2_w_gate, ffn2_w_up, ffn2_w_down)]
    w_in_all = _regroup_w_in(w_in)
    w_out_all = w_out.astype(BF16)
    for l in range(DEPTH):
        qkw_row = jnp.concatenate([jnp.tile(diff_q_norm_w[l].reshape(-1), DIFF_HEADS) * (DIFF_QK_DIM ** -0.5 * LOG2E),
                                   jnp.tile(diff_k_norm_w[l].reshape(-1), DIFF_HEADS)]).reshape(1, 2 * DIFF_WIDTH)
        xf, m_proj, z, xbc, small, qt, kn, vt = _ffn_inproj(xf, ffn1_norm_w[l], *ffn1, mix_norm_w[l], w_in_all,
                                                            qkw_row, g32, l)
        bias_row = _small_row([(_LANE_I, mlstm_gate_bias[l, 0]), (_LANE_F, mlstm_gate_bias[l, 1]),
                               (_LANE_DT, ssm_dt_bias[l])])
        y_m = _mlstm(m_proj, small, bias_row, mlstm_norm_w[l].reshape(1, MLSTM_WIDTH), g64, batch, seq)
        y_s = _ssd(z, xbc, small, bias_row, _small_row([(_LANE_DT, ssm_A_log[l])]), ssm_conv_w[l],
                   ssm_conv_b[l].reshape(1, SSM_CONV_DIM), jnp.repeat(ssm_D[l], SSM_HEAD_DIM).reshape(1, SSM_WIDTH),
                   ssm_norm_w[l].reshape(1, SSM_WIDTH), batch, seq)
        lam_init = 0.8 - 0.6 * math.exp(-0.3 * l)
        y_d = _diff_attn(qt, kn, vt, rel_bias, diff_lambda[l], jnp.tile(diff_subln_w[l], DIFF_HEADS).reshape(1, DIFF_WIDTH),
                         g64, lam_init, batch, seq)
        xf = _outproj_ffn(xf, y_m, y_s, y_d, w_out_all, ffn2_norm_w[l], *ffn2, l)
    return xf.reshape(batch, seq, D_MODEL)
```

```python
import functools
import math

import jax
import jax.numpy as jnp
from jax import lax
from jax.experimental import pallas as pl
from jax.experimental.pallas import tpu as pltpu

F32 = jnp.float32
BF16 = jnp.bfloat16

D_MODEL = 1024
DEPTH = 2
D_FF = 2816
MLSTM_HEADS = 4
MLSTM_HEAD_DIM = 64
MLSTM_WIDTH = MLSTM_HEADS * MLSTM_HEAD_DIM
SSM_HEADS = 8
SSM_HEAD_DIM = 64
SSM_WIDTH = SSM_HEADS * SSM_HEAD_DIM
SSM_STATE = 128
SSM_GROUPS = 2
SSM_CONV = 4
SSM_CONV_DIM = SSM_WIDTH + 2 * SSM_GROUPS * SSM_STATE
DIFF_HEADS = 4
DIFF_QK_DIM = 32
DIFF_V_DIM = 64
DIFF_WIDTH = DIFF_HEADS * DIFF_V_DIM
REL_BUCKETS = 32
REL_MAX_DIST = 128
NORM_EPS = 1e-6

LANES = 128
ROW_TILE = 512
FFN_CHUNK = 1408
MIX_CHUNK = 256
ATT_TILE = 256
VMEM_LIMIT = 56 * 1024 * 1024
NEG_BIG = -1e30
LOG2E = math.log2(math.e)
ONES_ROWS = 16

_C_M = 0
_C_Z = _C_M + 4 * MLSTM_WIDTH
_C_XBC = _C_Z + SSM_WIDTH
_C_SMALL = _C_XBC + SSM_CONV_DIM
_C_QK = _C_SMALL + LANES
_C_V = _C_QK + 4 * DIFF_HEADS * DIFF_QK_DIM
_C_END = _C_V + DIFF_WIDTH
_LANE_I = 0
_LANE_F = 8
_LANE_DT = 16
GATE_ROWS = 24


def _dot(a, b):
    return jnp.dot(a, b, preferred_element_type=F32)


def _dot_nt(a, b):
    return lax.dot_general(a, b, (((1,), (1,)), ((), ())), preferred_element_type=F32)


def _sigmoid(x):
    return 1.0 / (1.0 + jnp.exp(-x))


def _softplus(x):
    return jnp.maximum(x, 0.0) + jnp.log(1.0 + jnp.exp(-jnp.abs(x)))


def _rms_rows(x, w_row):
    ms = jnp.mean(x * x, axis=-1, keepdims=True)
    return x * lax.rsqrt(ms + NORM_EPS) * w_row


def _group_mean_sq(x, gmat):
    sq = x * x
    hi = sq.astype(BF16)
    lo = (sq - hi.astype(F32)).astype(BF16)
    return _dot(hi, gmat) + _dot(lo, gmat)


def _cumsum_rows(x):
    n = x.shape[0]
    row = lax.broadcasted_iota(jnp.int32, x.shape, 0)
    step = 1
    while step < n:
        x = x + jnp.where(row >= step, pltpu.roll(x, step, axis=0), 0.0)
        step *= 2
    return x


def _scan_lanes(x, op, identity):
    n = x.shape[1]
    lane = lax.broadcasted_iota(jnp.int32, x.shape, 1)
    step = 1
    while step < n:
        x = op(x, jnp.where(lane >= step, pltpu.roll(x, step, axis=1), identity))
        step *= 2
    return x


def _expand_heads(cols, width):
    rows = cols[0].shape[0]
    lane = lax.broadcasted_iota(jnp.int32, (rows, len(cols) * width), 1)
    out = cols[-1]
    for h in range(len(cols) - 2, -1, -1):
        out = jnp.where(lane < (h + 1) * width, cols[h], out)
    return jnp.broadcast_to(out, (rows, len(cols) * width))


def _block_diag(n, group, value):
    r = jnp.arange(n) // group
    return jnp.where(r[:, None] == r[None, :], value, 0.0).astype(BF16)


def _const_spec(shape):
    nd = len(shape)
    return pl.BlockSpec(shape, lambda *_: (0,) * nd)


def _layer_spec(shape, layer):
    nd = len(shape)
    return pl.BlockSpec((None,) + tuple(shape), lambda *_: (layer,) + (0,) * nd, pipeline_mode=pl.Buffered(1))


def _params(sem):
    return pltpu.CompilerParams(dimension_semantics=sem, vmem_limit_bytes=VMEM_LIMIT)


def _ffn_block(x, nw_ref, wg_ref, wu_ref, wd_ref):
    xn = _rms_rows(x, nw_ref[...]).astype(BF16)
    acc = None
    for c in range(D_FF // FFN_CHUNK):
        cols = slice(c * FFN_CHUNK, (c + 1) * FFN_CHUNK)
        g = _dot(xn, wg_ref[:, cols])
        u = _dot(xn, wu_ref[:, cols])
        h = (g * _sigmoid(g) * u).astype(BF16)
        part = _dot(h, wd_ref[cols, :])
        acc = part if acc is None else acc + part
    return x + 0.5 * acc


def _row_spec(width):
    return pl.BlockSpec((ROW_TILE, width), lambda i: (i, 0))


def _ffn_weight_specs(layer):
    return [_const_spec((1, D_MODEL)), _layer_spec((D_MODEL, D_FF), layer),
            _layer_spec((D_MODEL, D_FF), layer), _layer_spec((D_FF, D_MODEL), layer)]


def _store_transposed_tiles(out_ref, a):
    a_t = a.T.astype(BF16)
    for t in range(ROW_TILE // ATT_TILE):
        out_ref[t] = a_t[:, t * ATT_TILE:(t + 1) * ATT_TILE]


def _ffn_inproj_kernel(x_ref, fnw_ref, wg_ref, wu_ref, wd_ref, nw_ref, w_ref, qkw_ref, g32_ref, gbias_ref,
                       x_out_ref, mqt_ref, mk_ref, mvt_ref, mot_ref, gt_ref, z_ref, xbc_ref, small_ref,
                       qt_ref, k_ref, vt_ref):
    x = _ffn_block(x_ref[...], fnw_ref, wg_ref, wu_ref, wd_ref)
    x_out_ref[...] = x
    xn = _rms_rows(x, nw_ref[...]).astype(BF16)
    _store_transposed_tiles(mqt_ref, _dot(xn, w_ref[:, _C_M:_C_M + MLSTM_WIDTH]))
    mk_ref[...] = _dot(xn, w_ref[:, _C_M + MLSTM_WIDTH:_C_M + 2 * MLSTM_WIDTH]).astype(BF16)
    _store_transposed_tiles(mvt_ref, _dot(xn, w_ref[:, _C_M + 2 * MLSTM_WIDTH:_C_M + 3 * MLSTM_WIDTH]))
    _store_transposed_tiles(mot_ref, _dot(xn, w_ref[:, _C_M + 3 * MLSTM_WIDTH:_C_Z]))
    z_ref[...] = _dot(xn, w_ref[:, _C_Z:_C_XBC]).astype(BF16)
    xbc_ref[...] = _dot(xn, w_ref[:, _C_XBC:_C_SMALL]).astype(BF16)
    small = _dot(xn, w_ref[:, _C_SMALL:_C_QK])
    small_ref[...] = small
    gates = small.T[:_LANE_DT] + gbias_ref[...]
    for t in range(ROW_TILE // MIX_CHUNK):
        i_pre = gates[_LANE_I:_LANE_I + 8, t * MIX_CHUNK:(t + 1) * MIX_CHUNK]
        f_pre = gates[_LANE_F:_LANE_F + 8, t * MIX_CHUNK:(t + 1) * MIX_CHUNK]
        log_f = jnp.minimum(f_pre, 0.0) - jnp.log(1.0 + jnp.exp(-jnp.abs(f_pre)))
        b = _scan_lanes(log_f, jnp.add, 0.0)
        c = i_pre - b
        gt_ref[t] = jnp.concatenate([c, _scan_lanes(c, jnp.maximum, -jnp.inf), b], axis=0)
    _store_transposed_tiles(vt_ref, _dot(xn, w_ref[:, _C_V:_C_END]))
    half = (_C_V - _C_QK) // 2
    qk = []
    for lo in (_C_QK, _C_QK + half):
        a = _dot(xn, w_ref[:, lo:lo + half])
        ms = _group_mean_sq(a, g32_ref[...])
        qk.append(a * lax.rsqrt(ms + NORM_EPS) * qkw_ref[:, lo - _C_QK:lo - _C_QK + half])
    _store_transposed_tiles(qt_ref, qk[0])
    k_ref[...] = qk[1].astype(BF16)


def _ffn_inproj(x, ffn_norm_w, wg, wu, wd, norm_w, w_all, qkw_row, g32, gate_bias_rows, layer):
    n = x.shape[0]
    tiles = ROW_TILE // ATT_TILE
    t_spec = pl.BlockSpec((tiles, DIFF_WIDTH, ATT_TILE), lambda i: (i, 0, 0))
    t_shape = jax.ShapeDtypeStruct((n // ATT_TILE, DIFF_WIDTH, ATT_TILE), BF16)
    g_spec = pl.BlockSpec((tiles, GATE_ROWS, ATT_TILE), lambda i: (i, 0, 0))
    g_shape = jax.ShapeDtypeStruct((n // ATT_TILE, GATE_ROWS, ATT_TILE), F32)
    widths = (SSM_WIDTH, SSM_CONV_DIM, LANES)
    dtypes = (BF16, BF16, F32)
    return pl.pallas_call(
        _ffn_inproj_kernel,
        grid=(n // ROW_TILE,),
        in_specs=[_row_spec(D_MODEL)] + _ffn_weight_specs(layer)
        + [_const_spec((1, D_MODEL)), _layer_spec((D_MODEL, _C_END), layer),
           _const_spec((1, 2 * DIFF_WIDTH)), _const_spec((DIFF_WIDTH, DIFF_WIDTH)),
           _const_spec((_LANE_DT, ROW_TILE))],
        out_specs=[_row_spec(D_MODEL), t_spec, _row_spec(MLSTM_WIDTH), t_spec, t_spec, g_spec]
        + [_row_spec(w) for w in widths] + [t_spec, _row_spec(DIFF_WIDTH), t_spec],
        out_shape=[jax.ShapeDtypeStruct((n, D_MODEL), F32), t_shape, jax.ShapeDtypeStruct((n, MLSTM_WIDTH), BF16),
                   t_shape, t_shape, g_shape]
        + [jax.ShapeDtypeStruct((n, w), d) for w, d in zip(widths, dtypes)]
        + [t_shape, jax.ShapeDtypeStruct((n, DIFF_WIDTH), BF16), t_shape],
        compiler_params=_params(("parallel",)),
        name="ffn_inproj",
    )(x, ffn_norm_w.reshape(1, D_MODEL), wg, wu, wd, norm_w.reshape(1, D_MODEL), w_all, qkw_row, g32,
      gate_bias_rows)


def _mlstm_kernel(qt_ref, k_ref, vt_ref, ot_ref, gt_ref, nw_ref, yt_ref, ct_scr, nt_scr, m_scr):
    L = MIX_CHUNK
    H = MLSTM_HEADS
    W = MLSTM_HEAD_DIM

    @pl.when(pl.program_id(1) == 0)
    def _():
        ct_scr[...] = jnp.zeros_like(ct_scr)
        nt_scr[...] = jnp.zeros_like(nt_scr)
        m_scr[...] = jnp.zeros_like(m_scr)

    qt = qt_ref[0]
    ks = k_ref[...] * (MLSTM_HEAD_DIM ** -0.5)
    vt = vt_ref[0]
    gt = gt_ref[0]
    c, c_max, b = gt[0:8], gt[8:16], gt[16:24]
    m_prev = m_scr[...]
    big_m = jnp.maximum(m_prev, c_max)
    inter = jnp.exp(m_prev - big_m)
    floor = jnp.exp(-(b + big_m))
    b_last = jnp.broadcast_to(b[:, L - 1:L], b.shape)
    m_new = b_last + jnp.maximum(m_prev, jnp.broadcast_to(c_max[:, L - 1:L], b.shape))
    w = jnp.exp(b_last + c - m_new)
    decay = jnp.exp(b_last + m_prev - m_new)

    c2_cols = jnp.concatenate([c * LOG2E, jnp.zeros((LANES - 8, L), F32)], axis=0).T
    big_m2 = big_m * LOG2E
    causal = (lax.broadcasted_iota(jnp.int32, (L, L), 0) <= lax.broadcasted_iota(jnp.int32, (L, L), 1))
    row_head = lax.broadcasted_iota(jnp.int32, (H * W, L), 0) // W
    ones_rows = jnp.ones((ONES_ROWS, L), BF16)
    qk = [_dot(ks, jnp.where(row_head == h, qt, jnp.zeros_like(qt))) for h in range(H)]
    nds = []
    for h in range(H):
        d = jnp.exp2(jnp.where(causal, c2_cols[:, h:h + 1] - big_m2[h:h + 1, :], -jnp.inf))
        p = (qk[h] * d).astype(BF16)
        nds.append(_dot(jnp.concatenate([vt[h * W:(h + 1) * W, :], ones_rows], axis=0), p))
    cq = _dot(ct_scr[...].astype(BF16), qt)
    nq = _dot(nt_scr[...].astype(BF16), qt)
    outs = []
    for h in range(H):
        num = nds[h][:W] + inter[h:h + 1] * cq[h * W:(h + 1) * W]
        den = nds[h][W:W + 1] + inter[h:h + 1] * nq[h:h + 1]
        hh = num / jnp.maximum(jnp.abs(den), floor[h:h + 1])
        outs.append(hh * lax.rsqrt(jnp.mean(hh * hh, axis=0, keepdims=True) + NORM_EPS))
    hn = jnp.concatenate(outs, axis=0) * nw_ref[...]
    yt_ref[0] = (_sigmoid(ot_ref[0].astype(F32)) * hn).astype(BF16)

    w_full = jnp.concatenate([jnp.broadcast_to(w[h:h + 1], (W, L)) for h in range(H)], axis=0)
    g_new = _dot((vt.astype(F32) * w_full).astype(BF16), ks)
    decay_full = jnp.concatenate([jnp.broadcast_to(decay[h:h + 1, :1], (W, H * W)) for h in range(H)], axis=0)
    same_head = (lax.broadcasted_iota(jnp.int32, (H * W, H * W), 0) // W
                 == lax.broadcasted_iota(jnp.int32, (H * W, H * W), 1) // W)
    ct_scr[...] = decay_full * ct_scr[...] + jnp.where(same_head, g_new, 0.0)
    w16 = jnp.concatenate([w, jnp.zeros_like(w)], axis=0).astype(BF16)
    n_new = _dot(w16, ks)
    own_lanes = (lax.broadcasted_iota(jnp.int32, (16, H * W), 0)
                 == lax.broadcasted_iota(jnp.int32, (16, H * W), 1) // W)
    decay16 = jnp.broadcast_to(jnp.concatenate([decay[:, :1], jnp.zeros((8, 1), F32)], axis=0), (16, H * W))
    nt_scr[...] = decay16 * nt_scr[...] + jnp.where(own_lanes, n_new, 0.0)
    m_scr[...] = m_new


def _mlstm(mqt, mk, mvt, mot, gt, norm_w_rows, batch, seq):
    nc = seq // MIX_CHUNK
    n = batch * seq
    t_spec = pl.BlockSpec((1, MLSTM_WIDTH, MIX_CHUNK), lambda b, c: (b * nc + c, 0, 0))
    return pl.pallas_call(
        _mlstm_kernel,
        grid=(batch, nc),
        in_specs=[t_spec, pl.BlockSpec((MIX_CHUNK, MLSTM_WIDTH), lambda b, c: (b * nc + c, 0)), t_spec, t_spec,
                  pl.BlockSpec((1, GATE_ROWS, MIX_CHUNK), lambda b, c: (b * nc + c, 0, 0)),
                  _const_spec((MLSTM_WIDTH, MIX_CHUNK))],
        out_specs=t_spec,
        out_shape=jax.ShapeDtypeStruct((n // MIX_CHUNK, MLSTM_WIDTH, MIX_CHUNK), BF16),
        scratch_shapes=[pltpu.VMEM((MLSTM_WIDTH, MLSTM_WIDTH), F32),
                        pltpu.VMEM((16, MLSTM_WIDTH), F32),
                        pltpu.VMEM((8, MIX_CHUNK), F32)],
        compiler_params=_params(("arbitrary", "arbitrary")),
        name="mlstm",
    )(mqt, mk, mvt, mot, gt, norm_w_rows)


def _ssd_kernel(z_ref, xbc_ref, small_ref, bias_ref, alog_ref, convw_ref, convb_ref, dskip_ref, nw_ref,
                y_ref, halo_scr, t_scr):
    L = MIX_CHUNK
    P = SSM_HEAD_DIM
    HG = SSM_HEADS // SSM_GROUPS
    GW = HG * P

    @pl.when(pl.program_id(1) == 0)
    def _():
        halo_scr[...] = jnp.zeros_like(halo_scr)
        t_scr[...] = jnp.zeros_like(t_scr)

    xr = xbc_ref[...].astype(F32)
    ext = jnp.concatenate([halo_scr[...], xr], axis=0)
    halo_scr[...] = xr[L - 8:, :]
    conv = convb_ref[...] + convw_ref[SSM_CONV - 1:SSM_CONV, :] * xr
    for j in range(1, SSM_CONV):
        conv = conv + convw_ref[SSM_CONV - 1 - j:SSM_CONV - j, :] * ext[8 - j:8 - j + L, :]
    xa = conv * _sigmoid(conv)
    xs = xa[:, :SSM_WIDTH]
    bm = xa[:, SSM_WIDTH:SSM_WIDTH + SSM_GROUPS * SSM_STATE]
    cm = xa[:, SSM_WIDTH + SSM_GROUPS * SSM_STATE:]

    dt = _softplus(small_ref[...] + bias_ref[...])
    a_cs = _cumsum_rows(dt * (-jnp.exp(alog_ref[...])))
    a_t = a_cs.T
    a_last = a_cs[L - 1:L, :]
    causal = (lax.broadcasted_iota(jnp.int32, (L, L), 0) >= lax.broadcasted_iota(jnp.int32, (L, L), 1))
    head_of_lane = lax.broadcasted_iota(jnp.int32, (L, GW), 1) // P

    for grp in range(SSM_GROUPS):
        heads = [grp * HG + i for i in range(HG)]
        lanes = [_LANE_DT + h for h in heads]
        cg = cm[:, grp * SSM_STATE:(grp + 1) * SSM_STATE].astype(BF16)
        bg = bm[:, grp * SSM_STATE:(grp + 1) * SSM_STATE]
        xs_g = xs[:, grp * GW:(grp + 1) * GW]
        xc = xs_g * _expand_heads([dt[:, ln:ln + 1] for ln in lanes], P)
        scores = _dot_nt(cg, bg.astype(BF16))
        ps = []
        for ln in lanes:
            seg = jnp.exp(jnp.where(causal, a_cs[:, ln:ln + 1] - a_t[ln:ln + 1, :], -jnp.inf))
            ps.append((scores * seg).astype(BF16))
        xm = jnp.concatenate([jnp.where(head_of_lane == i, xc, 0.0).astype(BF16) for i in range(HG)], axis=0)
        y = _dot(jnp.concatenate(ps, axis=1), xm)
        t_prev = t_scr[grp]
        y = y + _dot(cg, t_prev.astype(BF16)) * _expand_heads([jnp.exp(a_cs[:, ln:ln + 1]) for ln in lanes], P)
        d_in = _expand_heads([jnp.exp(a_last[:, ln:ln + 1] - a_cs[:, ln:ln + 1]) for ln in lanes], P)
        chunk_decay = _expand_heads([jnp.exp(a_last[:, ln:ln + 1]) for ln in lanes], P)
        t_scr[grp] = chunk_decay * t_prev + _dot(bg.T.astype(BF16), (xc * d_in).astype(BF16))
        y = y + dskip_ref[:, grp * GW:(grp + 1) * GW] * xs_g
        zg = z_ref[:, grp * GW:(grp + 1) * GW].astype(F32)
        y = y * (zg * _sigmoid(zg))
        y_ref[:, grp * GW:(grp + 1) * GW] = _rms_rows(y, nw_ref[:, grp * GW:(grp + 1) * GW]).astype(BF16)


def _ssd(z, xbc, small, bias_row, alog_row, conv_w, conv_b_row, dskip_row, norm_w_row, batch, seq):
    nc = seq // MIX_CHUNK
    n = batch * seq

    def rows(width):
        return pl.BlockSpec((MIX_CHUNK, width), lambda b, c: (b * nc + c, 0))

    return pl.pallas_call(
        _ssd_kernel,
        grid=(batch, nc),
        in_specs=[rows(SSM_WIDTH), rows(SSM_CONV_DIM), rows(LANES),
                  _const_spec((1, LANES)), _const_spec((1, LANES)),
                  _const_spec((SSM_CONV, SSM_CONV_DIM)), _const_spec((1, SSM_CONV_DIM)),
                  _const_spec((1, SSM_WIDTH)), _const_spec((1, SSM_WIDTH))],
        out_specs=rows(SSM_WIDTH),
        out_shape=jax.ShapeDtypeStruct((n, SSM_WIDTH), BF16),
        scratch_shapes=[pltpu.VMEM((8, SSM_CONV_DIM), F32),
                        pltpu.VMEM((SSM_GROUPS, SSM_STATE, SSM_WIDTH // SSM_GROUPS), F32)],
        compiler_params=_params(("arbitrary", "arbitrary")),
        name="ssd",
    )(z, xbc, small, bias_row, alog_row, conv_w, conv_b_row, dskip_row, norm_w_row)


def _t5_bias_tile(rel, relb_ref, head):
    n = jnp.maximum(rel, 0)
    max_exact = REL_BUCKETS // 2
    nf = jnp.maximum(n, 1).astype(F32)
    large = max_exact + (jnp.log(nf / max_exact) / math.log(REL_MAX_DIST / max_exact)
                         * (REL_BUCKETS - max_exact)).astype(jnp.int32)
    large = jnp.minimum(large, REL_BUCKETS - 1)
    bucket = jnp.where(n < max_exact, n, large)
    far = relb_ref[REL_BUCKETS - 1, head]
    bias = jnp.zeros(rel.shape, F32)
    for bkt in range(REL_BUCKETS - 1):
        bias = jnp.where(bucket == bkt, (relb_ref[bkt, head] - far) * LOG2E, bias)
    return jnp.where(rel >= 0, bias, NEG_BIG)


def _attn_kernel(relb_ref, qt_ref, k_ref, vt_ref, lam_ref, subw_ref, g64_ref,
                 y_ref, bias_scr, qm_scr, m_scr, acc_scr, sa_scr, sb_scr, *, lam_init):
    T = ATT_TILE
    H = DIFF_HEADS
    DV = DIFF_V_DIM
    i = pl.program_id(1)

    @pl.when((pl.program_id(0) == 0) & (i == 0))
    def _():
        rel = lax.broadcasted_iota(jnp.int32, (T, T), 1) - lax.broadcasted_iota(jnp.int32, (T, T), 0)
        for h in range(H):
            bias_scr[0, h] = _t5_bias_tile(rel, relb_ref, h)
            bias_scr[1, h] = _t5_bias_tile(rel + T, relb_ref, h)

    qt = qt_ref[0]
    comp_of_row = lax.broadcasted_iota(jnp.int32, (DIFF_WIDTH, T), 0) // DIFF_QK_DIM
    for hc in range(2 * H):
        qm_scr[hc] = jnp.where(comp_of_row == hc, qt, jnp.zeros_like(qt))
    m_scr[...] = jnp.full(m_scr.shape, NEG_BIG, F32)
    acc_scr[...] = jnp.zeros_like(acc_scr)
    ones_rows = jnp.ones((ONES_ROWS, T), BF16)

    def logits_into(s_ref, j):
        kt = k_ref[pl.ds(pl.multiple_of(j * T, T), T), :]
        for hc in range(2 * H):
            s_ref[hc] = _dot(kt, qm_scr[hc])

    def consume(s_ref, j, bias_idx):
        vt = vt_ref[j]
        for hc in range(2 * H):
            h = hc // 2
            s = s_ref[hc]
            if bias_idx is not None:
                s = bias_scr[bias_idx, h] + s
            m_prev = m_scr[hc]
            m_next = jnp.maximum(m_prev, jnp.max(s, axis=0, keepdims=True))
            alpha = jnp.exp2(m_prev - m_next)
            p = jnp.exp2(s - m_next).astype(BF16)
            m_scr[hc] = m_next
            v_aug = jnp.concatenate([vt[h * DV:(h + 1) * DV, :], ones_rows], axis=0)
            acc_scr[hc] = alpha * acc_scr[hc] + _dot(v_aug, p)

    DIAG, NEAR = 0, 1
    n_far = jnp.maximum(i - 1, 0)
    logits_into(sa_scr, 0)

    def far_pair(t, carry):
        j = 2 * t
        logits_into(sb_scr, j + 1)
        consume(sa_scr, j, None)
        logits_into(sa_scr, j + 2)
        consume(sb_scr, j + 1, None)
        return carry

    lax.fori_loop(0, n_far // 2, far_pair, 0)
    u = 2 * (n_far // 2)
    odd = n_far % 2 == 1

    @pl.when(odd)
    def _():
        logits_into(sb_scr, u + 1)
        consume(sa_scr, u, None)
        logits_into(sa_scr, u + 2)
        consume(sb_scr, u + 1, NEAR)
        consume(sa_scr, u + 2, DIAG)

    @pl.when(jnp.logical_not(odd) & (i >= 1))
    def _():
        logits_into(sb_scr, u + 1)
        consume(sa_scr, u, NEAR)
        consume(sb_scr, u + 1, DIAG)

    @pl.when(i == 0)
    def _():
        consume(sa_scr, 0, DIAG)

    lam_p = lam_ref[...]
    lam = (jnp.exp(jnp.sum(lam_p[0:1] * lam_p[1:2], axis=1, keepdims=True))
           - jnp.exp(jnp.sum(lam_p[2:3] * lam_p[3:4], axis=1, keepdims=True)) + lam_init)

    def normalised(hc):
        return acc_scr[hc, :DV, :] / acc_scr[hc, DV:DV + 1, :]

    o_t = jnp.concatenate([normalised(2 * h) - lam * normalised(2 * h + 1) for h in range(H)], axis=0)
    o = o_t.T
    ms = _group_mean_sq(o, g64_ref[...])
    y_ref[...] = (o * lax.rsqrt(ms + NORM_EPS) * subw_ref[...] * (1.0 - lam_init)).astype(BF16)


def _diff_attn(qt, kn, vt, rel_bias, lambdas, subw_row, g64, lam_init, batch, seq):
    nq = seq // ATT_TILE
    n = batch * seq
    return pl.pallas_call(
        functools.partial(_attn_kernel, lam_init=lam_init),
        grid=(batch, nq),
        in_specs=[pl.BlockSpec(memory_space=pltpu.SMEM),
                  pl.BlockSpec((1, DIFF_WIDTH, ATT_TILE), lambda b, i: (b * nq + i, 0, 0)),
                  pl.BlockSpec((seq, DIFF_WIDTH), lambda b, i: (b, 0)),
                  pl.BlockSpec((nq, DIFF_WIDTH, ATT_TILE), lambda b, i: (b, 0, 0)),
                  _const_spec((4, DIFF_QK_DIM)), _const_spec((1, DIFF_WIDTH)),
                  _const_spec((DIFF_WIDTH, DIFF_WIDTH))],
        out_specs=pl.BlockSpec((ATT_TILE, DIFF_WIDTH), lambda b, i: (b * nq + i, 0)),
        out_shape=jax.ShapeDtypeStruct((n, DIFF_WIDTH), BF16),
        scratch_shapes=[pltpu.VMEM((2, DIFF_HEADS, ATT_TILE, ATT_TILE), F32),
                        pltpu.VMEM((2 * DIFF_HEADS, DIFF_WIDTH, ATT_TILE), BF16),
                        pltpu.VMEM((2 * DIFF_HEADS, 1, ATT_TILE), F32),
                        pltpu.VMEM((2 * DIFF_HEADS, DIFF_V_DIM + ONES_ROWS, ATT_TILE), F32),
                        pltpu.VMEM((2 * DIFF_HEADS, ATT_TILE, ATT_TILE), F32),
                        pltpu.VMEM((2 * DIFF_HEADS, ATT_TILE, ATT_TILE), F32)],
        compiler_params=_params(("arbitrary", "arbitrary")),
        name="diff_attn",
    )(rel_bias, qt, kn, vt, lambdas, subw_row, g64)


def _outproj_ffn_kernel(x_ref, ymt_ref, ys_ref, yd_ref, w_ref, fnw_ref, wg_ref, wu_ref, wd_ref, o_ref):
    y_m = jnp.concatenate([ymt_ref[t].astype(F32).T for t in range(ROW_TILE // MIX_CHUNK)], axis=0).astype(BF16)
    y = jnp.concatenate([y_m, ys_ref[...], yd_ref[...]], axis=1)
    x = x_ref[...] + _dot(y, w_ref[...])
    o_ref[...] = _ffn_block(x, fnw_ref, wg_ref, wu_ref, wd_ref)


def _outproj_ffn(x, y_m, y_s, y_d, w_out, ffn_norm_w, wg, wu, wd, layer):
    n = x.shape[0]
    return pl.pallas_call(
        _outproj_ffn_kernel,
        grid=(n // ROW_TILE,),
        in_specs=[_row_spec(D_MODEL),
                  pl.BlockSpec((ROW_TILE // MIX_CHUNK, MLSTM_WIDTH, MIX_CHUNK), lambda i: (i, 0, 0)),
                  _row_spec(SSM_WIDTH), _row_spec(DIFF_WIDTH),
                  _layer_spec((D_MODEL, D_MODEL), layer)] + _ffn_weight_specs(layer),
        out_specs=_row_spec(D_MODEL),
        out_shape=jax.ShapeDtypeStruct((n, D_MODEL), F32),
        compiler_params=_params(("parallel",)),
        name="outproj_ffn",
    )(x, y_m, y_s, y_d, w_out, ffn_norm_w.reshape(1, D_MODEL), wg, wu, wd)


def _regroup_w_in(w_in):
    sizes = (MLSTM_WIDTH, MLSTM_WIDTH, MLSTM_WIDTH, MLSTM_WIDTH, MLSTM_HEADS, MLSTM_HEADS,
             SSM_WIDTH, SSM_CONV_DIM, SSM_HEADS, 2 * DIFF_HEADS * DIFF_QK_DIM, 2 * DIFF_HEADS * DIFF_QK_DIM,
             DIFF_WIDTH)
    offs = [0]
    for s in sizes:
        offs.append(offs[-1] + s)
    w_in = w_in.astype(BF16)
    mq, mk, mv, mo, mi, mf, z, xbc, dt, dq, dk, dv = [w_in[..., offs[i]:offs[i + 1]] for i in range(len(sizes))]
    def pad(width):
        return jnp.zeros(w_in.shape[:-1] + (width,), BF16)

    small = [mi, pad(_LANE_F - _LANE_I - MLSTM_HEADS), mf, pad(_LANE_DT - _LANE_F - MLSTM_HEADS), dt,
             pad(LANES - _LANE_DT - SSM_HEADS)]
    return jnp.concatenate([mq, mk, mv, mo, z, xbc] + small + [dq, dk, dv], axis=-1)


def _small_row(values_by_lane):
    row = jnp.zeros((LANES,), F32)
    for lane, vals in values_by_lane:
        row = lax.dynamic_update_slice(row, vals.astype(F32), (lane,))
    return row.reshape(1, LANES)


def kernel(x, ffn1_norm_w, ffn1_w_gate, ffn1_w_up, ffn1_w_down, mix_norm_w, w_in, mlstm_gate_bias, mlstm_norm_w, ssm_conv_w, ssm_conv_b, ssm_dt_bias, ssm_A_log, ssm_D, ssm_norm_w, diff_q_norm_w, diff_k_norm_w, diff_lambda, diff_subln_w, rel_bias, w_out, ffn2_norm_w, ffn2_w_gate, ffn2_w_up, ffn2_w_down):
    batch, seq, d = x.shape
    assert d == D_MODEL and seq % MIX_CHUNK == 0 and seq % ATT_TILE == 0 and (batch * seq) % ROW_TILE == 0
    xf = x.reshape(batch * seq, D_MODEL)
    g32 = _block_diag(DIFF_WIDTH, DIFF_QK_DIM, 1.0 / DIFF_QK_DIM)
    g64 = _block_diag(DIFF_WIDTH, DIFF_V_DIM, 1.0 / DIFF_V_DIM)
    ffn1 = [w.astype(BF16) for w in (ffn1_w_gate, ffn1_w_up, ffn1_w_down)]
    ffn2 = [w.astype(BF16) for w in (ffn2_w_gate, ffn2_w_up, ffn2_w_down)]
    w_in_all = _regroup_w_in(w_in)
    w_out_all = w_out.astype(BF16)
    for l in range(DEPTH):
        qkw_row = jnp.concatenate([jnp.tile(diff_q_norm_w[l].reshape(-1), DIFF_HEADS) * (DIFF_QK_DIM ** -0.5 * LOG2E),
                                   jnp.tile(diff_k_norm_w[l].reshape(-1), DIFF_HEADS)]).reshape(1, 2 * DIFF_WIDTH)
        bias_row = _small_row([(_LANE_I, mlstm_gate_bias[l, 0]), (_LANE_F, mlstm_gate_bias[l, 1]),
                               (_LANE_DT, ssm_dt_bias[l])])
        gate_bias_rows = jnp.broadcast_to(bias_row[0, :_LANE_DT, None], (_LANE_DT, ROW_TILE))
        xf, mqt, mk, mvt, mot, gt, z, xbc, small, qt, kn, vt = _ffn_inproj(
            xf, ffn1_norm_w[l], *ffn1, mix_norm_w[l], w_in_all, qkw_row, g32, gate_bias_rows, l)
        y_m = _mlstm(mqt, mk, mvt, mot, gt, jnp.broadcast_to(mlstm_norm_w[l][:, None], (MLSTM_WIDTH, MIX_CHUNK)),
                     batch, seq)
        y_s = _ssd(z, xbc, small, bias_row, _small_row([(_LANE_DT, ssm_A_log[l])]), ssm_conv_w[l],
                   ssm_conv_b[l].reshape(1, SSM_CONV_DIM), jnp.repeat(ssm_D[l], SSM_HEAD_DIM).reshape(1, SSM_WIDTH),
                   ssm_norm_w[l].reshape(1, SSM_WIDTH), batch, seq)
        lam_init = 0.8 - 0.6 * math.exp(-0.3 * l)
        y_d = _diff_attn(qt, kn, vt, rel_bias, diff_lambda[l], jnp.tile(diff_subln_w[l], DIFF_HEADS).reshape(1, DIFF_WIDTH),
                         g64, lam_init, batch, seq)
        xf = _outproj_ffn(xf, y_m, y_s, y_d, w_out_all, ffn2_norm_w[l], *ffn2, l)
    return xf.reshape(batch, seq, D_MODEL)
```

```python
import functools
import math

import jax
import jax.numpy as jnp
from jax import lax
from jax.experimental import pallas as pl
from jax.experimental.pallas import tpu as pltpu

F32 = jnp.float32
BF16 = jnp.bfloat16

D_MODEL = 1024
DEPTH = 2
D_FF = 2816
MLSTM_HEADS = 4
MLSTM_HEAD_DIM = 64
MLSTM_WIDTH = MLSTM_HEADS * MLSTM_HEAD_DIM
SSM_HEADS = 8
SSM_HEAD_DIM = 64
SSM_WIDTH = SSM_HEADS * SSM_HEAD_DIM
SSM_STATE = 128
SSM_GROUPS = 2
SSM_CONV = 4
SSM_CONV_DIM = SSM_WIDTH + 2 * SSM_GROUPS * SSM_STATE
DIFF_HEADS = 4
DIFF_QK_DIM = 32
DIFF_V_DIM = 64
DIFF_WIDTH = DIFF_HEADS * DIFF_V_DIM
REL_BUCKETS = 32
REL_MAX_DIST = 128
NORM_EPS = 1e-6

LANES = 128
ROW_TILE = 512
FFN_CHUNK = 1408
MIX_CHUNK = 256
ATT_TILE = 256
VMEM_LIMIT = 56 * 1024 * 1024
NEG_BIG = -1e30
LOG2E = math.log2(math.e)
ONES_ROWS = 16

_C_M = 0
_C_Z = _C_M + 4 * MLSTM_WIDTH
_C_XBC = _C_Z + SSM_WIDTH
_C_SMALL = _C_XBC + SSM_CONV_DIM
_C_QK = _C_SMALL + LANES
_C_V = _C_QK + 4 * DIFF_HEADS * DIFF_QK_DIM
_C_END = _C_V + DIFF_WIDTH
_LANE_I = 0
_LANE_F = 8
_LANE_DT = 16
GATE_ROWS = 24


def _dot(a, b):
    return jnp.dot(a, b, preferred_element_type=F32)


def _sigmoid(x):
    return 1.0 / (1.0 + jnp.exp(-x))


def _softplus(x):
    return jnp.maximum(x, 0.0) + jnp.log(1.0 + jnp.exp(-jnp.abs(x)))


def _rms_rows(x, w_row):
    ms = jnp.mean(x * x, axis=-1, keepdims=True)
    return x * lax.rsqrt(ms + NORM_EPS) * w_row


def _group_mean_sq(x, gmat):
    sq = x * x
    hi = sq.astype(BF16)
    lo = (sq - hi.astype(F32)).astype(BF16)
    return _dot(hi, gmat) + _dot(lo, gmat)


def _scan_lanes(x, op, identity):
    n = x.shape[1]
    lane = lax.broadcasted_iota(jnp.int32, x.shape, 1)
    step = 1
    while step < n:
        x = op(x, jnp.where(lane >= step, pltpu.roll(x, step, axis=1), identity))
        step *= 2
    return x


def _block_diag(n, group, value):
    r = jnp.arange(n) // group
    return jnp.where(r[:, None] == r[None, :], value, 0.0).astype(BF16)


def _const_spec(shape):
    nd = len(shape)
    return pl.BlockSpec(shape, lambda *_: (0,) * nd)


def _layer_spec(shape, layer):
    nd = len(shape)
    return pl.BlockSpec((None,) + tuple(shape), lambda *_: (layer,) + (0,) * nd, pipeline_mode=pl.Buffered(1))


def _params(sem):
    return pltpu.CompilerParams(dimension_semantics=sem, vmem_limit_bytes=VMEM_LIMIT)


def _ffn_block(x, nw_ref, wg_ref, wu_ref, wd_ref):
    xn = _rms_rows(x, nw_ref[...]).astype(BF16)
    acc = None
    for c in range(D_FF // FFN_CHUNK):
        cols = slice(c * FFN_CHUNK, (c + 1) * FFN_CHUNK)
        g = _dot(xn, wg_ref[:, cols])
        u = _dot(xn, wu_ref[:, cols])
        h = (g * _sigmoid(g) * u).astype(BF16)
        part = _dot(h, wd_ref[cols, :])
        acc = part if acc is None else acc + part
    return x + 0.5 * acc


def _row_spec(width):
    return pl.BlockSpec((ROW_TILE, width), lambda i: (i, 0))


def _ffn_weight_specs(layer):
    return [_const_spec((1, D_MODEL)), _layer_spec((D_MODEL, D_FF), layer),
            _layer_spec((D_MODEL, D_FF), layer), _layer_spec((D_FF, D_MODEL), layer)]


def _store_transposed_tiles(out_ref, a):
    a_t = a.T.astype(BF16)
    for t in range(ROW_TILE // ATT_TILE):
        out_ref[t] = a_t[:, t * ATT_TILE:(t + 1) * ATT_TILE]


def _ffn_inproj_kernel(x_ref, fnw_ref, wg_ref, wu_ref, wd_ref, nw_ref, w_ref, qkw_ref, g32_ref, gbias_ref,
                       alog_ref, convw_ref, convb_ref,
                       x_out_ref, mqt_ref, mk_ref, mvt_ref, mot_ref, gt_ref,
                       zt_ref, sb_ref, sct_ref, sxt_ref, sxdt_ref, sa_ref, qt_ref, k_ref, vt_ref,
                       halo_scr, *, tiles_per_seq):
    @pl.when(pl.program_id(0) % tiles_per_seq == 0)
    def _():
        halo_scr[...] = jnp.zeros_like(halo_scr)

    x = _ffn_block(x_ref[...], fnw_ref, wg_ref, wu_ref, wd_ref)
    x_out_ref[...] = x
    xn = _rms_rows(x, nw_ref[...]).astype(BF16)

    def proj(lo, hi):
        return _dot(xn, w_ref[:, lo:hi])

    half = (_C_V - _C_QK) // 2
    p_q, p_k = proj(_C_QK, _C_QK + half), proj(_C_QK + half, _C_V)
    p_xbc = proj(_C_XBC, _C_SMALL)
    p_small = proj(_C_SMALL, _C_QK)
    p_mq = proj(_C_M, _C_M + MLSTM_WIDTH)
    p_mk = proj(_C_M + MLSTM_WIDTH, _C_M + 2 * MLSTM_WIDTH)
    p_mv = proj(_C_M + 2 * MLSTM_WIDTH, _C_M + 3 * MLSTM_WIDTH)
    p_mo = proj(_C_M + 3 * MLSTM_WIDTH, _C_Z)
    p_z = proj(_C_Z, _C_XBC)
    p_v = proj(_C_V, _C_END)

    qk = [a * lax.rsqrt(_group_mean_sq(a, g32_ref[...]) + NORM_EPS) * qkw_ref[:, i * half:(i + 1) * half]
          for i, a in enumerate((p_q, p_k))]
    _store_transposed_tiles(qt_ref, qk[0])
    k_ref[...] = qk[1].astype(BF16)
    _store_transposed_tiles(vt_ref, p_v)

    ext =jnp.concatenate([halo_scr[...], p_xbc], axis=0)
    halo_scr[...] = p_xbc[ROW_TILE - 8:, :]
    conv = convb_ref[...] + convw_ref[SSM_CONV - 1:SSM_CONV, :] * p_xbc
    for j in range(1, SSM_CONV):
        conv = conv + convw_ref[SSM_CONV - 1 - j:SSM_CONV - j, :] * ext[8 - j:8 - j + ROW_TILE, :]
    xa = conv * _sigmoid(conv)
    sb_ref[...] = xa[:, SSM_WIDTH:SSM_WIDTH + SSM_GROUPS * SSM_STATE].astype(BF16)
    _store_transposed_tiles(sct_ref, xa[:, SSM_WIDTH + SSM_GROUPS * SSM_STATE:])
    small_t = p_small.T + gbias_ref[...]
    dt = _softplus(small_t[_LANE_DT:_LANE_DT + SSM_HEADS])
    log_decay = dt * (-jnp.exp(alog_ref[...]))
    xs_t = xa[:, :SSM_WIDTH].T
    xdt_t = xs_t * jnp.concatenate([jnp.broadcast_to(dt[h:h + 1], (SSM_HEAD_DIM, ROW_TILE))
                                    for h in range(SSM_HEADS)], axis=0)
    for t in range(ROW_TILE // MIX_CHUNK):
        cols = slice(t * MIX_CHUNK, (t + 1) * MIX_CHUNK)
        sxt_ref[t] = xs_t[:, cols].astype(BF16)
        sxdt_ref[t] = xdt_t[:, cols].astype(BF16)
        sa_ref[t] = _scan_lanes(log_decay[:, cols], jnp.add, 0.0)
    _store_transposed_tiles(zt_ref, p_z)

    _store_transposed_tiles(mqt_ref, p_mq)
    mk_ref[...] = p_mk.astype(BF16)
    _store_transposed_tiles(mvt_ref, p_mv)
    _store_transposed_tiles(mot_ref, p_mo)
    gates = small_t[:_LANE_DT]
    for t in range(ROW_TILE // MIX_CHUNK):
        i_pre = gates[_LANE_I:_LANE_I + 8, t * MIX_CHUNK:(t + 1) * MIX_CHUNK]
        f_pre = gates[_LANE_F:_LANE_F + 8, t * MIX_CHUNK:(t + 1) * MIX_CHUNK]
        log_f = jnp.minimum(f_pre, 0.0) - jnp.log(1.0 + jnp.exp(-jnp.abs(f_pre)))
        b = _scan_lanes(log_f, jnp.add, 0.0)
        c = i_pre - b
        gt_ref[t] = jnp.concatenate([c, _scan_lanes(c, jnp.maximum, -jnp.inf), b], axis=0)


def _ffn_inproj(x, ffn_norm_w, wg, wu, wd, norm_w, w_all, qkw_row, g32, gate_bias_rows, alog_rows, conv_w, conv_b_row,
                layer, seq):
    n = x.shape[0]
    tiles = ROW_TILE // ATT_TILE

    def t_out(width, dtype=BF16):
        return (pl.BlockSpec((tiles, width, ATT_TILE), lambda i: (i, 0, 0)),
                jax.ShapeDtypeStruct((n // ATT_TILE, width, ATT_TILE), dtype))

    def r_out(width, dtype=BF16):
        return _row_spec(width), jax.ShapeDtypeStruct((n, width), dtype)

    outs = [r_out(D_MODEL, F32),
            t_out(MLSTM_WIDTH), r_out(MLSTM_WIDTH), t_out(MLSTM_WIDTH), t_out(MLSTM_WIDTH), t_out(GATE_ROWS, F32),
            t_out(SSM_WIDTH), r_out(SSM_GROUPS * SSM_STATE), t_out(SSM_GROUPS * SSM_STATE), t_out(SSM_WIDTH),
            t_out(SSM_WIDTH), t_out(SSM_HEADS, F32),
            t_out(DIFF_WIDTH), r_out(DIFF_WIDTH), t_out(DIFF_WIDTH)]
    return pl.pallas_call(
        functools.partial(_ffn_inproj_kernel, tiles_per_seq=seq // ROW_TILE),
        grid=(n // ROW_TILE,),
        in_specs=[_row_spec(D_MODEL)] + _ffn_weight_specs(layer)
        + [_const_spec((1, D_MODEL)), _layer_spec((D_MODEL, _C_END), layer),
           _const_spec((1, 2 * DIFF_WIDTH)), _const_spec((DIFF_WIDTH, DIFF_WIDTH)),
           _const_spec((LANES, ROW_TILE)), _const_spec((SSM_HEADS, ROW_TILE)),
           _const_spec((SSM_CONV, SSM_CONV_DIM)), _const_spec((1, SSM_CONV_DIM))],
        out_specs=[o[0] for o in outs],
        out_shape=[o[1] for o in outs],
        scratch_shapes=[pltpu.VMEM((8, SSM_CONV_DIM), F32)],
        compiler_params=_params(("arbitrary",)),
        name="ffn_inproj",
    )(x, ffn_norm_w.reshape(1, D_MODEL), wg, wu, wd, norm_w.reshape(1, D_MODEL), w_all, qkw_row, g32,
      gate_bias_rows, alog_rows, conv_w, conv_b_row)


def _mlstm_kernel(qt_ref, k_ref, vt_ref, ot_ref, gt_ref, nw_ref, yt_ref, ct_scr, nt_scr, m_scr):
    L = MIX_CHUNK
    H = MLSTM_HEADS
    W = MLSTM_HEAD_DIM

    @pl.when(pl.program_id(1) == 0)
    def _():
        ct_scr[...] = jnp.zeros_like(ct_scr)
        nt_scr[...] = jnp.zeros_like(nt_scr)
        m_scr[...] = jnp.zeros_like(m_scr)

    qt = qt_ref[0]
    ks = k_ref[...] * (MLSTM_HEAD_DIM ** -0.5)
    vt = vt_ref[0]
    gt = gt_ref[0]
    c, c_max, b = gt[0:8], gt[8:16], gt[16:24]
    m_prev = m_scr[...]
    big_m = jnp.maximum(m_prev, c_max)
    inter = jnp.exp(m_prev - big_m)
    floor = jnp.exp(-(b + big_m))
    b_last = jnp.broadcast_to(b[:, L - 1:L], b.shape)
    m_new = b_last + jnp.maximum(m_prev, jnp.broadcast_to(c_max[:, L - 1:L], b.shape))
    w = jnp.exp(b_last + c - m_new)
    decay = jnp.exp(b_last + m_prev - m_new)

    c2_cols = jnp.concatenate([c * LOG2E, jnp.zeros((LANES - 8, L), F32)], axis=0).T
    big_m2 = big_m * LOG2E
    causal = (lax.broadcasted_iota(jnp.int32, (L, L), 0) <= lax.broadcasted_iota(jnp.int32, (L, L), 1))
    row_head = lax.broadcasted_iota(jnp.int32, (H * W, L), 0) // W
    ones_rows = jnp.ones((ONES_ROWS, L), BF16)
    qk = [_dot(ks, jnp.where(row_head == h, qt, jnp.zeros_like(qt))) for h in range(H)]
    nds = []
    for h in range(H):
        d = jnp.exp2(jnp.where(causal, c2_cols[:, h:h + 1] - big_m2[h:h + 1, :], -jnp.inf))
        p = (qk[h] * d).astype(BF16)
        nds.append(_dot(jnp.concatenate([vt[h * W:(h + 1) * W, :], ones_rows], axis=0), p))
    cq = _dot(ct_scr[...].astype(BF16), qt)
    nq = _dot(nt_scr[...].astype(BF16), qt)
    outs = []
    for h in range(H):
        num = nds[h][:W] + inter[h:h + 1] * cq[h * W:(h + 1) * W]
        den = nds[h][W:W + 1] + inter[h:h + 1] * nq[h:h + 1]
        hh = num / jnp.maximum(jnp.abs(den), floor[h:h + 1])
        outs.append(hh * lax.rsqrt(jnp.mean(hh * hh, axis=0, keepdims=True) + NORM_EPS))
    hn = jnp.concatenate(outs, axis=0) * nw_ref[...]
    yt_ref[0] = (_sigmoid(ot_ref[0].astype(F32)) * hn).astype(BF16)

    w_full = jnp.concatenate([jnp.broadcast_to(w[h:h + 1], (W, L)) for h in range(H)], axis=0)
    g_new = _dot((vt.astype(F32) * w_full).astype(BF16), ks)
    decay_full = jnp.concatenate([jnp.broadcast_to(decay[h:h + 1, :1], (W, H * W)) for h in range(H)], axis=0)
    same_head = (lax.broadcasted_iota(jnp.int32, (H * W, H * W), 0) // W
                 == lax.broadcasted_iota(jnp.int32, (H * W, H * W), 1) // W)
    ct_scr[...] = decay_full * ct_scr[...] + jnp.where(same_head, g_new, 0.0)
    w16 = jnp.concatenate([w, jnp.zeros_like(w)], axis=0).astype(BF16)
    n_new = _dot(w16, ks)
    own_lanes = (lax.broadcasted_iota(jnp.int32, (16, H * W), 0)
                 == lax.broadcasted_iota(jnp.int32, (16, H * W), 1) // W)
    decay16 = jnp.broadcast_to(jnp.concatenate([decay[:, :1], jnp.zeros((8, 1), F32)], axis=0), (16, H * W))
    nt_scr[...] = decay16 * nt_scr[...] + jnp.where(own_lanes, n_new, 0.0)
    m_scr[...] = m_new


def _mlstm(mqt, mk, mvt, mot, gt, norm_w_rows, batch, seq):
    nc = seq // MIX_CHUNK
    n = batch * seq
    t_spec = pl.BlockSpec((1, MLSTM_WIDTH, MIX_CHUNK), lambda b, c: (b * nc + c, 0, 0))
    return pl.pallas_call(
        _mlstm_kernel,
        grid=(batch, nc),
        in_specs=[t_spec, pl.BlockSpec((MIX_CHUNK, MLSTM_WIDTH), lambda b, c: (b * nc + c, 0)), t_spec, t_spec,
                  pl.BlockSpec((1, GATE_ROWS, MIX_CHUNK), lambda b, c: (b * nc + c, 0, 0)),
                  _const_spec((MLSTM_WIDTH, MIX_CHUNK))],
        out_specs=t_spec,
        out_shape=jax.ShapeDtypeStruct((n // MIX_CHUNK, MLSTM_WIDTH, MIX_CHUNK), BF16),
        scratch_shapes=[pltpu.VMEM((MLSTM_WIDTH, MLSTM_WIDTH), F32),
                        pltpu.VMEM((16, MLSTM_WIDTH), F32),
                        pltpu.VMEM((8, MIX_CHUNK), F32)],
        compiler_params=_params(("arbitrary", "arbitrary")),
        name="mlstm",
    )(mqt, mk, mvt, mot, gt, norm_w_rows)


def _ssd_kernel(b_ref, ct_ref, xt_ref, xdt_ref, zt_ref, a_ref, dskip_ref, nw_ref, yt_ref, st_scr):
    L = MIX_CHUNK
    P = SSM_HEAD_DIM
    HG = SSM_HEADS // SSM_GROUPS
    GW = HG * P
    NS = SSM_STATE

    @pl.when(pl.program_id(1) == 0)
    def _():
        st_scr[...] = jnp.zeros_like(st_scr)

    a = a_ref[0]
    a_last = jnp.broadcast_to(a[:, L - 1:L], a.shape)
    exp_a = jnp.exp(a)
    d_in = jnp.exp(a_last - a)
    chunk_decay = jnp.exp(a_last)
    a2 = a * LOG2E
    a2_cols = jnp.concatenate([a2, jnp.zeros((LANES - SSM_HEADS, L), F32)], axis=0).T
    causal = (lax.broadcasted_iota(jnp.int32, (L, L), 0) <= lax.broadcasted_iota(jnp.int32, (L, L), 1))
    bn = b_ref[...]
    ct = ct_ref[0]
    xdt = xdt_ref[0]

    def per_head_rows(rows, grp, width):
        return jnp.concatenate([jnp.broadcast_to(rows[grp * HG + i:grp * HG + i + 1, :width], (P, width))
                                for i in range(HG)], axis=0)

    scores, y_off = [], []
    for grp in range(SSM_GROUPS):
        bg = bn[:, grp * NS:(grp + 1) * NS]
        cg = ct[grp * NS:(grp + 1) * NS, :]
        scores.append(_dot(bg, cg))
        st = st_scr[grp]
        y_off.append(_dot(st.astype(BF16), cg))
        x_in = (xdt[grp * GW:(grp + 1) * GW, :].astype(F32) * per_head_rows(d_in, grp, L)).astype(BF16)
        st_scr[grp] = per_head_rows(chunk_decay, grp, NS) * st + _dot(x_in, bg)
    for grp in range(SSM_GROUPS):
        y_diag = []
        for i in range(HG):
            h = grp * HG + i
            seg = jnp.exp2(jnp.where(causal, a2[h:h + 1, :] - a2_cols[:, h:h + 1], -jnp.inf))
            p = (scores[grp] * seg).astype(BF16)
            y_diag.append(_dot(xdt[h * P:(h + 1) * P, :], p))
        rows = slice(grp * GW, (grp + 1) * GW)
        y = (jnp.concatenate(y_diag, axis=0) + y_off[grp] * per_head_rows(exp_a, grp, L)
             + dskip_ref[rows, :] * xt_ref[0, rows, :].astype(F32))
        zg = zt_ref[0, rows, :].astype(F32)
        y = y * (zg * _sigmoid(zg))
        ms = jnp.mean(y * y, axis=0, keepdims=True)
        yt_ref[0, rows, :] = (y * lax.rsqrt(ms + NORM_EPS) * nw_ref[rows, :]).astype(BF16)


def _ssd(sb, sct, sxt, sxdt, zt, sa, dskip_rows, norm_w_rows, batch, seq):
    nc = seq // MIX_CHUNK
    n = batch * seq

    def t_spec(width):
        return pl.BlockSpec((1, width, MIX_CHUNK), lambda b, c: (b * nc + c, 0, 0))

    return pl.pallas_call(
        _ssd_kernel,
        grid=(batch, nc),
        in_specs=[pl.BlockSpec((MIX_CHUNK, SSM_GROUPS * SSM_STATE), lambda b, c: (b * nc + c, 0)),
                  t_spec(SSM_GROUPS * SSM_STATE), t_spec(SSM_WIDTH), t_spec(SSM_WIDTH), t_spec(SSM_WIDTH),
                  t_spec(SSM_HEADS), _const_spec((SSM_WIDTH, MIX_CHUNK)), _const_spec((SSM_WIDTH, MIX_CHUNK))],
        out_specs=t_spec(SSM_WIDTH),
        out_shape=jax.ShapeDtypeStruct((n // MIX_CHUNK, SSM_WIDTH, MIX_CHUNK), BF16),
        scratch_shapes=[pltpu.VMEM((SSM_GROUPS, SSM_WIDTH // SSM_GROUPS, SSM_STATE), F32)],
        compiler_params=_params(("arbitrary", "arbitrary")),
        name="ssd",
    )(sb, sct, sxt, sxdt, zt, sa, dskip_rows, norm_w_rows)


def _t5_bias_tile(rel, relb_ref, head):
    n = jnp.maximum(rel, 0)
    max_exact = REL_BUCKETS // 2
    nf = jnp.maximum(n, 1).astype(F32)
    large = max_exact + (jnp.log(nf / max_exact) / math.log(REL_MAX_DIST / max_exact)
                         * (REL_BUCKETS - max_exact)).astype(jnp.int32)
    large = jnp.minimum(large, REL_BUCKETS - 1)
    bucket = jnp.where(n < max_exact, n, large)
    far = relb_ref[REL_BUCKETS - 1, head]
    bias = jnp.zeros(rel.shape, F32)
    for bkt in range(REL_BUCKETS - 1):
        bias = jnp.where(bucket == bkt, (relb_ref[bkt, head] - far) * LOG2E, bias)
    return jnp.where(rel >= 0, bias, NEG_BIG)


def _attn_kernel(relb_ref, qt_ref, k_ref, vt_ref, lam_ref, subw_ref, g64_ref,
                 y_ref, bias_scr, qm_scr, m_scr, acc_scr, sa_scr, sb_scr, *, lam_init):
    T = ATT_TILE
    H = DIFF_HEADS
    DV = DIFF_V_DIM
    i = pl.program_id(1)

    @pl.when((pl.program_id(0) == 0) & (i == 0))
    def _():
        rel = lax.broadcasted_iota(jnp.int32, (T, T), 1) - lax.broadcasted_iota(jnp.int32, (T, T), 0)
        for h in range(H):
            bias_scr[0, h] = _t5_bias_tile(rel, relb_ref, h)
            bias_scr[1, h] = _t5_bias_tile(rel + T, relb_ref, h)

    qt = qt_ref[0]
    comp_of_row = lax.broadcasted_iota(jnp.int32, (DIFF_WIDTH, T), 0) // DIFF_QK_DIM
    for hc in range(2 * H):
        qm_scr[hc] = jnp.where(comp_of_row == hc, qt, jnp.zeros_like(qt))
    m_scr[...] = jnp.full(m_scr.shape, NEG_BIG, F32)
    acc_scr[...] = jnp.zeros_like(acc_scr)
    ones_rows = jnp.ones((ONES_ROWS, T), BF16)

    def logits_into(s_ref, j):
        kt = k_ref[pl.ds(pl.multiple_of(j * T, T), T), :]
        for hc in range(2 * H):
            s_ref[hc] = _dot(kt, qm_scr[hc])

    def consume(s_ref, j, bias_idx):
        vt = vt_ref[j]
        for hc in range(2 * H):
            h = hc // 2
            s = s_ref[hc]
            if bias_idx is not None:
                s = bias_scr[bias_idx, h] + s
            m_prev = m_scr[hc]
            m_next = jnp.maximum(m_prev, jnp.max(s, axis=0, keepdims=True))
            alpha = jnp.exp2(m_prev - m_next)
            p = jnp.exp2(s - m_next).astype(BF16)
            m_scr[hc] = m_next
            v_aug = jnp.concatenate([vt[h * DV:(h + 1) * DV, :], ones_rows], axis=0)
            acc_scr[hc] = alpha * acc_scr[hc] + _dot(v_aug, p)

    DIAG, NEAR = 0, 1
    n_far = jnp.maximum(i - 1, 0)
    logits_into(sa_scr, 0)

    def far_pair(t, carry):
        j = 2 * t
        logits_into(sb_scr, j + 1)
        consume(sa_scr, j, None)
        logits_into(sa_scr, j + 2)
        consume(sb_scr, j + 1, None)
        return carry

    lax.fori_loop(0, n_far // 2, far_pair, 0)
    u = 2 * (n_far // 2)
    odd = n_far % 2 == 1

    @pl.when(odd)
    def _():
        logits_into(sb_scr, u + 1)
        consume(sa_scr, u, None)
        logits_into(sa_scr, u + 2)
        consume(sb_scr, u + 1, NEAR)
        consume(sa_scr, u + 2, DIAG)

    @pl.when(jnp.logical_not(odd) & (i >= 1))
    def _():
        logits_into(sb_scr, u + 1)
        consume(sa_scr, u, NEAR)
        consume(sb_scr, u + 1, DIAG)

    @pl.when(i == 0)
    def _():
        consume(sa_scr, 0, DIAG)

    lam_p = lam_ref[...]
    lam = (jnp.exp(jnp.sum(lam_p[0:1] * lam_p[1:2], axis=1, keepdims=True))
           - jnp.exp(jnp.sum(lam_p[2:3] * lam_p[3:4], axis=1, keepdims=True)) + lam_init)

    def normalised(hc):
        return acc_scr[hc, :DV, :] / acc_scr[hc, DV:DV + 1, :]

    o_t = jnp.concatenate([normalised(2 * h) - lam * normalised(2 * h + 1) for h in range(H)], axis=0)
    o = o_t.T
    ms = _group_mean_sq(o, g64_ref[...])
    y_ref[...] = (o * lax.rsqrt(ms + NORM_EPS) * subw_ref[...] * (1.0 - lam_init)).astype(BF16)


def _diff_attn(qt, kn, vt, rel_bias, lambdas, subw_row, g64, lam_init, batch, seq):
    nq = seq // ATT_TILE
    n = batch * seq
    return pl.pallas_call(
        functools.partial(_attn_kernel, lam_init=lam_init),
        grid=(batch, nq),
        in_specs=[pl.BlockSpec(memory_space=pltpu.SMEM),
                  pl.BlockSpec((1, DIFF_WIDTH, ATT_TILE), lambda b, i: (b * nq + i, 0, 0)),
                  pl.BlockSpec((seq, DIFF_WIDTH), lambda b, i: (b, 0)),
                  pl.BlockSpec((nq, DIFF_WIDTH, ATT_TILE), lambda b, i: (b, 0, 0)),
                  _const_spec((4, DIFF_QK_DIM)), _const_spec((1, DIFF_WIDTH)),
                  _const_spec((DIFF_WIDTH, DIFF_WIDTH))],
        out_specs=pl.BlockSpec((ATT_TILE, DIFF_WIDTH), lambda b, i: (b * nq + i, 0)),
        out_shape=jax.ShapeDtypeStruct((n, DIFF_WIDTH), BF16),
        scratch_shapes=[pltpu.VMEM((2, DIFF_HEADS, ATT_TILE, ATT_TILE), F32),
                        pltpu.VMEM((2 * DIFF_HEADS, DIFF_WIDTH, ATT_TILE), BF16),
                        pltpu.VMEM((2 * DIFF_HEADS, 1, ATT_TILE), F32),
                        pltpu.VMEM((2 * DIFF_HEADS, DIFF_V_DIM + ONES_ROWS, ATT_TILE), F32),
                        pltpu.VMEM((2 * DIFF_HEADS, ATT_TILE, ATT_TILE), F32),
                        pltpu.VMEM((2 * DIFF_HEADS, ATT_TILE, ATT_TILE), F32)],
        compiler_params=_params(("arbitrary", "arbitrary")),
        name="diff_attn",
    )(rel_bias, qt, kn, vt, lambdas, subw_row, g64)


def _untransposed(yt_ref):
    return jnp.concatenate([yt_ref[t].astype(F32).T for t in range(yt_ref.shape[0])], axis=0).astype(BF16)


def _outproj_ffn_kernel(x_ref, ymt_ref, yst_ref, yd_ref, w_ref, fnw_ref, wg_ref, wu_ref, wd_ref, o_ref):
    y = jnp.concatenate([_untransposed(ymt_ref), _untransposed(yst_ref), yd_ref[...]], axis=1)
    x = x_ref[...] + _dot(y, w_ref[...])
    o_ref[...] = _ffn_block(x, fnw_ref, wg_ref, wu_ref, wd_ref)


def _outproj_ffn(x, y_m, y_s, y_d, w_out, ffn_norm_w, wg, wu, wd, layer):
    n = x.shape[0]
    return pl.pallas_call(
        _outproj_ffn_kernel,
        grid=(n // ROW_TILE,),
        in_specs=[_row_spec(D_MODEL),
                  pl.BlockSpec((ROW_TILE // MIX_CHUNK, MLSTM_WIDTH, MIX_CHUNK), lambda i: (i, 0, 0)),
                  pl.BlockSpec((ROW_TILE // MIX_CHUNK, SSM_WIDTH, MIX_CHUNK), lambda i: (i, 0, 0)),
                  _row_spec(DIFF_WIDTH),
                  _layer_spec((D_MODEL, D_MODEL), layer)] + _ffn_weight_specs(layer),
        out_specs=_row_spec(D_MODEL),
        out_shape=jax.ShapeDtypeStruct((n, D_MODEL), F32),
        compiler_params=_params(("parallel",)),
        name="outproj_ffn",
    )(x, y_m, y_s, y_d, w_out, ffn_norm_w.reshape(1, D_MODEL), wg, wu, wd)


def _regroup_w_in(w_in):
    sizes = (MLSTM_WIDTH, MLSTM_WIDTH, MLSTM_WIDTH, MLSTM_WIDTH, MLSTM_HEADS, MLSTM_HEADS,
             SSM_WIDTH, SSM_CONV_DIM, SSM_HEADS, 2 * DIFF_HEADS * DIFF_QK_DIM, 2 * DIFF_HEADS * DIFF_QK_DIM,
             DIFF_WIDTH)
    offs = [0]
    for s in sizes:
        offs.append(offs[-1] + s)
    w_in = w_in.astype(BF16)
    mq, mk, mv, mo, mi, mf, z, xbc, dt, dq, dk, dv = [w_in[..., offs[i]:offs[i + 1]] for i in range(len(sizes))]

    def pad(width):
        return jnp.zeros(w_in.shape[:-1] + (width,), BF16)

    small = [mi, pad(_LANE_F - _LANE_I - MLSTM_HEADS), mf, pad(_LANE_DT - _LANE_F - MLSTM_HEADS), dt,
             pad(LANES - _LANE_DT - SSM_HEADS)]
    return jnp.concatenate([mq, mk, mv, mo, z, xbc] + small + [dq, dk, dv], axis=-1)


def _small_row(values_by_lane):
    row = jnp.zeros((LANES,), F32)
    for lane, vals in values_by_lane:
        row = lax.dynamic_update_slice(row, vals.astype(F32), (lane,))
    return row.reshape(1, LANES)


def kernel(x, ffn1_norm_w, ffn1_w_gate, ffn1_w_up, ffn1_w_down, mix_norm_w, w_in, mlstm_gate_bias, mlstm_norm_w, ssm_conv_w, ssm_conv_b, ssm_dt_bias, ssm_A_log, ssm_D, ssm_norm_w, diff_q_norm_w, diff_k_norm_w, diff_lambda, diff_subln_w, rel_bias, w_out, ffn2_norm_w, ffn2_w_gate, ffn2_w_up, ffn2_w_down):
    batch, seq, d = x.shape
    assert d == D_MODEL and MIX_CHUNK == ATT_TILE and seq % ROW_TILE == 0 and ROW_TILE % MIX_CHUNK == 0
    xf = x.reshape(batch * seq, D_MODEL)
    g32 = _block_diag(DIFF_WIDTH, DIFF_QK_DIM, 1.0 / DIFF_QK_DIM)
    g64 = _block_diag(DIFF_WIDTH, DIFF_V_DIM, 1.0 / DIFF_V_DIM)
    ffn1 = [w.astype(BF16) for w in (ffn1_w_gate, ffn1_w_up, ffn1_w_down)]
    ffn2 = [w.astype(BF16) for w in (ffn2_w_gate, ffn2_w_up, ffn2_w_down)]
    w_in_all = _regroup_w_in(w_in)
    w_out_all = w_out.astype(BF16)
    for l in range(DEPTH):
        qkw_row = jnp.concatenate([jnp.tile(diff_q_norm_w[l].reshape(-1), DIFF_HEADS) * (DIFF_QK_DIM ** -0.5 * LOG2E),
                                   jnp.tile(diff_k_norm_w[l].reshape(-1), DIFF_HEADS)]).reshape(1, 2 * DIFF_WIDTH)
        bias_row = _small_row([(_LANE_I, mlstm_gate_bias[l, 0]), (_LANE_F, mlstm_gate_bias[l, 1]),
                               (_LANE_DT, ssm_dt_bias[l])])
        (xf, mqt, mk, mvt, mot, gt, zt, sb, sct, sxt, sxdt, sa, qt, kn, vt) = _ffn_inproj(
            xf, ffn1_norm_w[l], *ffn1, mix_norm_w[l], w_in_all, qkw_row, g32,
            jnp.broadcast_to(bias_row.reshape(LANES, 1), (LANES, ROW_TILE)),
            jnp.broadcast_to(ssm_A_log[l][:, None], (SSM_HEADS, ROW_TILE)),
            ssm_conv_w[l], ssm_conv_b[l].reshape(1, SSM_CONV_DIM), l, seq)
        y_m = _mlstm(mqt, mk, mvt, mot, gt, jnp.broadcast_to(mlstm_norm_w[l][:, None], (MLSTM_WIDTH, MIX_CHUNK)),
                     batch, seq)
        y_s = _ssd(sb, sct, sxt, sxdt, zt, sa,
                   jnp.broadcast_to(jnp.repeat(ssm_D[l], SSM_HEAD_DIM)[:, None], (SSM_WIDTH, MIX_CHUNK)),
                   jnp.broadcast_to(ssm_norm_w[l][:, None], (SSM_WIDTH, MIX_CHUNK)), batch, seq)
        lam_init = 0.8 - 0.6 * math.exp(-0.3 * l)
        y_d = _diff_attn(qt, kn, vt, rel_bias, diff_lambda[l], jnp.tile(diff_subln_w[l], DIFF_HEADS).reshape(1, DIFF_WIDTH),
                         g64, lam_init, batch, seq)
        xf = _outproj_ffn(xf, y_m, y_s, y_d, w_out_all, ffn2_norm_w[l], *ffn2, l)
    return xf.reshape(batch, seq, D_MODEL)
```

```python
import functools
import math

import jax
import jax.numpy as jnp
from jax import lax
from jax.experimental import pallas as pl
from jax.experimental.pallas import tpu as pltpu

F32 = jnp.float32
BF16 = jnp.bfloat16

D_MODEL = 1024
DEPTH = 2
D_FF = 2816
MLSTM_HEADS = 4
MLSTM_HEAD_DIM = 64
MLSTM_WIDTH = MLSTM_HEADS * MLSTM_HEAD_DIM
SSM_HEADS = 8
SSM_HEAD_DIM = 64
SSM_WIDTH = SSM_HEADS * SSM_HEAD_DIM
SSM_STATE = 128
SSM_GROUPS = 2
SSM_CONV = 4
SSM_CONV_DIM = SSM_WIDTH + 2 * SSM_GROUPS * SSM_STATE
DIFF_HEADS = 4
DIFF_QK_DIM = 32
DIFF_V_DIM = 64
DIFF_WIDTH = DIFF_HEADS * DIFF_V_DIM
REL_BUCKETS = 32
REL_MAX_DIST = 128
NORM_EPS = 1e-6

LANES = 128
ROW_TILE = 512
FFN_CHUNK = 1408
MIX_CHUNK = 256
ATT_TILE = 256
VMEM_LIMIT = 56 * 1024 * 1024
NEG_BIG = -1e30
LOG2E = math.log2(math.e)
ONES_ROWS = 16

_C_M = 0
_C_Z = _C_M + 4 * MLSTM_WIDTH
_C_XBC = _C_Z + SSM_WIDTH
_C_SMALL = _C_XBC + SSM_CONV_DIM
_C_QK = _C_SMALL + LANES
_C_V = _C_QK + 4 * DIFF_HEADS * DIFF_QK_DIM
_C_END = _C_V + DIFF_WIDTH
_LANE_I = 0
_LANE_F = 8
_LANE_DT = 16
GATE_ROWS = 24


def _dot(a, b):
    return jnp.dot(a, b, preferred_element_type=F32)


def _sigmoid(x):
    return 1.0 / (1.0 + jnp.exp(-x))


def _softplus(x):
    return jnp.maximum(x, 0.0) + jnp.log(1.0 + jnp.exp(-jnp.abs(x)))


def _rms_rows(x, w_row):
    ms = jnp.mean(x * x, axis=-1, keepdims=True)
    return x * lax.rsqrt(ms + NORM_EPS) * w_row


def _group_mean_sq(x, gmat):
    sq = x * x
    hi = sq.astype(BF16)
    lo = (sq - hi.astype(F32)).astype(BF16)
    return _dot(hi, gmat) + _dot(lo, gmat)


def _scan_lanes(x, op, identity):
    n = x.shape[1]
    lane = lax.broadcasted_iota(jnp.int32, x.shape, 1)
    step = 1
    while step < n:
        x = op(x, jnp.where(lane >= step, pltpu.roll(x, step, axis=1), identity))
        step *= 2
    return x


def _block_diag(n, group, value):
    r = jnp.arange(n) // group
    return jnp.where(r[:, None] == r[None, :], value, 0.0).astype(BF16)


def _const_spec(shape):
    nd = len(shape)
    return pl.BlockSpec(shape, lambda *_: (0,) * nd)


def _layer_spec(shape, layer):
    nd = len(shape)
    return pl.BlockSpec((None,) + tuple(shape), lambda *_: (layer,) + (0,) * nd, pipeline_mode=pl.Buffered(1))


def _params(sem):
    return pltpu.CompilerParams(dimension_semantics=sem, vmem_limit_bytes=VMEM_LIMIT)


def _ffn_block(x, nw_ref, wg_ref, wu_ref, wd_ref):
    xn = _rms_rows(x, nw_ref[...]).astype(BF16)
    acc = None
    for c in range(D_FF // FFN_CHUNK):
        cols = slice(c * FFN_CHUNK, (c + 1) * FFN_CHUNK)
        g = _dot(xn, wg_ref[:, cols])
        u = _dot(xn, wu_ref[:, cols])
        h = (g * _sigmoid(g) * u).astype(BF16)
        part = _dot(h, wd_ref[cols, :])
        acc = part if acc is None else acc + part
    return x + 0.5 * acc


def _row_spec(width):
    return pl.BlockSpec((ROW_TILE, width), lambda i: (i, 0))


def _ffn_weight_specs(layer):
    return [_const_spec((1, D_MODEL)), _layer_spec((D_MODEL, D_FF), layer),
            _layer_spec((D_MODEL, D_FF), layer), _layer_spec((D_FF, D_MODEL), layer)]


def _store_transposed_tiles(out_ref, a):
    a_t = a.T.astype(BF16)
    for t in range(ROW_TILE // ATT_TILE):
        out_ref[t] = a_t[:, t * ATT_TILE:(t + 1) * ATT_TILE]


def _ffn_inproj_kernel(x_ref, fnw_ref, wg_ref, wu_ref, wd_ref, nw_ref, w_ref, qkw_ref, g32_ref, gbias_ref,
                       alog_ref, convw_ref, convb_ref,
                       x_out_ref, mqt_ref, mk_ref, mvt_ref, mot_ref, gt_ref,
                       zt_ref, sb_ref, sct_ref, sxt_ref, sxdt_ref, sa_ref, qt_ref, k_ref, vt_ref,
                       halo_scr, *, tiles_per_seq):
    @pl.when(pl.program_id(0) % tiles_per_seq == 0)
    def _():
        halo_scr[...] = jnp.zeros_like(halo_scr)

    x = _ffn_block(x_ref[...], fnw_ref, wg_ref, wu_ref, wd_ref)
    x_out_ref[...] = x
    xn = _rms_rows(x, nw_ref[...]).astype(BF16)

    def proj(lo, hi):
        return _dot(xn, w_ref[:, lo:hi])

    half = (_C_V - _C_QK) // 2
    p_xbc = proj(_C_XBC, _C_SMALL)
    p_small = proj(_C_SMALL, _C_QK)
    p_q, p_k = proj(_C_QK, _C_QK + half), proj(_C_QK + half, _C_V)
    p_mq = proj(_C_M, _C_M + MLSTM_WIDTH)
    p_mk = proj(_C_M + MLSTM_WIDTH, _C_M + 2 * MLSTM_WIDTH)
    p_mv = proj(_C_M + 2 * MLSTM_WIDTH, _C_M + 3 * MLSTM_WIDTH)
    p_mo = proj(_C_M + 3 * MLSTM_WIDTH, _C_Z)
    p_z = proj(_C_Z, _C_XBC)
    p_v = proj(_C_V, _C_END)

    qk = [a * lax.rsqrt(_group_mean_sq(a, g32_ref[...]) + NORM_EPS) * qkw_ref[:, i * half:(i + 1) * half]
          for i, a in enumerate((p_q, p_k))]
    _store_transposed_tiles(qt_ref, qk[0])
    k_ref[...] = qk[1].astype(BF16)
    _store_transposed_tiles(vt_ref, p_v)

    ext =jnp.concatenate([halo_scr[...], p_xbc], axis=0)
    halo_scr[...] = p_xbc[ROW_TILE - 8:, :]
    conv = convb_ref[...] + convw_ref[SSM_CONV - 1:SSM_CONV, :] * p_xbc
    for j in range(1, SSM_CONV):
        conv = conv + convw_ref[SSM_CONV - 1 - j:SSM_CONV - j, :] * ext[8 - j:8 - j + ROW_TILE, :]
    xa = conv * _sigmoid(conv)
    sb_ref[...] = xa[:, SSM_WIDTH:SSM_WIDTH + SSM_GROUPS * SSM_STATE].astype(BF16)
    _store_transposed_tiles(sct_ref, xa[:, SSM_WIDTH + SSM_GROUPS * SSM_STATE:])
    small_t = p_small.T + gbias_ref[...]
    dt = _softplus(small_t[_LANE_DT:_LANE_DT + SSM_HEADS])
    log_decay = dt * (-jnp.exp(alog_ref[...]))
    xs_t = xa[:, :SSM_WIDTH].T
    xdt_t = xs_t * jnp.concatenate([jnp.broadcast_to(dt[h:h + 1], (SSM_HEAD_DIM, ROW_TILE))
                                    for h in range(SSM_HEADS)], axis=0)
    for t in range(ROW_TILE // MIX_CHUNK):
        cols = slice(t * MIX_CHUNK, (t + 1) * MIX_CHUNK)
        sxt_ref[t] = xs_t[:, cols].astype(BF16)
        sxdt_ref[t] = xdt_t[:, cols].astype(BF16)
        sa_ref[t] = _scan_lanes(log_decay[:, cols], jnp.add, 0.0)
    _store_transposed_tiles(zt_ref, p_z)

    _store_transposed_tiles(mqt_ref, p_mq)
    mk_ref[...] = p_mk.astype(BF16)
    _store_transposed_tiles(mvt_ref, p_mv)
    _store_transposed_tiles(mot_ref, p_mo)
    gates = small_t[:_LANE_DT]
    for t in range(ROW_TILE // MIX_CHUNK):
        i_pre = gates[_LANE_I:_LANE_I + 8, t * MIX_CHUNK:(t + 1) * MIX_CHUNK]
        f_pre = gates[_LANE_F:_LANE_F + 8, t * MIX_CHUNK:(t + 1) * MIX_CHUNK]
        log_f = jnp.minimum(f_pre, 0.0) - jnp.log(1.0 + jnp.exp(-jnp.abs(f_pre)))
        b = _scan_lanes(log_f, jnp.add, 0.0)
        c = i_pre - b
        gt_ref[t] = jnp.concatenate([c, _scan_lanes(c, jnp.maximum, -jnp.inf), b], axis=0)


def _ffn_inproj(x, ffn_norm_w, wg, wu, wd, norm_w, w_all, qkw_row, g32, gate_bias_rows, alog_rows, conv_w, conv_b_row,
                layer, seq):
    n = x.shape[0]
    tiles = ROW_TILE // ATT_TILE

    def t_out(width, dtype=BF16):
        return (pl.BlockSpec((tiles, width, ATT_TILE), lambda i: (i, 0, 0)),
                jax.ShapeDtypeStruct((n // ATT_TILE, width, ATT_TILE), dtype))

    def r_out(width, dtype=BF16):
        return _row_spec(width), jax.ShapeDtypeStruct((n, width), dtype)

    outs = [r_out(D_MODEL, F32),
            t_out(MLSTM_WIDTH), r_out(MLSTM_WIDTH), t_out(MLSTM_WIDTH), t_out(MLSTM_WIDTH), t_out(GATE_ROWS, F32),
            t_out(SSM_WIDTH), r_out(SSM_GROUPS * SSM_STATE), t_out(SSM_GROUPS * SSM_STATE), t_out(SSM_WIDTH),
            t_out(SSM_WIDTH), t_out(SSM_HEADS, F32),
            t_out(DIFF_WIDTH), r_out(DIFF_WIDTH), t_out(DIFF_WIDTH)]
    return pl.pallas_call(
        functools.partial(_ffn_inproj_kernel, tiles_per_seq=seq // ROW_TILE),
        grid=(n // ROW_TILE,),
        in_specs=[_row_spec(D_MODEL)] + _ffn_weight_specs(layer)
        + [_const_spec((1, D_MODEL)), _layer_spec((D_MODEL, _C_END), layer),
           _const_spec((1, 2 * DIFF_WIDTH)), _const_spec((DIFF_WIDTH, DIFF_WIDTH)),
           _const_spec((LANES, ROW_TILE)), _const_spec((SSM_HEADS, ROW_TILE)),
           _const_spec((SSM_CONV, SSM_CONV_DIM)), _const_spec((1, SSM_CONV_DIM))],
        out_specs=[o[0] for o in outs],
        out_shape=[o[1] for o in outs],
        scratch_shapes=[pltpu.VMEM((8, SSM_CONV_DIM), F32)],
        compiler_params=_params(("arbitrary",)),
        name="ffn_inproj",
    )(x, ffn_norm_w.reshape(1, D_MODEL), wg, wu, wd, norm_w.reshape(1, D_MODEL), w_all, qkw_row, g32,
      gate_bias_rows, alog_rows, conv_w, conv_b_row)


def _mlstm_kernel(qt_ref, k_ref, vt_ref, ot_ref, gt_ref, nw_ref, yt_ref, ct_scr, nt_scr, m_scr):
    L = MIX_CHUNK
    H = MLSTM_HEADS
    W = MLSTM_HEAD_DIM

    @pl.when(pl.program_id(1) == 0)
    def _():
        ct_scr[...] = jnp.zeros_like(ct_scr)
        nt_scr[...] = jnp.zeros_like(nt_scr)
        m_scr[...] = jnp.zeros_like(m_scr)

    qt = qt_ref[0]
    ks = k_ref[...] * (MLSTM_HEAD_DIM ** -0.5)
    vt = vt_ref[0]
    gt = gt_ref[0]
    c, c_max, b = gt[0:8], gt[8:16], gt[16:24]
    m_prev = m_scr[...]
    big_m = jnp.maximum(m_prev, c_max)
    inter = jnp.exp(m_prev - big_m)
    floor = jnp.exp(-(b + big_m))
    b_last = jnp.broadcast_to(b[:, L - 1:L], b.shape)
    m_new = b_last + jnp.maximum(m_prev, jnp.broadcast_to(c_max[:, L - 1:L], b.shape))
    w = jnp.exp(b_last + c - m_new)
    decay = jnp.exp(b_last + m_prev - m_new)

    c2_cols = jnp.concatenate([c * LOG2E, jnp.zeros((LANES - 8, L), F32)], axis=0).T
    big_m2 = big_m * LOG2E
    causal = (lax.broadcasted_iota(jnp.int32, (L, L), 0) <= lax.broadcasted_iota(jnp.int32, (L, L), 1))
    row_head = lax.broadcasted_iota(jnp.int32, (H * W, L), 0) // W
    ones_rows = jnp.ones((ONES_ROWS, L), BF16)
    qk = [_dot(ks, jnp.where(row_head == h, qt, jnp.zeros_like(qt))) for h in range(H)]
    nds = []
    for h in range(H):
        d = jnp.exp2(jnp.where(causal, c2_cols[:, h:h + 1] - big_m2[h:h + 1, :], -jnp.inf))
        p = (qk[h] * d).astype(BF16)
        nds.append(_dot(jnp.concatenate([vt[h * W:(h + 1) * W, :], ones_rows], axis=0), p))
    cq = _dot(ct_scr[...].astype(BF16), qt)
    nq = _dot(nt_scr[...].astype(BF16), qt)
    outs = []
    for h in range(H):
        num = nds[h][:W] + inter[h:h + 1] * cq[h * W:(h + 1) * W]
        den = nds[h][W:W + 1] + inter[h:h + 1] * nq[h:h + 1]
        hh = num / jnp.maximum(jnp.abs(den), floor[h:h + 1])
        outs.append(hh * lax.rsqrt(jnp.mean(hh * hh, axis=0, keepdims=True) + NORM_EPS))
    hn = jnp.concatenate(outs, axis=0) * nw_ref[...]
    yt_ref[0] = (_sigmoid(ot_ref[0].astype(F32)) * hn).astype(BF16)

    w_full = jnp.concatenate([jnp.broadcast_to(w[h:h + 1], (W, L)) for h in range(H)], axis=0)
    g_new = _dot((vt.astype(F32) * w_full).astype(BF16), ks)
    decay_full = jnp.concatenate([jnp.broadcast_to(decay[h:h + 1, :1], (W, H * W)) for h in range(H)], axis=0)
    same_head = (lax.broadcasted_iota(jnp.int32, (H * W, H * W), 0) // W
                 == lax.broadcasted_iota(jnp.int32, (H * W, H * W), 1) // W)
    ct_scr[...] = decay_full * ct_scr[...] + jnp.where(same_head, g_new, 0.0)
    w16 = jnp.concatenate([w, jnp.zeros_like(w)], axis=0).astype(BF16)
    n_new = _dot(w16, ks)
    own_lanes = (lax.broadcasted_iota(jnp.int32, (16, H * W), 0)
                 == lax.broadcasted_iota(jnp.int32, (16, H * W), 1) // W)
    decay16 = jnp.broadcast_to(jnp.concatenate([decay[:, :1], jnp.zeros((8, 1), F32)], axis=0), (16, H * W))
    nt_scr[...] = decay16 * nt_scr[...] + jnp.where(own_lanes, n_new, 0.0)
    m_scr[...] = m_new


def _mlstm(mqt, mk, mvt, mot, gt, norm_w_rows, batch, seq):
    nc = seq // MIX_CHUNK
    n = batch * seq
    t_spec = pl.BlockSpec((1, MLSTM_WIDTH, MIX_CHUNK), lambda b, c: (b * nc + c, 0, 0))
    return pl.pallas_call(
        _mlstm_kernel,
        grid=(batch, nc),
        in_specs=[t_spec, pl.BlockSpec((MIX_CHUNK, MLSTM_WIDTH), lambda b, c: (b * nc + c, 0)), t_spec, t_spec,
                  pl.BlockSpec((1, GATE_ROWS, MIX_CHUNK), lambda b, c: (b * nc + c, 0, 0)),
                  _const_spec((MLSTM_WIDTH, MIX_CHUNK))],
        out_specs=t_spec,
        out_shape=jax.ShapeDtypeStruct((n // MIX_CHUNK, MLSTM_WIDTH, MIX_CHUNK), BF16),
        scratch_shapes=[pltpu.VMEM((MLSTM_WIDTH, MLSTM_WIDTH), F32),
                        pltpu.VMEM((16, MLSTM_WIDTH), F32),
                        pltpu.VMEM((8, MIX_CHUNK), F32)],
        compiler_params=_params(("arbitrary", "arbitrary")),
        name="mlstm",
    )(mqt, mk, mvt, mot, gt, norm_w_rows)


def _ssd_kernel(b_ref, ct_ref, xt_ref, xdt_ref, zt_ref, a_ref, dskip_ref, nw_ref, yt_ref, st_scr):
    L = MIX_CHUNK
    P = SSM_HEAD_DIM
    HG = SSM_HEADS // SSM_GROUPS
    GW = HG * P
    NS = SSM_STATE

    @pl.when(pl.program_id(1) == 0)
    def _():
        st_scr[...] = jnp.zeros_like(st_scr)

    a = a_ref[0]
    a_last = jnp.broadcast_to(a[:, L - 1:L], a.shape)
    exp_a = jnp.exp(a)
    d_in = jnp.exp(a_last - a)
    chunk_decay = jnp.exp(a_last)
    a2 = a * LOG2E
    a2_cols = jnp.concatenate([a2, jnp.zeros((LANES - SSM_HEADS, L), F32)], axis=0).T
    causal = (lax.broadcasted_iota(jnp.int32, (L, L), 0) <= lax.broadcasted_iota(jnp.int32, (L, L), 1))
    bn = b_ref[...]
    ct = ct_ref[0]
    xdt = xdt_ref[0]

    def per_head_rows(rows, grp, width):
        return jnp.concatenate([jnp.broadcast_to(rows[grp * HG + i:grp * HG + i + 1, :width], (P, width))
                                for i in range(HG)], axis=0)

    scores, y_off = [], []
    for grp in range(SSM_GROUPS):
        bg = bn[:, grp * NS:(grp + 1) * NS]
        cg = ct[grp * NS:(grp + 1) * NS, :]
        scores.append(_dot(bg, cg))
        st = st_scr[grp]
        y_off.append(_dot(st.astype(BF16), cg))
        x_in = (xdt[grp * GW:(grp + 1) * GW, :].astype(F32) * per_head_rows(d_in, grp, L)).astype(BF16)
        st_scr[grp] = per_head_rows(chunk_decay, grp, NS) * st + _dot(x_in, bg)
    for grp in range(SSM_GROUPS):
        y_diag = []
        for i in range(HG):
            h = grp * HG + i
            seg = jnp.exp2(jnp.where(causal, a2[h:h + 1, :] - a2_cols[:, h:h + 1], -jnp.inf))
            p = (scores[grp] * seg).astype(BF16)
            y_diag.append(_dot(xdt[h * P:(h + 1) * P, :], p))
        rows = slice(grp * GW, (grp + 1) * GW)
        y = (jnp.concatenate(y_diag, axis=0) + y_off[grp] * per_head_rows(exp_a, grp, L)
             + dskip_ref[rows, :] * xt_ref[0, rows, :].astype(F32))
        zg = zt_ref[0, rows, :].astype(F32)
        y = y * (zg * _sigmoid(zg))
        ms = jnp.mean(y * y, axis=0, keepdims=True)
        yt_ref[0, rows, :] = (y * lax.rsqrt(ms + NORM_EPS) * nw_ref[rows, :]).astype(BF16)


def _ssd(sb, sct, sxt, sxdt, zt, sa, dskip_rows, norm_w_rows, batch, seq):
    nc = seq // MIX_CHUNK
    n = batch * seq

    def t_spec(width):
        return pl.BlockSpec((1, width, MIX_CHUNK), lambda b, c: (b * nc + c, 0, 0))

    return pl.pallas_call(
        _ssd_kernel,
        grid=(batch, nc),
        in_specs=[pl.BlockSpec((MIX_CHUNK, SSM_GROUPS * SSM_STATE), lambda b, c: (b * nc + c, 0)),
                  t_spec(SSM_GROUPS * SSM_STATE), t_spec(SSM_WIDTH), t_spec(SSM_WIDTH), t_spec(SSM_WIDTH),
                  t_spec(SSM_HEADS), _const_spec((SSM_WIDTH, MIX_CHUNK)), _const_spec((SSM_WIDTH, MIX_CHUNK))],
        out_specs=t_spec(SSM_WIDTH),
        out_shape=jax.ShapeDtypeStruct((n // MIX_CHUNK, SSM_WIDTH, MIX_CHUNK), BF16),
        scratch_shapes=[pltpu.VMEM((SSM_GROUPS, SSM_WIDTH // SSM_GROUPS, SSM_STATE), F32)],
        compiler_params=_params(("arbitrary", "arbitrary")),
        name="ssd",
    )(sb, sct, sxt, sxdt, zt, sa, dskip_rows, norm_w_rows)


def _t5_bias_tile(rel, relb_ref, head):
    n = jnp.maximum(rel, 0)
    max_exact = REL_BUCKETS // 2
    nf = jnp.maximum(n, 1).astype(F32)
    large = max_exact + (jnp.log(nf / max_exact) / math.log(REL_MAX_DIST / max_exact)
                         * (REL_BUCKETS - max_exact)).astype(jnp.int32)
    large = jnp.minimum(large, REL_BUCKETS - 1)
    bucket = jnp.where(n < max_exact, n, large)
    far = relb_ref[REL_BUCKETS - 1, head]
    bias = jnp.zeros(rel.shape, F32)
    for bkt in range(REL_BUCKETS - 1):
        bias = jnp.where(bucket == bkt, (relb_ref[bkt, head] - far) * LOG2E, bias)
    return jnp.where(rel >= 0, bias, NEG_BIG)


def _attn_kernel(relb_ref, qt_ref, k_ref, vt_ref, lam_ref, subw_ref, g64_ref,
                 y_ref, bias_scr, qm_scr, m_scr, acc_scr, sa_scr, sb_scr, *, lam_init):
    T = ATT_TILE
    H = DIFF_HEADS
    DV = DIFF_V_DIM
    i = pl.program_id(1)

    @pl.when((pl.program_id(0) == 0) & (i == 0))
    def _():
        rel = lax.broadcasted_iota(jnp.int32, (T, T), 1) - lax.broadcasted_iota(jnp.int32, (T, T), 0)
        for h in range(H):
            bias_scr[0, h] = _t5_bias_tile(rel, relb_ref, h)
            bias_scr[1, h] = _t5_bias_tile(rel + T, relb_ref, h)

    qt = qt_ref[0]
    comp_of_row = lax.broadcasted_iota(jnp.int32, (DIFF_WIDTH, T), 0) // DIFF_QK_DIM
    for hc in range(2 * H):
        qm_scr[hc] = jnp.where(comp_of_row == hc, qt, jnp.zeros_like(qt))
    m_scr[...] = jnp.full(m_scr.shape, NEG_BIG, F32)
    acc_scr[...] = jnp.zeros_like(acc_scr)
    ones_rows = jnp.ones((ONES_ROWS, T), BF16)

    def logits_one(s_ref, kt, hc):
        s_ref[hc] = _dot(kt, qm_scr[hc])

    def key_tile(j):
        return k_ref[pl.ds(pl.multiple_of(j * T, T), T), :]

    def logits_into(s_ref, j):
        kt = key_tile(j)
        for hc in range(2 * H):
            logits_one(s_ref, kt, hc)

    def consume_one(s_ref, vt, hc, bias_idx):
        h = hc // 2
        s = s_ref[hc]
        if bias_idx is not None:
            s = bias_scr[bias_idx, h] + s
        m_prev = m_scr[hc]
        m_next = jnp.maximum(m_prev, jnp.max(s, axis=0, keepdims=True))
        alpha = jnp.exp2(m_prev - m_next)
        p = jnp.exp2(s - m_next).astype(BF16)
        m_scr[hc] = m_next
        v_aug = jnp.concatenate([vt[h * DV:(h + 1) * DV, :], ones_rows], axis=0)
        acc_scr[hc] = alpha * acc_scr[hc] + _dot(v_aug, p)

    def consume(s_ref, j, bias_idx):
        vt = vt_ref[j]
        for hc in range(2 * H):
            consume_one(s_ref, vt, hc, bias_idx)

    LEAD = 2

    def step(s_next, j_next, s_cur, j_cur, bias_idx):
        kt = key_tile(j_next)
        vt = vt_ref[j_cur]
        for hc in range(LEAD):
            logits_one(s_next, kt, hc)
        for hc in range(2 * H):
            if hc + LEAD < 2 * H:
                logits_one(s_next, kt, hc + LEAD)
            consume_one(s_cur, vt, hc, bias_idx)

    DIAG, NEAR = 0, 1
    n_far = jnp.maximum(i - 1, 0)
    logits_into(sa_scr, 0)

    def far_pair(t, carry):
        j = 2 * t
        step(sb_scr, j + 1, sa_scr, j, None)
        step(sa_scr, j + 2, sb_scr, j + 1, None)
        return carry

    lax.fori_loop(0, n_far // 2, far_pair, 0)
    u = 2 * (n_far // 2)
    odd = n_far % 2 == 1

    @pl.when(odd)
    def _():
        step(sb_scr, u + 1, sa_scr, u, None)
        step(sa_scr, u + 2, sb_scr, u + 1, NEAR)
        consume(sa_scr, u + 2, DIAG)

    @pl.when(jnp.logical_not(odd) & (i >= 1))
    def _():
        step(sb_scr, u + 1, sa_scr, u, NEAR)
        consume(sb_scr, u + 1, DIAG)

    @pl.when(i == 0)
    def _():
        consume(sa_scr, 0, DIAG)

    lam_p = lam_ref[...]
    lam = (jnp.exp(jnp.sum(lam_p[0:1] * lam_p[1:2], axis=1, keepdims=True))
           - jnp.exp(jnp.sum(lam_p[2:3] * lam_p[3:4], axis=1, keepdims=True)) + lam_init)

    def normalised(hc):
        return acc_scr[hc, :DV, :] / acc_scr[hc, DV:DV + 1, :]

    o_t = jnp.concatenate([normalised(2 * h) - lam * normalised(2 * h + 1) for h in range(H)], axis=0)
    o = o_t.T
    ms = _group_mean_sq(o, g64_ref[...])
    y_ref[...] = (o * lax.rsqrt(ms + NORM_EPS) * subw_ref[...] * (1.0 - lam_init)).astype(BF16)


def _diff_attn(qt, kn, vt, rel_bias, lambdas, subw_row, g64, lam_init, batch, seq):
    nq = seq // ATT_TILE
    n = batch * seq
    return pl.pallas_call(
        functools.partial(_attn_kernel, lam_init=lam_init),
        grid=(batch, nq),
        in_specs=[pl.BlockSpec(memory_space=pltpu.SMEM),
                  pl.BlockSpec((1, DIFF_WIDTH, ATT_TILE), lambda b, i: (b * nq + i, 0, 0)),
                  pl.BlockSpec((seq, DIFF_WIDTH), lambda b, i: (b, 0)),
                  pl.BlockSpec((nq, DIFF_WIDTH, ATT_TILE), lambda b, i: (b, 0, 0)),
                  _const_spec((4, DIFF_QK_DIM)), _const_spec((1, DIFF_WIDTH)),
                  _const_spec((DIFF_WIDTH, DIFF_WIDTH))],
        out_specs=pl.BlockSpec((ATT_TILE, DIFF_WIDTH), lambda b, i: (b * nq + i, 0)),
        out_shape=jax.ShapeDtypeStruct((n, DIFF_WIDTH), BF16),
        scratch_shapes=[pltpu.VMEM((2, DIFF_HEADS, ATT_TILE, ATT_TILE), F32),
                        pltpu.VMEM((2 * DIFF_HEADS, DIFF_WIDTH, ATT_TILE), BF16),
                        pltpu.VMEM((2 * DIFF_HEADS, 1, ATT_TILE), F32),
                        pltpu.VMEM((2 * DIFF_HEADS, DIFF_V_DIM + ONES_ROWS, ATT_TILE), F32),
                        pltpu.VMEM((2 * DIFF_HEADS, ATT_TILE, ATT_TILE), F32),
                        pltpu.VMEM((2 * DIFF_HEADS, ATT_TILE, ATT_TILE), F32)],
        compiler_params=_params(("arbitrary", "arbitrary")),
        name="diff_attn",
    )(rel_bias, qt, kn, vt, lambdas, subw_row, g64)


def _untransposed(yt_ref):
    return jnp.concatenate([yt_ref[t].astype(F32).T for t in range(yt_ref.shape[0])], axis=0).astype(BF16)


def _outproj_ffn_kernel(x_ref, ymt_ref, yst_ref, yd_ref, w_ref, fnw_ref, wg_ref, wu_ref, wd_ref, o_ref):
    y = jnp.concatenate([_untransposed(ymt_ref), _untransposed(yst_ref), yd_ref[...]], axis=1)
    x = x_ref[...] + _dot(y, w_ref[...])
    o_ref[...] = _ffn_block(x, fnw_ref, wg_ref, wu_ref, wd_ref)


def _outproj_ffn(x, y_m, y_s, y_d, w_out, ffn_norm_w, wg, wu, wd, layer):
    n = x.shape[0]
    return pl.pallas_call(
        _outproj_ffn_kernel,
        grid=(n // ROW_TILE,),
        in_specs=[_row_spec(D_MODEL),
                  pl.BlockSpec((ROW_TILE // MIX_CHUNK, MLSTM_WIDTH, MIX_CHUNK), lambda i: (i, 0, 0)),
                  pl.BlockSpec((ROW_TILE // MIX_CHUNK, SSM_WIDTH, MIX_CHUNK), lambda i: (i, 0, 0)),
                  _row_spec(DIFF_WIDTH),
                  _layer_spec((D_MODEL, D_MODEL), layer)] + _ffn_weight_specs(layer),
        out_specs=_row_spec(D_MODEL),
        out_shape=jax.ShapeDtypeStruct((n, D_MODEL), F32),
        compiler_params=_params(("parallel",)),
        name="outproj_ffn",
    )(x, y_m, y_s, y_d, w_out, ffn_norm_w.reshape(1, D_MODEL), wg, wu, wd)


def _regroup_w_in(w_in):
    sizes = (MLSTM_WIDTH, MLSTM_WIDTH, MLSTM_WIDTH, MLSTM_WIDTH, MLSTM_HEADS, MLSTM_HEADS,
             SSM_WIDTH, SSM_CONV_DIM, SSM_HEADS, 2 * DIFF_HEADS * DIFF_QK_DIM, 2 * DIFF_HEADS * DIFF_QK_DIM,
             DIFF_WIDTH)
    offs = [0]
    for s in sizes:
        offs.append(offs[-1] + s)
    w_in = w_in.astype(BF16)
    mq, mk, mv, mo, mi, mf, z, xbc, dt, dq, dk, dv = [w_in[..., offs[i]:offs[i + 1]] for i in range(len(sizes))]

    def pad(width):
        return jnp.zeros(w_in.shape[:-1] + (width,), BF16)

    small = [mi, pad(_LANE_F - _LANE_I - MLSTM_HEADS), mf, pad(_LANE_DT - _LANE_F - MLSTM_HEADS), dt,
             pad(LANES - _LANE_DT - SSM_HEADS)]
    return jnp.concatenate([mq, mk, mv, mo, z, xbc] + small + [dq, dk, dv], axis=-1)


def _small_row(values_by_lane):
    row = jnp.zeros((LANES,), F32)
    for lane, vals in values_by_lane:
        row = lax.dynamic_update_slice(row, vals.astype(F32), (lane,))
    return row.reshape(1, LANES)


def kernel(x, ffn1_norm_w, ffn1_w_gate, ffn1_w_up, ffn1_w_down, mix_norm_w, w_in, mlstm_gate_bias, mlstm_norm_w, ssm_conv_w, ssm_conv_b, ssm_dt_bias, ssm_A_log, ssm_D, ssm_norm_w, diff_q_norm_w, diff_k_norm_w, diff_lambda, diff_subln_w, rel_bias, w_out, ffn2_norm_w, ffn2_w_gate, ffn2_w_up, ffn2_w_down):
    batch, seq, d = x.shape
    assert d == D_MODEL and MIX_CHUNK == ATT_TILE and seq % ROW_TILE == 0 and ROW_TILE % MIX_CHUNK == 0
    xf = x.reshape(batch * seq, D_MODEL)
    g32 = _block_diag(DIFF_WIDTH, DIFF_QK_DIM, 1.0 / DIFF_QK_DIM)
    g64 = _block_diag(DIFF_WIDTH, DIFF_V_DIM, 1.0 / DIFF_V_DIM)
    ffn1 = [w.astype(BF16) for w in (ffn1_w_gate, ffn1_w_up, ffn1_w_down)]
    ffn2 = [w.astype(BF16) for w in (ffn2_w_gate, ffn2_w_up, ffn2_w_down)]
    w_in_all = _regroup_w_in(w_in)
    w_out_all = w_out.astype(BF16)
    for l in range(DEPTH):
        qkw_row = jnp.concatenate([jnp.tile(diff_q_norm_w[l].reshape(-1), DIFF_HEADS) * (DIFF_QK_DIM ** -0.5 * LOG2E),
                                   jnp.tile(diff_k_norm_w[l].reshape(-1), DIFF_HEADS)]).reshape(1, 2 * DIFF_WIDTH)
        bias_row = _small_row([(_LANE_I, mlstm_gate_bias[l, 0]), (_LANE_F, mlstm_gate_bias[l, 1]),
                               (_LANE_DT, ssm_dt_bias[l])])
        (xf, mqt, mk, mvt, mot, gt, zt, sb, sct, sxt, sxdt, sa, qt, kn, vt) = _ffn_inproj(
            xf, ffn1_norm_w[l], *ffn1, mix_norm_w[l], w_in_all, qkw_row, g32,
            jnp.broadcast_to(bias_row.reshape(LANES, 1), (LANES, ROW_TILE)),
            jnp.broadcast_to(ssm_A_log[l][:, None], (SSM_HEADS, ROW_TILE)),
            ssm_conv_w[l], ssm_conv_b[l].reshape(1, SSM_CONV_DIM), l, seq)
        y_m = _mlstm(mqt, mk, mvt, mot, gt, jnp.broadcast_to(mlstm_norm_w[l][:, None], (MLSTM_WIDTH, MIX_CHUNK)),
                     batch, seq)
        y_s = _ssd(sb, sct, sxt, sxdt, zt, sa,
                   jnp.broadcast_to(jnp.repeat(ssm_D[l], SSM_HEAD_DIM)[:, None], (SSM_WIDTH, MIX_CHUNK)),
                   jnp.broadcast_to(ssm_norm_w[l][:, None], (SSM_WIDTH, MIX_CHUNK)), batch, seq)
        lam_init = 0.8 - 0.6 * math.exp(-0.3 * l)
        y_d = _diff_attn(qt, kn, vt, rel_bias, diff_lambda[l], jnp.tile(diff_subln_w[l], DIFF_HEADS).reshape(1, DIFF_WIDTH),
                         g64, lam_init, batch, seq)
        xf = _outproj_ffn(xf, y_m, y_s, y_d, w_out_all, ffn2_norm_w[l], *ffn2, l)
    return xf.reshape(batch, seq, D_MODEL)
```

```python
import functools
import math

import jax
import jax.numpy as jnp
from jax import lax
from jax.experimental import pallas as pl
from jax.experimental.pallas import tpu as pltpu

F32 = jnp.float32
BF16 = jnp.bfloat16

D_MODEL = 1024
DEPTH = 2
D_FF = 2816
MLSTM_HEADS = 4
MLSTM_HEAD_DIM = 64
MLSTM_WIDTH = MLSTM_HEADS * MLSTM_HEAD_DIM
SSM_HEADS = 8
SSM_HEAD_DIM = 64
SSM_WIDTH = SSM_HEADS * SSM_HEAD_DIM
SSM_STATE = 128
SSM_GROUPS = 2
SSM_CONV = 4
SSM_CONV_DIM = SSM_WIDTH + 2 * SSM_GROUPS * SSM_STATE
DIFF_HEADS = 4
DIFF_QK_DIM = 32
DIFF_V_DIM = 64
DIFF_WIDTH = DIFF_HEADS * DIFF_V_DIM
REL_BUCKETS = 32
REL_MAX_DIST = 128
NORM_EPS = 1e-6

LANES = 128
ROW_TILE = 512
FFN_CHUNK = 1408
MIX_CHUNK = 256
MIX_STEP_CHUNKS = 2
ATT_TILE = 256
VMEM_LIMIT = 56 * 1024 * 1024
NEG_BIG = -1e30
LOG2E = math.log2(math.e)
ONES_ROWS = 16

_C_M = 0
_C_Z = _C_M + 4 * MLSTM_WIDTH
_C_XBC = _C_Z + SSM_WIDTH
_C_SMALL = _C_XBC + SSM_CONV_DIM
_C_QK = _C_SMALL + LANES
_C_V = _C_QK + 4 * DIFF_HEADS * DIFF_QK_DIM
_C_END = _C_V + DIFF_WIDTH
_LANE_I = 0
_LANE_F = 8
_LANE_DT = 16
GATE_ROWS = 24


def _dot(a, b):
    return jnp.dot(a, b, preferred_element_type=F32)


def _sigmoid(x):
    return 1.0 / (1.0 + jnp.exp(-x))


def _softplus(x):
    return jnp.maximum(x, 0.0) + jnp.log(1.0 + jnp.exp(-jnp.abs(x)))


def _rms_rows(x, w_row):
    ms = jnp.mean(x * x, axis=-1, keepdims=True)
    return x * lax.rsqrt(ms + NORM_EPS) * w_row


def _group_mean_sq(x, gmat):
    sq = x * x
    hi = sq.astype(BF16)
    lo = (sq - hi.astype(F32)).astype(BF16)
    return _dot(hi, gmat) + _dot(lo, gmat)


def _scan_lanes(x, op, identity):
    n = x.shape[1]
    lane = lax.broadcasted_iota(jnp.int32, x.shape, 1)
    step = 1
    while step < n:
        x = op(x, jnp.where(lane >= step, pltpu.roll(x, step, axis=1), identity))
        step *= 2
    return x


def _block_diag(n, group, value):
    r = jnp.arange(n) // group
    return jnp.where(r[:, None] == r[None, :], value, 0.0).astype(BF16)


def _const_spec(shape):
    nd = len(shape)
    return pl.BlockSpec(shape, lambda *_: (0,) * nd)


def _layer_spec(shape, layer):
    nd = len(shape)
    return pl.BlockSpec((None,) + tuple(shape), lambda *_: (layer,) + (0,) * nd, pipeline_mode=pl.Buffered(1))


def _params(sem):
    return pltpu.CompilerParams(dimension_semantics=sem, vmem_limit_bytes=VMEM_LIMIT)


def _ffn_block(x, nw_ref, wg_ref, wu_ref, wd_ref):
    xn = _rms_rows(x, nw_ref[...]).astype(BF16)
    acc = None
    for c in range(D_FF // FFN_CHUNK):
        cols = slice(c * FFN_CHUNK, (c + 1) * FFN_CHUNK)
        g = _dot(xn, wg_ref[:, cols])
        u = _dot(xn, wu_ref[:, cols])
        h = (g * _sigmoid(g) * u).astype(BF16)
        part = _dot(h, wd_ref[cols, :])
        acc = part if acc is None else acc + part
    return x + 0.5 * acc


def _row_spec(width):
    return pl.BlockSpec((ROW_TILE, width), lambda i: (i, 0))


def _ffn_weight_specs(layer):
    return [_const_spec((1, D_MODEL)), _layer_spec((D_MODEL, D_FF), layer),
            _layer_spec((D_MODEL, D_FF), layer), _layer_spec((D_FF, D_MODEL), layer)]


def _store_transposed_tiles(out_ref, a):
    a_t = a.T.astype(BF16)
    for t in range(ROW_TILE // ATT_TILE):
        out_ref[t] = a_t[:, t * ATT_TILE:(t + 1) * ATT_TILE]


def _ffn_inproj_kernel(x_ref, fnw_ref, wg_ref, wu_ref, wd_ref, nw_ref, w_ref, qkw_ref, g32_ref, gbias_ref,
                       alog_ref, convw_ref, convb_ref,
                       x_out_ref, mqt_ref, mk_ref, mvt_ref, mot_ref, gt_ref,
                       zt_ref, sb_ref, sct_ref, sxt_ref, sxdt_ref, sa_ref, qt_ref, k_ref, vt_ref,
                       halo_scr, *, tiles_per_seq):
    @pl.when(pl.program_id(0) % tiles_per_seq == 0)
    def _():
        halo_scr[...] = jnp.zeros_like(halo_scr)

    x = _ffn_block(x_ref[...], fnw_ref, wg_ref, wu_ref, wd_ref)
    x_out_ref[...] = x
    xn = _rms_rows(x, nw_ref[...]).astype(BF16)

    def proj(lo, hi):
        return _dot(xn, w_ref[:, lo:hi])

    half = (_C_V - _C_QK) // 2
    p_xbc = proj(_C_XBC, _C_SMALL)
    p_small = proj(_C_SMALL, _C_QK)
    p_q, p_k = proj(_C_QK, _C_QK + half), proj(_C_QK + half, _C_V)
    p_mq = proj(_C_M, _C_M + MLSTM_WIDTH)
    p_mk = proj(_C_M + MLSTM_WIDTH, _C_M + 2 * MLSTM_WIDTH)
    p_mv = proj(_C_M + 2 * MLSTM_WIDTH, _C_M + 3 * MLSTM_WIDTH)
    p_mo = proj(_C_M + 3 * MLSTM_WIDTH, _C_Z)
    p_z = proj(_C_Z, _C_XBC)
    p_v = proj(_C_V, _C_END)

    qk = [a * lax.rsqrt(_group_mean_sq(a, g32_ref[...]) + NORM_EPS) * qkw_ref[:, i * half:(i + 1) * half]
          for i, a in enumerate((p_q, p_k))]
    _store_transposed_tiles(qt_ref, qk[0])
    k_ref[...] = qk[1].astype(BF16)
    _store_transposed_tiles(vt_ref, p_v)

    ext = jnp.concatenate([halo_scr[...], p_xbc], axis=0)
    halo_scr[...] = p_xbc[ROW_TILE - 8:, :]
    conv = convb_ref[...] + convw_ref[SSM_CONV - 1:SSM_CONV, :] * p_xbc
    for j in range(1, SSM_CONV):
        conv = conv + convw_ref[SSM_CONV - 1 - j:SSM_CONV - j, :] * ext[8 - j:8 - j + ROW_TILE, :]
    xa = conv * _sigmoid(conv)
    sb_ref[...] = xa[:, SSM_WIDTH:SSM_WIDTH + SSM_GROUPS * SSM_STATE].astype(BF16)
    _store_transposed_tiles(sct_ref, xa[:, SSM_WIDTH + SSM_GROUPS * SSM_STATE:])
    small_t = p_small.T + gbias_ref[...]
    dt = _softplus(small_t[_LANE_DT:_LANE_DT + SSM_HEADS])
    log_decay = dt * (-jnp.exp(alog_ref[...]))
    xs_t = xa[:, :SSM_WIDTH].T
    xdt_t = xs_t * jnp.concatenate([jnp.broadcast_to(dt[h:h + 1], (SSM_HEAD_DIM, ROW_TILE))
                                    for h in range(SSM_HEADS)], axis=0)
    for t in range(ROW_TILE // MIX_CHUNK):
        cols = slice(t * MIX_CHUNK, (t + 1) * MIX_CHUNK)
        sxt_ref[t] = xs_t[:, cols].astype(BF16)
        sxdt_ref[t] = xdt_t[:, cols].astype(BF16)
        sa_ref[t] = _scan_lanes(log_decay[:, cols], jnp.add, 0.0)
    _store_transposed_tiles(zt_ref, p_z)

    _store_transposed_tiles(mqt_ref, p_mq)
    mk_ref[...] = p_mk.astype(BF16)
    _store_transposed_tiles(mvt_ref, p_mv)
    _store_transposed_tiles(mot_ref, p_mo)
    gates = small_t[:_LANE_DT]
    for t in range(ROW_TILE // MIX_CHUNK):
        i_pre = gates[_LANE_I:_LANE_I + 8, t * MIX_CHUNK:(t + 1) * MIX_CHUNK]
        f_pre = gates[_LANE_F:_LANE_F + 8, t * MIX_CHUNK:(t + 1) * MIX_CHUNK]
        log_f = jnp.minimum(f_pre, 0.0) - jnp.log(1.0 + jnp.exp(-jnp.abs(f_pre)))
        b = _scan_lanes(log_f, jnp.add, 0.0)
        c = i_pre - b
        gt_ref[t] = jnp.concatenate([c, _scan_lanes(c, jnp.maximum, -jnp.inf), b], axis=0)


def _ffn_inproj(x, ffn_norm_w, wg, wu, wd, norm_w, w_all, qkw_row, g32, gate_bias_rows, alog_rows, conv_w, conv_b_row,
                layer, seq):
    n = x.shape[0]
    tiles = ROW_TILE // ATT_TILE

    def t_out(width, dtype=BF16):
        return (pl.BlockSpec((tiles, width, ATT_TILE), lambda i: (i, 0, 0)),
                jax.ShapeDtypeStruct((n // ATT_TILE, width, ATT_TILE), dtype))

    def r_out(width, dtype=BF16):
        return _row_spec(width), jax.ShapeDtypeStruct((n, width), dtype)

    outs = [r_out(D_MODEL, F32),
            t_out(MLSTM_WIDTH), r_out(MLSTM_WIDTH), t_out(MLSTM_WIDTH), t_out(MLSTM_WIDTH), t_out(GATE_ROWS, F32),
            t_out(SSM_WIDTH), r_out(SSM_GROUPS * SSM_STATE), t_out(SSM_GROUPS * SSM_STATE), t_out(SSM_WIDTH),
            t_out(SSM_WIDTH), t_out(SSM_HEADS, F32),
            t_out(DIFF_WIDTH), r_out(DIFF_WIDTH), t_out(DIFF_WIDTH)]
    return pl.pallas_call(
        functools.partial(_ffn_inproj_kernel, tiles_per_seq=seq // ROW_TILE),
        grid=(n // ROW_TILE,),
        in_specs=[_row_spec(D_MODEL)] + _ffn_weight_specs(layer)
        + [_const_spec((1, D_MODEL)), _layer_spec((D_MODEL, _C_END), layer),
           _const_spec((1, 2 * DIFF_WIDTH)), _const_spec((DIFF_WIDTH, DIFF_WIDTH)),
           _const_spec((LANES, ROW_TILE)), _const_spec((SSM_HEADS, ROW_TILE)),
           _const_spec((SSM_CONV, SSM_CONV_DIM)), _const_spec((1, SSM_CONV_DIM))],
        out_specs=[o[0] for o in outs],
        out_shape=[o[1] for o in outs],
        scratch_shapes=[pltpu.VMEM((8, SSM_CONV_DIM), F32)],
        compiler_params=_params(("arbitrary",)),
        name="ffn_inproj",
    )(x, ffn_norm_w.reshape(1, D_MODEL), wg, wu, wd, norm_w.reshape(1, D_MODEL), w_all, qkw_row, g32,
      gate_bias_rows, alog_rows, conv_w, conv_b_row)


def _mlstm_kernel(qt_ref, k_ref, vt_ref, ot_ref, gt_ref, nw_ref, yt_ref, ct_scr, nt_scr, m_scr):
    @pl.when(pl.program_id(1) == 0)
    def _():
        ct_scr[...] = jnp.zeros_like(ct_scr)
        nt_scr[...] = jnp.zeros_like(nt_scr)
        m_scr[...] = jnp.zeros_like(m_scr)

    for t in range(MIX_STEP_CHUNKS):
        _mlstm_chunk(t, qt_ref, k_ref, vt_ref, ot_ref, gt_ref, nw_ref, yt_ref, ct_scr, nt_scr, m_scr)


def _mlstm_chunk(t, qt_ref, k_ref, vt_ref, ot_ref, gt_ref, nw_ref, yt_ref, ct_scr, nt_scr, m_scr):
    L = MIX_CHUNK
    H = MLSTM_HEADS
    W = MLSTM_HEAD_DIM
    qt = qt_ref[t]
    ks = k_ref[t * L:(t + 1) * L, :] * (MLSTM_HEAD_DIM ** -0.5)
    vt = vt_ref[t]
    gt = gt_ref[t]
    c, c_max, b = gt[0:8], gt[8:16], gt[16:24]
    m_prev = m_scr[...]
    big_m = jnp.maximum(m_prev, c_max)
    inter = jnp.exp(m_prev - big_m)
    floor = jnp.exp(-(b + big_m))
    b_last = jnp.broadcast_to(b[:, L - 1:L], b.shape)
    m_new = b_last + jnp.maximum(m_prev, jnp.broadcast_to(c_max[:, L - 1:L], b.shape))
    w = jnp.exp(b_last + c - m_new)
    decay = jnp.exp(b_last + m_prev - m_new)

    c2_cols = jnp.concatenate([c * LOG2E, jnp.zeros((LANES - 8, L), F32)], axis=0).T
    big_m2 = big_m * LOG2E
    causal = (lax.broadcasted_iota(jnp.int32, (L, L), 0) <= lax.broadcasted_iota(jnp.int32, (L, L), 1))
    row_head = lax.broadcasted_iota(jnp.int32, (H * W, L), 0) // W
    ones_rows = jnp.ones((ONES_ROWS, L), BF16)
    qk = [_dot(ks, jnp.where(row_head == h, qt, jnp.zeros_like(qt))) for h in range(H)]
    nds = []
    for h in range(H):
        d = jnp.exp2(jnp.where(causal, c2_cols[:, h:h + 1] - big_m2[h:h + 1, :], -jnp.inf))
        p = (qk[h] * d).astype(BF16)
        nds.append(_dot(jnp.concatenate([vt[h * W:(h + 1) * W, :], ones_rows], axis=0), p))
    cq = _dot(ct_scr[...].astype(BF16), qt)
    nq = _dot(nt_scr[...].astype(BF16), qt)
    outs = []
    for h in range(H):
        num = nds[h][:W] + inter[h:h + 1] * cq[h * W:(h + 1) * W]
        den = nds[h][W:W + 1] + inter[h:h + 1] * nq[h:h + 1]
        hh = num / jnp.maximum(jnp.abs(den), floor[h:h + 1])
        outs.append(hh * lax.rsqrt(jnp.mean(hh * hh, axis=0, keepdims=True) + NORM_EPS))
    hn = jnp.concatenate(outs, axis=0) * nw_ref[...]
    yt_ref[t] = (_sigmoid(ot_ref[t].astype(F32)) * hn).astype(BF16)

    w_full = jnp.concatenate([jnp.broadcast_to(w[h:h + 1], (W, L)) for h in range(H)], axis=0)
    g_new = _dot((vt.astype(F32) * w_full).astype(BF16), ks)
    decay_full = jnp.concatenate([jnp.broadcast_to(decay[h:h + 1, :1], (W, H * W)) for h in range(H)], axis=0)
    same_head = (lax.broadcasted_iota(jnp.int32, (H * W, H * W), 0) // W
                 == lax.broadcasted_iota(jnp.int32, (H * W, H * W), 1) // W)
    ct_scr[...] = decay_full * ct_scr[...] + jnp.where(same_head, g_new, 0.0)
    w16 = jnp.concatenate([w, jnp.zeros_like(w)], axis=0).astype(BF16)
    n_new = _dot(w16, ks)
    own_lanes = (lax.broadcasted_iota(jnp.int32, (16, H * W), 0)
                 == lax.broadcasted_iota(jnp.int32, (16, H * W), 1) // W)
    decay16 = jnp.broadcast_to(jnp.concatenate([decay[:, :1], jnp.zeros((8, 1), F32)], axis=0), (16, H * W))
    nt_scr[...] = decay16 * nt_scr[...] + jnp.where(own_lanes, n_new, 0.0)
    m_scr[...] = m_new


def _mlstm(mqt, mk, mvt, mot, gt, norm_w_rows, batch, seq):
    ns = seq // (MIX_CHUNK * MIX_STEP_CHUNKS)
    n = batch * seq
    t_spec = pl.BlockSpec((MIX_STEP_CHUNKS, MLSTM_WIDTH, MIX_CHUNK), lambda b, c: (b * ns + c, 0, 0))
    return pl.pallas_call(
        _mlstm_kernel,
        grid=(batch, ns),
        in_specs=[t_spec, pl.BlockSpec((MIX_STEP_CHUNKS * MIX_CHUNK, MLSTM_WIDTH), lambda b, c: (b * ns + c, 0)),
                  t_spec, t_spec,
                  pl.BlockSpec((MIX_STEP_CHUNKS, GATE_ROWS, MIX_CHUNK), lambda b, c: (b * ns + c, 0, 0)),
                  _const_spec((MLSTM_WIDTH, MIX_CHUNK))],
        out_specs=t_spec,
        out_shape=jax.ShapeDtypeStruct((n // MIX_CHUNK, MLSTM_WIDTH, MIX_CHUNK), BF16),
        scratch_shapes=[pltpu.VMEM((MLSTM_WIDTH, MLSTM_WIDTH), F32),
                        pltpu.VMEM((16, MLSTM_WIDTH), F32),
                        pltpu.VMEM((8, MIX_CHUNK), F32)],
        compiler_params=_params(("arbitrary", "arbitrary")),
        name="mlstm",
    )(mqt, mk, mvt, mot, gt, norm_w_rows)


def _ssd_kernel(b_ref, ct_ref, xt_ref, xdt_ref, zt_ref, a_ref, dskip_ref, nw_ref, yt_ref, st_scr):
    @pl.when(pl.program_id(1) == 0)
    def _():
        st_scr[...] = jnp.zeros_like(st_scr)

    for t in range(MIX_STEP_CHUNKS):
        _ssd_chunk(t, b_ref, ct_ref, xt_ref, xdt_ref, zt_ref, a_ref, dskip_ref, nw_ref, yt_ref, st_scr)


def _ssd_chunk(t, b_ref, ct_ref, xt_ref, xdt_ref, zt_ref, a_ref, dskip_ref, nw_ref, yt_ref, st_scr):
    L = MIX_CHUNK
    P = SSM_HEAD_DIM
    HG = SSM_HEADS // SSM_GROUPS
    GW = HG * P
    NS = SSM_STATE
    a = a_ref[t]
    a_last = jnp.broadcast_to(a[:, L - 1:L], a.shape)
    exp_a = jnp.exp(a)
    d_in = jnp.exp(a_last - a)
    chunk_decay = jnp.exp(a_last)
    a2 = a * LOG2E
    a2_cols = jnp.concatenate([a2, jnp.zeros((LANES - SSM_HEADS, L), F32)], axis=0).T
    causal = (lax.broadcasted_iota(jnp.int32, (L, L), 0) <= lax.broadcasted_iota(jnp.int32, (L, L), 1))
    bn = b_ref[t * L:(t + 1) * L, :]
    ct = ct_ref[t]
    xdt = xdt_ref[t]

    def per_head_rows(rows, grp, width):
        return jnp.concatenate([jnp.broadcast_to(rows[grp * HG + i:grp * HG + i + 1, :width], (P, width))
                                for i in range(HG)], axis=0)

    scores, y_off = [], []
    for grp in range(SSM_GROUPS):
        bg = bn[:, grp * NS:(grp + 1) * NS]
        cg = ct[grp * NS:(grp + 1) * NS, :]
        scores.append(_dot(bg, cg))
        st = st_scr[grp]
        y_off.append(_dot(st.astype(BF16), cg))
        x_in = (xdt[grp * GW:(grp + 1) * GW, :].astype(F32) * per_head_rows(d_in, grp, L)).astype(BF16)
        st_scr[grp] = per_head_rows(chunk_decay, grp, NS) * st + _dot(x_in, bg)
    for grp in range(SSM_GROUPS):
        y_diag = []
        for i in range(HG):
            h = grp * HG + i
            seg = jnp.exp2(jnp.where(causal, a2[h:h + 1, :] - a2_cols[:, h:h + 1], -jnp.inf))
            p = (scores[grp] * seg).astype(BF16)
            y_diag.append(_dot(xdt[h * P:(h + 1) * P, :], p))
        rows = slice(grp * GW, (grp + 1) * GW)
        y = (jnp.concatenate(y_diag, axis=0) + y_off[grp] * per_head_rows(exp_a, grp, L)
             + dskip_ref[rows, :] * xt_ref[t, rows, :].astype(F32))
        zg = zt_ref[t, rows, :].astype(F32)
        y = y * (zg * _sigmoid(zg))
        ms = jnp.mean(y * y, axis=0, keepdims=True)
        yt_ref[t, rows, :] = (y * lax.rsqrt(ms + NORM_EPS) * nw_ref[rows, :]).astype(BF16)


def _ssd(sb, sct, sxt, sxdt, zt, sa, dskip_rows, norm_w_rows, batch, seq):
    ns = seq // (MIX_CHUNK * MIX_STEP_CHUNKS)
    n = batch * seq

    def t_spec(width):
        return pl.BlockSpec((MIX_STEP_CHUNKS, width, MIX_CHUNK), lambda b, c: (b * ns + c, 0, 0))

    return pl.pallas_call(
        _ssd_kernel,
        grid=(batch, ns),
        in_specs=[pl.BlockSpec((MIX_STEP_CHUNKS * MIX_CHUNK, SSM_GROUPS * SSM_STATE), lambda b, c: (b * ns + c, 0)),
                  t_spec(SSM_GROUPS * SSM_STATE), t_spec(SSM_WIDTH), t_spec(SSM_WIDTH), t_spec(SSM_WIDTH),
                  t_spec(SSM_HEADS), _const_spec((SSM_WIDTH, MIX_CHUNK)), _const_spec((SSM_WIDTH, MIX_CHUNK))],
        out_specs=t_spec(SSM_WIDTH),
        out_shape=jax.ShapeDtypeStruct((n // MIX_CHUNK, SSM_WIDTH, MIX_CHUNK), BF16),
        scratch_shapes=[pltpu.VMEM((SSM_GROUPS, SSM_WIDTH // SSM_GROUPS, SSM_STATE), F32)],
        compiler_params=_params(("arbitrary", "arbitrary")),
        name="ssd",
    )(sb, sct, sxt, sxdt, zt, sa, dskip_rows, norm_w_rows)


def _t5_bias_tile(rel, relb_ref, head):
    n = jnp.maximum(rel, 0)
    max_exact = REL_BUCKETS // 2
    nf = jnp.maximum(n, 1).astype(F32)
    large = max_exact + (jnp.log(nf / max_exact) / math.log(REL_MAX_DIST / max_exact)
                         * (REL_BUCKETS - max_exact)).astype(jnp.int32)
    large = jnp.minimum(large, REL_BUCKETS - 1)
    bucket = jnp.where(n < max_exact, n, large)
    far = relb_ref[REL_BUCKETS - 1, head]
    bias = jnp.zeros(rel.shape, F32)
    for bkt in range(REL_BUCKETS - 1):
        bias = jnp.where(bucket == bkt, (relb_ref[bkt, head] - far) * LOG2E, bias)
    return jnp.where(rel >= 0, bias, NEG_BIG)


def _attn_kernel(relb_ref, qt_ref, k_ref, vt_ref, lam_ref, subw_ref, g64_ref,
                 y_ref, bias_scr, qm_scr, m_scr, acc_scr, sa_scr, sb_scr, *, lam_init):
    T = ATT_TILE
    H = DIFF_HEADS
    DV = DIFF_V_DIM
    i = pl.program_id(1)

    @pl.when((pl.program_id(0) == 0) & (i == 0))
    def _():
        rel = lax.broadcasted_iota(jnp.int32, (T, T), 1) - lax.broadcasted_iota(jnp.int32, (T, T), 0)
        for h in range(H):
            bias_scr[0, h] = _t5_bias_tile(rel, relb_ref, h)
            bias_scr[1, h] = _t5_bias_tile(rel + T, relb_ref, h)

    qt = qt_ref[0]
    comp_of_row = lax.broadcasted_iota(jnp.int32, (DIFF_WIDTH, T), 0) // DIFF_QK_DIM
    for hc in range(2 * H):
        qm_scr[hc] = jnp.where(comp_of_row == hc, qt, jnp.zeros_like(qt))
    m_scr[...] = jnp.full(m_scr.shape, NEG_BIG, F32)
    acc_scr[...] = jnp.zeros_like(acc_scr)
    ones_rows = jnp.ones((ONES_ROWS, T), BF16)

    def logits_one(s_ref, kt, hc):
        s_ref[hc] = _dot(kt, qm_scr[hc])

    def key_tile(j):
        return k_ref[pl.ds(pl.multiple_of(j * T, T), T), :]

    def logits_into(s_ref, j):
        kt = key_tile(j)
        for hc in range(2 * H):
            logits_one(s_ref, kt, hc)

    def consume_one(s_ref, vt, hc, bias_idx):
        h = hc // 2
        s = s_ref[hc]
        if bias_idx is not None:
            s = bias_scr[bias_idx, h] + s
        m_prev = m_scr[hc]
        m_next = jnp.maximum(m_prev, jnp.max(s, axis=0, keepdims=True))
        alpha = jnp.exp2(m_prev - m_next)
        p = jnp.exp2(s - m_next).astype(BF16)
        m_scr[hc] = m_next
        v_aug = jnp.concatenate([vt[h * DV:(h + 1) * DV, :], ones_rows], axis=0)
        acc_scr[hc] = alpha * acc_scr[hc] + _dot(v_aug, p)

    def consume(s_ref, j, bias_idx):
        vt = vt_ref[j]
        for hc in range(2 * H):
            consume_one(s_ref, vt, hc, bias_idx)

    LEAD = 2

    def step(s_next, j_next, s_cur, j_cur, bias_idx):
        kt = key_tile(j_next)
        vt = vt_ref[j_cur]
        for hc in range(LEAD):
            logits_one(s_next, kt, hc)
        for hc in range(2 * H):
            if hc + LEAD < 2 * H:
                logits_one(s_next, kt, hc + LEAD)
            consume_one(s_cur, vt, hc, bias_idx)

    DIAG, NEAR = 0, 1
    n_far = jnp.maximum(i - 1, 0)
    logits_into(sa_scr, 0)

    def far_pair(t, carry):
        j = 2 * t
        step(sb_scr, j + 1, sa_scr, j, None)
        step(sa_scr, j + 2, sb_scr, j + 1, None)
        return carry

    lax.fori_loop(0, n_far // 2, far_pair, 0)
    u = 2 * (n_far // 2)
    odd = n_far % 2 == 1

    @pl.when(odd)
    def _():
        step(sb_scr, u + 1, sa_scr, u, None)
        step(sa_scr, u + 2, sb_scr, u + 1, NEAR)
        consume(sa_scr, u + 2, DIAG)

    @pl.when(jnp.logical_not(odd) & (i >= 1))
    def _():
        step(sb_scr, u + 1, sa_scr, u, NEAR)
        consume(sb_scr, u + 1, DIAG)

    @pl.when(i == 0)
    def _():
        consume(sa_scr, 0, DIAG)

    lam_p = lam_ref[...]
    lam = (jnp.exp(jnp.sum(lam_p[0:1] * lam_p[1:2], axis=1, keepdims=True))
           - jnp.exp(jnp.sum(lam_p[2:3] * lam_p[3:4], axis=1, keepdims=True)) + lam_init)

    def normalised(hc):
        return acc_scr[hc, :DV, :] / acc_scr[hc, DV:DV + 1, :]

    o_t = jnp.concatenate([normalised(2 * h) - lam * normalised(2 * h + 1) for h in range(H)], axis=0)
    o = o_t.T
    ms = _group_mean_sq(o, g64_ref[...])
    y_ref[...] = (o * lax.rsqrt(ms + NORM_EPS) * subw_ref[...] * (1.0 - lam_init)).astype(BF16)


def _diff_attn(qt, kn, vt, rel_bias, lambdas, subw_row, g64, lam_init, batch, seq):
    nq = seq // ATT_TILE
    n = batch * seq
    return pl.pallas_call(
        functools.partial(_attn_kernel, lam_init=lam_init),
        grid=(batch, nq),
        in_specs=[pl.BlockSpec(memory_space=pltpu.SMEM),
                  pl.BlockSpec((1, DIFF_WIDTH, ATT_TILE), lambda b, i: (b * nq + i, 0, 0)),
                  pl.BlockSpec((seq, DIFF_WIDTH), lambda b, i: (b, 0)),
                  pl.BlockSpec((nq, DIFF_WIDTH, ATT_TILE), lambda b, i: (b, 0, 0)),
                  _const_spec((4, DIFF_QK_DIM)), _const_spec((1, DIFF_WIDTH)),
                  _const_spec((DIFF_WIDTH, DIFF_WIDTH))],
        out_specs=pl.BlockSpec((ATT_TILE, DIFF_WIDTH), lambda b, i: (b * nq + i, 0)),
        out_shape=jax.ShapeDtypeStruct((n, DIFF_WIDTH), BF16),
        scratch_shapes=[pltpu.VMEM((2, DIFF_HEADS, ATT_TILE, ATT_TILE), F32),
                        pltpu.VMEM((2 * DIFF_HEADS, DIFF_WIDTH, ATT_TILE), BF16),
                        pltpu.VMEM((2 * DIFF_HEADS, 1, ATT_TILE), F32),
                        pltpu.VMEM((2 * DIFF_HEADS, DIFF_V_DIM + ONES_ROWS, ATT_TILE), F32),
                        pltpu.VMEM((2 * DIFF_HEADS, ATT_TILE, ATT_TILE), F32),
                        pltpu.VMEM((2 * DIFF_HEADS, ATT_TILE, ATT_TILE), F32)],
        compiler_params=_params(("arbitrary", "arbitrary")),
        name="diff_attn",
    )(rel_bias, qt, kn, vt, lambdas, subw_row, g64)


def _untransposed(yt_ref):
    return jnp.concatenate([yt_ref[t].astype(F32).T for t in range(yt_ref.shape[0])], axis=0).astype(BF16)


def _outproj_ffn_kernel(x_ref, ymt_ref, yst_ref, yd_ref, w_ref, fnw_ref, wg_ref, wu_ref, wd_ref, o_ref):
    y = jnp.concatenate([_untransposed(ymt_ref), _untransposed(yst_ref), yd_ref[...]], axis=1)
    x = x_ref[...] + _dot(y, w_ref[...])
    o_ref[...] = _ffn_block(x, fnw_ref, wg_ref, wu_ref, wd_ref)


def _outproj_ffn(x, y_m, y_s, y_d, w_out, ffn_norm_w, wg, wu, wd, layer):
    n = x.shape[0]
    return pl.pallas_call(
        _outproj_ffn_kernel,
        grid=(n // ROW_TILE,),
        in_specs=[_row_spec(D_MODEL),
                  pl.BlockSpec((ROW_TILE // MIX_CHUNK, MLSTM_WIDTH, MIX_CHUNK), lambda i: (i, 0, 0)),
                  pl.BlockSpec((ROW_TILE // MIX_CHUNK, SSM_WIDTH, MIX_CHUNK), lambda i: (i, 0, 0)),
                  _row_spec(DIFF_WIDTH),
                  _layer_spec((D_MODEL, D_MODEL), layer)] + _ffn_weight_specs(layer),
        out_specs=_row_spec(D_MODEL),
        out_shape=jax.ShapeDtypeStruct((n, D_MODEL), F32),
        compiler_params=_params(("parallel",)),
        name="outproj_ffn",
    )(x, y_m, y_s, y_d, w_out, ffn_norm_w.reshape(1, D_MODEL), wg, wu, wd)


def _regroup_w_in(w_in):
    sizes = (MLSTM_WIDTH, MLSTM_WIDTH, MLSTM_WIDTH, MLSTM_WIDTH, MLSTM_HEADS, MLSTM_HEADS,
             SSM_WIDTH, SSM_CONV_DIM, SSM_HEADS, 2 * DIFF_HEADS * DIFF_QK_DIM, 2 * DIFF_HEADS * DIFF_QK_DIM,
             DIFF_WIDTH)
    offs = [0]
    for s in sizes:
        offs.append(offs[-1] + s)
    w_in = w_in.astype(BF16)
    mq, mk, mv, mo, mi, mf, z, xbc, dt, dq, dk, dv = [w_in[..., offs[i]:offs[i + 1]] for i in range(len(sizes))]

    def pad(width):
        return jnp.zeros(w_in.shape[:-1] + (width,), BF16)

    small = [mi, pad(_LANE_F - _LANE_I - MLSTM_HEADS), mf, pad(_LANE_DT - _LANE_F - MLSTM_HEADS), dt,
             pad(LANES - _LANE_DT - SSM_HEADS)]
    return jnp.concatenate([mq, mk, mv, mo, z, xbc] + small + [dq, dk, dv], axis=-1)


def _small_row(values_by_lane):
    row = jnp.zeros((LANES,), F32)
    for lane, vals in values_by_lane:
        row = lax.dynamic_update_slice(row, vals.astype(F32), (lane,))
    return row.reshape(1, LANES)


def kernel(x, ffn1_norm_w, ffn1_w_gate, ffn1_w_up, ffn1_w_down, mix_norm_w, w_in, mlstm_gate_bias, mlstm_norm_w, ssm_conv_w, ssm_conv_b, ssm_dt_bias, ssm_A_log, ssm_D, ssm_norm_w, diff_q_norm_w, diff_k_norm_w, diff_lambda, diff_subln_w, rel_bias, w_out, ffn2_norm_w, ffn2_w_gate, ffn2_w_up, ffn2_w_down):
    batch, seq, d = x.shape
    assert d == D_MODEL and MIX_CHUNK == ATT_TILE and seq % ROW_TILE == 0 and ROW_TILE % MIX_CHUNK == 0
    xf = x.reshape(batch * seq, D_MODEL)
    g32 = _block_diag(DIFF_WIDTH, DIFF_QK_DIM, 1.0 / DIFF_QK_DIM)
    g64 = _block_diag(DIFF_WIDTH, DIFF_V_DIM, 1.0 / DIFF_V_DIM)
    ffn1 = [w.astype(BF16) for w in (ffn1_w_gate, ffn1_w_up, ffn1_w_down)]
    ffn2 = [w.astype(BF16) for w in (ffn2_w_gate, ffn2_w_up, ffn2_w_down)]
    w_in_all = _regroup_w_in(w_in)
    w_out_all = w_out.astype(BF16)
    for l in range(DEPTH):
        qkw_row = jnp.concatenate([jnp.tile(diff_q_norm_w[l].reshape(-1), DIFF_HEADS) * (DIFF_QK_DIM ** -0.5 * LOG2E),
                                   jnp.tile(diff_k_norm_w[l].reshape(-1), DIFF_HEADS)]).reshape(1, 2 * DIFF_WIDTH)
        bias_row = _small_row([(_LANE_I, mlstm_gate_bias[l, 0]), (_LANE_F, mlstm_gate_bias[l, 1]),
                               (_LANE_DT, ssm_dt_bias[l])])
        (xf, mqt, mk, mvt, mot, gt, zt, sb, sct, sxt, sxdt, sa, qt, kn, vt) = _ffn_inproj(
            xf, ffn1_norm_w[l], *ffn1, mix_norm_w[l], w_in_all, qkw_row, g32,
            jnp.broadcast_to(bias_row.reshape(LANES, 1), (LANES, ROW_TILE)),
            jnp.broadcast_to(ssm_A_log[l][:, None], (SSM_HEADS, ROW_TILE)),
            ssm_conv_w[l], ssm_conv_b[l].reshape(1, SSM_CONV_DIM), l, seq)
        y_m = _mlstm(mqt, mk, mvt, mot, gt, jnp.broadcast_to(mlstm_norm_w[l][:, None], (MLSTM_WIDTH, MIX_CHUNK)),
                     batch, seq)
        y_s = _ssd(sb, sct, sxt, sxdt, zt, sa,
                   jnp.broadcast_to(jnp.repeat(ssm_D[l], SSM_HEAD_DIM)[:, None], (SSM_WIDTH, MIX_CHUNK)),
                   jnp.broadcast_to(ssm_norm_w[l][:, None], (SSM_WIDTH, MIX_CHUNK)), batch, seq)
        lam_init = 0.8 - 0.6 * math.exp(-0.3 * l)
        y_d = _diff_attn(qt, kn, vt, rel_bias, diff_lambda[l], jnp.tile(diff_subln_w[l], DIFF_HEADS).reshape(1, DIFF_WIDTH),
                         g64, lam_init, batch, seq)
        xf = _outproj_ffn(xf, y_m, y_s, y_d, w_out_all, ffn2_norm_w[l], *ffn2, l)
    return xf.reshape(batch, seq, D_MODEL)
```

```python
import functools
import math

import jax
import jax.numpy as jnp
from jax import lax
from jax.experimental import pallas as pl
from jax.experimental.pallas import tpu as pltpu

F32 = jnp.float32
BF16 = jnp.bfloat16

D_MODEL = 1024
DEPTH = 2
D_FF = 2816
MLSTM_HEADS = 4
MLSTM_HEAD_DIM = 64
MLSTM_WIDTH = MLSTM_HEADS * MLSTM_HEAD_DIM
SSM_HEADS = 8
SSM_HEAD_DIM = 64
SSM_WIDTH = SSM_HEADS * SSM_HEAD_DIM
SSM_STATE = 128
SSM_GROUPS = 2
SSM_CONV = 4
SSM_CONV_DIM = SSM_WIDTH + 2 * SSM_GROUPS * SSM_STATE
DIFF_HEADS = 4
DIFF_QK_DIM = 32
DIFF_V_DIM = 64
DIFF_WIDTH = DIFF_HEADS * DIFF_V_DIM
REL_BUCKETS = 32
REL_MAX_DIST = 128
NORM_EPS = 1e-6

LANES = 128
ROW_TILE = 512
FFN_CHUNK = 1408
MIX_CHUNK = 256
MIX_STEP_CHUNKS = 2
ATT_TILE = 256
VMEM_LIMIT = 56 * 1024 * 1024
NEG_BIG = -1e30
LOG2E = math.log2(math.e)
ONES_ROWS = 16

_C_M = 0
_C_Z = _C_M + 4 * MLSTM_WIDTH
_C_XBC = _C_Z + SSM_WIDTH
_C_SMALL = _C_XBC + SSM_CONV_DIM
_C_QK = _C_SMALL + LANES
_C_V = _C_QK + 4 * DIFF_HEADS * DIFF_QK_DIM
_C_END = _C_V + DIFF_WIDTH
_LANE_I = 0
_LANE_F = 8
_LANE_DT = 16
GATE_ROWS = 24
D_IN = 4 * MLSTM_WIDTH + 2 * MLSTM_HEADS + SSM_WIDTH + SSM_CONV_DIM + SSM_HEADS + 3 * DIFF_WIDTH


def _regroup_plan():
    sizes = (4 * MLSTM_WIDTH, MLSTM_HEADS, MLSTM_HEADS, SSM_WIDTH, SSM_CONV_DIM, SSM_HEADS, 3 * DIFF_WIDTH)
    dests = (_C_M, _C_SMALL + _LANE_I, _C_SMALL + _LANE_F, _C_Z, _C_XBC, _C_SMALL + _LANE_DT, _C_QK)
    plan, src = [], 0
    for dst, width in zip(dests, sizes):
        plan.append((dst, src, width))
        src += width
    assert src == D_IN
    return plan


def _dot(a, b):
    return jnp.dot(a, b, preferred_element_type=F32)


def _sigmoid(x):
    return 1.0 / (1.0 + jnp.exp(-x))


def _softplus(x):
    return jnp.maximum(x, 0.0) + jnp.log(1.0 + jnp.exp(-jnp.abs(x)))


def _rms_rows(x, w_row):
    ms = jnp.mean(x * x, axis=-1, keepdims=True)
    return x * lax.rsqrt(ms + NORM_EPS) * w_row


def _group_mean_sq(x, gmat):
    sq = x * x
    hi = sq.astype(BF16)
    lo = (sq - hi.astype(F32)).astype(BF16)
    return _dot(hi, gmat) + _dot(lo, gmat)


def _scan_lanes(x, op, identity):
    n = x.shape[1]
    lane = lax.broadcasted_iota(jnp.int32, x.shape, 1)
    step = 1
    while step < n:
        x = op(x, jnp.where(lane >= step, pltpu.roll(x, step, axis=1), identity))
        step *= 2
    return x


def _block_diag(n, group, value):
    r = jnp.arange(n) // group
    return jnp.where(r[:, None] == r[None, :], value, 0.0).astype(BF16)


def _const_spec(shape):
    nd = len(shape)
    return pl.BlockSpec(shape, lambda *_: (0,) * nd)


def _layer_spec(shape, layer):
    nd = len(shape)
    return pl.BlockSpec((None,) + tuple(shape), lambda *_: (layer,) + (0,) * nd, pipeline_mode=pl.Buffered(1))


def _params(sem):
    return pltpu.CompilerParams(dimension_semantics=sem, vmem_limit_bytes=VMEM_LIMIT)


def _ffn_block(x, nw_ref, wg_ref, wu_ref, wd_ref):
    xn = _rms_rows(x, nw_ref[...]).astype(BF16)
    acc = None
    for c in range(D_FF // FFN_CHUNK):
        cols = slice(c * FFN_CHUNK, (c + 1) * FFN_CHUNK)
        g = _dot(xn, wg_ref[:, cols])
        u = _dot(xn, wu_ref[:, cols])
        h = (g * _sigmoid(g) * u).astype(BF16)
        part = _dot(h, wd_ref[cols, :])
        acc = part if acc is None else acc + part
    return x + 0.5 * acc


def _row_spec(width):
    return pl.BlockSpec((ROW_TILE, width), lambda i: (i, 0))


def _ffn_weight_specs(layer):
    return [_const_spec((1, D_MODEL)), _layer_spec((D_MODEL, D_FF), layer),
            _layer_spec((D_MODEL, D_FF), layer), _layer_spec((D_FF, D_MODEL), layer)]


def _store_transposed_tiles(out_ref, a):
    a_t = a.T.astype(BF16)
    for t in range(ROW_TILE // ATT_TILE):
        out_ref[t] = a_t[:, t * ATT_TILE:(t + 1) * ATT_TILE]


def _ffn_inproj_kernel(x_ref, fnw_ref, wg_ref, wu_ref, wd_ref, nw_ref, w_ref, qkw_ref, g32_ref, gbias_ref,
                       alog_ref, convw_ref, convb_ref,
                       x_out_ref, mqt_ref, mk_ref, mvt_ref, mot_ref, gt_ref,
                       zt_ref, sb_ref, sct_ref, sxt_ref, sxdt_ref, sa_ref, qt_ref, k_ref, vt_ref,
                       halo_scr, w_scr, *, tiles_per_seq):
    @pl.when(pl.program_id(0) == 0)
    def _():
        w_scr[:, _C_SMALL:_C_QK] = jnp.zeros((D_MODEL, LANES), BF16)
        for dst, src, width in _regroup_plan():
            w_scr[:, dst:dst + width] = w_ref[:, src:src + width]

    @pl.when(pl.program_id(0) % tiles_per_seq == 0)
    def _():
        halo_scr[...] = jnp.zeros_like(halo_scr)

    x = _ffn_block(x_ref[...], fnw_ref, wg_ref, wu_ref, wd_ref)
    x_out_ref[...] = x
    xn = _rms_rows(x, nw_ref[...]).astype(BF16)

    def proj(lo, hi):
        return _dot(xn, w_scr[:, lo:hi])

    half = (_C_V - _C_QK) // 2
    p_xbc = proj(_C_XBC, _C_SMALL)
    p_small = proj(_C_SMALL, _C_QK)
    p_q, p_k = proj(_C_QK, _C_QK + half), proj(_C_QK + half, _C_V)
    p_mq = proj(_C_M, _C_M + MLSTM_WIDTH)
    p_mk = proj(_C_M + MLSTM_WIDTH, _C_M + 2 * MLSTM_WIDTH)
    p_mv = proj(_C_M + 2 * MLSTM_WIDTH, _C_M + 3 * MLSTM_WIDTH)
    p_mo = proj(_C_M + 3 * MLSTM_WIDTH, _C_Z)
    p_z = proj(_C_Z, _C_XBC)
    p_v = proj(_C_V, _C_END)

    qk = [a * lax.rsqrt(_group_mean_sq(a, g32_ref[...]) + NORM_EPS) * qkw_ref[:, i * half:(i + 1) * half]
          for i, a in enumerate((p_q, p_k))]
    _store_transposed_tiles(qt_ref, qk[0])
    k_ref[...] = qk[1].astype(BF16)
    _store_transposed_tiles(vt_ref, p_v)

    ext = jnp.concatenate([halo_scr[...], p_xbc], axis=0)
    halo_scr[...] = p_xbc[ROW_TILE - 8:, :]
    conv = convb_ref[...] + convw_ref[SSM_CONV - 1:SSM_CONV, :] * p_xbc
    for j in range(1, SSM_CONV):
        conv = conv + convw_ref[SSM_CONV - 1 - j:SSM_CONV - j, :] * ext[8 - j:8 - j + ROW_TILE, :]
    xa = conv * _sigmoid(conv)
    sb_ref[...] = xa[:, SSM_WIDTH:SSM_WIDTH + SSM_GROUPS * SSM_STATE].astype(BF16)
    _store_transposed_tiles(sct_ref, xa[:, SSM_WIDTH + SSM_GROUPS * SSM_STATE:])
    small_t = p_small.T + gbias_ref[...]
    dt = _softplus(small_t[_LANE_DT:_LANE_DT + SSM_HEADS])
    log_decay = dt * (-jnp.exp(alog_ref[...]))
    xs_t = xa[:, :SSM_WIDTH].T
    xdt_t = xs_t * jnp.concatenate([jnp.broadcast_to(dt[h:h + 1], (SSM_HEAD_DIM, ROW_TILE))
                                    for h in range(SSM_HEADS)], axis=0)
    for t in range(ROW_TILE // MIX_CHUNK):
        cols = slice(t * MIX_CHUNK, (t + 1) * MIX_CHUNK)
        sxt_ref[t] = xs_t[:, cols].astype(BF16)
        sxdt_ref[t] = xdt_t[:, cols].astype(BF16)
        sa_ref[t] = _scan_lanes(log_decay[:, cols], jnp.add, 0.0)
    _store_transposed_tiles(zt_ref, p_z)

    _store_transposed_tiles(mqt_ref, p_mq)
    mk_ref[...] = p_mk.astype(BF16)
    _store_transposed_tiles(mvt_ref, p_mv)
    _store_transposed_tiles(mot_ref, p_mo)
    gates = small_t[:_LANE_DT]
    for t in range(ROW_TILE // MIX_CHUNK):
        i_pre = gates[_LANE_I:_LANE_I + 8, t * MIX_CHUNK:(t + 1) * MIX_CHUNK]
        f_pre = gates[_LANE_F:_LANE_F + 8, t * MIX_CHUNK:(t + 1) * MIX_CHUNK]
        log_f = jnp.minimum(f_pre, 0.0) - jnp.log(1.0 + jnp.exp(-jnp.abs(f_pre)))
        b = _scan_lanes(log_f, jnp.add, 0.0)
        c = i_pre - b
        gt_ref[t] = jnp.concatenate([c, _scan_lanes(c, jnp.maximum, -jnp.inf), b], axis=0)


def _ffn_inproj(x, ffn_norm_w, wg, wu, wd, norm_w, w_all, qkw_row, g32, gate_bias_rows, alog_rows, conv_w, conv_b_row,
                layer, seq):
    n = x.shape[0]
    tiles = ROW_TILE // ATT_TILE

    def t_out(width, dtype=BF16):
        return (pl.BlockSpec((tiles, width, ATT_TILE), lambda i: (i, 0, 0)),
                jax.ShapeDtypeStruct((n // ATT_TILE, width, ATT_TILE), dtype))

    def r_out(width, dtype=BF16):
        return _row_spec(width), jax.ShapeDtypeStruct((n, width), dtype)

    outs = [r_out(D_MODEL, F32),
            t_out(MLSTM_WIDTH), r_out(MLSTM_WIDTH), t_out(MLSTM_WIDTH), t_out(MLSTM_WIDTH), t_out(GATE_ROWS, F32),
            t_out(SSM_WIDTH), r_out(SSM_GROUPS * SSM_STATE), t_out(SSM_GROUPS * SSM_STATE), t_out(SSM_WIDTH),
            t_out(SSM_WIDTH), t_out(SSM_HEADS, F32),
            t_out(DIFF_WIDTH), r_out(DIFF_WIDTH), t_out(DIFF_WIDTH)]
    return pl.pallas_call(
        functools.partial(_ffn_inproj_kernel, tiles_per_seq=seq // ROW_TILE),
        grid=(n // ROW_TILE,),
        in_specs=[_row_spec(D_MODEL)] + _ffn_weight_specs(layer)
        + [_const_spec((1, D_MODEL)), _layer_spec((D_MODEL, D_IN), layer),
           _const_spec((1, 2 * DIFF_WIDTH)), _const_spec((DIFF_WIDTH, DIFF_WIDTH)),
           _const_spec((LANES, ROW_TILE)), _const_spec((SSM_HEADS, ROW_TILE)),
           _const_spec((SSM_CONV, SSM_CONV_DIM)), _const_spec((1, SSM_CONV_DIM))],
        out_specs=[o[0] for o in outs],
        out_shape=[o[1] for o in outs],
        scratch_shapes=[pltpu.VMEM((8, SSM_CONV_DIM), F32), pltpu.VMEM((D_MODEL, _C_END), BF16)],
        compiler_params=_params(("arbitrary",)),
        name="ffn_inproj",
    )(x, ffn_norm_w.reshape(1, D_MODEL), wg, wu, wd, norm_w.reshape(1, D_MODEL), w_all, qkw_row, g32,
      gate_bias_rows, alog_rows, conv_w, conv_b_row)


def _mlstm_kernel(qt_ref, k_ref, vt_ref, ot_ref, gt_ref, nw_ref, yt_ref, ct_scr, nt_scr, m_scr):
    @pl.when(pl.program_id(1) == 0)
    def _():
        ct_scr[...] = jnp.zeros_like(ct_scr)
        nt_scr[...] = jnp.zeros_like(nt_scr)
        m_scr[...] = jnp.zeros_like(m_scr)

    for t in range(MIX_STEP_CHUNKS):
        _mlstm_chunk(t, qt_ref, k_ref, vt_ref, ot_ref, gt_ref, nw_ref, yt_ref, ct_scr, nt_scr, m_scr)


def _mlstm_chunk(t, qt_ref, k_ref, vt_ref, ot_ref, gt_ref, nw_ref, yt_ref, ct_scr, nt_scr, m_scr):
    L = MIX_CHUNK
    H = MLSTM_HEADS
    W = MLSTM_HEAD_DIM
    qt = qt_ref[t]
    ks = k_ref[t * L:(t + 1) * L, :] * (MLSTM_HEAD_DIM ** -0.5)
    vt = vt_ref[t]
    gt = gt_ref[t]
    c, c_max, b = gt[0:8], gt[8:16], gt[16:24]
    m_prev = m_scr[...]
    big_m = jnp.maximum(m_prev, c_max)
    inter = jnp.exp(m_prev - big_m)
    floor = jnp.exp(-(b + big_m))
    b_last = jnp.broadcast_to(b[:, L - 1:L], b.shape)
    m_new = b_last + jnp.maximum(m_prev, jnp.broadcast_to(c_max[:, L - 1:L], b.shape))
    w = jnp.exp(b_last + c - m_new)
    decay = jnp.exp(b_last + m_prev - m_new)

    c2_cols = jnp.concatenate([c * LOG2E, jnp.zeros((LANES - 8, L), F32)], axis=0).T
    big_m2 = big_m * LOG2E
    causal = (lax.broadcasted_iota(jnp.int32, (L, L), 0) <= lax.broadcasted_iota(jnp.int32, (L, L), 1))
    row_head = lax.broadcasted_iota(jnp.int32, (H * W, L), 0) // W
    qk = [_dot(ks, jnp.where(row_head == h, qt, jnp.zeros_like(qt))) for h in range(H)]
    nums, dens = [], []
    for h in range(H):
        d = jnp.exp2(jnp.where(causal, c2_cols[:, h:h + 1] - big_m2[h:h + 1, :], -jnp.inf))
        s = qk[h] * d
        dens.append(jnp.sum(s, axis=0, keepdims=True))
        nums.append(_dot(vt[h * W:(h + 1) * W, :], s.astype(BF16)))
    cq = _dot(ct_scr[...].astype(BF16), qt)
    nq = _dot(nt_scr[...].astype(BF16), qt)
    outs = []
    for h in range(H):
        num = nums[h] + inter[h:h + 1] * cq[h * W:(h + 1) * W]
        den = dens[h] + inter[h:h + 1] * nq[h:h + 1]
        hh = num / jnp.maximum(jnp.abs(den), floor[h:h + 1])
        outs.append(hh * lax.rsqrt(jnp.mean(hh * hh, axis=0, keepdims=True) + NORM_EPS))
    hn = jnp.concatenate(outs, axis=0) * nw_ref[...]
    yt_ref[t] = (_sigmoid(ot_ref[t].astype(F32)) * hn).astype(BF16)

    w_full = jnp.concatenate([jnp.broadcast_to(w[h:h + 1], (W, L)) for h in range(H)], axis=0)
    g_new = _dot((vt.astype(F32) * w_full).astype(BF16), ks)
    decay_full = jnp.concatenate([jnp.broadcast_to(decay[h:h + 1, :1], (W, H * W)) for h in range(H)], axis=0)
    same_head = (lax.broadcasted_iota(jnp.int32, (H * W, H * W), 0) // W
                 == lax.broadcasted_iota(jnp.int32, (H * W, H * W), 1) // W)
    ct_scr[...] = decay_full * ct_scr[...] + jnp.where(same_head, g_new, 0.0)
    w16 = jnp.concatenate([w, jnp.zeros_like(w)], axis=0).astype(BF16)
    n_new = _dot(w16, ks)
    own_lanes = (lax.broadcasted_iota(jnp.int32, (16, H * W), 0)
                 == lax.broadcasted_iota(jnp.int32, (16, H * W), 1) // W)
    decay16 = jnp.broadcast_to(jnp.concatenate([decay[:, :1], jnp.zeros((8, 1), F32)], axis=0), (16, H * W))
    nt_scr[...] = decay16 * nt_scr[...] + jnp.where(own_lanes, n_new, 0.0)
    m_scr[...] = m_new


def _mlstm(mqt, mk, mvt, mot, gt, norm_w_rows, batch, seq):
    ns = seq // (MIX_CHUNK * MIX_STEP_CHUNKS)
    n = batch * seq
    t_spec = pl.BlockSpec((MIX_STEP_CHUNKS, MLSTM_WIDTH, MIX_CHUNK), lambda b, c: (b * ns + c, 0, 0))
    return pl.pallas_call(
        _mlstm_kernel,
        grid=(batch, ns),
        in_specs=[t_spec, pl.BlockSpec((MIX_STEP_CHUNKS * MIX_CHUNK, MLSTM_WIDTH), lambda b, c: (b * ns + c, 0)),
                  t_spec, t_spec,
                  pl.BlockSpec((MIX_STEP_CHUNKS, GATE_ROWS, MIX_CHUNK), lambda b, c: (b * ns + c, 0, 0)),
                  _const_spec((MLSTM_WIDTH, MIX_CHUNK))],
        out_specs=t_spec,
        out_shape=jax.ShapeDtypeStruct((n // MIX_CHUNK, MLSTM_WIDTH, MIX_CHUNK), BF16),
        scratch_shapes=[pltpu.VMEM((MLSTM_WIDTH, MLSTM_WIDTH), F32),
                        pltpu.VMEM((16, MLSTM_WIDTH), F32),
                        pltpu.VMEM((8, MIX_CHUNK), F32)],
        compiler_params=_params(("arbitrary", "arbitrary")),
        name="mlstm",
    )(mqt, mk, mvt, mot, gt, norm_w_rows)


def _ssd_kernel(b_ref, ct_ref, xt_ref, xdt_ref, zt_ref, a_ref, dskip_ref, nw_ref, yt_ref, st_scr):
    @pl.when(pl.program_id(1) == 0)
    def _():
        st_scr[...] = jnp.zeros_like(st_scr)

    for t in range(MIX_STEP_CHUNKS):
        _ssd_chunk(t, b_ref, ct_ref, xt_ref, xdt_ref, zt_ref, a_ref, dskip_ref, nw_ref, yt_ref, st_scr)


def _ssd_chunk(t, b_ref, ct_ref, xt_ref, xdt_ref, zt_ref, a_ref, dskip_ref, nw_ref, yt_ref, st_scr):
    L = MIX_CHUNK
    P = SSM_HEAD_DIM
    HG = SSM_HEADS // SSM_GROUPS
    GW = HG * P
    NS = SSM_STATE
    a = a_ref[t]
    a_last = jnp.broadcast_to(a[:, L - 1:L], a.shape)
    exp_a = jnp.exp(a)
    d_in = jnp.exp(a_last - a)
    chunk_decay = jnp.exp(a_last)
    a2 = a * LOG2E
    a2_cols = jnp.concatenate([a2, jnp.zeros((LANES - SSM_HEADS, L), F32)], axis=0).T
    causal = (lax.broadcasted_iota(jnp.int32, (L, L), 0) <= lax.broadcasted_iota(jnp.int32, (L, L), 1))
    bn = b_ref[t * L:(t + 1) * L, :]
    ct = ct_ref[t]
    xdt = xdt_ref[t]

    def per_head_rows(rows, grp, width):
        return jnp.concatenate([jnp.broadcast_to(rows[grp * HG + i:grp * HG + i + 1, :width], (P, width))
                                for i in range(HG)], axis=0)

    scores, y_off = [], []
    for grp in range(SSM_GROUPS):
        bg = bn[:, grp * NS:(grp + 1) * NS]
        cg = ct[grp * NS:(grp + 1) * NS, :]
        scores.append(_dot(bg, cg))
        st = st_scr[grp]
        y_off.append(_dot(st.astype(BF16), cg))
        x_in = (xdt[grp * GW:(grp + 1) * GW, :].astype(F32) * per_head_rows(d_in, grp, L)).astype(BF16)
        st_scr[grp] = per_head_rows(chunk_decay, grp, NS) * st + _dot(x_in, bg)
    for grp in range(SSM_GROUPS):
        y_diag = []
        for i in range(HG):
            h = grp * HG + i
            seg = jnp.exp2(jnp.where(causal, a2[h:h + 1, :] - a2_cols[:, h:h + 1], -jnp.inf))
            p = (scores[grp] * seg).astype(BF16)
            y_diag.append(_dot(xdt[h * P:(h + 1) * P, :], p))
        rows = slice(grp * GW, (grp + 1) * GW)
        y = (jnp.concatenate(y_diag, axis=0) + y_off[grp] * per_head_rows(exp_a, grp, L)
             + dskip_ref[rows, :] * xt_ref[t, rows, :].astype(F32))
        zg = zt_ref[t, rows, :].astype(F32)
        y = y * (zg * _sigmoid(zg))
        ms = jnp.mean(y * y, axis=0, keepdims=True)
        yt_ref[t, rows, :] = (y * lax.rsqrt(ms + NORM_EPS) * nw_ref[rows, :]).astype(BF16)


def _ssd(sb, sct, sxt, sxdt, zt, sa, dskip_rows, norm_w_rows, batch, seq):
    ns = seq // (MIX_CHUNK * MIX_STEP_CHUNKS)
    n = batch * seq

    def t_spec(width):
        return pl.BlockSpec((MIX_STEP_CHUNKS, width, MIX_CHUNK), lambda b, c: (b * ns + c, 0, 0))

    return pl.pallas_call(
        _ssd_kernel,
        grid=(batch, ns),
        in_specs=[pl.BlockSpec((MIX_STEP_CHUNKS * MIX_CHUNK, SSM_GROUPS * SSM_STATE), lambda b, c: (b * ns + c, 0)),
                  t_spec(SSM_GROUPS * SSM_STATE), t_spec(SSM_WIDTH), t_spec(SSM_WIDTH), t_spec(SSM_WIDTH),
                  t_spec(SSM_HEADS), _const_spec((SSM_WIDTH, MIX_CHUNK)), _const_spec((SSM_WIDTH, MIX_CHUNK))],
        out_specs=t_spec(SSM_WIDTH),
        out_shape=jax.ShapeDtypeStruct((n // MIX_CHUNK, SSM_WIDTH, MIX_CHUNK), BF16),
        scratch_shapes=[pltpu.VMEM((SSM_GROUPS, SSM_WIDTH // SSM_GROUPS, SSM_STATE), F32)],
        compiler_params=_params(("arbitrary", "arbitrary")),
        name="ssd",
    )(sb, sct, sxt, sxdt, zt, sa, dskip_rows, norm_w_rows)


def _t5_bias_tile(rel, relb_ref, head):
    n = jnp.maximum(rel, 0)
    max_exact = REL_BUCKETS // 2
    nf = jnp.maximum(n, 1).astype(F32)
    large = max_exact + (jnp.log(nf / max_exact) / math.log(REL_MAX_DIST / max_exact)
                         * (REL_BUCKETS - max_exact)).astype(jnp.int32)
    large = jnp.minimum(large, REL_BUCKETS - 1)
    bucket = jnp.where(n < max_exact, n, large)
    far = relb_ref[REL_BUCKETS - 1, head]
    bias = jnp.zeros(rel.shape, F32)
    for bkt in range(REL_BUCKETS - 1):
        bias = jnp.where(bucket == bkt, (relb_ref[bkt, head] - far) * LOG2E, bias)
    return jnp.where(rel >= 0, bias, NEG_BIG)


def _attn_kernel(relb_ref, qt_ref, k_ref, vt_ref, lam_ref, subw_ref, g64_ref,
                 y_ref, bias_scr, qm_scr, m_scr, acc_scr, sa_scr, sb_scr, *, lam_init):
    T = ATT_TILE
    H = DIFF_HEADS
    DV = DIFF_V_DIM
    i = pl.program_id(1)

    @pl.when((pl.program_id(0) == 0) & (i == 0))
    def _():
        rel = lax.broadcasted_iota(jnp.int32, (T, T), 1) - lax.broadcasted_iota(jnp.int32, (T, T), 0)
        for h in range(H):
            bias_scr[0, h] = _t5_bias_tile(rel, relb_ref, h)
            bias_scr[1, h] = _t5_bias_tile(rel + T, relb_ref, h)

    qt = qt_ref[0]
    comp_of_row = lax.broadcasted_iota(jnp.int32, (DIFF_WIDTH, T), 0) // DIFF_QK_DIM
    for hc in range(2 * H):
        qm_scr[hc] = jnp.where(comp_of_row == hc, qt, jnp.zeros_like(qt))
    m_scr[...] = jnp.full(m_scr.shape, NEG_BIG, F32)
    acc_scr[...] = jnp.zeros_like(acc_scr)
    ones_rows = jnp.ones((ONES_ROWS, T), BF16)

    def logits_one(s_ref, kt, hc):
        s_ref[hc] = _dot(kt, qm_scr[hc])

    def key_tile(j):
        return k_ref[pl.ds(pl.multiple_of(j * T, T), T), :]

    def logits_into(s_ref, j):
        kt = key_tile(j)
        for hc in range(2 * H):
            logits_one(s_ref, kt, hc)

    def consume_one(s_ref, vt, hc, bias_idx):
        h = hc // 2
        s = s_ref[hc]
        if bias_idx is not None:
            s = bias_scr[bias_idx, h] + s
        m_prev = m_scr[hc]
        m_next = jnp.maximum(m_prev, jnp.max(s, axis=0, keepdims=True))
        alpha = jnp.exp2(m_prev - m_next)
        p = jnp.exp2(s - m_next).astype(BF16)
        m_scr[hc] = m_next
        v_aug = jnp.concatenate([vt[h * DV:(h + 1) * DV, :], ones_rows], axis=0)
        acc_scr[hc] = alpha * acc_scr[hc] + _dot(v_aug, p)

    def consume(s_ref, j, bias_idx):
        vt = vt_ref[j]
        for hc in range(2 * H):
            consume_one(s_ref, vt, hc, bias_idx)

    LEAD = 2

    def step(s_next, j_next, s_cur, j_cur, bias_idx):
        kt = key_tile(j_next)
        vt = vt_ref[j_cur]
        for hc in range(LEAD):
            logits_one(s_next, kt, hc)
        for hc in range(2 * H):
            if hc + LEAD < 2 * H:
                logits_one(s_next, kt, hc + LEAD)
            consume_one(s_cur, vt, hc, bias_idx)

    DIAG, NEAR = 0, 1
    n_far = jnp.maximum(i - 1, 0)
    logits_into(sa_scr, 0)

    def far_pair(t, carry):
        j = 2 * t
        step(sb_scr, j + 1, sa_scr, j, None)
        step(sa_scr, j + 2, sb_scr, j + 1, None)
        return carry

    lax.fori_loop(0, n_far // 2, far_pair, 0)
    u = 2 * (n_far // 2)
    odd = n_far % 2 == 1

    @pl.when(odd)
    def _():
        step(sb_scr, u + 1, sa_scr, u, None)
        step(sa_scr, u + 2, sb_scr, u + 1, NEAR)
        consume(sa_scr, u + 2, DIAG)

    @pl.when(jnp.logical_not(odd) & (i >= 1))
    def _():
        step(sb_scr, u + 1, sa_scr, u, NEAR)
        consume(sb_scr, u + 1, DIAG)

    @pl.when(i == 0)
    def _():
        consume(sa_scr, 0, DIAG)

    lam_p = lam_ref[...]
    lam = (jnp.exp(jnp.sum(lam_p[0:1] * lam_p[1:2], axis=1, keepdims=True))
           - jnp.exp(jnp.sum(lam_p[2:3] * lam_p[3:4], axis=1, keepdims=True)) + lam_init)

    def normalised(hc):
        return acc_scr[hc, :DV, :] / acc_scr[hc, DV:DV + 1, :]

    o_t = jnp.concatenate([normalised(2 * h) - lam * normalised(2 * h + 1) for h in range(H)], axis=0)
    o = o_t.T
    ms = _group_mean_sq(o, g64_ref[...])
    y_ref[...] = (o * lax.rsqrt(ms + NORM_EPS) * subw_ref[...] * (1.0 - lam_init)).astype(BF16)


def _diff_attn(qt, kn, vt, rel_bias, lambdas, subw_row, g64, lam_init, batch, seq):
    nq = seq // ATT_TILE
    n = batch * seq
    return pl.pallas_call(
        functools.partial(_attn_kernel, lam_init=lam_init),
        grid=(batch, nq),
        in_specs=[pl.BlockSpec(memory_space=pltpu.SMEM),
                  pl.BlockSpec((1, DIFF_WIDTH, ATT_TILE), lambda b, i: (b * nq + i, 0, 0)),
                  pl.BlockSpec((seq, DIFF_WIDTH), lambda b, i: (b, 0)),
                  pl.BlockSpec((nq, DIFF_WIDTH, ATT_TILE), lambda b, i: (b, 0, 0)),
                  _const_spec((4, DIFF_QK_DIM)), _const_spec((1, DIFF_WIDTH)),
                  _const_spec((DIFF_WIDTH, DIFF_WIDTH))],
        out_specs=pl.BlockSpec((ATT_TILE, DIFF_WIDTH), lambda b, i: (b * nq + i, 0)),
        out_shape=jax.ShapeDtypeStruct((n, DIFF_WIDTH), BF16),
        scratch_shapes=[pltpu.VMEM((2, DIFF_HEADS, ATT_TILE, ATT_TILE), F32),
                        pltpu.VMEM((2 * DIFF_HEADS, DIFF_WIDTH, ATT_TILE), BF16),
                        pltpu.VMEM((2 * DIFF_HEADS, 1, ATT_TILE), F32),
                        pltpu.VMEM((2 * DIFF_HEADS, DIFF_V_DIM + ONES_ROWS, ATT_TILE), F32),
                        pltpu.VMEM((2 * DIFF_HEADS, ATT_TILE, ATT_TILE), F32),
                        pltpu.VMEM((2 * DIFF_HEADS, ATT_TILE, ATT_TILE), F32)],
        compiler_params=_params(("arbitrary", "arbitrary")),
        name="diff_attn",
    )(rel_bias, qt, kn, vt, lambdas, subw_row, g64)


def _untransposed(yt_ref):
    return jnp.concatenate([yt_ref[t].astype(F32).T for t in range(yt_ref.shape[0])], axis=0).astype(BF16)


def _outproj_ffn_kernel(x_ref, ymt_ref, yst_ref, yd_ref, w_ref, fnw_ref, wg_ref, wu_ref, wd_ref, o_ref):
    y = jnp.concatenate([_untransposed(ymt_ref), _untransposed(yst_ref), yd_ref[...]], axis=1)
    x = x_ref[...] + _dot(y, w_ref[...])
    o_ref[...] = _ffn_block(x, fnw_ref, wg_ref, wu_ref, wd_ref)


def _outproj_ffn(x, y_m, y_s, y_d, w_out, ffn_norm_w, wg, wu, wd, layer):
    n = x.shape[0]
    return pl.pallas_call(
        _outproj_ffn_kernel,
        grid=(n // ROW_TILE,),
        in_specs=[_row_spec(D_MODEL),
                  pl.BlockSpec((ROW_TILE // MIX_CHUNK, MLSTM_WIDTH, MIX_CHUNK), lambda i: (i, 0, 0)),
                  pl.BlockSpec((ROW_TILE // MIX_CHUNK, SSM_WIDTH, MIX_CHUNK), lambda i: (i, 0, 0)),
                  _row_spec(DIFF_WIDTH),
                  _layer_spec((D_MODEL, D_MODEL), layer)] + _ffn_weight_specs(layer),
        out_specs=_row_spec(D_MODEL),
        out_shape=jax.ShapeDtypeStruct((n, D_MODEL), F32),
        compiler_params=_params(("parallel",)),
        name="outproj_ffn",
    )(x, y_m, y_s, y_d, w_out, ffn_norm_w.reshape(1, D_MODEL), wg, wu, wd)


def _small_row(values_by_lane):
    row = jnp.zeros((LANES,), F32)
    for lane, vals in values_by_lane:
        row = lax.dynamic_update_slice(row, vals.astype(F32), (lane,))
    return row.reshape(1, LANES)


def kernel(x, ffn1_norm_w, ffn1_w_gate, ffn1_w_up, ffn1_w_down, mix_norm_w, w_in, mlstm_gate_bias, mlstm_norm_w, ssm_conv_w, ssm_conv_b, ssm_dt_bias, ssm_A_log, ssm_D, ssm_norm_w, diff_q_norm_w, diff_k_norm_w, diff_lambda, diff_subln_w, rel_bias, w_out, ffn2_norm_w, ffn2_w_gate, ffn2_w_up, ffn2_w_down):
    batch, seq, d = x.shape
    assert d == D_MODEL and MIX_CHUNK == ATT_TILE and seq % ROW_TILE == 0 and ROW_TILE % MIX_CHUNK == 0
    xf = x.reshape(batch * seq, D_MODEL)
    g32 = _block_diag(DIFF_WIDTH, DIFF_QK_DIM, 1.0 / DIFF_QK_DIM)
    g64 = _block_diag(DIFF_WIDTH, DIFF_V_DIM, 1.0 / DIFF_V_DIM)
    ffn1 = [w.astype(BF16) for w in (ffn1_w_gate, ffn1_w_up, ffn1_w_down)]
    ffn2 = [w.astype(BF16) for w in (ffn2_w_gate, ffn2_w_up, ffn2_w_down)]
    w_in_all = w_in.astype(BF16)
    w_out_all = w_out.astype(BF16)
    for l in range(DEPTH):
        qkw_row = jnp.concatenate([jnp.tile(diff_q_norm_w[l].reshape(-1), DIFF_HEADS) * (DIFF_QK_DIM ** -0.5 * LOG2E),
                                   jnp.tile(diff_k_norm_w[l].reshape(-1), DIFF_HEADS)]).reshape(1, 2 * DIFF_WIDTH)
        bias_row = _small_row([(_LANE_I, mlstm_gate_bias[l, 0]), (_LANE_F, mlstm_gate_bias[l, 1]),
                               (_LANE_DT, ssm_dt_bias[l])])
        (xf, mqt, mk, mvt, mot, gt, zt, sb, sct, sxt, sxdt, sa, qt, kn, vt) = _ffn_inproj(
            xf, ffn1_norm_w[l], *ffn1, mix_norm_w[l], w_in_all, qkw_row, g32,
            jnp.broadcast_to(bias_row.reshape(LANES, 1), (LANES, ROW_TILE)),
            jnp.broadcast_to(ssm_A_log[l][:, None], (SSM_HEADS, ROW_TILE)),
            ssm_conv_w[l], ssm_conv_b[l].reshape(1, SSM_CONV_DIM), l, seq)
        y_m = _mlstm(mqt, mk, mvt, mot, gt, jnp.broadcast_to(mlstm_norm_w[l][:, None], (MLSTM_WIDTH, MIX_CHUNK)),
                     batch, seq)
        y_s = _ssd(sb, sct, sxt, sxdt, zt, sa,
                   jnp.broadcast_to(jnp.repeat(ssm_D[l], SSM_HEAD_DIM)[:, None], (SSM_WIDTH, MIX_CHUNK)),
                   jnp.broadcast_to(ssm_norm_w[l][:, None], (SSM_WIDTH, MIX_CHUNK)), batch, seq)
        lam_init = 0.8 - 0.6 * math.exp(-0.3 * l)
        y_d = _diff_attn(qt, kn, vt, rel_bias, diff_lambda[l], jnp.tile(diff_subln_w[l], DIFF_HEADS).reshape(1, DIFF_WIDTH),
                         g64, lam_init, batch, seq)
        xf = _outproj_ffn(xf, y_m, y_s, y_d, w_out_all, ffn2_norm_w[l], *ffn2, l)
    return xf.reshape(batch, seq, D_MODEL)
```

```python
import functools
import math

import jax
import jax.numpy as jnp
from jax import lax
from jax.experimental import pallas as pl
from jax.experimental.pallas import tpu as pltpu

F32 = jnp.float32
BF16 = jnp.bfloat16

D_MODEL = 1024
DEPTH = 2
D_FF = 2816
MLSTM_HEADS = 4
MLSTM_HEAD_DIM = 64
MLSTM_WIDTH = MLSTM_HEADS * MLSTM_HEAD_DIM
SSM_HEADS = 8
SSM_HEAD_DIM = 64
SSM_WIDTH = SSM_HEADS * SSM_HEAD_DIM
SSM_STATE = 128
SSM_GROUPS = 2
SSM_CONV = 4
SSM_CONV_DIM = SSM_WIDTH + 2 * SSM_GROUPS * SSM_STATE
DIFF_HEADS = 4
DIFF_QK_DIM = 32
DIFF_V_DIM = 64
DIFF_WIDTH = DIFF_HEADS * DIFF_V_DIM
REL_BUCKETS = 32
REL_MAX_DIST = 128
NORM_EPS = 1e-6

LANES = 128
ROW_TILE = 512
FFN_CHUNK = 1408
MIX_CHUNK = 256
MIX_STEP_CHUNKS = 4
ATT_TILE = 256
VMEM_LIMIT = 56 * 1024 * 1024
NEG_BIG = -1e30
LOG2E = math.log2(math.e)
ONES_ROWS = 16

_C_M = 0
_C_Z = _C_M + 4 * MLSTM_WIDTH
_C_XBC = _C_Z + SSM_WIDTH
_C_SMALL = _C_XBC + SSM_CONV_DIM
_C_QK = _C_SMALL + LANES
_C_V = _C_QK + 4 * DIFF_HEADS * DIFF_QK_DIM
_C_END = _C_V + DIFF_WIDTH
_LANE_I = 0
_LANE_F = 8
_LANE_DT = 16
GATE_ROWS = 24
D_IN = 4 * MLSTM_WIDTH + 2 * MLSTM_HEADS + SSM_WIDTH + SSM_CONV_DIM + SSM_HEADS + 3 * DIFF_WIDTH


def _regroup_plan():
    sizes = (4 * MLSTM_WIDTH, MLSTM_HEADS, MLSTM_HEADS, SSM_WIDTH, SSM_CONV_DIM, SSM_HEADS, 3 * DIFF_WIDTH)
    dests = (_C_M, _C_SMALL + _LANE_I, _C_SMALL + _LANE_F, _C_Z, _C_XBC, _C_SMALL + _LANE_DT, _C_QK)
    plan, src = [], 0
    for dst, width in zip(dests, sizes):
        plan.append((dst, src, width))
        src += width
    assert src == D_IN
    return plan


def _dot(a, b):
    return jnp.dot(a, b, preferred_element_type=F32)


def _sigmoid(x):
    return 1.0 / (1.0 + jnp.exp(-x))


def _softplus(x):
    return jnp.maximum(x, 0.0) + jnp.log(1.0 + jnp.exp(-jnp.abs(x)))


def _rms_rows(x, w_row):
    ms = jnp.mean(x * x, axis=-1, keepdims=True)
    return x * lax.rsqrt(ms + NORM_EPS) * w_row


def _group_mean_sq(x, gmat):
    sq = x * x
    hi = sq.astype(BF16)
    lo = (sq - hi.astype(F32)).astype(BF16)
    return _dot(hi, gmat) + _dot(lo, gmat)


def _scan_lanes(x, op, identity):
    n = x.shape[1]
    lane = lax.broadcasted_iota(jnp.int32, x.shape, 1)
    step = 1
    while step < n:
        x = op(x, jnp.where(lane >= step, pltpu.roll(x, step, axis=1), identity))
        step *= 2
    return x


def _block_diag(n, group, value):
    r = jnp.arange(n) // group
    return jnp.where(r[:, None] == r[None, :], value, 0.0).astype(BF16)


def _const_spec(shape):
    nd = len(shape)
    return pl.BlockSpec(shape, lambda *_: (0,) * nd)


def _params(sem):
    return pltpu.CompilerParams(dimension_semantics=sem, vmem_limit_bytes=VMEM_LIMIT)


def _ffn_block(x, nw_ref, wg_ref, wu_ref, wd_ref):
    xn = _rms_rows(x, nw_ref[...]).astype(BF16)
    acc = None
    for c in range(D_FF // FFN_CHUNK):
        cols = slice(c * FFN_CHUNK, (c + 1) * FFN_CHUNK)
        g = _dot(xn, wg_ref[:, cols])
        u = _dot(xn, wu_ref[:, cols])
        h = (g * _sigmoid(g) * u).astype(BF16)
        part = _dot(h, wd_ref[cols, :])
        acc = part if acc is None else acc + part
    return x + 0.5 * acc


def _row_spec(width):
    return pl.BlockSpec((ROW_TILE, width), lambda i: (i, 0))


WIDE_STAGE_ROWS = 128
TALL_STAGE_ROWS = 352
_HBM = pl.BlockSpec(memory_space=pl.ANY)


def _stage_weight(w_hbm, layer, rows, stage, sem, store):
    n_rows, n_cols = w_hbm.shape[1], w_hbm.shape[2]
    assert n_rows % rows == 0

    def copy(c):
        return pltpu.make_async_copy(w_hbm.at[layer, pl.ds(c * rows, rows), :],
                                     stage.at[c % 2, pl.ds(0, rows), pl.ds(0, n_cols)], sem.at[c % 2])

    copy(0).start()
    for c in range(n_rows // rows):
        if c + 1 < n_rows // rows:
            copy(c + 1).start()
        copy(c).wait()
        store(c * rows, stage[c % 2, :rows, :n_cols])


def _stage_ffn_weights(layer, wg_hbm, wu_hbm, wd_hbm, wg_scr, wu_scr, wd_scr, wide, wide_sem, tall, tall_sem):
    def into(dst):
        def store(r0, chunk):
            dst[r0:r0 + chunk.shape[0], :] = chunk.astype(BF16)
        return store

    _stage_weight(wg_hbm, layer, WIDE_STAGE_ROWS, wide, wide_sem, into(wg_scr))
    _stage_weight(wu_hbm, layer, WIDE_STAGE_ROWS, wide, wide_sem, into(wu_scr))
    _stage_weight(wd_hbm, layer, TALL_STAGE_ROWS, tall, tall_sem, into(wd_scr))


def _ffn_weight_scratch():
    return [pltpu.VMEM((D_MODEL, D_FF), BF16), pltpu.VMEM((D_MODEL, D_FF), BF16), pltpu.VMEM((D_FF, D_MODEL), BF16),
            pltpu.VMEM((2, WIDE_STAGE_ROWS, D_IN), F32), pltpu.SemaphoreType.DMA((2,)),
            pltpu.VMEM((2, TALL_STAGE_ROWS, D_MODEL), F32), pltpu.SemaphoreType.DMA((2,))]


def _store_transposed_tiles(out_ref, a):
    a_t = a.T.astype(BF16)
    for t in range(ROW_TILE // ATT_TILE):
        out_ref[t] = a_t[:, t * ATT_TILE:(t + 1) * ATT_TILE]


def _ffn_inproj_kernel(x_ref, fnw_ref, wg_hbm, wu_hbm, wd_hbm, nw_ref, w_hbm, qkw_ref, g32_ref, gbias_ref,
                       alog_ref, convw_ref, convb_ref,
                       x_out_ref, mqt_ref, mk_ref, mvt_ref, mot_ref, gt_ref,
                       zt_ref, sb_ref, sct_ref, sxt_ref, sxdt_ref, sa_ref, qt_ref, k_ref, vt_ref,
                       halo_scr, w_scr, wg_ref, wu_ref, wd_ref, wide, wide_sem, tall, tall_sem,
                       *, tiles_per_seq, layer):
    @pl.when(pl.program_id(0) == 0)
    def _():
        _stage_ffn_weights(layer, wg_hbm, wu_hbm, wd_hbm, wg_ref, wu_ref, wd_ref, wide, wide_sem, tall, tall_sem)
        w_scr[:, _C_SMALL:_C_QK] = jnp.zeros((D_MODEL, LANES), BF16)

        def regrouped(r0, chunk):
            for dst, src, width in _regroup_plan():
                w_scr[r0:r0 + chunk.shape[0], dst:dst + width] = chunk[:, src:src + width].astype(BF16)

        _stage_weight(w_hbm, layer, WIDE_STAGE_ROWS, wide, wide_sem, regrouped)

    @pl.when(pl.program_id(0) % tiles_per_seq == 0)
    def _():
        halo_scr[...] = jnp.zeros_like(halo_scr)

    x = _ffn_block(x_ref[...], fnw_ref, wg_ref, wu_ref, wd_ref)
    x_out_ref[...] = x
    xn = _rms_rows(x, nw_ref[...]).astype(BF16)

    def proj(lo, hi):
        return _dot(xn, w_scr[:, lo:hi])

    half = (_C_V - _C_QK) // 2
    p_xbc = proj(_C_XBC, _C_SMALL)
    p_small = proj(_C_SMALL, _C_QK)
    p_q, p_k = proj(_C_QK, _C_QK + half), proj(_C_QK + half, _C_V)
    p_mq = proj(_C_M, _C_M + MLSTM_WIDTH)
    p_mk = proj(_C_M + MLSTM_WIDTH, _C_M + 2 * MLSTM_WIDTH)
    p_mv = proj(_C_M + 2 * MLSTM_WIDTH, _C_M + 3 * MLSTM_WIDTH)
    p_mo = proj(_C_M + 3 * MLSTM_WIDTH, _C_Z)
    p_z = proj(_C_Z, _C_XBC)
    p_v = proj(_C_V, _C_END)

    qk = [a * lax.rsqrt(_group_mean_sq(a, g32_ref[...]) + NORM_EPS) * qkw_ref[:, i * half:(i + 1) * half]
          for i, a in enumerate((p_q, p_k))]
    _store_transposed_tiles(qt_ref, qk[0])
    k_ref[...] = qk[1].astype(BF16)
    _store_transposed_tiles(vt_ref, p_v)

    ext = jnp.concatenate([halo_scr[...], p_xbc], axis=0)
    halo_scr[...] = p_xbc[ROW_TILE - 8:, :]
    conv = convb_ref[...] + convw_ref[SSM_CONV - 1:SSM_CONV, :] * p_xbc
    for j in range(1, SSM_CONV):
        conv = conv + convw_ref[SSM_CONV - 1 - j:SSM_CONV - j, :] * ext[8 - j:8 - j + ROW_TILE, :]
    xa = conv * _sigmoid(conv)
    sb_ref[...] = xa[:, SSM_WIDTH:SSM_WIDTH + SSM_GROUPS * SSM_STATE].astype(BF16)
    _store_transposed_tiles(sct_ref, xa[:, SSM_WIDTH + SSM_GROUPS * SSM_STATE:])
    small_t = p_small.T + gbias_ref[...]
    dt = _softplus(small_t[_LANE_DT:_LANE_DT + SSM_HEADS])
    log_decay = dt * (-jnp.exp(alog_ref[...]))
    xs_t = xa[:, :SSM_WIDTH].T
    xdt_t = xs_t * jnp.concatenate([jnp.broadcast_to(dt[h:h + 1], (SSM_HEAD_DIM, ROW_TILE))
                                    for h in range(SSM_HEADS)], axis=0)
    for t in range(ROW_TILE // MIX_CHUNK):
        cols = slice(t * MIX_CHUNK, (t + 1) * MIX_CHUNK)
        sxt_ref[t] = xs_t[:, cols].astype(BF16)
        sxdt_ref[t] = xdt_t[:, cols].astype(BF16)
        sa_ref[t] = _scan_lanes(log_decay[:, cols], jnp.add, 0.0)
    _store_transposed_tiles(zt_ref, p_z)

    _store_transposed_tiles(mqt_ref, p_mq)
    mk_ref[...] = p_mk.astype(BF16)
    _store_transposed_tiles(mvt_ref, p_mv)
    _store_transposed_tiles(mot_ref, p_mo)
    gates = small_t[:_LANE_DT]
    for t in range(ROW_TILE // MIX_CHUNK):
        i_pre = gates[_LANE_I:_LANE_I + 8, t * MIX_CHUNK:(t + 1) * MIX_CHUNK]
        f_pre = gates[_LANE_F:_LANE_F + 8, t * MIX_CHUNK:(t + 1) * MIX_CHUNK]
        log_f = jnp.minimum(f_pre, 0.0) - jnp.log(1.0 + jnp.exp(-jnp.abs(f_pre)))
        b = _scan_lanes(log_f, jnp.add, 0.0)
        c = i_pre - b
        gt_ref[t] = jnp.concatenate([c, _scan_lanes(c, jnp.maximum, -jnp.inf), b], axis=0)


def _ffn_inproj(x, ffn_norm_w, wg, wu, wd, norm_w, w_all, qkw_row, g32, gate_bias_rows, alog_rows, conv_w, conv_b_row,
                layer, seq):
    n = x.shape[0]
    tiles = ROW_TILE // ATT_TILE

    def t_out(width, dtype=BF16):
        return (pl.BlockSpec((tiles, width, ATT_TILE), lambda i: (i, 0, 0)),
                jax.ShapeDtypeStruct((n // ATT_TILE, width, ATT_TILE), dtype))

    def r_out(width, dtype=BF16):
        return _row_spec(width), jax.ShapeDtypeStruct((n, width), dtype)

    outs = [r_out(D_MODEL, F32),
            t_out(MLSTM_WIDTH), r_out(MLSTM_WIDTH), t_out(MLSTM_WIDTH), t_out(MLSTM_WIDTH), t_out(GATE_ROWS, F32),
            t_out(SSM_WIDTH), r_out(SSM_GROUPS * SSM_STATE), t_out(SSM_GROUPS * SSM_STATE), t_out(SSM_WIDTH),
            t_out(SSM_WIDTH), t_out(SSM_HEADS, F32),
            t_out(DIFF_WIDTH), r_out(DIFF_WIDTH), t_out(DIFF_WIDTH)]
    return pl.pallas_call(
        functools.partial(_ffn_inproj_kernel, tiles_per_seq=seq // ROW_TILE, layer=layer),
        grid=(n // ROW_TILE,),
        in_specs=[_row_spec(D_MODEL), _const_spec((1, D_MODEL)), _HBM, _HBM, _HBM, _const_spec((1, D_MODEL)), _HBM,
                  _const_spec((1, 2 * DIFF_WIDTH)), _const_spec((DIFF_WIDTH, DIFF_WIDTH)),
                  _const_spec((LANES, ROW_TILE)), _const_spec((SSM_HEADS, ROW_TILE)),
                  _const_spec((SSM_CONV, SSM_CONV_DIM)), _const_spec((1, SSM_CONV_DIM))],
        out_specs=[o[0] for o in outs],
        out_shape=[o[1] for o in outs],
        scratch_shapes=[pltpu.VMEM((8, SSM_CONV_DIM), F32), pltpu.VMEM((D_MODEL, _C_END), BF16)]
        + _ffn_weight_scratch(),
        compiler_params=_params(("arbitrary",)),
        name="ffn_inproj",
    )(x, ffn_norm_w.reshape(1, D_MODEL), wg, wu, wd, norm_w.reshape(1, D_MODEL), w_all, qkw_row, g32,
      gate_bias_rows, alog_rows, conv_w, conv_b_row)


def _mlstm_kernel(qt_ref, k_ref, vt_ref, ot_ref, gt_ref, nw_ref, yt_ref, ct_scr, nt_scr, m_scr):
    @pl.when(pl.program_id(1) == 0)
    def _():
        ct_scr[...] = jnp.zeros_like(ct_scr)
        nt_scr[...] = jnp.zeros_like(nt_scr)
        m_scr[...] = jnp.zeros_like(m_scr)

    for t in range(MIX_STEP_CHUNKS):
        _mlstm_chunk(t, qt_ref, k_ref, vt_ref, ot_ref, gt_ref, nw_ref, yt_ref, ct_scr, nt_scr, m_scr)


def _mlstm_chunk(t, qt_ref, k_ref, vt_ref, ot_ref, gt_ref, nw_ref, yt_ref, ct_scr, nt_scr, m_scr):
    L = MIX_CHUNK
    H = MLSTM_HEADS
    W = MLSTM_HEAD_DIM
    qt = qt_ref[t]
    ks = k_ref[t * L:(t + 1) * L, :] * (MLSTM_HEAD_DIM ** -0.5)
    vt = vt_ref[t]
    gt = gt_ref[t]
    c, c_max, b = gt[0:8], gt[8:16], gt[16:24]
    m_prev = m_scr[...]
    big_m = jnp.maximum(m_prev, c_max)
    inter = jnp.exp(m_prev - big_m)
    floor = jnp.exp(-(b + big_m))
    b_last = jnp.broadcast_to(b[:, L - 1:L], b.shape)
    m_new = b_last + jnp.maximum(m_prev, jnp.broadcast_to(c_max[:, L - 1:L], b.shape))
    w = jnp.exp(b_last + c - m_new)
    decay = jnp.exp(b_last + m_prev - m_new)

    c2_cols = jnp.concatenate([c * LOG2E, jnp.zeros((LANES - 8, L), F32)], axis=0).T
    big_m2 = big_m * LOG2E
    causal = (lax.broadcasted_iota(jnp.int32, (L, L), 0) <= lax.broadcasted_iota(jnp.int32, (L, L), 1))
    row_head = lax.broadcasted_iota(jnp.int32, (H * W, L), 0) // W
    qk = [_dot(ks, jnp.where(row_head == h, qt, jnp.zeros_like(qt))) for h in range(H)]
    nums, dens = [], []
    for h in range(H):
        d = jnp.exp2(jnp.where(causal, c2_cols[:, h:h + 1] - big_m2[h:h + 1, :], -jnp.inf))
        s = qk[h] * d
        dens.append(jnp.sum(s, axis=0, keepdims=True))
        nums.append(_dot(vt[h * W:(h + 1) * W, :], s.astype(BF16)))
    cq = _dot(ct_scr[...].astype(BF16), qt)
    nq = _dot(nt_scr[...].astype(BF16), qt)
    outs = []
    for h in range(H):
        num = nums[h] + inter[h:h + 1] * cq[h * W:(h + 1) * W]
        den = dens[h] + inter[h:h + 1] * nq[h:h + 1]
        hh = num / jnp.maximum(jnp.abs(den), floor[h:h + 1])
        outs.append(hh * lax.rsqrt(jnp.mean(hh * hh, axis=0, keepdims=True) + NORM_EPS))
    hn = jnp.concatenate(outs, axis=0) * nw_ref[...]
    yt_ref[t] = (_sigmoid(ot_ref[t].astype(F32)) * hn).astype(BF16)

    w_full = jnp.concatenate([jnp.broadcast_to(w[h:h + 1], (W, L)) for h in range(H)], axis=0)
    g_new = _dot((vt.astype(F32) * w_full).astype(BF16), ks)
    decay_full = jnp.concatenate([jnp.broadcast_to(decay[h:h + 1, :1], (W, H * W)) for h in range(H)], axis=0)
    same_head = (lax.broadcasted_iota(jnp.int32, (H * W, H * W), 0) // W
                 == lax.broadcasted_iota(jnp.int32, (H * W, H * W), 1) // W)
    ct_scr[...] = decay_full * ct_scr[...] + jnp.where(same_head, g_new, 0.0)
    w16 = jnp.concatenate([w, jnp.zeros_like(w)], axis=0).astype(BF16)
    n_new = _dot(w16, ks)
    own_lanes = (lax.broadcasted_iota(jnp.int32, (16, H * W), 0)
                 == lax.broadcasted_iota(jnp.int32, (16, H * W), 1) // W)
    decay16 = jnp.broadcast_to(jnp.concatenate([decay[:, :1], jnp.zeros((8, 1), F32)], axis=0), (16, H * W))
    nt_scr[...] = decay16 * nt_scr[...] + jnp.where(own_lanes, n_new, 0.0)
    m_scr[...] = m_new


def _mlstm(mqt, mk, mvt, mot, gt, norm_w_rows, batch, seq):
    ns = seq // (MIX_CHUNK * MIX_STEP_CHUNKS)
    n = batch * seq
    t_spec = pl.BlockSpec((MIX_STEP_CHUNKS, MLSTM_WIDTH, MIX_CHUNK), lambda b, c: (b * ns + c, 0, 0))
    return pl.pallas_call(
        _mlstm_kernel,
        grid=(batch, ns),
        in_specs=[t_spec, pl.BlockSpec((MIX_STEP_CHUNKS * MIX_CHUNK, MLSTM_WIDTH), lambda b, c: (b * ns + c, 0)),
                  t_spec, t_spec,
                  pl.BlockSpec((MIX_STEP_CHUNKS, GATE_ROWS, MIX_CHUNK), lambda b, c: (b * ns + c, 0, 0)),
                  _const_spec((MLSTM_WIDTH, MIX_CHUNK))],
        out_specs=t_spec,
        out_shape=jax.ShapeDtypeStruct((n // MIX_CHUNK, MLSTM_WIDTH, MIX_CHUNK), BF16),
        scratch_shapes=[pltpu.VMEM((MLSTM_WIDTH, MLSTM_WIDTH), F32),
                        pltpu.VMEM((16, MLSTM_WIDTH), F32),
                        pltpu.VMEM((8, MIX_CHUNK), F32)],
        compiler_params=_params(("arbitrary", "arbitrary")),
        name="mlstm",
    )(mqt, mk, mvt, mot, gt, norm_w_rows)


def _ssd_kernel(b_ref, ct_ref, xt_ref, xdt_ref, zt_ref, a_ref, dskip_ref, nw_ref, yt_ref, st_scr):
    @pl.when(pl.program_id(1) == 0)
    def _():
        st_scr[...] = jnp.zeros_like(st_scr)

    for t in range(MIX_STEP_CHUNKS):
        _ssd_chunk(t, b_ref, ct_ref, xt_ref, xdt_ref, zt_ref, a_ref, dskip_ref, nw_ref, yt_ref, st_scr)


def _ssd_chunk(t, b_ref, ct_ref, xt_ref, xdt_ref, zt_ref, a_ref, dskip_ref, nw_ref, yt_ref, st_scr):
    L = MIX_CHUNK
    P = SSM_HEAD_DIM
    HG = SSM_HEADS // SSM_GROUPS
    GW = HG * P
    NS = SSM_STATE
    a = a_ref[t]
    a_last = jnp.broadcast_to(a[:, L - 1:L], a.shape)
    exp_a = jnp.exp(a)
    d_in = jnp.exp(a_last - a)
    chunk_decay = jnp.exp(a_last)
    a2 = a * LOG2E
    a2_cols = jnp.concatenate([a2, jnp.zeros((LANES - SSM_HEADS, L), F32)], axis=0).T
    causal = (lax.broadcasted_iota(jnp.int32, (L, L), 0) <= lax.broadcasted_iota(jnp.int32, (L, L), 1))
    bn = b_ref[t * L:(t + 1) * L, :]
    ct = ct_ref[t]
    xdt = xdt_ref[t]

    def per_head_rows(rows, grp, width):
        return jnp.concatenate([jnp.broadcast_to(rows[grp * HG + i:grp * HG + i + 1, :width], (P, width))
                                for i in range(HG)], axis=0)

    scores, y_off = [], []
    for grp in range(SSM_GROUPS):
        bg = bn[:, grp * NS:(grp + 1) * NS]
        cg = ct[grp * NS:(grp + 1) * NS, :]
        scores.append(_dot(bg, cg))
        st = st_scr[grp]
        y_off.append(_dot(st.astype(BF16), cg))
        x_in = (xdt[grp * GW:(grp + 1) * GW, :].astype(F32) * per_head_rows(d_in, grp, L)).astype(BF16)
        st_scr[grp] = per_head_rows(chunk_decay, grp, NS) * st + _dot(x_in, bg)
    for grp in range(SSM_GROUPS):
        y_diag = []
        for i in range(HG):
            h = grp * HG + i
            seg = jnp.exp2(jnp.where(causal, a2[h:h + 1, :] - a2_cols[:, h:h + 1], -jnp.inf))
            p = (scores[grp] * seg).astype(BF16)
            y_diag.append(_dot(xdt[h * P:(h + 1) * P, :], p))
        rows = slice(grp * GW, (grp + 1) * GW)
        y = (jnp.concatenate(y_diag, axis=0) + y_off[grp] * per_head_rows(exp_a, grp, L)
             + dskip_ref[rows, :] * xt_ref[t, rows, :].astype(F32))
        zg = zt_ref[t, rows, :].astype(F32)
        y = y * (zg * _sigmoid(zg))
        ms = jnp.mean(y * y, axis=0, keepdims=True)
        yt_ref[t, rows, :] = (y * lax.rsqrt(ms + NORM_EPS) * nw_ref[rows, :]).astype(BF16)


def _ssd(sb, sct, sxt, sxdt, zt, sa, dskip_rows, norm_w_rows, batch, seq):
    ns = seq // (MIX_CHUNK * MIX_STEP_CHUNKS)
    n = batch * seq

    def t_spec(width):
        return pl.BlockSpec((MIX_STEP_CHUNKS, width, MIX_CHUNK), lambda b, c: (b * ns + c, 0, 0))

    return pl.pallas_call(
        _ssd_kernel,
        grid=(batch, ns),
        in_specs=[pl.BlockSpec((MIX_STEP_CHUNKS * MIX_CHUNK, SSM_GROUPS * SSM_STATE), lambda b, c: (b * ns + c, 0)),
                  t_spec(SSM_GROUPS * SSM_STATE), t_spec(SSM_WIDTH), t_spec(SSM_WIDTH), t_spec(SSM_WIDTH),
                  t_spec(SSM_HEADS), _const_spec((SSM_WIDTH, MIX_CHUNK)), _const_spec((SSM_WIDTH, MIX_CHUNK))],
        out_specs=t_spec(SSM_WIDTH),
        out_shape=jax.ShapeDtypeStruct((n // MIX_CHUNK, SSM_WIDTH, MIX_CHUNK), BF16),
        scratch_shapes=[pltpu.VMEM((SSM_GROUPS, SSM_WIDTH // SSM_GROUPS, SSM_STATE), F32)],
        compiler_params=_params(("arbitrary", "arbitrary")),
        name="ssd",
    )(sb, sct, sxt, sxdt, zt, sa, dskip_rows, norm_w_rows)


def _t5_bias_tile(rel, relb_ref, head):
    n = jnp.maximum(rel, 0)
    max_exact = REL_BUCKETS // 2
    nf = jnp.maximum(n, 1).astype(F32)
    large = max_exact + (jnp.log(nf / max_exact) / math.log(REL_MAX_DIST / max_exact)
                         * (REL_BUCKETS - max_exact)).astype(jnp.int32)
    large = jnp.minimum(large, REL_BUCKETS - 1)
    bucket = jnp.where(n < max_exact, n, large)
    far = relb_ref[REL_BUCKETS - 1, head]
    bias = jnp.zeros(rel.shape, F32)
    for bkt in range(REL_BUCKETS - 1):
        bias = jnp.where(bucket == bkt, (relb_ref[bkt, head] - far) * LOG2E, bias)
    return jnp.where(rel >= 0, bias, NEG_BIG)


def _attn_kernel(relb_ref, qt_ref, k_ref, vt_ref, lam_ref, subw_ref, g64_ref,
                 y_ref, bias_scr, qm_scr, m_scr, acc_scr, sa_scr, sb_scr, *, lam_init):
    T = ATT_TILE
    H = DIFF_HEADS
    DV = DIFF_V_DIM
    i = pl.program_id(1)

    @pl.when((pl.program_id(0) == 0) & (i == 0))
    def _():
        rel = lax.broadcasted_iota(jnp.int32, (T, T), 1) - lax.broadcasted_iota(jnp.int32, (T, T), 0)
        for h in range(H):
            bias_scr[0, h] = _t5_bias_tile(rel, relb_ref, h)
            bias_scr[1, h] = _t5_bias_tile(rel + T, relb_ref, h)

    qt = qt_ref[0]
    comp_of_row = lax.broadcasted_iota(jnp.int32, (DIFF_WIDTH, T), 0) // DIFF_QK_DIM
    for hc in range(2 * H):
        qm_scr[hc] = jnp.where(comp_of_row == hc, qt, jnp.zeros_like(qt))
    m_scr[...] = jnp.full(m_scr.shape, NEG_BIG, F32)
    acc_scr[...] = jnp.zeros_like(acc_scr)
    ones_rows = jnp.ones((ONES_ROWS, T), BF16)

    def logits_one(s_ref, kt, hc):
        s_ref[hc] = _dot(kt, qm_scr[hc])

    def key_tile(j):
        return k_ref[pl.ds(pl.multiple_of(j * T, T), T), :]

    def logits_into(s_ref, j):
        kt = key_tile(j)
        for hc in range(2 * H):
            logits_one(s_ref, kt, hc)

    def consume_one(s_ref, vt, hc, bias_idx):
        h = hc // 2
        s = s_ref[hc]
        if bias_idx is not None:
            s = bias_scr[bias_idx, h] + s
        m_prev = m_scr[hc]
        m_next = jnp.maximum(m_prev, jnp.max(s, axis=0, keepdims=True))
        alpha = jnp.exp2(m_prev - m_next)
        p = jnp.exp2(s - m_next).astype(BF16)
        m_scr[hc] = m_next
        v_aug = jnp.concatenate([vt[h * DV:(h + 1) * DV, :], ones_rows], axis=0)
        acc_scr[hc] = alpha * acc_scr[hc] + _dot(v_aug, p)

    def consume(s_ref, j, bias_idx):
        vt = vt_ref[j]
        for hc in range(2 * H):
            consume_one(s_ref, vt, hc, bias_idx)

    LEAD = 2

    def step(s_next, j_next, s_cur, j_cur, bias_idx):
        kt = key_tile(j_next)
        vt = vt_ref[j_cur]
        for hc in range(LEAD):
            logits_one(s_next, kt, hc)
        for hc in range(2 * H):
            if hc + LEAD < 2 * H:
                logits_one(s_next, kt, hc + LEAD)
            consume_one(s_cur, vt, hc, bias_idx)

    DIAG, NEAR = 0, 1
    n_far = jnp.maximum(i - 1, 0)
    logits_into(sa_scr, 0)

    def far_pair(t, carry):
        j = 2 * t
        step(sb_scr, j + 1, sa_scr, j, None)
        step(sa_scr, j + 2, sb_scr, j + 1, None)
        return carry

    lax.fori_loop(0, n_far // 2, far_pair, 0)
    u = 2 * (n_far // 2)
    odd = n_far % 2 == 1

    @pl.when(odd)
    def _():
        step(sb_scr, u + 1, sa_scr, u, None)
        step(sa_scr, u + 2, sb_scr, u + 1, NEAR)
        consume(sa_scr, u + 2, DIAG)

    @pl.when(jnp.logical_not(odd) & (i >= 1))
    def _():
        step(sb_scr, u + 1, sa_scr, u, NEAR)
        consume(sb_scr, u + 1, DIAG)

    @pl.when(i == 0)
    def _():
        consume(sa_scr, 0, DIAG)

    lam_p = lam_ref[...]
    lam = (jnp.exp(jnp.sum(lam_p[0:1] * lam_p[1:2], axis=1, keepdims=True))
           - jnp.exp(jnp.sum(lam_p[2:3] * lam_p[3:4], axis=1, keepdims=True)) + lam_init)

    def normalised(hc):
        return acc_scr[hc, :DV, :] / acc_scr[hc, DV:DV + 1, :]

    o_t = jnp.concatenate([normalised(2 * h) - lam * normalised(2 * h + 1) for h in range(H)], axis=0)
    o = o_t.T
    ms = _group_mean_sq(o, g64_ref[...])
    y_ref[...] = (o * lax.rsqrt(ms + NORM_EPS) * subw_ref[...] * (1.0 - lam_init)).astype(BF16)


def _diff_attn(qt, kn, vt, rel_bias, lambdas, subw_row, g64, lam_init, batch, seq):
    nq = seq // ATT_TILE
    n = batch * seq
    return pl.pallas_call(
        functools.partial(_attn_kernel, lam_init=lam_init),
        grid=(batch, nq),
        in_specs=[pl.BlockSpec(memory_space=pltpu.SMEM),
                  pl.BlockSpec((1, DIFF_WIDTH, ATT_TILE), lambda b, i: (b * nq + i, 0, 0)),
                  pl.BlockSpec((seq, DIFF_WIDTH), lambda b, i: (b, 0)),
                  pl.BlockSpec((nq, DIFF_WIDTH, ATT_TILE), lambda b, i: (b, 0, 0)),
                  _const_spec((4, DIFF_QK_DIM)), _const_spec((1, DIFF_WIDTH)),
                  _const_spec((DIFF_WIDTH, DIFF_WIDTH))],
        out_specs=pl.BlockSpec((ATT_TILE, DIFF_WIDTH), lambda b, i: (b * nq + i, 0)),
        out_shape=jax.ShapeDtypeStruct((n, DIFF_WIDTH), BF16),
        scratch_shapes=[pltpu.VMEM((2, DIFF_HEADS, ATT_TILE, ATT_TILE), F32),
                        pltpu.VMEM((2 * DIFF_HEADS, DIFF_WIDTH, ATT_TILE), BF16),
                        pltpu.VMEM((2 * DIFF_HEADS, 1, ATT_TILE), F32),
                        pltpu.VMEM((2 * DIFF_HEADS, DIFF_V_DIM + ONES_ROWS, ATT_TILE), F32),
                        pltpu.VMEM((2 * DIFF_HEADS, ATT_TILE, ATT_TILE), F32),
                        pltpu.VMEM((2 * DIFF_HEADS, ATT_TILE, ATT_TILE), F32)],
        compiler_params=_params(("arbitrary", "arbitrary")),
        name="diff_attn",
    )(rel_bias, qt, kn, vt, lambdas, subw_row, g64)


def _untransposed(yt_ref):
    return jnp.concatenate([yt_ref[t].astype(F32).T for t in range(yt_ref.shape[0])], axis=0).astype(BF16)


def _outproj_ffn_kernel(x_ref, ymt_ref, yst_ref, yd_ref, w_hbm, fnw_ref, wg_hbm, wu_hbm, wd_hbm, o_ref,
                        w_ref, wg_ref, wu_ref, wd_ref, wide, wide_sem, tall, tall_sem, *, layer):
    @pl.when(pl.program_id(0) == 0)
    def _():
        def store(r0, chunk):
            w_ref[r0:r0 + chunk.shape[0], :] = chunk.astype(BF16)

        _stage_weight(w_hbm, layer, D_MODEL // 4, tall, tall_sem, store)
        _stage_ffn_weights(layer, wg_hbm, wu_hbm, wd_hbm, wg_ref, wu_ref, wd_ref, wide, wide_sem, tall, tall_sem)

    y = jnp.concatenate([_untransposed(ymt_ref), _untransposed(yst_ref), yd_ref[...]], axis=1)
    x = x_ref[...] + _dot(y, w_ref[...])
    o_ref[...] = _ffn_block(x, fnw_ref, wg_ref, wu_ref, wd_ref)


def _outproj_ffn(x, y_m, y_s, y_d, w_out, ffn_norm_w, wg, wu, wd, layer):
    n = x.shape[0]
    return pl.pallas_call(
        functools.partial(_outproj_ffn_kernel, layer=layer),
        grid=(n // ROW_TILE,),
        in_specs=[_row_spec(D_MODEL),
                  pl.BlockSpec((ROW_TILE // MIX_CHUNK, MLSTM_WIDTH, MIX_CHUNK), lambda i: (i, 0, 0)),
                  pl.BlockSpec((ROW_TILE // MIX_CHUNK, SSM_WIDTH, MIX_CHUNK), lambda i: (i, 0, 0)),
                  _row_spec(DIFF_WIDTH), _HBM, _const_spec((1, D_MODEL)), _HBM, _HBM, _HBM],
        out_specs=_row_spec(D_MODEL),
        out_shape=jax.ShapeDtypeStruct((n, D_MODEL), F32),
        scratch_shapes=[pltpu.VMEM((D_MODEL, D_MODEL), BF16)] + _ffn_weight_scratch(),
        compiler_params=_params(("arbitrary",)),
        name="outproj_ffn",
    )(x, y_m, y_s, y_d, w_out, ffn_norm_w.reshape(1, D_MODEL), wg, wu, wd)


def _small_row(values_by_lane):
    row = jnp.zeros((LANES,), F32)
    for lane, vals in values_by_lane:
        row = lax.dynamic_update_slice(row, vals.astype(F32), (lane,))
    return row.reshape(1, LANES)


def kernel(x, ffn1_norm_w, ffn1_w_gate, ffn1_w_up, ffn1_w_down, mix_norm_w, w_in, mlstm_gate_bias, mlstm_norm_w, ssm_conv_w, ssm_conv_b, ssm_dt_bias, ssm_A_log, ssm_D, ssm_norm_w, diff_q_norm_w, diff_k_norm_w, diff_lambda, diff_subln_w, rel_bias, w_out, ffn2_norm_w, ffn2_w_gate, ffn2_w_up, ffn2_w_down):
    batch, seq, d = x.shape
    assert d == D_MODEL and MIX_CHUNK == ATT_TILE and seq % ROW_TILE == 0 and ROW_TILE % MIX_CHUNK == 0
    assert seq % (MIX_CHUNK * MIX_STEP_CHUNKS) == 0
    xf = x.reshape(batch * seq, D_MODEL)
    g32 = _block_diag(DIFF_WIDTH, DIFF_QK_DIM, 1.0 / DIFF_QK_DIM)
    g64 = _block_diag(DIFF_WIDTH, DIFF_V_DIM, 1.0 / DIFF_V_DIM)
    ffn1 = (ffn1_w_gate, ffn1_w_up, ffn1_w_down)
    ffn2 = (ffn2_w_gate, ffn2_w_up, ffn2_w_down)
    for l in range(DEPTH):
        qkw_row = jnp.concatenate([jnp.tile(diff_q_norm_w[l].reshape(-1), DIFF_HEADS) * (DIFF_QK_DIM ** -0.5 * LOG2E),
                                   jnp.tile(diff_k_norm_w[l].reshape(-1), DIFF_HEADS)]).reshape(1, 2 * DIFF_WIDTH)
        bias_row = _small_row([(_LANE_I, mlstm_gate_bias[l, 0]), (_LANE_F, mlstm_gate_bias[l, 1]),
                               (_LANE_DT, ssm_dt_bias[l])])
        (xf, mqt, mk, mvt, mot, gt, zt, sb, sct, sxt, sxdt, sa, qt, kn, vt) = _ffn_inproj(
            xf, ffn1_norm_w[l], *ffn1, mix_norm_w[l], w_in, qkw_row, g32,
            jnp.broadcast_to(bias_row.reshape(LANES, 1), (LANES, ROW_TILE)),
            jnp.broadcast_to(ssm_A_log[l][:, None], (SSM_HEADS, ROW_TILE)),
            ssm_conv_w[l], ssm_conv_b[l].reshape(1, SSM_CONV_DIM), l, seq)
        y_m = _mlstm(mqt, mk, mvt, mot, gt, jnp.broadcast_to(mlstm_norm_w[l][:, None], (MLSTM_WIDTH, MIX_CHUNK)),
                     batch, seq)
        y_s = _ssd(sb, sct, sxt, sxdt, zt, sa,
                   jnp.broadcast_to(jnp.repeat(ssm_D[l], SSM_HEAD_DIM)[:, None], (SSM_WIDTH, MIX_CHUNK)),
                   jnp.broadcast_to(ssm_norm_w[l][:, None], (SSM_WIDTH, MIX_CHUNK)), batch, seq)
        lam_init = 0.8 - 0.6 * math.exp(-0.3 * l)
        y_d = _diff_attn(qt, kn, vt, rel_bias, diff_lambda[l], jnp.tile(diff_subln_w[l], DIFF_HEADS).reshape(1, DIFF_WIDTH),
                         g64, lam_init, batch, seq)
        xf = _outproj_ffn(xf, y_m, y_s, y_d, w_out, ffn2_norm_w[l], *ffn2, l)
    return xf.reshape(batch, seq, D_MODEL)
```

```python
import functools
import math

import jax
import jax.numpy as jnp
from jax import lax
from jax.experimental import pallas as pl
from jax.experimental.pallas import tpu as pltpu

F32 = jnp.float32
BF16 = jnp.bfloat16

D_MODEL = 1024
DEPTH = 2
D_FF = 2816
MLSTM_HEADS = 4
MLSTM_HEAD_DIM = 64
MLSTM_WIDTH = MLSTM_HEADS * MLSTM_HEAD_DIM
SSM_HEADS = 8
SSM_HEAD_DIM = 64
SSM_WIDTH = SSM_HEADS * SSM_HEAD_DIM
SSM_STATE = 128
SSM_GROUPS = 2
SSM_CONV = 4
SSM_CONV_DIM = SSM_WIDTH + 2 * SSM_GROUPS * SSM_STATE
DIFF_HEADS = 4
DIFF_QK_DIM = 32
DIFF_V_DIM = 64
DIFF_WIDTH = DIFF_HEADS * DIFF_V_DIM
REL_BUCKETS = 32
REL_MAX_DIST = 128
NORM_EPS = 1e-6

LANES = 128
ROW_TILE = 512
MXU_COLS = 256
FFN_SPLITS = (0, 6 * MXU_COLS, D_FF)
MIX_CHUNK = 256
MIX_STEP_CHUNKS = 4
ATT_TILE = 256
VMEM_LIMIT = 56 * 1024 * 1024
NEG_BIG = -1e30
LOG2E = math.log2(math.e)
ONES_ROWS = 16

_C_M = 0
_C_Z = _C_M + 4 * MLSTM_WIDTH
_C_XBC = _C_Z + SSM_WIDTH
_C_SMALL = _C_XBC + SSM_CONV_DIM
_C_QK = _C_SMALL + LANES
_C_V = _C_QK + 4 * DIFF_HEADS * DIFF_QK_DIM
_C_END = _C_V + DIFF_WIDTH
_LANE_I = 0
_LANE_F = 8
_LANE_DT = 16
GATE_ROWS = 24
D_IN = 4 * MLSTM_WIDTH + 2 * MLSTM_HEADS + SSM_WIDTH + SSM_CONV_DIM + SSM_HEADS + 3 * DIFF_WIDTH


def _regroup_plan():
    sizes = (4 * MLSTM_WIDTH, MLSTM_HEADS, MLSTM_HEADS, SSM_WIDTH, SSM_CONV_DIM, SSM_HEADS, 3 * DIFF_WIDTH)
    dests = (_C_M, _C_SMALL + _LANE_I, _C_SMALL + _LANE_F, _C_Z, _C_XBC, _C_SMALL + _LANE_DT, _C_QK)
    plan, src = [], 0
    for dst, width in zip(dests, sizes):
        plan.append((dst, src, width))
        src += width
    assert src == D_IN
    return plan


def _dot(a, b):
    return jnp.dot(a, b, preferred_element_type=F32)


def _sigmoid(x):
    return 1.0 / (1.0 + jnp.exp(-x))


def _softplus(x):
    return jnp.maximum(x, 0.0) + jnp.log(1.0 + jnp.exp(-jnp.abs(x)))


def _rms_rows(x, w_row):
    ms = jnp.mean(x * x, axis=-1, keepdims=True)
    return x * lax.rsqrt(ms + NORM_EPS) * w_row


def _group_mean_sq(x, gmat):
    sq = x * x
    hi = sq.astype(BF16)
    lo = (sq - hi.astype(F32)).astype(BF16)
    return _dot(hi, gmat) + _dot(lo, gmat)


def _scan_lanes(x, op, identity):
    n = x.shape[1]
    lane = lax.broadcasted_iota(jnp.int32, x.shape, 1)
    step = 1
    while step < n:
        x = op(x, jnp.where(lane >= step, pltpu.roll(x, step, axis=1), identity))
        step *= 2
    return x


def _block_diag(n, group, value):
    r = jnp.arange(n) // group
    return jnp.where(r[:, None] == r[None, :], value, 0.0).astype(BF16)


def _const_spec(shape):
    nd = len(shape)
    return pl.BlockSpec(shape, lambda *_: (0,) * nd)


def _params(sem):
    return pltpu.CompilerParams(dimension_semantics=sem, vmem_limit_bytes=VMEM_LIMIT)


def _ffn_block(x, nw_ref, wg_ref, wu_ref, wd_ref):
    xn = _rms_rows(x, nw_ref[...]).astype(BF16)
    acc = None
    for lo, hi in zip(FFN_SPLITS[:-1], FFN_SPLITS[1:]):
        assert (hi - lo) % MXU_COLS == 0
        cols = slice(lo, hi)
        g = _dot(xn, wg_ref[:, cols])
        u = _dot(xn, wu_ref[:, cols])
        h = (g * _sigmoid(g) * u).astype(BF16)
        part = _dot(h, wd_ref[cols, :])
        acc = part if acc is None else acc + part
    return x + 0.5 * acc


def _row_spec(width):
    return pl.BlockSpec((ROW_TILE, width), lambda i: (i, 0))


WIDE_STAGE_ROWS = 128
TALL_STAGE_ROWS = 352
_HBM = pl.BlockSpec(memory_space=pl.ANY)


def _stage_weight(w_hbm, layer, rows, stage, sem, store):
    n_rows, n_cols = w_hbm.shape[1], w_hbm.shape[2]
    assert n_rows % rows == 0

    def copy(c):
        return pltpu.make_async_copy(w_hbm.at[layer, pl.ds(c * rows, rows), :],
                                     stage.at[c % 2, pl.ds(0, rows), pl.ds(0, n_cols)], sem.at[c % 2])

    copy(0).start()
    for c in range(n_rows // rows):
        if c + 1 < n_rows // rows:
            copy(c + 1).start()
        copy(c).wait()
        store(c * rows, stage[c % 2, :rows, :n_cols])


def _stage_ffn_weights(layer, wg_hbm, wu_hbm, wd_hbm, wg_scr, wu_scr, wd_scr, wide, wide_sem, tall, tall_sem):
    def into(dst):
        def store(r0, chunk):
            dst[r0:r0 + chunk.shape[0], :] = chunk.astype(BF16)
        return store

    _stage_weight(wg_hbm, layer, WIDE_STAGE_ROWS, wide, wide_sem, into(wg_scr))
    _stage_weight(wu_hbm, layer, WIDE_STAGE_ROWS, wide, wide_sem, into(wu_scr))
    _stage_weight(wd_hbm, layer, TALL_STAGE_ROWS, tall, tall_sem, into(wd_scr))


def _ffn_weight_scratch():
    return [pltpu.VMEM((D_MODEL, D_FF), BF16), pltpu.VMEM((D_MODEL, D_FF), BF16), pltpu.VMEM((D_FF, D_MODEL), BF16),
            pltpu.VMEM((2, WIDE_STAGE_ROWS, D_IN), F32), pltpu.SemaphoreType.DMA((2,)),
            pltpu.VMEM((2, TALL_STAGE_ROWS, D_MODEL), F32), pltpu.SemaphoreType.DMA((2,))]


def _store_transposed_tiles(out_ref, a):
    a_t = a.T.astype(BF16)
    for t in range(ROW_TILE // ATT_TILE):
        out_ref[t] = a_t[:, t * ATT_TILE:(t + 1) * ATT_TILE]


def _ffn_inproj_kernel(x_ref, fnw_ref, wg_hbm, wu_hbm, wd_hbm, nw_ref, w_hbm, qkw_ref, g32_ref, gbias_ref,
                       alog_ref, convw_ref, convb_ref,
                       x_out_ref, mqt_ref, mk_ref, mvt_ref, mot_ref, gt_ref,
                       zt_ref, sb_ref, sct_ref, sxt_ref, sxdt_ref, sa_ref, qt_ref, k_ref, vt_ref,
                       halo_scr, w_scr, wg_ref, wu_ref, wd_ref, wide, wide_sem, tall, tall_sem,
                       *, tiles_per_seq, layer):
    @pl.when(pl.program_id(0) == 0)
    def _():
        _stage_ffn_weights(layer, wg_hbm, wu_hbm, wd_hbm, wg_ref, wu_ref, wd_ref, wide, wide_sem, tall, tall_sem)
        w_scr[:, _C_SMALL:_C_QK] = jnp.zeros((D_MODEL, LANES), BF16)

        def regrouped(r0, chunk):
            for dst, src, width in _regroup_plan():
                w_scr[r0:r0 + chunk.shape[0], dst:dst + width] = chunk[:, src:src + width].astype(BF16)

        _stage_weight(w_hbm, layer, WIDE_STAGE_ROWS, wide, wide_sem, regrouped)

    @pl.when(pl.program_id(0) % tiles_per_seq == 0)
    def _():
        halo_scr[...] = jnp.zeros_like(halo_scr)

    x = _ffn_block(x_ref[...], fnw_ref, wg_ref, wu_ref, wd_ref)
    x_out_ref[...] = x
    xn = _rms_rows(x, nw_ref[...]).astype(BF16)

    def proj(lo, hi):
        return _dot(xn, w_scr[:, lo:hi])

    half = (_C_V - _C_QK) // 2
    p_xbc = proj(_C_XBC, _C_SMALL)
    p_small = proj(_C_SMALL, _C_QK)
    p_q, p_k = proj(_C_QK, _C_QK + half), proj(_C_QK + half, _C_V)
    p_mq = proj(_C_M, _C_M + MLSTM_WIDTH)
    p_mk = proj(_C_M + MLSTM_WIDTH, _C_M + 2 * MLSTM_WIDTH)
    p_mv = proj(_C_M + 2 * MLSTM_WIDTH, _C_M + 3 * MLSTM_WIDTH)
    p_mo = proj(_C_M + 3 * MLSTM_WIDTH, _C_Z)
    p_z = proj(_C_Z, _C_XBC)
    p_v = proj(_C_V, _C_END)

    qk = [a * lax.rsqrt(_group_mean_sq(a, g32_ref[...]) + NORM_EPS) * qkw_ref[:, i * half:(i + 1) * half]
          for i, a in enumerate((p_q, p_k))]
    _store_transposed_tiles(qt_ref, qk[0])
    k_ref[...] = qk[1].astype(BF16)
    _store_transposed_tiles(vt_ref, p_v)

    ext = jnp.concatenate([halo_scr[...], p_xbc], axis=0)
    halo_scr[...] = p_xbc[ROW_TILE - 8:, :]
    conv = convb_ref[...] + convw_ref[SSM_CONV - 1:SSM_CONV, :] * p_xbc
    for j in range(1, SSM_CONV):
        conv = conv + convw_ref[SSM_CONV - 1 - j:SSM_CONV - j, :] * ext[8 - j:8 - j + ROW_TILE, :]
    xa = conv * _sigmoid(conv)
    sb_ref[...] = xa[:, SSM_WIDTH:SSM_WIDTH + SSM_GROUPS * SSM_STATE].astype(BF16)
    _store_transposed_tiles(sct_ref, xa[:, SSM_WIDTH + SSM_GROUPS * SSM_STATE:])
    small_t = p_small.T + gbias_ref[...]
    dt = _softplus(small_t[_LANE_DT:_LANE_DT + SSM_HEADS])
    log_decay = dt * (-jnp.exp(alog_ref[...]))
    xs_t = xa[:, :SSM_WIDTH].T
    xdt_t = xs_t * jnp.concatenate([jnp.broadcast_to(dt[h:h + 1], (SSM_HEAD_DIM, ROW_TILE))
                                    for h in range(SSM_HEADS)], axis=0)
    for t in range(ROW_TILE // MIX_CHUNK):
        cols = slice(t * MIX_CHUNK, (t + 1) * MIX_CHUNK)
        sxt_ref[t] = xs_t[:, cols].astype(BF16)
        sxdt_ref[t] = xdt_t[:, cols].astype(BF16)
        sa_ref[t] = _scan_lanes(log_decay[:, cols], jnp.add, 0.0)
    _store_transposed_tiles(zt_ref, p_z)

    _store_transposed_tiles(mqt_ref, p_mq)
    mk_ref[...] = p_mk.astype(BF16)
    _store_transposed_tiles(mvt_ref, p_mv)
    _store_transposed_tiles(mot_ref, p_mo)
    gates = small_t[:_LANE_DT]
    for t in range(ROW_TILE // MIX_CHUNK):
        i_pre = gates[_LANE_I:_LANE_I + 8, t * MIX_CHUNK:(t + 1) * MIX_CHUNK]
        f_pre = gates[_LANE_F:_LANE_F + 8, t * MIX_CHUNK:(t + 1) * MIX_CHUNK]
        log_f = jnp.minimum(f_pre, 0.0) - jnp.log(1.0 + jnp.exp(-jnp.abs(f_pre)))
        b = _scan_lanes(log_f, jnp.add, 0.0)
        c = i_pre - b
        gt_ref[t] = jnp.concatenate([c, _scan_lanes(c, jnp.maximum, -jnp.inf), b], axis=0)


def _ffn_inproj(x, ffn_norm_w, wg, wu, wd, norm_w, w_all, qkw_row, g32, gate_bias_rows, alog_rows, conv_w, conv_b_row,
                layer, seq):
    n = x.shape[0]
    tiles = ROW_TILE // ATT_TILE

    def t_out(width, dtype=BF16):
        return (pl.BlockSpec((tiles, width, ATT_TILE), lambda i: (i, 0, 0)),
                jax.ShapeDtypeStruct((n // ATT_TILE, width, ATT_TILE), dtype))

    def r_out(width, dtype=BF16):
        return _row_spec(width), jax.ShapeDtypeStruct((n, width), dtype)

    outs = [r_out(D_MODEL, F32),
            t_out(MLSTM_WIDTH), r_out(MLSTM_WIDTH), t_out(MLSTM_WIDTH), t_out(MLSTM_WIDTH), t_out(GATE_ROWS, F32),
            t_out(SSM_WIDTH), r_out(SSM_GROUPS * SSM_STATE), t_out(SSM_GROUPS * SSM_STATE), t_out(SSM_WIDTH),
            t_out(SSM_WIDTH), t_out(SSM_HEADS, F32),
            t_out(DIFF_WIDTH), r_out(DIFF_WIDTH), t_out(DIFF_WIDTH)]
    return pl.pallas_call(
        functools.partial(_ffn_inproj_kernel, tiles_per_seq=seq // ROW_TILE, layer=layer),
        grid=(n // ROW_TILE,),
        in_specs=[_row_spec(D_MODEL), _const_spec((1, D_MODEL)), _HBM, _HBM, _HBM, _const_spec((1, D_MODEL)), _HBM,
                  _const_spec((1, 2 * DIFF_WIDTH)), _const_spec((DIFF_WIDTH, DIFF_WIDTH)),
                  _const_spec((LANES, ROW_TILE)), _const_spec((SSM_HEADS, ROW_TILE)),
                  _const_spec((SSM_CONV, SSM_CONV_DIM)), _const_spec((1, SSM_CONV_DIM))],
        out_specs=[o[0] for o in outs],
        out_shape=[o[1] for o in outs],
        scratch_shapes=[pltpu.VMEM((8, SSM_CONV_DIM), F32), pltpu.VMEM((D_MODEL, _C_END), BF16)]
        + _ffn_weight_scratch(),
        compiler_params=_params(("arbitrary",)),
        name="ffn_inproj",
    )(x, ffn_norm_w.reshape(1, D_MODEL), wg, wu, wd, norm_w.reshape(1, D_MODEL), w_all, qkw_row, g32,
      gate_bias_rows, alog_rows, conv_w, conv_b_row)


def _mlstm_kernel(qt_ref, k_ref, vt_ref, ot_ref, gt_ref, nw_ref, yt_ref, ct_scr, nt_scr, m_scr):
    @pl.when(pl.program_id(1) == 0)
    def _():
        ct_scr[...] = jnp.zeros_like(ct_scr)
        nt_scr[...] = jnp.zeros_like(nt_scr)
        m_scr[...] = jnp.zeros_like(m_scr)

    for t in range(MIX_STEP_CHUNKS):
        _mlstm_chunk(t, qt_ref, k_ref, vt_ref, ot_ref, gt_ref, nw_ref, yt_ref, ct_scr, nt_scr, m_scr)


def _mlstm_chunk(t, qt_ref, k_ref, vt_ref, ot_ref, gt_ref, nw_ref, yt_ref, ct_scr, nt_scr, m_scr):
    L = MIX_CHUNK
    H = MLSTM_HEADS
    W = MLSTM_HEAD_DIM
    qt = qt_ref[t]
    ks = k_ref[t * L:(t + 1) * L, :] * (MLSTM_HEAD_DIM ** -0.5)
    vt = vt_ref[t]
    gt = gt_ref[t]
    c, c_max, b = gt[0:8], gt[8:16], gt[16:24]
    m_prev = m_scr[...]
    big_m = jnp.maximum(m_prev, c_max)
    inter = jnp.exp(m_prev - big_m)
    floor = jnp.exp(-(b + big_m))
    b_last = jnp.broadcast_to(b[:, L - 1:L], b.shape)
    m_new = b_last + jnp.maximum(m_prev, jnp.broadcast_to(c_max[:, L - 1:L], b.shape))
    w = jnp.exp(b_last + c - m_new)
    decay = jnp.exp(b_last + m_prev - m_new)

    c2_cols = jnp.concatenate([c * LOG2E, jnp.zeros((LANES - 8, L), F32)], axis=0).T
    big_m2 = big_m * LOG2E
    causal = (lax.broadcasted_iota(jnp.int32, (L, L), 0) <= lax.broadcasted_iota(jnp.int32, (L, L), 1))
    row_head = lax.broadcasted_iota(jnp.int32, (H * W, L), 0) // W
    qk = [_dot(ks, jnp.where(row_head == h, qt, jnp.zeros_like(qt))) for h in range(H)]
    nums, dens = [], []
    for h in range(H):
        d = jnp.exp2(jnp.where(causal, c2_cols[:, h:h + 1] - big_m2[h:h + 1, :], -jnp.inf))
        s = qk[h] * d
        dens.append(jnp.sum(s, axis=0, keepdims=True))
        nums.append(_dot(vt[h * W:(h + 1) * W, :], s.astype(BF16)))
    cq = _dot(ct_scr[...].astype(BF16), qt)
    nq = _dot(nt_scr[...].astype(BF16), qt)
    outs = []
    for h in range(H):
        num = nums[h] + inter[h:h + 1] * cq[h * W:(h + 1) * W]
        den = dens[h] + inter[h:h + 1] * nq[h:h + 1]
        hh = num / jnp.maximum(jnp.abs(den), floor[h:h + 1])
        outs.append(hh * lax.rsqrt(jnp.mean(hh * hh, axis=0, keepdims=True) + NORM_EPS))
    hn = jnp.concatenate(outs, axis=0) * nw_ref[...]
    yt_ref[t] = (_sigmoid(ot_ref[t].astype(F32)) * hn).astype(BF16)

    w_full = jnp.concatenate([jnp.broadcast_to(w[h:h + 1], (W, L)) for h in range(H)], axis=0)
    g_new = _dot((vt.astype(F32) * w_full).astype(BF16), ks)
    decay_full = jnp.concatenate([jnp.broadcast_to(decay[h:h + 1, :1], (W, H * W)) for h in range(H)], axis=0)
    same_head = (lax.broadcasted_iota(jnp.int32, (H * W, H * W), 0) // W
                 == lax.broadcasted_iota(jnp.int32, (H * W, H * W), 1) // W)
    ct_scr[...] = decay_full * ct_scr[...] + jnp.where(same_head, g_new, 0.0)
    w16 = jnp.concatenate([w, jnp.zeros_like(w)], axis=0).astype(BF16)
    n_new = _dot(w16, ks)
    own_lanes = (lax.broadcasted_iota(jnp.int32, (16, H * W), 0)
                 == lax.broadcasted_iota(jnp.int32, (16, H * W), 1) // W)
    decay16 = jnp.broadcast_to(jnp.concatenate([decay[:, :1], jnp.zeros((8, 1), F32)], axis=0), (16, H * W))
    nt_scr[...] = decay16 * nt_scr[...] + jnp.where(own_lanes, n_new, 0.0)
    m_scr[...] = m_new


def _mlstm(mqt, mk, mvt, mot, gt, norm_w_rows, batch, seq):
    ns = seq // (MIX_CHUNK * MIX_STEP_CHUNKS)
    n = batch * seq
    t_spec = pl.BlockSpec((MIX_STEP_CHUNKS, MLSTM_WIDTH, MIX_CHUNK), lambda b, c: (b * ns + c, 0, 0))
    return pl.pallas_call(
        _mlstm_kernel,
        grid=(batch, ns),
        in_specs=[t_spec, pl.BlockSpec((MIX_STEP_CHUNKS * MIX_CHUNK, MLSTM_WIDTH), lambda b, c: (b * ns + c, 0)),
                  t_spec, t_spec,
                  pl.BlockSpec((MIX_STEP_CHUNKS, GATE_ROWS, MIX_CHUNK), lambda b, c: (b * ns + c, 0, 0)),
                  _const_spec((MLSTM_WIDTH, MIX_CHUNK))],
        out_specs=t_spec,
        out_shape=jax.ShapeDtypeStruct((n // MIX_CHUNK, MLSTM_WIDTH, MIX_CHUNK), BF16),
        scratch_shapes=[pltpu.VMEM((MLSTM_WIDTH, MLSTM_WIDTH), F32),
                        pltpu.VMEM((16, MLSTM_WIDTH), F32),
                        pltpu.VMEM((8, MIX_CHUNK), F32)],
        compiler_params=_params(("arbitrary", "arbitrary")),
        name="mlstm",
    )(mqt, mk, mvt, mot, gt, norm_w_rows)


def _ssd_kernel(b_ref, ct_ref, xt_ref, xdt_ref, zt_ref, a_ref, dskip_ref, nw_ref, yt_ref, st_scr):
    @pl.when(pl.program_id(1) == 0)
    def _():
        st_scr[...] = jnp.zeros_like(st_scr)

    for t in range(MIX_STEP_CHUNKS):
        _ssd_chunk(t, b_ref, ct_ref, xt_ref, xdt_ref, zt_ref, a_ref, dskip_ref, nw_ref, yt_ref, st_scr)


def _ssd_chunk(t, b_ref, ct_ref, xt_ref, xdt_ref, zt_ref, a_ref, dskip_ref, nw_ref, yt_ref, st_scr):
    L = MIX_CHUNK
    P = SSM_HEAD_DIM
    HG = SSM_HEADS // SSM_GROUPS
    GW = HG * P
    NS = SSM_STATE
    a = a_ref[t]
    a_last = jnp.broadcast_to(a[:, L - 1:L], a.shape)
    exp_a = jnp.exp(a)
    d_in = jnp.exp(a_last - a)
    chunk_decay = jnp.exp(a_last)
    a2 = a * LOG2E
    a2_cols = jnp.concatenate([a2, jnp.zeros((LANES - SSM_HEADS, L), F32)], axis=0).T
    causal = (lax.broadcasted_iota(jnp.int32, (L, L), 0) <= lax.broadcasted_iota(jnp.int32, (L, L), 1))
    bn = b_ref[t * L:(t + 1) * L, :]
    ct = ct_ref[t]
    xdt = xdt_ref[t]

    def per_head_rows(rows, grp, width):
        return jnp.concatenate([jnp.broadcast_to(rows[grp * HG + i:grp * HG + i + 1, :width], (P, width))
                                for i in range(HG)], axis=0)

    scores, y_off = [], []
    for grp in range(SSM_GROUPS):
        bg = bn[:, grp * NS:(grp + 1) * NS]
        cg = ct[grp * NS:(grp + 1) * NS, :]
        scores.append(_dot(bg, cg))
        st = st_scr[grp]
        y_off.append(_dot(st.astype(BF16), cg))
        x_in = (xdt[grp * GW:(grp + 1) * GW, :].astype(F32) * per_head_rows(d_in, grp, L)).astype(BF16)
        st_scr[grp] = per_head_rows(chunk_decay, grp, NS) * st + _dot(x_in, bg)
    for grp in range(SSM_GROUPS):
        y_diag = []
        for i in range(HG):
            h = grp * HG + i
            seg = jnp.exp2(jnp.where(causal, a2[h:h + 1, :] - a2_cols[:, h:h + 1], -jnp.inf))
            p = (scores[grp] * seg).astype(BF16)
            y_diag.append(_dot(xdt[h * P:(h + 1) * P, :], p))
        rows = slice(grp * GW, (grp + 1) * GW)
        y = (jnp.concatenate(y_diag, axis=0) + y_off[grp] * per_head_rows(exp_a, grp, L)
             + dskip_ref[rows, :] * xt_ref[t, rows, :].astype(F32))
        zg = zt_ref[t, rows, :].astype(F32)
        y = y * (zg * _sigmoid(zg))
        ms = jnp.mean(y * y, axis=0, keepdims=True)
        yt_ref[t, rows, :] = (y * lax.rsqrt(ms + NORM_EPS) * nw_ref[rows, :]).astype(BF16)


def _ssd(sb, sct, sxt, sxdt, zt, sa, dskip_rows, norm_w_rows, batch, seq):
    ns = seq // (MIX_CHUNK * MIX_STEP_CHUNKS)
    n = batch * seq

    def t_spec(width):
        return pl.BlockSpec((MIX_STEP_CHUNKS, width, MIX_CHUNK), lambda b, c: (b * ns + c, 0, 0))

    return pl.pallas_call(
        _ssd_kernel,
        grid=(batch, ns),
        in_specs=[pl.BlockSpec((MIX_STEP_CHUNKS * MIX_CHUNK, SSM_GROUPS * SSM_STATE), lambda b, c: (b * ns + c, 0)),
                  t_spec(SSM_GROUPS * SSM_STATE), t_spec(SSM_WIDTH), t_spec(SSM_WIDTH), t_spec(SSM_WIDTH),
                  t_spec(SSM_HEADS), _const_spec((SSM_WIDTH, MIX_CHUNK)), _const_spec((SSM_WIDTH, MIX_CHUNK))],
        out_specs=t_spec(SSM_WIDTH),
        out_shape=jax.ShapeDtypeStruct((n // MIX_CHUNK, SSM_WIDTH, MIX_CHUNK), BF16),
        scratch_shapes=[pltpu.VMEM((SSM_GROUPS, SSM_WIDTH // SSM_GROUPS, SSM_STATE), F32)],
        compiler_params=_params(("arbitrary", "arbitrary")),
        name="ssd",
    )(sb, sct, sxt, sxdt, zt, sa, dskip_rows, norm_w_rows)


def _t5_bias_tile(rel, relb_ref, head):
    n = jnp.maximum(rel, 0)
    max_exact = REL_BUCKETS // 2
    nf = jnp.maximum(n, 1).astype(F32)
    large = max_exact + (jnp.log(nf / max_exact) / math.log(REL_MAX_DIST / max_exact)
                         * (REL_BUCKETS - max_exact)).astype(jnp.int32)
    large = jnp.minimum(large, REL_BUCKETS - 1)
    bucket = jnp.where(n < max_exact, n, large)
    far = relb_ref[REL_BUCKETS - 1, head]
    bias = jnp.zeros(rel.shape, F32)
    for bkt in range(REL_BUCKETS - 1):
        bias = jnp.where(bucket == bkt, (relb_ref[bkt, head] - far) * LOG2E, bias)
    return jnp.where(rel >= 0, bias, NEG_BIG)


def _attn_kernel(relb_ref, qt_ref, k_ref, vt_ref, lam_ref, subw_ref, g64_ref,
                 y_ref, bias_scr, qm_scr, m_scr, acc_scr, sa_scr, sb_scr, *, lam_init):
    T = ATT_TILE
    H = DIFF_HEADS
    DV = DIFF_V_DIM
    i = pl.program_id(1)

    @pl.when((pl.program_id(0) == 0) & (i == 0))
    def _():
        rel = lax.broadcasted_iota(jnp.int32, (T, T), 1) - lax.broadcasted_iota(jnp.int32, (T, T), 0)
        for h in range(H):
            bias_scr[0, h] = _t5_bias_tile(rel, relb_ref, h)
            bias_scr[1, h] = _t5_bias_tile(rel + T, relb_ref, h)

    qt = qt_ref[0]
    comp_of_row = lax.broadcasted_iota(jnp.int32, (DIFF_WIDTH, T), 0) // DIFF_QK_DIM
    for hc in range(2 * H):
        qm_scr[hc] = jnp.where(comp_of_row == hc, qt, jnp.zeros_like(qt))
    m_scr[...] = jnp.full(m_scr.shape, NEG_BIG, F32)
    acc_scr[...] = jnp.zeros_like(acc_scr)
    ones_rows = jnp.ones((ONES_ROWS, T), BF16)

    def logits_one(s_ref, kt, hc):
        s_ref[hc] = _dot(kt, qm_scr[hc])

    def key_tile(j):
        return k_ref[pl.ds(pl.multiple_of(j * T, T), T), :]

    def logits_into(s_ref, j):
        kt = key_tile(j)
        for hc in range(2 * H):
            logits_one(s_ref, kt, hc)

    def consume_one(s_ref, vt, hc, bias_idx):
        h = hc // 2
        s = s_ref[hc]
        if bias_idx is not None:
            s = bias_scr[bias_idx, h] + s
        m_prev = m_scr[hc]
        m_next = jnp.maximum(m_prev, jnp.max(s, axis=0, keepdims=True))
        alpha = jnp.exp2(m_prev - m_next)
        p = jnp.exp2(s - m_next).astype(BF16)
        m_scr[hc] = m_next
        v_aug = jnp.concatenate([vt[h * DV:(h + 1) * DV, :], ones_rows], axis=0)
        acc_scr[hc] = alpha * acc_scr[hc] + _dot(v_aug, p)

    def consume(s_ref, j, bias_idx):
        vt = vt_ref[j]
        for hc in range(2 * H):
            consume_one(s_ref, vt, hc, bias_idx)

    LEAD = 2

    def step(s_next, j_next, s_cur, j_cur, bias_idx):
        kt = key_tile(j_next)
        vt = vt_ref[j_cur]
        for hc in range(LEAD):
            logits_one(s_next, kt, hc)
        for hc in range(2 * H):
            if hc + LEAD < 2 * H:
                logits_one(s_next, kt, hc + LEAD)
            consume_one(s_cur, vt, hc, bias_idx)

    DIAG, NEAR = 0, 1
    n_far = jnp.maximum(i - 1, 0)
    logits_into(sa_scr, 0)

    def far_pair(t, carry):
        j = 2 * t
        step(sb_scr, j + 1, sa_scr, j, None)
        step(sa_scr, j + 2, sb_scr, j + 1, None)
        return carry

    lax.fori_loop(0, n_far // 2, far_pair, 0)
    u = 2 * (n_far // 2)
    odd = n_far % 2 == 1

    @pl.when(odd)
    def _():
        step(sb_scr, u + 1, sa_scr, u, None)
        step(sa_scr, u + 2, sb_scr, u + 1, NEAR)
        consume(sa_scr, u + 2, DIAG)

    @pl.when(jnp.logical_not(odd) & (i >= 1))
    def _():
        step(sb_scr, u + 1, sa_scr, u, NEAR)
        consume(sb_scr, u + 1, DIAG)

    @pl.when(i == 0)
    def _():
        consume(sa_scr, 0, DIAG)

    lam_p = lam_ref[...]
    lam = (jnp.exp(jnp.sum(lam_p[0:1] * lam_p[1:2], axis=1, keepdims=True))
           - jnp.exp(jnp.sum(lam_p[2:3] * lam_p[3:4], axis=1, keepdims=True)) + lam_init)

    def normalised(hc):
        return acc_scr[hc, :DV, :] / acc_scr[hc, DV:DV + 1, :]

    o_t = jnp.concatenate([normalised(2 * h) - lam * normalised(2 * h + 1) for h in range(H)], axis=0)
    o = o_t.T
    ms = _group_mean_sq(o, g64_ref[...])
    y_ref[...] = (o * lax.rsqrt(ms + NORM_EPS) * subw_ref[...] * (1.0 - lam_init)).astype(BF16)


def _diff_attn(qt, kn, vt, rel_bias, lambdas, subw_row, g64, lam_init, batch, seq):
    nq = seq // ATT_TILE
    n = batch * seq
    return pl.pallas_call(
        functools.partial(_attn_kernel, lam_init=lam_init),
        grid=(batch, nq),
        in_specs=[pl.BlockSpec(memory_space=pltpu.SMEM),
                  pl.BlockSpec((1, DIFF_WIDTH, ATT_TILE), lambda b, i: (b * nq + i, 0, 0)),
                  pl.BlockSpec((seq, DIFF_WIDTH), lambda b, i: (b, 0)),
                  pl.BlockSpec((nq, DIFF_WIDTH, ATT_TILE), lambda b, i: (b, 0, 0)),
                  _const_spec((4, DIFF_QK_DIM)), _const_spec((1, DIFF_WIDTH)),
                  _const_spec((DIFF_WIDTH, DIFF_WIDTH))],
        out_specs=pl.BlockSpec((ATT_TILE, DIFF_WIDTH), lambda b, i: (b * nq + i, 0)),
        out_shape=jax.ShapeDtypeStruct((n, DIFF_WIDTH), BF16),
        scratch_shapes=[pltpu.VMEM((2, DIFF_HEADS, ATT_TILE, ATT_TILE), F32),
                        pltpu.VMEM((2 * DIFF_HEADS, DIFF_WIDTH, ATT_TILE), BF16),
                        pltpu.VMEM((2 * DIFF_HEADS, 1, ATT_TILE), F32),
                        pltpu.VMEM((2 * DIFF_HEADS, DIFF_V_DIM + ONES_ROWS, ATT_TILE), F32),
                        pltpu.VMEM((2 * DIFF_HEADS, ATT_TILE, ATT_TILE), F32),
                        pltpu.VMEM((2 * DIFF_HEADS, ATT_TILE, ATT_TILE), F32)],
        compiler_params=_params(("arbitrary", "arbitrary")),
        name="diff_attn",
    )(rel_bias, qt, kn, vt, lambdas, subw_row, g64)


def _untransposed(yt_ref):
    return jnp.concatenate([yt_ref[t].astype(F32).T for t in range(yt_ref.shape[0])], axis=0).astype(BF16)


def _outproj_ffn_kernel(x_ref, ymt_ref, yst_ref, yd_ref, w_hbm, fnw_ref, wg_hbm, wu_hbm, wd_hbm, o_ref,
                        w_ref, wg_ref, wu_ref, wd_ref, wide, wide_sem, tall, tall_sem, *, layer):
    @pl.when(pl.program_id(0) == 0)
    def _():
        def store(r0, chunk):
            w_ref[r0:r0 + chunk.shape[0], :] = chunk.astype(BF16)

        _stage_weight(w_hbm, layer, D_MODEL // 4, tall, tall_sem, store)
        _stage_ffn_weights(layer, wg_hbm, wu_hbm, wd_hbm, wg_ref, wu_ref, wd_ref, wide, wide_sem, tall, tall_sem)

    y = jnp.concatenate([_untransposed(ymt_ref), _untransposed(yst_ref), yd_ref[...]], axis=1)
    x = x_ref[...] + _dot(y, w_ref[...])
    o_ref[...] = _ffn_block(x, fnw_ref, wg_ref, wu_ref, wd_ref)


def _outproj_ffn(x, y_m, y_s, y_d, w_out, ffn_norm_w, wg, wu, wd, layer):
    n = x.shape[0]
    return pl.pallas_call(
        functools.partial(_outproj_ffn_kernel, layer=layer),
        grid=(n // ROW_TILE,),
        in_specs=[_row_spec(D_MODEL),
                  pl.BlockSpec((ROW_TILE // MIX_CHUNK, MLSTM_WIDTH, MIX_CHUNK), lambda i: (i, 0, 0)),
                  pl.BlockSpec((ROW_TILE // MIX_CHUNK, SSM_WIDTH, MIX_CHUNK), lambda i: (i, 0, 0)),
                  _row_spec(DIFF_WIDTH), _HBM, _const_spec((1, D_MODEL)), _HBM, _HBM, _HBM],
        out_specs=_row_spec(D_MODEL),
        out_shape=jax.ShapeDtypeStruct((n, D_MODEL), F32),
        scratch_shapes=[pltpu.VMEM((D_MODEL, D_MODEL), BF16)] + _ffn_weight_scratch(),
        compiler_params=_params(("arbitrary",)),
        name="outproj_ffn",
    )(x, y_m, y_s, y_d, w_out, ffn_norm_w.reshape(1, D_MODEL), wg, wu, wd)


def _small_row(values_by_lane):
    row = jnp.zeros((LANES,), F32)
    for lane, vals in values_by_lane:
        row = lax.dynamic_update_slice(row, vals.astype(F32), (lane,))
    return row.reshape(1, LANES)


def kernel(x, ffn1_norm_w, ffn1_w_gate, ffn1_w_up, ffn1_w_down, mix_norm_w, w_in, mlstm_gate_bias, mlstm_norm_w, ssm_conv_w, ssm_conv_b, ssm_dt_bias, ssm_A_log, ssm_D, ssm_norm_w, diff_q_norm_w, diff_k_norm_w, diff_lambda, diff_subln_w, rel_bias, w_out, ffn2_norm_w, ffn2_w_gate, ffn2_w_up, ffn2_w_down):
    batch, seq, d = x.shape
    assert d == D_MODEL and MIX_CHUNK == ATT_TILE and seq % ROW_TILE == 0 and ROW_TILE % MIX_CHUNK == 0
    assert seq % (MIX_CHUNK * MIX_STEP_CHUNKS) == 0
    xf = x.reshape(batch * seq, D_MODEL)
    g32 = _block_diag(DIFF_WIDTH, DIFF_QK_DIM, 1.0 / DIFF_QK_DIM)
    g64 = _block_diag(DIFF_WIDTH, DIFF_V_DIM, 1.0 / DIFF_V_DIM)
    ffn1 = (ffn1_w_gate, ffn1_w_up, ffn1_w_down)
    ffn2 = (ffn2_w_gate, ffn2_w_up, ffn2_w_down)
    for l in range(DEPTH):
        qkw_row = jnp.concatenate([jnp.tile(diff_q_norm_w[l].reshape(-1), DIFF_HEADS) * (DIFF_QK_DIM ** -0.5 * LOG2E),
                                   jnp.tile(diff_k_norm_w[l].reshape(-1), DIFF_HEADS)]).reshape(1, 2 * DIFF_WIDTH)
        bias_row = _small_row([(_LANE_I, mlstm_gate_bias[l, 0]), (_LANE_F, mlstm_gate_bias[l, 1]),
                               (_LANE_DT, ssm_dt_bias[l])])
        (xf, mqt, mk, mvt, mot, gt, zt, sb, sct, sxt, sxdt, sa, qt, kn, vt) = _ffn_inproj(
            xf, ffn1_norm_w[l], *ffn1, mix_norm_w[l], w_in, qkw_row, g32,
            jnp.broadcast_to(bias_row.reshape(LANES, 1), (LANES, ROW_TILE)),
            jnp.broadcast_to(ssm_A_log[l][:, None], (SSM_HEADS, ROW_TILE)),
            ssm_conv_w[l], ssm_conv_b[l].reshape(1, SSM_CONV_DIM), l, seq)
        y_m = _mlstm(mqt, mk, mvt, mot, gt, jnp.broadcast_to(mlstm_norm_w[l][:, None], (MLSTM_WIDTH, MIX_CHUNK)),
                     batch, seq)
        y_s = _ssd(sb, sct, sxt, sxdt, zt, sa,
                   jnp.broadcast_to(jnp.repeat(ssm_D[l], SSM_HEAD_DIM)[:, None], (SSM_WIDTH, MIX_CHUNK)),
                   jnp.broadcast_to(ssm_norm_w[l][:, None], (SSM_WIDTH, MIX_CHUNK)), batch, seq)
        lam_init = 0.8 - 0.6 * math.exp(-0.3 * l)
        y_d = _diff_attn(qt, kn, vt, rel_bias, diff_lambda[l], jnp.tile(diff_subln_w[l], DIFF_HEADS).reshape(1, DIFF_WIDTH),
                         g64, lam_init, batch, seq)
        xf = _outproj_ffn(xf, y_m, y_s, y_d, w_out, ffn2_norm_w[l], *ffn2, l)
    return xf.reshape(batch, seq, D_MODEL)
```

```python
import functools
import math

import jax
import jax.numpy as jnp
from jax import lax
from jax.experimental import pallas as pl
from jax.experimental.pallas import tpu as pltpu

F32 = jnp.float32
BF16 = jnp.bfloat16

D_MODEL = 1024
DEPTH = 2
D_FF = 2816
MLSTM_HEADS = 4
MLSTM_HEAD_DIM = 64
MLSTM_WIDTH = MLSTM_HEADS * MLSTM_HEAD_DIM
SSM_HEADS = 8
SSM_HEAD_DIM = 64
SSM_WIDTH = SSM_HEADS * SSM_HEAD_DIM
SSM_STATE = 128
SSM_GROUPS = 2
SSM_CONV = 4
SSM_CONV_DIM = SSM_WIDTH + 2 * SSM_GROUPS * SSM_STATE
DIFF_HEADS = 4
DIFF_QK_DIM = 32
DIFF_V_DIM = 64
DIFF_WIDTH = DIFF_HEADS * DIFF_V_DIM
REL_BUCKETS = 32
REL_MAX_DIST = 128
NORM_EPS = 1e-6

LANES = 128
ROW_TILE = 512
MXU_COLS = 256
FFN_SPLITS = (0, 6 * MXU_COLS, D_FF)
MIX_CHUNK = 256
MIX_STEP_CHUNKS = 4
ATT_TILE = 256
VMEM_LIMIT = 56 * 1024 * 1024
NEG_BIG = -1e30
LOG2E = math.log2(math.e)
ONES_ROWS = 16

_C_M = 0
_C_Z = _C_M + 4 * MLSTM_WIDTH
_C_XBC = _C_Z + SSM_WIDTH
_C_SMALL = _C_XBC + SSM_CONV_DIM
_C_QK = _C_SMALL + LANES
_C_V = _C_QK + 4 * DIFF_HEADS * DIFF_QK_DIM
_C_END = _C_V + DIFF_WIDTH
_LANE_I = 0
_LANE_F = 8
_LANE_DT = 16
GATE_ROWS = 24
D_IN = 4 * MLSTM_WIDTH + 2 * MLSTM_HEADS + SSM_WIDTH + SSM_CONV_DIM + SSM_HEADS + 3 * DIFF_WIDTH


def _regroup_plan():
    sizes = (4 * MLSTM_WIDTH, MLSTM_HEADS, MLSTM_HEADS, SSM_WIDTH, SSM_CONV_DIM, SSM_HEADS, 3 * DIFF_WIDTH)
    dests = (_C_M, _C_SMALL + _LANE_I, _C_SMALL + _LANE_F, _C_Z, _C_XBC, _C_SMALL + _LANE_DT, _C_QK)
    plan, src = [], 0
    for dst, width in zip(dests, sizes):
        plan.append((dst, src, width))
        src += width
    assert src == D_IN
    return plan


def _dot(a, b):
    return jnp.dot(a, b, preferred_element_type=F32)


def _sigmoid(x):
    return 1.0 / (1.0 + jnp.exp(-x))


def _softplus(x):
    return jnp.maximum(x, 0.0) + jnp.log(1.0 + jnp.exp(-jnp.abs(x)))


def _rms_rows(x, w_row):
    ms = jnp.mean(x * x, axis=-1, keepdims=True)
    return x * lax.rsqrt(ms + NORM_EPS) * w_row


def _group_mean_sq(x, gmat):
    sq = x * x
    hi = sq.astype(BF16)
    lo = (sq - hi.astype(F32)).astype(BF16)
    return _dot(hi, gmat) + _dot(lo, gmat)


def _scan_lanes(x, op, identity):
    n = x.shape[1]
    lane = lax.broadcasted_iota(jnp.int32, x.shape, 1)
    step = 1
    while step < n:
        x = op(x, jnp.where(lane >= step, pltpu.roll(x, step, axis=1), identity))
        step *= 2
    return x


def _block_diag(n, group, value):
    r = jnp.arange(n) // group
    return jnp.where(r[:, None] == r[None, :], value, 0.0).astype(BF16)


def _const_spec(shape):
    nd = len(shape)
    return pl.BlockSpec(shape, lambda *_: (0,) * nd)


def _params(sem):
    return pltpu.CompilerParams(dimension_semantics=sem, vmem_limit_bytes=VMEM_LIMIT)


def _ffn_block(x, nw_ref, wg_ref, wu_ref, wd_ref):
    xn = _rms_rows(x, nw_ref[...]).astype(BF16)
    acc = None
    for lo, hi in zip(FFN_SPLITS[:-1], FFN_SPLITS[1:]):
        assert (hi - lo) % MXU_COLS == 0
        cols = slice(lo, hi)
        g = _dot(xn, wg_ref[:, cols])
        u = _dot(xn, wu_ref[:, cols])
        h = (g * _sigmoid(g) * u).astype(BF16)
        part = _dot(h, wd_ref[cols, :])
        acc = part if acc is None else acc + part
    return x + 0.5 * acc


def _row_spec(width):
    return pl.BlockSpec((ROW_TILE, width), lambda i: (i, 0))


WIDE_STAGE_ROWS = 128
TALL_STAGE_ROWS = 352
_HBM = pl.BlockSpec(memory_space=pl.ANY)


def _stage_weight(w_hbm, layer, rows, stage, sem, store):
    n_rows, n_cols = w_hbm.shape[1], w_hbm.shape[2]
    assert n_rows % rows == 0

    def copy(c):
        return pltpu.make_async_copy(w_hbm.at[layer, pl.ds(c * rows, rows), :],
                                     stage.at[c % 2, pl.ds(0, rows), pl.ds(0, n_cols)], sem.at[c % 2])

    copy(0).start()
    for c in range(n_rows // rows):
        if c + 1 < n_rows // rows:
            copy(c + 1).start()
        copy(c).wait()
        store(c * rows, stage[c % 2, :rows, :n_cols])


def _stage_ffn_weights(layer, wg_hbm, wu_hbm, wd_hbm, wg_scr, wu_scr, wd_scr, wide, wide_sem, tall, tall_sem):
    def into(dst):
        def store(r0, chunk):
            dst[r0:r0 + chunk.shape[0], :] = chunk.astype(BF16)
        return store

    _stage_weight(wg_hbm, layer, WIDE_STAGE_ROWS, wide, wide_sem, into(wg_scr))
    _stage_weight(wu_hbm, layer, WIDE_STAGE_ROWS, wide, wide_sem, into(wu_scr))
    _stage_weight(wd_hbm, layer, TALL_STAGE_ROWS, tall, tall_sem, into(wd_scr))


def _ffn_weight_scratch():
    return [pltpu.VMEM((D_MODEL, D_FF), BF16), pltpu.VMEM((D_MODEL, D_FF), BF16), pltpu.VMEM((D_FF, D_MODEL), BF16),
            pltpu.VMEM((2, WIDE_STAGE_ROWS, D_IN), F32), pltpu.SemaphoreType.DMA((2,)),
            pltpu.VMEM((2, TALL_STAGE_ROWS, D_MODEL), F32), pltpu.SemaphoreType.DMA((2,))]


def _store_transposed_tiles(out_ref, a):
    a_t = a.T.astype(BF16)
    for t in range(ROW_TILE // ATT_TILE):
        out_ref[t] = a_t[:, t * ATT_TILE:(t + 1) * ATT_TILE]


def _ffn_inproj_kernel(x_ref, fnw_ref, wg_hbm, wu_hbm, wd_hbm, nw_ref, w_hbm, qkw_ref, g32_ref, gbias_ref,
                       alog_ref, convw_ref, convb_ref,
                       x_out_ref, mqt_ref, mk_ref, mvt_ref, mot_ref, gt_ref,
                       zt_ref, sb_ref, sct_ref, sxt_ref, sxdt_ref, sa_ref, qt_ref, k_ref, vt_ref,
                       halo_scr, w_scr, wg_ref, wu_ref, wd_ref, wide, wide_sem, tall, tall_sem,
                       *, tiles_per_seq, layer):
    @pl.when(pl.program_id(0) == 0)
    def _():
        _stage_ffn_weights(layer, wg_hbm, wu_hbm, wd_hbm, wg_ref, wu_ref, wd_ref, wide, wide_sem, tall, tall_sem)
        w_scr[:, _C_SMALL:_C_QK] = jnp.zeros((D_MODEL, LANES), BF16)

        def regrouped(r0, chunk):
            for dst, src, width in _regroup_plan():
                w_scr[r0:r0 + chunk.shape[0], dst:dst + width] = chunk[:, src:src + width].astype(BF16)

        _stage_weight(w_hbm, layer, WIDE_STAGE_ROWS, wide, wide_sem, regrouped)

    @pl.when(pl.program_id(0) % tiles_per_seq == 0)
    def _():
        halo_scr[...] = jnp.zeros_like(halo_scr)

    x = _ffn_block(x_ref[...], fnw_ref, wg_ref, wu_ref, wd_ref)
    x_out_ref[...] = x
    xn = _rms_rows(x, nw_ref[...]).astype(BF16)

    def proj(lo, hi):
        return _dot(xn, w_scr[:, lo:hi])

    half = (_C_V - _C_QK) // 2
    p_xbc = proj(_C_XBC, _C_SMALL)
    p_small = proj(_C_SMALL, _C_QK)
    p_q, p_k = proj(_C_QK, _C_QK + half), proj(_C_QK + half, _C_V)
    p_mq = proj(_C_M, _C_M + MLSTM_WIDTH)
    p_mk = proj(_C_M + MLSTM_WIDTH, _C_M + 2 * MLSTM_WIDTH)
    p_mv = proj(_C_M + 2 * MLSTM_WIDTH, _C_M + 3 * MLSTM_WIDTH)
    p_mo = proj(_C_M + 3 * MLSTM_WIDTH, _C_Z)
    p_z = proj(_C_Z, _C_XBC)
    p_v = proj(_C_V, _C_END)

    qk = [a * lax.rsqrt(_group_mean_sq(a, g32_ref[...]) + NORM_EPS) * qkw_ref[:, i * half:(i + 1) * half]
          for i, a in enumerate((p_q, p_k))]
    _store_transposed_tiles(qt_ref, qk[0])
    k_ref[...] = qk[1].astype(BF16)
    _store_transposed_tiles(vt_ref, p_v)

    ext = jnp.concatenate([halo_scr[...], p_xbc], axis=0)
    halo_scr[...] = p_xbc[ROW_TILE - 8:, :]
    conv = convb_ref[...] + convw_ref[SSM_CONV - 1:SSM_CONV, :] * p_xbc
    for j in range(1, SSM_CONV):
        conv = conv + convw_ref[SSM_CONV - 1 - j:SSM_CONV - j, :] * ext[8 - j:8 - j + ROW_TILE, :]
    xa = conv * _sigmoid(conv)
    sb_ref[...] = xa[:, SSM_WIDTH:SSM_WIDTH + SSM_GROUPS * SSM_STATE].astype(BF16)
    _store_transposed_tiles(sct_ref, xa[:, SSM_WIDTH + SSM_GROUPS * SSM_STATE:])
    small_t = p_small.T + gbias_ref[...]
    dt = _softplus(small_t[_LANE_DT:_LANE_DT + SSM_HEADS])
    log_decay = dt * (-jnp.exp(alog_ref[...]))
    xs_t = xa[:, :SSM_WIDTH].T
    xdt_t = xs_t * jnp.concatenate([jnp.broadcast_to(dt[h:h + 1], (SSM_HEAD_DIM, ROW_TILE))
                                    for h in range(SSM_HEADS)], axis=0)
    for t in range(ROW_TILE // MIX_CHUNK):
        cols = slice(t * MIX_CHUNK, (t + 1) * MIX_CHUNK)
        sxt_ref[t] = xs_t[:, cols].astype(BF16)
        sxdt_ref[t] = xdt_t[:, cols].astype(BF16)
        sa_ref[t] = _scan_lanes(log_decay[:, cols], jnp.add, 0.0)
    _store_transposed_tiles(zt_ref, p_z)

    _store_transposed_tiles(mqt_ref, p_mq)
    mk_ref[...] = p_mk.astype(BF16)
    _store_transposed_tiles(mvt_ref, p_mv)
    _store_transposed_tiles(mot_ref, p_mo)
    gates = small_t[:_LANE_DT]
    for t in range(ROW_TILE // MIX_CHUNK):
        i_pre = gates[_LANE_I:_LANE_I + 8, t * MIX_CHUNK:(t + 1) * MIX_CHUNK]
        f_pre = gates[_LANE_F:_LANE_F + 8, t * MIX_CHUNK:(t + 1) * MIX_CHUNK]
        log_f = jnp.minimum(f_pre, 0.0) - jnp.log(1.0 + jnp.exp(-jnp.abs(f_pre)))
        b = _scan_lanes(log_f, jnp.add, 0.0)
        c = i_pre - b
        gt_ref[t] = jnp.concatenate([c, _scan_lanes(c, jnp.maximum, -jnp.inf), b], axis=0)


def _ffn_inproj(x, ffn_norm_w, wg, wu, wd, norm_w, w_all, qkw_row, g32, gate_bias_rows, alog_rows, conv_w, conv_b_row,
                layer, seq):
    n = x.shape[0]
    tiles = ROW_TILE // ATT_TILE

    def t_out(width, dtype=BF16):
        return (pl.BlockSpec((tiles, width, ATT_TILE), lambda i: (i, 0, 0)),
                jax.ShapeDtypeStruct((n // ATT_TILE, width, ATT_TILE), dtype))

    def r_out(width, dtype=BF16):
        return _row_spec(width), jax.ShapeDtypeStruct((n, width), dtype)

    outs = [r_out(D_MODEL, F32),
            t_out(MLSTM_WIDTH), r_out(MLSTM_WIDTH), t_out(MLSTM_WIDTH), t_out(MLSTM_WIDTH), t_out(GATE_ROWS, F32),
            t_out(SSM_WIDTH), r_out(SSM_GROUPS * SSM_STATE), t_out(SSM_GROUPS * SSM_STATE), t_out(SSM_WIDTH),
            t_out(SSM_WIDTH), t_out(SSM_HEADS, F32),
            t_out(DIFF_WIDTH), r_out(DIFF_WIDTH), t_out(DIFF_WIDTH)]
    return pl.pallas_call(
        functools.partial(_ffn_inproj_kernel, tiles_per_seq=seq // ROW_TILE, layer=layer),
        grid=(n // ROW_TILE,),
        in_specs=[_row_spec(D_MODEL), _const_spec((1, D_MODEL)), _HBM, _HBM, _HBM, _const_spec((1, D_MODEL)), _HBM,
                  _const_spec((1, 2 * DIFF_WIDTH)), _const_spec((DIFF_WIDTH, DIFF_WIDTH)),
                  _const_spec((LANES, ROW_TILE)), _const_spec((SSM_HEADS, ROW_TILE)),
                  _const_spec((SSM_CONV, SSM_CONV_DIM)), _const_spec((1, SSM_CONV_DIM))],
        out_specs=[o[0] for o in outs],
        out_shape=[o[1] for o in outs],
        scratch_shapes=[pltpu.VMEM((8, SSM_CONV_DIM), F32), pltpu.VMEM((D_MODEL, _C_END), BF16)]
        + _ffn_weight_scratch(),
        compiler_params=_params(("arbitrary",)),
        name="ffn_inproj",
    )(x, ffn_norm_w.reshape(1, D_MODEL), wg, wu, wd, norm_w.reshape(1, D_MODEL), w_all, qkw_row, g32,
      gate_bias_rows, alog_rows, conv_w, conv_b_row)


def _mlstm_kernel(qt_ref, k_ref, vt_ref, ot_ref, gt_ref, nw_ref, yt_ref, ct_scr, nt_scr, m_scr):
    @pl.when(pl.program_id(1) == 0)
    def _():
        ct_scr[...] = jnp.zeros_like(ct_scr)
        nt_scr[...] = jnp.zeros_like(nt_scr)
        m_scr[...] = jnp.zeros_like(m_scr)

    for t in range(MIX_STEP_CHUNKS):
        _mlstm_chunk(t, qt_ref, k_ref, vt_ref, ot_ref, gt_ref, nw_ref, yt_ref, ct_scr, nt_scr, m_scr)


def _mlstm_chunk(t, qt_ref, k_ref, vt_ref, ot_ref, gt_ref, nw_ref, yt_ref, ct_scr, nt_scr, m_scr):
    L = MIX_CHUNK
    H = MLSTM_HEADS
    W = MLSTM_HEAD_DIM
    qt = qt_ref[t]
    ks = k_ref[t * L:(t + 1) * L, :] * (MLSTM_HEAD_DIM ** -0.5)
    vt = vt_ref[t]
    gt = gt_ref[t]
    c, c_max, b = gt[0:8], gt[8:16], gt[16:24]
    m_prev = m_scr[...]
    big_m = jnp.maximum(m_prev, c_max)
    inter = jnp.exp(m_prev - big_m)
    floor = jnp.exp(-(b + big_m))
    b_last = jnp.broadcast_to(b[:, L - 1:L], b.shape)
    m_new = b_last + jnp.maximum(m_prev, jnp.broadcast_to(c_max[:, L - 1:L], b.shape))
    w = jnp.exp(b_last + c - m_new)
    decay = jnp.exp(b_last + m_prev - m_new)

    c2_cols = jnp.concatenate([c * LOG2E, jnp.zeros((LANES - 8, L), F32)], axis=0).T
    big_m2 = big_m * LOG2E
    causal = (lax.broadcasted_iota(jnp.int32, (L, L), 0) <= lax.broadcasted_iota(jnp.int32, (L, L), 1))
    row_head = lax.broadcasted_iota(jnp.int32, (H * W, L), 0) // W
    qk = [_dot(ks, jnp.where(row_head == h, qt, jnp.zeros_like(qt))) for h in range(H)]
    nums, dens = [], []
    for h in range(H):
        d = jnp.exp2(jnp.where(causal, c2_cols[:, h:h + 1] - big_m2[h:h + 1, :], -jnp.inf))
        s = qk[h] * d
        dens.append(jnp.sum(s, axis=0, keepdims=True))
        nums.append(_dot(vt[h * W:(h + 1) * W, :], s.astype(BF16)))
    cq = _dot(ct_scr[...].astype(BF16), qt)
    nq = _dot(nt_scr[...].astype(BF16), qt)
    outs = []
    for h in range(H):
        num = nums[h] + inter[h:h + 1] * cq[h * W:(h + 1) * W]
        den = dens[h] + inter[h:h + 1] * nq[h:h + 1]
        hh = num / jnp.maximum(jnp.abs(den), floor[h:h + 1])
        outs.append(hh * lax.rsqrt(jnp.mean(hh * hh, axis=0, keepdims=True) + NORM_EPS))
    hn = jnp.concatenate(outs, axis=0) * nw_ref[...]
    yt_ref[t] = (_sigmoid(ot_ref[t].astype(F32)) * hn).astype(BF16)

    w_full = jnp.concatenate([jnp.broadcast_to(w[h:h + 1], (W, L)) for h in range(H)], axis=0)
    g_new = _dot((vt.astype(F32) * w_full).astype(BF16), ks)
    decay_full = jnp.concatenate([jnp.broadcast_to(decay[h:h + 1, :1], (W, H * W)) for h in range(H)], axis=0)
    same_head = (lax.broadcasted_iota(jnp.int32, (H * W, H * W), 0) // W
                 == lax.broadcasted_iota(jnp.int32, (H * W, H * W), 1) // W)
    ct_scr[...] = decay_full * ct_scr[...] + jnp.where(same_head, g_new, 0.0)
    w16 = jnp.concatenate([w, jnp.zeros_like(w)], axis=0).astype(BF16)
    n_new = _dot(w16, ks)
    own_lanes = (lax.broadcasted_iota(jnp.int32, (16, H * W), 0)
                 == lax.broadcasted_iota(jnp.int32, (16, H * W), 1) // W)
    decay16 = jnp.broadcast_to(jnp.concatenate([decay[:, :1], jnp.zeros((8, 1), F32)], axis=0), (16, H * W))
    nt_scr[...] = decay16 * nt_scr[...] + jnp.where(own_lanes, n_new, 0.0)
    m_scr[...] = m_new


def _mlstm(mqt, mk, mvt, mot, gt, norm_w_rows, batch, seq):
    ns = seq // (MIX_CHUNK * MIX_STEP_CHUNKS)
    n = batch * seq
    t_spec = pl.BlockSpec((MIX_STEP_CHUNKS, MLSTM_WIDTH, MIX_CHUNK), lambda b, c: (b * ns + c, 0, 0))
    return pl.pallas_call(
        _mlstm_kernel,
        grid=(batch, ns),
        in_specs=[t_spec, pl.BlockSpec((MIX_STEP_CHUNKS * MIX_CHUNK, MLSTM_WIDTH), lambda b, c: (b * ns + c, 0)),
                  t_spec, t_spec,
                  pl.BlockSpec((MIX_STEP_CHUNKS, GATE_ROWS, MIX_CHUNK), lambda b, c: (b * ns + c, 0, 0)),
                  _const_spec((MLSTM_WIDTH, MIX_CHUNK))],
        out_specs=t_spec,
        out_shape=jax.ShapeDtypeStruct((n // MIX_CHUNK, MLSTM_WIDTH, MIX_CHUNK), BF16),
        scratch_shapes=[pltpu.VMEM((MLSTM_WIDTH, MLSTM_WIDTH), F32),
                        pltpu.VMEM((16, MLSTM_WIDTH), F32),
                        pltpu.VMEM((8, MIX_CHUNK), F32)],
        compiler_params=_params(("arbitrary", "arbitrary")),
        name="mlstm",
    )(mqt, mk, mvt, mot, gt, norm_w_rows)


def _ssd_kernel(b_ref, ct_ref, xt_ref, xdt_ref, zt_ref, a_ref, dskip_ref, nw_ref, yt_ref, st_scr):
    @pl.when(pl.program_id(1) == 0)
    def _():
        st_scr[...] = jnp.zeros_like(st_scr)

    for t in range(MIX_STEP_CHUNKS):
        _ssd_chunk(t, b_ref, ct_ref, xt_ref, xdt_ref, zt_ref, a_ref, dskip_ref, nw_ref, yt_ref, st_scr)


def _ssd_chunk(t, b_ref, ct_ref, xt_ref, xdt_ref, zt_ref, a_ref, dskip_ref, nw_ref, yt_ref, st_scr):
    L = MIX_CHUNK
    P = SSM_HEAD_DIM
    HG = SSM_HEADS // SSM_GROUPS
    GW = HG * P
    NS = SSM_STATE
    a = a_ref[t]
    a_last = jnp.broadcast_to(a[:, L - 1:L], a.shape)
    exp_a = jnp.exp(a)
    d_in = jnp.exp(a_last - a)
    chunk_decay = jnp.exp(a_last)
    a2 = a * LOG2E
    a2_cols = jnp.concatenate([a2, jnp.zeros((LANES - SSM_HEADS, L), F32)], axis=0).T
    causal = (lax.broadcasted_iota(jnp.int32, (L, L), 0) <= lax.broadcasted_iota(jnp.int32, (L, L), 1))
    bn = b_ref[t * L:(t + 1) * L, :]
    ct = ct_ref[t]
    xdt = xdt_ref[t]

    def per_head_rows(rows, grp, width):
        return jnp.concatenate([jnp.broadcast_to(rows[grp * HG + i:grp * HG + i + 1, :width], (P, width))
                                for i in range(HG)], axis=0)

    scores, y_off = [], []
    for grp in range(SSM_GROUPS):
        bg = bn[:, grp * NS:(grp + 1) * NS]
        cg = ct[grp * NS:(grp + 1) * NS, :]
        scores.append(_dot(bg, cg))
        st = st_scr[grp]
        y_off.append(_dot(st.astype(BF16), cg))
        x_in = (xdt[grp * GW:(grp + 1) * GW, :].astype(F32) * per_head_rows(d_in, grp, L)).astype(BF16)
        st_scr[grp] = per_head_rows(chunk_decay, grp, NS) * st + _dot(x_in, bg)
    for grp in range(SSM_GROUPS):
        y_diag = []
        for i in range(HG):
            h = grp * HG + i
            seg = jnp.exp2(jnp.where(causal, a2[h:h + 1, :] - a2_cols[:, h:h + 1], -jnp.inf))
            p = (scores[grp] * seg).astype(BF16)
            y_diag.append(_dot(xdt[h * P:(h + 1) * P, :], p))
        rows = slice(grp * GW, (grp + 1) * GW)
        y = (jnp.concatenate(y_diag, axis=0) + y_off[grp] * per_head_rows(exp_a, grp, L)
             + dskip_ref[rows, :] * xt_ref[t, rows, :].astype(F32))
        zg = zt_ref[t, rows, :].astype(F32)
        y = y * (zg * _sigmoid(zg))
        ms = jnp.mean(y * y, axis=0, keepdims=True)
        yt_ref[t, rows, :] = (y * lax.rsqrt(ms + NORM_EPS) * nw_ref[rows, :]).astype(BF16)


def _ssd(sb, sct, sxt, sxdt, zt, sa, dskip_rows, norm_w_rows, batch, seq):
    ns = seq // (MIX_CHUNK * MIX_STEP_CHUNKS)
    n = batch * seq

    def t_spec(width):
        return pl.BlockSpec((MIX_STEP_CHUNKS, width, MIX_CHUNK), lambda b, c: (b * ns + c, 0, 0))

    return pl.pallas_call(
        _ssd_kernel,
        grid=(batch, ns),
        in_specs=[pl.BlockSpec((MIX_STEP_CHUNKS * MIX_CHUNK, SSM_GROUPS * SSM_STATE), lambda b, c: (b * ns + c, 0)),
                  t_spec(SSM_GROUPS * SSM_STATE), t_spec(SSM_WIDTH), t_spec(SSM_WIDTH), t_spec(SSM_WIDTH),
                  t_spec(SSM_HEADS), _const_spec((SSM_WIDTH, MIX_CHUNK)), _const_spec((SSM_WIDTH, MIX_CHUNK))],
        out_specs=t_spec(SSM_WIDTH),
        out_shape=jax.ShapeDtypeStruct((n // MIX_CHUNK, SSM_WIDTH, MIX_CHUNK), BF16),
        scratch_shapes=[pltpu.VMEM((SSM_GROUPS, SSM_WIDTH // SSM_GROUPS, SSM_STATE), F32)],
        compiler_params=_params(("arbitrary", "arbitrary")),
        name="ssd",
    )(sb, sct, sxt, sxdt, zt, sa, dskip_rows, norm_w_rows)


def _t5_bias_tile(rel, relb_ref, head):
    n = jnp.maximum(rel, 0)
    max_exact = REL_BUCKETS // 2
    nf = jnp.maximum(n, 1).astype(F32)
    large = max_exact + (jnp.log(nf / max_exact) / math.log(REL_MAX_DIST / max_exact)
                         * (REL_BUCKETS - max_exact)).astype(jnp.int32)
    large = jnp.minimum(large, REL_BUCKETS - 1)
    bucket = jnp.where(n < max_exact, n, large)
    far = relb_ref[REL_BUCKETS - 1, head]
    bias = jnp.zeros(rel.shape, F32)
    for bkt in range(REL_BUCKETS - 1):
        bias = jnp.where(bucket == bkt, (relb_ref[bkt, head] - far) * LOG2E, bias)
    return jnp.where(rel >= 0, bias, NEG_BIG)


def _attn_kernel(relb_ref, qt_ref, k_ref, vt_ref, lam_ref, subw_ref, g64_ref,
                 y_ref, bias_scr, qm_scr, m_scr, acc_scr, sa_scr, sb_scr, *, lam_init):
    T = ATT_TILE
    H = DIFF_HEADS
    DV = DIFF_V_DIM
    pair = pl.program_id(1)

    @pl.when((pl.program_id(0) == 0) & (pair == 0))
    def _():
        rel = lax.broadcasted_iota(jnp.int32, (T, T), 1) - lax.broadcasted_iota(jnp.int32, (T, T), 0)
        for h in range(H):
            bias_scr[0, h] = _t5_bias_tile(rel, relb_ref, h)
            bias_scr[1, h] = _t5_bias_tile(rel + T, relb_ref, h)

    comp_of_row = lax.broadcasted_iota(jnp.int32, (DIFF_WIDTH, T), 0) // DIFF_QK_DIM
    for r in range(2):
        qt = qt_ref[r]
        for hc in range(2 * H):
            qm_scr[r, hc] = jnp.where(comp_of_row == hc, qt, jnp.zeros_like(qt))
    m_scr[...] = jnp.full(m_scr.shape, NEG_BIG, F32)
    acc_scr[...] = jnp.zeros_like(acc_scr)
    ones_rows = jnp.ones((ONES_ROWS, T), BF16)

    def key_tile(j):
        return k_ref[pl.ds(pl.multiple_of(j * T, T), T), :]

    def logits_one(s_ref, kt, r, hc):
        s_ref[hc] = _dot(kt, qm_scr[r, hc])

    def consume_one(s_ref, vt, r, hc, bias_idx):
        h = hc // 2
        s = s_ref[hc]
        if bias_idx is not None:
            s = bias_scr[bias_idx, h] + s
        m_prev = m_scr[r, hc]
        m_next = jnp.maximum(m_prev, jnp.max(s, axis=0, keepdims=True))
        alpha = jnp.exp2(m_prev - m_next)
        p = jnp.exp2(s - m_next).astype(BF16)
        m_scr[r, hc] = m_next
        v_aug = jnp.concatenate([vt[h * DV:(h + 1) * DV, :], ones_rows], axis=0)
        acc_scr[r, hc] = alpha * acc_scr[r, hc] + _dot(v_aug, p)

    LEAD = 2

    def step(s_next, j_next, r_next, s_cur, j_cur, r_cur, bias_idx):
        kt = key_tile(j_next)
        vt = vt_ref[j_cur]
        for hc in range(LEAD):
            logits_one(s_next, kt, r_next, hc)
        for hc in range(2 * H):
            if hc + LEAD < 2 * H:
                logits_one(s_next, kt, r_next, hc + LEAD)
            consume_one(s_cur, vt, r_cur, hc, bias_idx)

    DIAG, NEAR = 0, 1

    kt0 = key_tile(0)
    for hc in range(2 * H):
        logits_one(sa_scr, kt0, 0, hc)

    def far_pair_first(t, carry):
        j = 2 * t
        step(sb_scr, j + 1, 0, sa_scr, j, 0, None)
        step(sa_scr, j + 2, 0, sb_scr, j + 1, 0, None)
        return carry

    lax.fori_loop(0, jnp.maximum(pair - 1, 0), far_pair_first, 0)

    @pl.when(pair >= 1)
    def _():
        u = 2 * pair - 2
        step(sb_scr, u + 1, 0, sa_scr, u, 0, None)
        step(sa_scr, u + 2, 0, sb_scr, u + 1, 0, NEAR)

    step(sb_scr, 0, 1, sa_scr, 2 * pair, 0, DIAG)

    def far_pair_second(t, carry):
        j = 2 * t
        step(sa_scr, j + 1, 1, sb_scr, j, 1, None)
        step(sb_scr, j + 2, 1, sa_scr, j + 1, 1, None)
        return carry

    lax.fori_loop(0, pair, far_pair_second, 0)
    step(sa_scr, 2 * pair + 1, 1, sb_scr, 2 * pair, 1, NEAR)
    vt_last = vt_ref[2 * pair + 1]
    for hc in range(2 * H):
        consume_one(sa_scr, vt_last, 1, hc, DIAG)

    lam_p = lam_ref[...]
    lam = (jnp.exp(jnp.sum(lam_p[0:1] * lam_p[1:2], axis=1, keepdims=True))
           - jnp.exp(jnp.sum(lam_p[2:3] * lam_p[3:4], axis=1, keepdims=True)) + lam_init)
    for r in range(2):
        def normalised(hc):
            return acc_scr[r, hc, :DV, :] / acc_scr[r, hc, DV:DV + 1, :]

        o_t = jnp.concatenate([normalised(2 * h) - lam * normalised(2 * h + 1) for h in range(H)], axis=0)
        o = o_t.T
        ms = _group_mean_sq(o, g64_ref[...])
        y_ref[r * T:(r + 1) * T, :] = (o * lax.rsqrt(ms + NORM_EPS) * subw_ref[...] * (1.0 - lam_init)).astype(BF16)


def _diff_attn(qt, kn, vt, rel_bias, lambdas, subw_row, g64, lam_init, batch, seq):
    nk = seq // ATT_TILE
    npairs = nk // 2
    n = batch * seq
    return pl.pallas_call(
        functools.partial(_attn_kernel, lam_init=lam_init),
        grid=(batch, npairs),
        in_specs=[pl.BlockSpec(memory_space=pltpu.SMEM),
                  pl.BlockSpec((2, DIFF_WIDTH, ATT_TILE), lambda b, i: (b * npairs + i, 0, 0)),
                  pl.BlockSpec((seq, DIFF_WIDTH), lambda b, i: (b, 0)),
                  pl.BlockSpec((nk, DIFF_WIDTH, ATT_TILE), lambda b, i: (b, 0, 0)),
                  _const_spec((4, DIFF_QK_DIM)), _const_spec((1, DIFF_WIDTH)),
                  _const_spec((DIFF_WIDTH, DIFF_WIDTH))],
        out_specs=pl.BlockSpec((2 * ATT_TILE, DIFF_WIDTH), lambda b, i: (b * npairs + i, 0)),
        out_shape=jax.ShapeDtypeStruct((n, DIFF_WIDTH), BF16),
        scratch_shapes=[pltpu.VMEM((2, DIFF_HEADS, ATT_TILE, ATT_TILE), F32),
                        pltpu.VMEM((2, 2 * DIFF_HEADS, DIFF_WIDTH, ATT_TILE), BF16),
                        pltpu.VMEM((2, 2 * DIFF_HEADS, 1, ATT_TILE), F32),
                        pltpu.VMEM((2, 2 * DIFF_HEADS, DIFF_V_DIM + ONES_ROWS, ATT_TILE), F32),
                        pltpu.VMEM((2 * DIFF_HEADS, ATT_TILE, ATT_TILE), F32),
                        pltpu.VMEM((2 * DIFF_HEADS, ATT_TILE, ATT_TILE), F32)],
        compiler_params=_params(("arbitrary", "arbitrary")),
        name="diff_attn",
    )(rel_bias, qt, kn, vt, lambdas, subw_row, g64)


def _untransposed(yt_ref):
    return jnp.concatenate([yt_ref[t].astype(F32).T for t in range(yt_ref.shape[0])], axis=0).astype(BF16)


def _outproj_ffn_kernel(x_ref, ymt_ref, yst_ref, yd_ref, w_hbm, fnw_ref, wg_hbm, wu_hbm, wd_hbm, o_ref,
                        w_ref, wg_ref, wu_ref, wd_ref, wide, wide_sem, tall, tall_sem, *, layer):
    @pl.when(pl.program_id(0) == 0)
    def _():
        def store(r0, chunk):
            w_ref[r0:r0 + chunk.shape[0], :] = chunk.astype(BF16)

        _stage_weight(w_hbm, layer, D_MODEL // 4, tall, tall_sem, store)
        _stage_ffn_weights(layer, wg_hbm, wu_hbm, wd_hbm, wg_ref, wu_ref, wd_ref, wide, wide_sem, tall, tall_sem)

    y = jnp.concatenate([_untransposed(ymt_ref), _untransposed(yst_ref), yd_ref[...]], axis=1)
    x = x_ref[...] + _dot(y, w_ref[...])
    o_ref[...] = _ffn_block(x, fnw_ref, wg_ref, wu_ref, wd_ref)


def _outproj_ffn(x, y_m, y_s, y_d, w_out, ffn_norm_w, wg, wu, wd, layer):
    n = x.shape[0]
    return pl.pallas_call(
        functools.partial(_outproj_ffn_kernel, layer=layer),
        grid=(n // ROW_TILE,),
        in_specs=[_row_spec(D_MODEL),
                  pl.BlockSpec((ROW_TILE // MIX_CHUNK, MLSTM_WIDTH, MIX_CHUNK), lambda i: (i, 0, 0)),
                  pl.BlockSpec((ROW_TILE // MIX_CHUNK, SSM_WIDTH, MIX_CHUNK), lambda i: (i, 0, 0)),
                  _row_spec(DIFF_WIDTH), _HBM, _const_spec((1, D_MODEL)), _HBM, _HBM, _HBM],
        out_specs=_row_spec(D_MODEL),
        out_shape=jax.ShapeDtypeStruct((n, D_MODEL), F32),
        scratch_shapes=[pltpu.VMEM((D_MODEL, D_MODEL), BF16)] + _ffn_weight_scratch(),
        compiler_params=_params(("arbitrary",)),
        name="outproj_ffn",
    )(x, y_m, y_s, y_d, w_out, ffn_norm_w.reshape(1, D_MODEL), wg, wu, wd)


def _small_row(values_by_lane):
    row = jnp.zeros((LANES,), F32)
    for lane, vals in values_by_lane:
        row = lax.dynamic_update_slice(row, vals.astype(F32), (lane,))
    return row.reshape(1, LANES)


def kernel(x, ffn1_norm_w, ffn1_w_gate, ffn1_w_up, ffn1_w_down, mix_norm_w, w_in, mlstm_gate_bias, mlstm_norm_w, ssm_conv_w, ssm_conv_b, ssm_dt_bias, ssm_A_log, ssm_D, ssm_norm_w, diff_q_norm_w, diff_k_norm_w, diff_lambda, diff_subln_w, rel_bias, w_out, ffn2_norm_w, ffn2_w_gate, ffn2_w_up, ffn2_w_down):
    batch, seq, d = x.shape
    assert d == D_MODEL and MIX_CHUNK == ATT_TILE and seq % ROW_TILE == 0 and ROW_TILE % MIX_CHUNK == 0
    assert seq % (MIX_CHUNK * MIX_STEP_CHUNKS) == 0 and seq % (2 * ATT_TILE) == 0
    xf = x.reshape(batch * seq, D_MODEL)
    g32 = _block_diag(DIFF_WIDTH, DIFF_QK_DIM, 1.0 / DIFF_QK_DIM)
    g64 = _block_diag(DIFF_WIDTH, DIFF_V_DIM, 1.0 / DIFF_V_DIM)
    ffn1 = (ffn1_w_gate, ffn1_w_up, ffn1_w_down)
    ffn2 = (ffn2_w_gate, ffn2_w_up, ffn2_w_down)
    for l in range(DEPTH):
        qkw_row = jnp.concatenate([jnp.tile(diff_q_norm_w[l].reshape(-1), DIFF_HEADS) * (DIFF_QK_DIM ** -0.5 * LOG2E),
                                   jnp.tile(diff_k_norm_w[l].reshape(-1), DIFF_HEADS)]).reshape(1, 2 * DIFF_WIDTH)
        bias_row = _small_row([(_LANE_I, mlstm_gate_bias[l, 0]), (_LANE_F, mlstm_gate_bias[l, 1]),
                               (_LANE_DT, ssm_dt_bias[l])])
        (xf, mqt, mk, mvt, mot, gt, zt, sb, sct, sxt, sxdt, sa, qt, kn, vt) = _ffn_inproj(
            xf, ffn1_norm_w[l], *ffn1, mix_norm_w[l], w_in, qkw_row, g32,
            jnp.broadcast_to(bias_row.reshape(LANES, 1), (LANES, ROW_TILE)),
            jnp.broadcast_to(ssm_A_log[l][:, None], (SSM_HEADS, ROW_TILE)),
            ssm_conv_w[l], ssm_conv_b[l].reshape(1, SSM_CONV_DIM), l, seq)
        y_m = _mlstm(mqt, mk, mvt, mot, gt, jnp.broadcast_to(mlstm_norm_w[l][:, None], (MLSTM_WIDTH, MIX_CHUNK)),
                     batch, seq)
        y_s = _ssd(sb, sct, sxt, sxdt, zt, sa,
                   jnp.broadcast_to(jnp.repeat(ssm_D[l], SSM_HEAD_DIM)[:, None], (SSM_WIDTH, MIX_CHUNK)),
                   jnp.broadcast_to(ssm_norm_w[l][:, None], (SSM_WIDTH, MIX_CHUNK)), batch, seq)
        lam_init = 0.8 - 0.6 * math.exp(-0.3 * l)
        y_d = _diff_attn(qt, kn, vt, rel_bias, diff_lambda[l], jnp.tile(diff_subln_w[l], DIFF_HEADS).reshape(1, DIFF_WIDTH),
                         g64, lam_init, batch, seq)
        xf = _outproj_ffn(xf, y_m, y_s, y_d, w_out, ffn2_norm_w[l], *ffn2, l)
    return xf.reshape(batch, seq, D_MODEL)
```

```python
import functools
import math

import jax
import jax.numpy as jnp
from jax import lax
from jax.experimental import pallas as pl
from jax.experimental.pallas import tpu as pltpu

F32 = jnp.float32
BF16 = jnp.bfloat16

D_MODEL = 1024
DEPTH = 2
D_FF = 2816
MLSTM_HEADS = 4
MLSTM_HEAD_DIM = 64
MLSTM_WIDTH = MLSTM_HEADS * MLSTM_HEAD_DIM
SSM_HEADS = 8
SSM_HEAD_DIM = 64
SSM_WIDTH = SSM_HEADS * SSM_HEAD_DIM
SSM_STATE = 128
SSM_GROUPS = 2
SSM_CONV = 4
SSM_CONV_DIM = SSM_WIDTH + 2 * SSM_GROUPS * SSM_STATE
DIFF_HEADS = 4
DIFF_QK_DIM = 32
DIFF_V_DIM = 64
DIFF_WIDTH = DIFF_HEADS * DIFF_V_DIM
REL_BUCKETS = 32
REL_MAX_DIST = 128
NORM_EPS = 1e-6

LANES = 128
ROW_TILE = 512
MXU_COLS = 256
FFN_SPLITS = (0, 6 * MXU_COLS, D_FF)
MIX_CHUNK = 256
MIX_STEP_CHUNKS = 4
ATT_TILE = 256
ATT_STEP_TILES = 4
VMEM_LIMIT = 56 * 1024 * 1024
NEG_BIG = -1e30
LOG2E = math.log2(math.e)
ONES_ROWS = 16

_C_M = 0
_C_Z = _C_M + 4 * MLSTM_WIDTH
_C_XBC = _C_Z + SSM_WIDTH
_C_SMALL = _C_XBC + SSM_CONV_DIM
_C_QK = _C_SMALL + LANES
_C_V = _C_QK + 4 * DIFF_HEADS * DIFF_QK_DIM
_C_END = _C_V + DIFF_WIDTH
_LANE_I = 0
_LANE_F = 8
_LANE_DT = 16
GATE_ROWS = 24
D_IN = 4 * MLSTM_WIDTH + 2 * MLSTM_HEADS + SSM_WIDTH + SSM_CONV_DIM + SSM_HEADS + 3 * DIFF_WIDTH


def _regroup_plan():
    sizes = (4 * MLSTM_WIDTH, MLSTM_HEADS, MLSTM_HEADS, SSM_WIDTH, SSM_CONV_DIM, SSM_HEADS, 3 * DIFF_WIDTH)
    dests = (_C_M, _C_SMALL + _LANE_I, _C_SMALL + _LANE_F, _C_Z, _C_XBC, _C_SMALL + _LANE_DT, _C_QK)
    plan, src = [], 0
    for dst, width in zip(dests, sizes):
        plan.append((dst, src, width))
        src += width
    assert src == D_IN
    return plan


def _dot(a, b):
    return jnp.dot(a, b, preferred_element_type=F32)


def _sigmoid(x):
    return 1.0 / (1.0 + jnp.exp(-x))


def _softplus(x):
    return jnp.maximum(x, 0.0) + jnp.log(1.0 + jnp.exp(-jnp.abs(x)))


def _rms_rows(x, w_row):
    ms = jnp.mean(x * x, axis=-1, keepdims=True)
    return x * lax.rsqrt(ms + NORM_EPS) * w_row


def _group_mean_sq(x, gmat):
    sq = x * x
    hi = sq.astype(BF16)
    lo = (sq - hi.astype(F32)).astype(BF16)
    return _dot(hi, gmat) + _dot(lo, gmat)


def _scan_lanes(x, op, identity):
    n = x.shape[1]
    lane = lax.broadcasted_iota(jnp.int32, x.shape, 1)
    step = 1
    while step < n:
        x = op(x, jnp.where(lane >= step, pltpu.roll(x, step, axis=1), identity))
        step *= 2
    return x


def _block_diag(n, group, value):
    r = jnp.arange(n) // group
    return jnp.where(r[:, None] == r[None, :], value, 0.0).astype(BF16)


def _const_spec(shape):
    nd = len(shape)
    return pl.BlockSpec(shape, lambda *_: (0,) * nd)


def _params(sem):
    return pltpu.CompilerParams(dimension_semantics=sem, vmem_limit_bytes=VMEM_LIMIT)


def _ffn_block(x, nw_ref, wg_ref, wu_ref, wd_ref):
    xn = _rms_rows(x, nw_ref[...]).astype(BF16)
    acc = None
    for lo, hi in zip(FFN_SPLITS[:-1], FFN_SPLITS[1:]):
        assert (hi - lo) % MXU_COLS == 0
        cols = slice(lo, hi)
        g = _dot(xn, wg_ref[:, cols])
        u = _dot(xn, wu_ref[:, cols])
        h = (g * _sigmoid(g) * u).astype(BF16)
        part = _dot(h, wd_ref[cols, :])
        acc = part if acc is None else acc + part
    return x + 0.5 * acc


def _row_spec(width):
    return pl.BlockSpec((ROW_TILE, width), lambda i: (i, 0))


WIDE_STAGE_ROWS = 128
TALL_STAGE_ROWS = 352
_HBM = pl.BlockSpec(memory_space=pl.ANY)


def _stage_weight(w_hbm, layer, rows, stage, sem, store):
    n_rows, n_cols = w_hbm.shape[1], w_hbm.shape[2]
    assert n_rows % rows == 0

    def copy(c):
        return pltpu.make_async_copy(w_hbm.at[layer, pl.ds(c * rows, rows), :],
                                     stage.at[c % 2, pl.ds(0, rows), pl.ds(0, n_cols)], sem.at[c % 2])

    copy(0).start()
    for c in range(n_rows // rows):
        if c + 1 < n_rows // rows:
            copy(c + 1).start()
        copy(c).wait()
        store(c * rows, stage[c % 2, :rows, :n_cols])


def _stage_ffn_weights(layer, wg_hbm, wu_hbm, wd_hbm, wg_scr, wu_scr, wd_scr, wide, wide_sem, tall, tall_sem):
    def into(dst):
        def store(r0, chunk):
            dst[r0:r0 + chunk.shape[0], :] = chunk.astype(BF16)
        return store

    _stage_weight(wg_hbm, layer, WIDE_STAGE_ROWS, wide, wide_sem, into(wg_scr))
    _stage_weight(wu_hbm, layer, WIDE_STAGE_ROWS, wide, wide_sem, into(wu_scr))
    _stage_weight(wd_hbm, layer, TALL_STAGE_ROWS, tall, tall_sem, into(wd_scr))


def _ffn_weight_scratch():
    return [pltpu.VMEM((D_MODEL, D_FF), BF16), pltpu.VMEM((D_MODEL, D_FF), BF16), pltpu.VMEM((D_FF, D_MODEL), BF16),
            pltpu.VMEM((2, WIDE_STAGE_ROWS, D_IN), F32), pltpu.SemaphoreType.DMA((2,)),
            pltpu.VMEM((2, TALL_STAGE_ROWS, D_MODEL), F32), pltpu.SemaphoreType.DMA((2,))]


def _store_transposed_tiles(out_ref, a):
    a_t = a.T.astype(BF16)
    for t in range(ROW_TILE // ATT_TILE):
        out_ref[t] = a_t[:, t * ATT_TILE:(t + 1) * ATT_TILE]


def _ffn_inproj_kernel(x_ref, fnw_ref, wg_hbm, wu_hbm, wd_hbm, nw_ref, w_hbm, qkw_ref, g32_ref, gbias_ref,
                       alog_ref, convw_ref, convb_ref,
                       x_out_ref, mqt_ref, mk_ref, mvt_ref, mot_ref, gt_ref,
                       zt_ref, sb_ref, sct_ref, sxt_ref, sxdt_ref, sa_ref, qt_ref, k_ref, vt_ref,
                       halo_scr, w_scr, wg_ref, wu_ref, wd_ref, wide, wide_sem, tall, tall_sem,
                       *, tiles_per_seq, layer):
    @pl.when(pl.program_id(0) == 0)
    def _():
        _stage_ffn_weights(layer, wg_hbm, wu_hbm, wd_hbm, wg_ref, wu_ref, wd_ref, wide, wide_sem, tall, tall_sem)
        w_scr[:, _C_SMALL:_C_QK] = jnp.zeros((D_MODEL, LANES), BF16)

        def regrouped(r0, chunk):
            for dst, src, width in _regroup_plan():
                w_scr[r0:r0 + chunk.shape[0], dst:dst + width] = chunk[:, src:src + width].astype(BF16)

        _stage_weight(w_hbm, layer, WIDE_STAGE_ROWS, wide, wide_sem, regrouped)

    @pl.when(pl.program_id(0) % tiles_per_seq == 0)
    def _():
        halo_scr[...] = jnp.zeros_like(halo_scr)

    x = _ffn_block(x_ref[...], fnw_ref, wg_ref, wu_ref, wd_ref)
    x_out_ref[...] = x
    xn = _rms_rows(x, nw_ref[...]).astype(BF16)

    def proj(lo, hi):
        return _dot(xn, w_scr[:, lo:hi])

    half = (_C_V - _C_QK) // 2
    p_xbc = proj(_C_XBC, _C_SMALL)
    p_small = proj(_C_SMALL, _C_QK)
    p_q, p_k = proj(_C_QK, _C_QK + half), proj(_C_QK + half, _C_V)
    p_mq = proj(_C_M, _C_M + MLSTM_WIDTH)
    p_mk = proj(_C_M + MLSTM_WIDTH, _C_M + 2 * MLSTM_WIDTH)
    p_mv = proj(_C_M + 2 * MLSTM_WIDTH, _C_M + 3 * MLSTM_WIDTH)
    p_mo = proj(_C_M + 3 * MLSTM_WIDTH, _C_Z)
    p_z = proj(_C_Z, _C_XBC)
    p_v = proj(_C_V, _C_END)

    qk = [a * lax.rsqrt(_group_mean_sq(a, g32_ref[...]) + NORM_EPS) * qkw_ref[:, i * half:(i + 1) * half]
          for i, a in enumerate((p_q, p_k))]
    _store_transposed_tiles(qt_ref, qk[0])
    k_ref[...] = qk[1].astype(BF16)
    _store_transposed_tiles(vt_ref, p_v)

    ext = jnp.concatenate([halo_scr[...], p_xbc], axis=0)
    halo_scr[...] = p_xbc[ROW_TILE - 8:, :]
    conv = convb_ref[...] + convw_ref[SSM_CONV - 1:SSM_CONV, :] * p_xbc
    for j in range(1, SSM_CONV):
        conv = conv + convw_ref[SSM_CONV - 1 - j:SSM_CONV - j, :] * ext[8 - j:8 - j + ROW_TILE, :]
    xa = conv * _sigmoid(conv)
    sb_ref[...] = xa[:, SSM_WIDTH:SSM_WIDTH + SSM_GROUPS * SSM_STATE].astype(BF16)
    _store_transposed_tiles(sct_ref, xa[:, SSM_WIDTH + SSM_GROUPS * SSM_STATE:])
    small_t = p_small.T + gbias_ref[...]
    dt = _softplus(small_t[_LANE_DT:_LANE_DT + SSM_HEADS])
    log_decay = dt * (-jnp.exp(alog_ref[...]))
    xs_t = xa[:, :SSM_WIDTH].T
    xdt_t = xs_t * jnp.concatenate([jnp.broadcast_to(dt[h:h + 1], (SSM_HEAD_DIM, ROW_TILE))
                                    for h in range(SSM_HEADS)], axis=0)
    for t in range(ROW_TILE // MIX_CHUNK):
        cols = slice(t * MIX_CHUNK, (t + 1) * MIX_CHUNK)
        sxt_ref[t] = xs_t[:, cols].astype(BF16)
        sxdt_ref[t] = xdt_t[:, cols].astype(BF16)
        sa_ref[t] = _scan_lanes(log_decay[:, cols], jnp.add, 0.0)
    _store_transposed_tiles(zt_ref, p_z)

    _store_transposed_tiles(mqt_ref, p_mq)
    mk_ref[...] = p_mk.astype(BF16)
    _store_transposed_tiles(mvt_ref, p_mv)
    _store_transposed_tiles(mot_ref, p_mo)
    gates = small_t[:_LANE_DT]
    for t in range(ROW_TILE // MIX_CHUNK):
        i_pre = gates[_LANE_I:_LANE_I + 8, t * MIX_CHUNK:(t + 1) * MIX_CHUNK]
        f_pre = gates[_LANE_F:_LANE_F + 8, t * MIX_CHUNK:(t + 1) * MIX_CHUNK]
        log_f = jnp.minimum(f_pre, 0.0) - jnp.log(1.0 + jnp.exp(-jnp.abs(f_pre)))
        b = _scan_lanes(log_f, jnp.add, 0.0)
        c = i_pre - b
        gt_ref[t] = jnp.concatenate([c, _scan_lanes(c, jnp.maximum, -jnp.inf), b], axis=0)


def _ffn_inproj(x, ffn_norm_w, wg, wu, wd, norm_w, w_all, qkw_row, g32, gate_bias_rows, alog_rows, conv_w, conv_b_row,
                layer, seq):
    n = x.shape[0]
    tiles = ROW_TILE // ATT_TILE

    def t_out(width, dtype=BF16):
        return (pl.BlockSpec((tiles, width, ATT_TILE), lambda i: (i, 0, 0)),
                jax.ShapeDtypeStruct((n // ATT_TILE, width, ATT_TILE), dtype))

    def r_out(width, dtype=BF16):
        return _row_spec(width), jax.ShapeDtypeStruct((n, width), dtype)

    outs = [r_out(D_MODEL, F32),
            t_out(MLSTM_WIDTH), r_out(MLSTM_WIDTH), t_out(MLSTM_WIDTH), t_out(MLSTM_WIDTH), t_out(GATE_ROWS, F32),
            t_out(SSM_WIDTH), r_out(SSM_GROUPS * SSM_STATE), t_out(SSM_GROUPS * SSM_STATE), t_out(SSM_WIDTH),
            t_out(SSM_WIDTH), t_out(SSM_HEADS, F32),
            t_out(DIFF_WIDTH), r_out(DIFF_WIDTH), t_out(DIFF_WIDTH)]
    return pl.pallas_call(
        functools.partial(_ffn_inproj_kernel, tiles_per_seq=seq // ROW_TILE, layer=layer),
        grid=(n // ROW_TILE,),
        in_specs=[_row_spec(D_MODEL), _const_spec((1, D_MODEL)), _HBM, _HBM, _HBM, _const_spec((1, D_MODEL)), _HBM,
                  _const_spec((1, 2 * DIFF_WIDTH)), _const_spec((DIFF_WIDTH, DIFF_WIDTH)),
                  _const_spec((LANES, ROW_TILE)), _const_spec((SSM_HEADS, ROW_TILE)),
                  _const_spec((SSM_CONV, SSM_CONV_DIM)), _const_spec((1, SSM_CONV_DIM))],
        out_specs=[o[0] for o in outs],
        out_shape=[o[1] for o in outs],
        scratch_shapes=[pltpu.VMEM((8, SSM_CONV_DIM), F32), pltpu.VMEM((D_MODEL, _C_END), BF16)]
        + _ffn_weight_scratch(),
        compiler_params=_params(("arbitrary",)),
        name="ffn_inproj",
    )(x, ffn_norm_w.reshape(1, D_MODEL), wg, wu, wd, norm_w.reshape(1, D_MODEL), w_all, qkw_row, g32,
      gate_bias_rows, alog_rows, conv_w, conv_b_row)


def _mlstm_kernel(qt_ref, k_ref, vt_ref, ot_ref, gt_ref, nw_ref, yt_ref, ct_scr, nt_scr, m_scr):
    @pl.when(pl.program_id(1) == 0)
    def _():
        ct_scr[...] = jnp.zeros_like(ct_scr)
        nt_scr[...] = jnp.zeros_like(nt_scr)
        m_scr[...] = jnp.zeros_like(m_scr)

    for t in range(MIX_STEP_CHUNKS):
        _mlstm_chunk(t, qt_ref, k_ref, vt_ref, ot_ref, gt_ref, nw_ref, yt_ref, ct_scr, nt_scr, m_scr)


def _mlstm_chunk(t, qt_ref, k_ref, vt_ref, ot_ref, gt_ref, nw_ref, yt_ref, ct_scr, nt_scr, m_scr):
    L = MIX_CHUNK
    H = MLSTM_HEADS
    W = MLSTM_HEAD_DIM
    qt = qt_ref[t]
    ks = k_ref[t * L:(t + 1) * L, :] * (MLSTM_HEAD_DIM ** -0.5)
    vt = vt_ref[t]
    gt = gt_ref[t]
    c, c_max, b = gt[0:8], gt[8:16], gt[16:24]
    m_prev = m_scr[...]
    big_m = jnp.maximum(m_prev, c_max)
    inter = jnp.exp(m_prev - big_m)
    floor = jnp.exp(-(b + big_m))
    b_last = jnp.broadcast_to(b[:, L - 1:L], b.shape)
    m_new = b_last + jnp.maximum(m_prev, jnp.broadcast_to(c_max[:, L - 1:L], b.shape))
    w = jnp.exp(b_last + c - m_new)
    decay = jnp.exp(b_last + m_prev - m_new)

    c2_cols = jnp.concatenate([c * LOG2E, jnp.zeros((LANES - 8, L), F32)], axis=0).T
    big_m2 = big_m * LOG2E
    causal = (lax.broadcasted_iota(jnp.int32, (L, L), 0) <= lax.broadcasted_iota(jnp.int32, (L, L), 1))
    row_head = lax.broadcasted_iota(jnp.int32, (H * W, L), 0) // W
    qk = [_dot(ks, jnp.where(row_head == h, qt, jnp.zeros_like(qt))) for h in range(H)]
    nums, dens = [], []
    for h in range(H):
        d = jnp.exp2(jnp.where(causal, c2_cols[:, h:h + 1] - big_m2[h:h + 1, :], -jnp.inf))
        s = qk[h] * d
        dens.append(jnp.sum(s, axis=0, keepdims=True))
        nums.append(_dot(vt[h * W:(h + 1) * W, :], s.astype(BF16)))
    cq = _dot(ct_scr[...].astype(BF16), qt)
    nq = _dot(nt_scr[...].astype(BF16), qt)
    outs = []
    for h in range(H):
        num = nums[h] + inter[h:h + 1] * cq[h * W:(h + 1) * W]
        den = dens[h] + inter[h:h + 1] * nq[h:h + 1]
        hh = num / jnp.maximum(jnp.abs(den), floor[h:h + 1])
        outs.append(hh * lax.rsqrt(jnp.mean(hh * hh, axis=0, keepdims=True) + NORM_EPS))
    hn = jnp.concatenate(outs, axis=0) * nw_ref[...]
    yt_ref[t] = (_sigmoid(ot_ref[t].astype(F32)) * hn).astype(BF16)

    w_full = jnp.concatenate([jnp.broadcast_to(w[h:h + 1], (W, L)) for h in range(H)], axis=0)
    g_new = _dot((vt.astype(F32) * w_full).astype(BF16), ks)
    decay_full = jnp.concatenate([jnp.broadcast_to(decay[h:h + 1, :1], (W, H * W)) for h in range(H)], axis=0)
    same_head = (lax.broadcasted_iota(jnp.int32, (H * W, H * W), 0) // W
                 == lax.broadcasted_iota(jnp.int32, (H * W, H * W), 1) // W)
    ct_scr[...] = decay_full * ct_scr[...] + jnp.where(same_head, g_new, 0.0)
    w16 = jnp.concatenate([w, jnp.zeros_like(w)], axis=0).astype(BF16)
    n_new = _dot(w16, ks)
    own_lanes = (lax.broadcasted_iota(jnp.int32, (16, H * W), 0)
                 == lax.broadcasted_iota(jnp.int32, (16, H * W), 1) // W)
    decay16 = jnp.broadcast_to(jnp.concatenate([decay[:, :1], jnp.zeros((8, 1), F32)], axis=0), (16, H * W))
    nt_scr[...] = decay16 * nt_scr[...] + jnp.where(own_lanes, n_new, 0.0)
    m_scr[...] = m_new


def _mlstm(mqt, mk, mvt, mot, gt, norm_w_rows, batch, seq):
    ns = seq // (MIX_CHUNK * MIX_STEP_CHUNKS)
    n = batch * seq
    t_spec = pl.BlockSpec((MIX_STEP_CHUNKS, MLSTM_WIDTH, MIX_CHUNK), lambda b, c: (b * ns + c, 0, 0))
    return pl.pallas_call(
        _mlstm_kernel,
        grid=(batch, ns),
        in_specs=[t_spec, pl.BlockSpec((MIX_STEP_CHUNKS * MIX_CHUNK, MLSTM_WIDTH), lambda b, c: (b * ns + c, 0)),
                  t_spec, t_spec,
                  pl.BlockSpec((MIX_STEP_CHUNKS, GATE_ROWS, MIX_CHUNK), lambda b, c: (b * ns + c, 0, 0)),
                  _const_spec((MLSTM_WIDTH, MIX_CHUNK))],
        out_specs=t_spec,
        out_shape=jax.ShapeDtypeStruct((n // MIX_CHUNK, MLSTM_WIDTH, MIX_CHUNK), BF16),
        scratch_shapes=[pltpu.VMEM((MLSTM_WIDTH, MLSTM_WIDTH), F32),
                        pltpu.VMEM((16, MLSTM_WIDTH), F32),
                        pltpu.VMEM((8, MIX_CHUNK), F32)],
        compiler_params=_params(("arbitrary", "arbitrary")),
        name="mlstm",
    )(mqt, mk, mvt, mot, gt, norm_w_rows)


def _ssd_kernel(b_ref, ct_ref, xt_ref, xdt_ref, zt_ref, a_ref, dskip_ref, nw_ref, yt_ref, st_scr):
    @pl.when(pl.program_id(1) == 0)
    def _():
        st_scr[...] = jnp.zeros_like(st_scr)

    for t in range(MIX_STEP_CHUNKS):
        _ssd_chunk(t, b_ref, ct_ref, xt_ref, xdt_ref, zt_ref, a_ref, dskip_ref, nw_ref, yt_ref, st_scr)


def _ssd_chunk(t, b_ref, ct_ref, xt_ref, xdt_ref, zt_ref, a_ref, dskip_ref, nw_ref, yt_ref, st_scr):
    L = MIX_CHUNK
    P = SSM_HEAD_DIM
    HG = SSM_HEADS // SSM_GROUPS
    GW = HG * P
    NS = SSM_STATE
    a = a_ref[t]
    a_last = jnp.broadcast_to(a[:, L - 1:L], a.shape)
    exp_a = jnp.exp(a)
    d_in = jnp.exp(a_last - a)
    chunk_decay = jnp.exp(a_last)
    a2 = a * LOG2E
    a2_cols = jnp.concatenate([a2, jnp.zeros((LANES - SSM_HEADS, L), F32)], axis=0).T
    causal = (lax.broadcasted_iota(jnp.int32, (L, L), 0) <= lax.broadcasted_iota(jnp.int32, (L, L), 1))
    bn = b_ref[t * L:(t + 1) * L, :]
    ct = ct_ref[t]
    xdt = xdt_ref[t]

    def per_head_rows(rows, grp, width):
        return jnp.concatenate([jnp.broadcast_to(rows[grp * HG + i:grp * HG + i + 1, :width], (P, width))
                                for i in range(HG)], axis=0)

    scores, y_off = [], []
    for grp in range(SSM_GROUPS):
        bg = bn[:, grp * NS:(grp + 1) * NS]
        cg = ct[grp * NS:(grp + 1) * NS, :]
        scores.append(_dot(bg, cg))
        st = st_scr[grp]
        y_off.append(_dot(st.astype(BF16), cg))
        x_in = (xdt[grp * GW:(grp + 1) * GW, :].astype(F32) * per_head_rows(d_in, grp, L)).astype(BF16)
        st_scr[grp] = per_head_rows(chunk_decay, grp, NS) * st + _dot(x_in, bg)
    for grp in range(SSM_GROUPS):
        y_diag = []
        for i in range(HG):
            h = grp * HG + i
            seg = jnp.exp2(jnp.where(causal, a2[h:h + 1, :] - a2_cols[:, h:h + 1], -jnp.inf))
            p = (scores[grp] * seg).astype(BF16)
            y_diag.append(_dot(xdt[h * P:(h + 1) * P, :], p))
        rows = slice(grp * GW, (grp + 1) * GW)
        y = (jnp.concatenate(y_diag, axis=0) + y_off[grp] * per_head_rows(exp_a, grp, L)
             + dskip_ref[rows, :] * xt_ref[t, rows, :].astype(F32))
        zg = zt_ref[t, rows, :].astype(F32)
        y = y * (zg * _sigmoid(zg))
        ms = jnp.mean(y * y, axis=0, keepdims=True)
        yt_ref[t, rows, :] = (y * lax.rsqrt(ms + NORM_EPS) * nw_ref[rows, :]).astype(BF16)


def _ssd(sb, sct, sxt, sxdt, zt, sa, dskip_rows, norm_w_rows, batch, seq):
    ns = seq // (MIX_CHUNK * MIX_STEP_CHUNKS)
    n = batch * seq

    def t_spec(width):
        return pl.BlockSpec((MIX_STEP_CHUNKS, width, MIX_CHUNK), lambda b, c: (b * ns + c, 0, 0))

    return pl.pallas_call(
        _ssd_kernel,
        grid=(batch, ns),
        in_specs=[pl.BlockSpec((MIX_STEP_CHUNKS * MIX_CHUNK, SSM_GROUPS * SSM_STATE), lambda b, c: (b * ns + c, 0)),
                  t_spec(SSM_GROUPS * SSM_STATE), t_spec(SSM_WIDTH), t_spec(SSM_WIDTH), t_spec(SSM_WIDTH),
                  t_spec(SSM_HEADS), _const_spec((SSM_WIDTH, MIX_CHUNK)), _const_spec((SSM_WIDTH, MIX_CHUNK))],
        out_specs=t_spec(SSM_WIDTH),
        out_shape=jax.ShapeDtypeStruct((n // MIX_CHUNK, SSM_WIDTH, MIX_CHUNK), BF16),
        scratch_shapes=[pltpu.VMEM((SSM_GROUPS, SSM_WIDTH // SSM_GROUPS, SSM_STATE), F32)],
        compiler_params=_params(("arbitrary", "arbitrary")),
        name="ssd",
    )(sb, sct, sxt, sxdt, zt, sa, dskip_rows, norm_w_rows)


def _t5_bias_tile(rel, relb_ref, head):
    n = jnp.maximum(rel, 0)
    max_exact = REL_BUCKETS // 2
    nf = jnp.maximum(n, 1).astype(F32)
    large = max_exact + (jnp.log(nf / max_exact) / math.log(REL_MAX_DIST / max_exact)
                         * (REL_BUCKETS - max_exact)).astype(jnp.int32)
    large = jnp.minimum(large, REL_BUCKETS - 1)
    bucket = jnp.where(n < max_exact, n, large)
    far = relb_ref[REL_BUCKETS - 1, head]
    bias = jnp.zeros(rel.shape, F32)
    for bkt in range(REL_BUCKETS - 1):
        bias = jnp.where(bucket == bkt, (relb_ref[bkt, head] - far) * LOG2E, bias)
    return jnp.where(rel >= 0, bias, NEG_BIG)


def _attn_kernel(relb_ref, qt_ref, k_ref, vt_ref, lam_ref, subw_ref, g64_ref,
                 y_ref, bias_scr, qm_scr, m_scr, acc_scr, sa_scr, sb_scr, *, lam_init):
    T = ATT_TILE
    H = DIFF_HEADS
    DV = DIFF_V_DIM
    R = ATT_STEP_TILES
    step_idx = pl.program_id(1)

    @pl.when((pl.program_id(0) == 0) & (step_idx == 0))
    def _():
        rel = lax.broadcasted_iota(jnp.int32, (T, T), 1) - lax.broadcasted_iota(jnp.int32, (T, T), 0)
        for h in range(H):
            bias_scr[0, h] = _t5_bias_tile(rel, relb_ref, h)
            bias_scr[1, h] = _t5_bias_tile(rel + T, relb_ref, h)

    comp_of_row = lax.broadcasted_iota(jnp.int32, (DIFF_WIDTH, T), 0) // DIFF_QK_DIM
    for r in range(R):
        qt = qt_ref[r]
        for hc in range(2 * H):
            qm_scr[r, hc] = jnp.where(comp_of_row == hc, qt, jnp.zeros_like(qt))
    m_scr[...] = jnp.full(m_scr.shape, NEG_BIG, F32)
    acc_scr[...] = jnp.zeros_like(acc_scr)
    ones_rows = jnp.ones((ONES_ROWS, T), BF16)

    def key_tile(j):
        return k_ref[pl.ds(pl.multiple_of(j * T, T), T), :]

    def logits_one(s_ref, kt, r, hc):
        s_ref[hc] = _dot(kt, qm_scr[r, hc])

    def consume_one(s_ref, vt, r, hc, bias_idx):
        h = hc // 2
        s = s_ref[hc]
        if bias_idx is not None:
            s = bias_scr[bias_idx, h] + s
        m_prev = m_scr[r, hc]
        m_next = jnp.maximum(m_prev, jnp.max(s, axis=0, keepdims=True))
        alpha = jnp.exp2(m_prev - m_next)
        p = jnp.exp2(s - m_next).astype(BF16)
        m_scr[r, hc] = m_next
        v_aug = jnp.concatenate([vt[h * DV:(h + 1) * DV, :], ones_rows], axis=0)
        acc_scr[r, hc] = alpha * acc_scr[r, hc] + _dot(v_aug, p)

    LEAD = 2

    def step(s_next, j_next, r_next, s_cur, j_cur, r_cur, bias_idx):
        kt = key_tile(j_next)
        vt = vt_ref[j_cur]
        for hc in range(LEAD):
            logits_one(s_next, kt, r_next, hc)
        for hc in range(2 * H):
            if hc + LEAD < 2 * H:
                logits_one(s_next, kt, r_next, hc + LEAD)
            consume_one(s_cur, vt, r_cur, hc, bias_idx)

    DIAG, NEAR = 0, 1

    def far_pairs(count, cur, oth, r):
        def body(t, carry):
            j = 2 * t
            step(oth, j + 1, r, cur, j, r, None)
            step(cur, j + 2, r, oth, j + 1, r, None)
            return carry

        lax.fori_loop(0, count, body, 0)

    cur, oth = sa_scr, sb_scr
    kt0 = key_tile(0)
    for hc in range(2 * H):
        logits_one(cur, kt0, 0, hc)
    for r in range(R):
        d = R * step_idx + r
        if r % 2 == 1:
            far_pairs((d - 1) // 2, cur, oth, r)
            step(oth, d, r, cur, d - 1, r, NEAR)
            diag, free = oth, cur
        else:
            far_pairs(jnp.maximum(d // 2 - 1, 0), cur, oth, r)

            def near_tiles(d=d, cur=cur, oth=oth, r=r):
                step(oth, d - 1, r, cur, d - 2, r, None)
                step(cur, d, r, oth, d - 1, r, NEAR)

            if r == 0:
                pl.when(d >= 1)(near_tiles)
            else:
                near_tiles()
            diag, free = cur, oth
        if r + 1 < R:
            step(free, 0, r + 1, diag, d, r, DIAG)
            cur, oth = free, diag
        else:
            vt_last = vt_ref[d]
            for hc in range(2 * H):
                consume_one(diag, vt_last, r, hc, DIAG)

    lam_p = lam_ref[...]
    lam = (jnp.exp(jnp.sum(lam_p[0:1] * lam_p[1:2], axis=1, keepdims=True))
           - jnp.exp(jnp.sum(lam_p[2:3] * lam_p[3:4], axis=1, keepdims=True)) + lam_init)
    for r in range(R):
        def normalised(hc):
            return acc_scr[r, hc, :DV, :] / acc_scr[r, hc, DV:DV + 1, :]

        o_t = jnp.concatenate([normalised(2 * h) - lam * normalised(2 * h + 1) for h in range(H)], axis=0)
        o = o_t.T
        ms = _group_mean_sq(o, g64_ref[...])
        y_ref[r * T:(r + 1) * T, :] = (o * lax.rsqrt(ms + NORM_EPS) * subw_ref[...] * (1.0 - lam_init)).astype(BF16)


def _diff_attn(qt, kn, vt, rel_bias, lambdas, subw_row, g64, lam_init, batch, seq):
    nk = seq // ATT_TILE
    nsteps = nk // ATT_STEP_TILES
    n = batch * seq
    return pl.pallas_call(
        functools.partial(_attn_kernel, lam_init=lam_init),
        grid=(batch, nsteps),
        in_specs=[pl.BlockSpec(memory_space=pltpu.SMEM),
                  pl.BlockSpec((ATT_STEP_TILES, DIFF_WIDTH, ATT_TILE), lambda b, i: (b * nsteps + i, 0, 0)),
                  pl.BlockSpec((seq, DIFF_WIDTH), lambda b, i: (b, 0)),
                  pl.BlockSpec((nk, DIFF_WIDTH, ATT_TILE), lambda b, i: (b, 0, 0)),
                  _const_spec((4, DIFF_QK_DIM)), _const_spec((1, DIFF_WIDTH)),
                  _const_spec((DIFF_WIDTH, DIFF_WIDTH))],
        out_specs=pl.BlockSpec((ATT_STEP_TILES * ATT_TILE, DIFF_WIDTH), lambda b, i: (b * nsteps + i, 0)),
        out_shape=jax.ShapeDtypeStruct((n, DIFF_WIDTH), BF16),
        scratch_shapes=[pltpu.VMEM((2, DIFF_HEADS, ATT_TILE, ATT_TILE), F32),
                        pltpu.VMEM((ATT_STEP_TILES, 2 * DIFF_HEADS, DIFF_WIDTH, ATT_TILE), BF16),
                        pltpu.VMEM((ATT_STEP_TILES, 2 * DIFF_HEADS, 1, ATT_TILE), F32),
                        pltpu.VMEM((ATT_STEP_TILES, 2 * DIFF_HEADS, DIFF_V_DIM + ONES_ROWS, ATT_TILE), F32),
                        pltpu.VMEM((2 * DIFF_HEADS, ATT_TILE, ATT_TILE), F32),
                        pltpu.VMEM((2 * DIFF_HEADS, ATT_TILE, ATT_TILE), F32)],
        compiler_params=_params(("arbitrary", "arbitrary")),
        name="diff_attn",
    )(rel_bias, qt, kn, vt, lambdas, subw_row, g64)


def _untransposed(yt_ref):
    return jnp.concatenate([yt_ref[t].astype(F32).T for t in range(yt_ref.shape[0])], axis=0).astype(BF16)


def _outproj_ffn_kernel(x_ref, ymt_ref, yst_ref, yd_ref, w_hbm, fnw_ref, wg_hbm, wu_hbm, wd_hbm, o_ref,
                        w_ref, wg_ref, wu_ref, wd_ref, wide, wide_sem, tall, tall_sem, *, layer):
    @pl.when(pl.program_id(0) == 0)
    def _():
        def store(r0, chunk):
            w_ref[r0:r0 + chunk.shape[0], :] = chunk.astype(BF16)

        _stage_weight(w_hbm, layer, D_MODEL // 4, tall, tall_sem, store)
        _stage_ffn_weights(layer, wg_hbm, wu_hbm, wd_hbm, wg_ref, wu_ref, wd_ref, wide, wide_sem, tall, tall_sem)

    y = jnp.concatenate([_untransposed(ymt_ref), _untransposed(yst_ref), yd_ref[...]], axis=1)
    x = x_ref[...] + _dot(y, w_ref[...])
    o_ref[...] = _ffn_block(x, fnw_ref, wg_ref, wu_ref, wd_ref)


def _outproj_ffn(x, y_m, y_s, y_d, w_out, ffn_norm_w, wg, wu, wd, layer):
    n = x.shape[0]
    return pl.pallas_call(
        functools.partial(_outproj_ffn_kernel, layer=layer),
        grid=(n // ROW_TILE,),
        in_specs=[_row_spec(D_MODEL),
                  pl.BlockSpec((ROW_TILE // MIX_CHUNK, MLSTM_WIDTH, MIX_CHUNK), lambda i: (i, 0, 0)),
                  pl.BlockSpec((ROW_TILE // MIX_CHUNK, SSM_WIDTH, MIX_CHUNK), lambda i: (i, 0, 0)),
                  _row_spec(DIFF_WIDTH), _HBM, _const_spec((1, D_MODEL)), _HBM, _HBM, _HBM],
        out_specs=_row_spec(D_MODEL),
        out_shape=jax.ShapeDtypeStruct((n, D_MODEL), F32),
        scratch_shapes=[pltpu.VMEM((D_MODEL, D_MODEL), BF16)] + _ffn_weight_scratch(),
        compiler_params=_params(("arbitrary",)),
        name="outproj_ffn",
    )(x, y_m, y_s, y_d, w_out, ffn_norm_w.reshape(1, D_MODEL), wg, wu, wd)


def _small_row(values_by_lane):
    row = jnp.zeros((LANES,), F32)
    for lane, vals in values_by_lane:
        row = lax.dynamic_update_slice(row, vals.astype(F32), (lane,))
    return row.reshape(1, LANES)


def kernel(x, ffn1_norm_w, ffn1_w_gate, ffn1_w_up, ffn1_w_down, mix_norm_w, w_in, mlstm_gate_bias, mlstm_norm_w, ssm_conv_w, ssm_conv_b, ssm_dt_bias, ssm_A_log, ssm_D, ssm_norm_w, diff_q_norm_w, diff_k_norm_w, diff_lambda, diff_subln_w, rel_bias, w_out, ffn2_norm_w, ffn2_w_gate, ffn2_w_up, ffn2_w_down):
    batch, seq, d = x.shape
    assert d == D_MODEL and MIX_CHUNK == ATT_TILE and seq % ROW_TILE == 0 and ROW_TILE % MIX_CHUNK == 0
    assert seq % (MIX_CHUNK * MIX_STEP_CHUNKS) == 0 and seq % (ATT_STEP_TILES * ATT_TILE) == 0 and ATT_STEP_TILES % 2 == 0
    xf = x.reshape(batch * seq, D_MODEL)
    g32 = _block_diag(DIFF_WIDTH, DIFF_QK_DIM, 1.0 / DIFF_QK_DIM)
    g64 = _block_diag(DIFF_WIDTH, DIFF_V_DIM, 1.0 / DIFF_V_DIM)
    ffn1 = (ffn1_w_gate, ffn1_w_up, ffn1_w_down)
    ffn2 = (ffn2_w_gate, ffn2_w_up, ffn2_w_down)
    for l in range(DEPTH):
        qkw_row = jnp.concatenate([jnp.tile(diff_q_norm_w[l].reshape(-1), DIFF_HEADS) * (DIFF_QK_DIM ** -0.5 * LOG2E),
                                   jnp.tile(diff_k_norm_w[l].reshape(-1), DIFF_HEADS)]).reshape(1, 2 * DIFF_WIDTH)
        bias_row = _small_row([(_LANE_I, mlstm_gate_bias[l, 0]), (_LANE_F, mlstm_gate_bias[l, 1]),
                               (_LANE_DT, ssm_dt_bias[l])])
        (xf, mqt, mk, mvt, mot, gt, zt, sb, sct, sxt, sxdt, sa, qt, kn, vt) = _ffn_inproj(
            xf, ffn1_norm_w[l], *ffn1, mix_norm_w[l], w_in, qkw_row, g32,
            jnp.broadcast_to(bias_row.reshape(LANES, 1), (LANES, ROW_TILE)),
            jnp.broadcast_to(ssm_A_log[l][:, None], (SSM_HEADS, ROW_TILE)),
            ssm_conv_w[l], ssm_conv_b[l].reshape(1, SSM_CONV_DIM), l, seq)
        y_m = _mlstm(mqt, mk, mvt, mot, gt, jnp.broadcast_to(mlstm_norm_w[l][:, None], (MLSTM_WIDTH, MIX_CHUNK)),
                     batch, seq)
        y_s = _ssd(sb, sct, sxt, sxdt, zt, sa,
                   jnp.broadcast_to(jnp.repeat(ssm_D[l], SSM_HEAD_DIM)[:, None], (SSM_WIDTH, MIX_CHUNK)),
                   jnp.broadcast_to(ssm_norm_w[l][:, None], (SSM_WIDTH, MIX_CHUNK)), batch, seq)
        lam_init = 0.8 - 0.6 * math.exp(-0.3 * l)
        y_d = _diff_attn(qt, kn, vt, rel_bias, diff_lambda[l], jnp.tile(diff_subln_w[l], DIFF_HEADS).reshape(1, DIFF_WIDTH),
                         g64, lam_init, batch, seq)
        xf = _outproj_ffn(xf, y_m, y_s, y_d, w_out, ffn2_norm_w[l], *ffn2, l)
    return xf.reshape(batch, seq, D_MODEL)
```

```python
import functools
import math

import jax
import jax.numpy as jnp
from jax import lax
from jax.experimental import pallas as pl
from jax.experimental.pallas import tpu as pltpu

F32 = jnp.float32
BF16 = jnp.bfloat16

D_MODEL = 1024
DEPTH = 2
D_FF = 2816
MLSTM_HEADS = 4
MLSTM_HEAD_DIM = 64
MLSTM_WIDTH = MLSTM_HEADS * MLSTM_HEAD_DIM
SSM_HEADS = 8
SSM_HEAD_DIM = 64
SSM_WIDTH = SSM_HEADS * SSM_HEAD_DIM
SSM_STATE = 128
SSM_GROUPS = 2
SSM_CONV = 4
SSM_CONV_DIM = SSM_WIDTH + 2 * SSM_GROUPS * SSM_STATE
DIFF_HEADS = 4
DIFF_QK_DIM = 32
DIFF_V_DIM = 64
DIFF_WIDTH = DIFF_HEADS * DIFF_V_DIM
REL_BUCKETS = 32
REL_MAX_DIST = 128
NORM_EPS = 1e-6

LANES = 128
ROW_TILE = 512
MXU_COLS = 256
FFN_SPLITS = (0, 6 * MXU_COLS, D_FF)
MIX_CHUNK = 256
MIX_STEP_CHUNKS = 8
ATT_TILE = 256
ATT_STEP_TILES = 4
VMEM_LIMIT = 56 * 1024 * 1024
NEG_BIG = -1e30
LOG2E = math.log2(math.e)
ONES_ROWS = 16

_C_M = 0
_C_Z = _C_M + 4 * MLSTM_WIDTH
_C_XBC = _C_Z + SSM_WIDTH
_C_SMALL = _C_XBC + SSM_CONV_DIM
_C_QK = _C_SMALL + LANES
_C_V = _C_QK + 4 * DIFF_HEADS * DIFF_QK_DIM
_C_END = _C_V + DIFF_WIDTH
_LANE_I = 0
_LANE_F = 8
_LANE_DT = 16
GATE_ROWS = 24
D_IN = 4 * MLSTM_WIDTH + 2 * MLSTM_HEADS + SSM_WIDTH + SSM_CONV_DIM + SSM_HEADS + 3 * DIFF_WIDTH


def _regroup_plan():
    sizes = (4 * MLSTM_WIDTH, MLSTM_HEADS, MLSTM_HEADS, SSM_WIDTH, SSM_CONV_DIM, SSM_HEADS, 3 * DIFF_WIDTH)
    dests = (_C_M, _C_SMALL + _LANE_I, _C_SMALL + _LANE_F, _C_Z, _C_XBC, _C_SMALL + _LANE_DT, _C_QK)
    plan, src = [], 0
    for dst, width in zip(dests, sizes):
        plan.append((dst, src, width))
        src += width
    assert src == D_IN
    return plan


def _dot(a, b):
    return jnp.dot(a, b, preferred_element_type=F32)


def _sigmoid(x):
    return 1.0 / (1.0 + jnp.exp(-x))


def _softplus(x):
    return jnp.maximum(x, 0.0) + jnp.log(1.0 + jnp.exp(-jnp.abs(x)))


def _rms_rows(x, w_row):
    ms = jnp.mean(x * x, axis=-1, keepdims=True)
    return x * lax.rsqrt(ms + NORM_EPS) * w_row


def _group_mean_sq(x, gmat):
    sq = x * x
    hi = sq.astype(BF16)
    lo = (sq - hi.astype(F32)).astype(BF16)
    return _dot(hi, gmat) + _dot(lo, gmat)


def _scan_lanes(x, op, identity):
    n = x.shape[1]
    lane = lax.broadcasted_iota(jnp.int32, x.shape, 1)
    step = 1
    while step < n:
        x = op(x, jnp.where(lane >= step, pltpu.roll(x, step, axis=1), identity))
        step *= 2
    return x


def _block_diag(n, group, value):
    r = jnp.arange(n) // group
    return jnp.where(r[:, None] == r[None, :], value, 0.0).astype(BF16)


def _const_spec(shape):
    nd = len(shape)
    return pl.BlockSpec(shape, lambda *_: (0,) * nd)


def _params(sem):
    return pltpu.CompilerParams(dimension_semantics=sem, vmem_limit_bytes=VMEM_LIMIT)


def _ffn_block(x, nw_ref, wg_ref, wu_ref, wd_ref):
    xn = _rms_rows(x, nw_ref[...]).astype(BF16)
    acc = None
    for lo, hi in zip(FFN_SPLITS[:-1], FFN_SPLITS[1:]):
        assert (hi - lo) % MXU_COLS == 0
        cols = slice(lo, hi)
        g = _dot(xn, wg_ref[:, cols])
        u = _dot(xn, wu_ref[:, cols])
        h = (g * _sigmoid(g) * u).astype(BF16)
        part = _dot(h, wd_ref[cols, :])
        acc = part if acc is None else acc + part
    return x + 0.5 * acc


def _row_spec(width):
    return pl.BlockSpec((ROW_TILE, width), lambda i: (i, 0))


WIDE_STAGE_ROWS = 128
TALL_STAGE_ROWS = 352
_HBM = pl.BlockSpec(memory_space=pl.ANY)


def _stage_weight(w_hbm, layer, rows, stage, sem, store):
    n_rows, n_cols = w_hbm.shape[1], w_hbm.shape[2]
    assert n_rows % rows == 0

    def copy(c):
        return pltpu.make_async_copy(w_hbm.at[layer, pl.ds(c * rows, rows), :],
                                     stage.at[c % 2, pl.ds(0, rows), pl.ds(0, n_cols)], sem.at[c % 2])

    copy(0).start()
    for c in range(n_rows // rows):
        if c + 1 < n_rows // rows:
            copy(c + 1).start()
        copy(c).wait()
        store(c * rows, stage[c % 2, :rows, :n_cols])


def _stage_ffn_weights(layer, wg_hbm, wu_hbm, wd_hbm, wg_scr, wu_scr, wd_scr, wide, wide_sem, tall, tall_sem):
    def into(dst):
        def store(r0, chunk):
            dst[r0:r0 + chunk.shape[0], :] = chunk.astype(BF16)
        return store

    _stage_weight(wg_hbm, layer, WIDE_STAGE_ROWS, wide, wide_sem, into(wg_scr))
    _stage_weight(wu_hbm, layer, WIDE_STAGE_ROWS, wide, wide_sem, into(wu_scr))
    _stage_weight(wd_hbm, layer, TALL_STAGE_ROWS, tall, tall_sem, into(wd_scr))


def _ffn_weight_scratch():
    return [pltpu.VMEM((D_MODEL, D_FF), BF16), pltpu.VMEM((D_MODEL, D_FF), BF16), pltpu.VMEM((D_FF, D_MODEL), BF16),
            pltpu.VMEM((2, WIDE_STAGE_ROWS, D_IN), F32), pltpu.SemaphoreType.DMA((2,)),
            pltpu.VMEM((2, TALL_STAGE_ROWS, D_MODEL), F32), pltpu.SemaphoreType.DMA((2,))]


def _store_transposed_tiles(out_ref, a):
    a_t = a.T.astype(BF16)
    for t in range(ROW_TILE // ATT_TILE):
        out_ref[t] = a_t[:, t * ATT_TILE:(t + 1) * ATT_TILE]


def _ffn_inproj_kernel(x_ref, fnw_ref, wg_hbm, wu_hbm, wd_hbm, nw_ref, w_hbm, qkw_ref, g32_ref, gbias_ref,
                       alog_ref, convw_ref, convb_ref,
                       x_out_ref, mqt_ref, mk_ref, mvt_ref, mot_ref, gt_ref,
                       zt_ref, sb_ref, sct_ref, sxt_ref, sxdt_ref, sa_ref, qt_ref, k_ref, vt_ref,
                       halo_scr, w_scr, wg_ref, wu_ref, wd_ref, wide, wide_sem, tall, tall_sem,
                       *, tiles_per_seq, layer):
    @pl.when(pl.program_id(0) == 0)
    def _():
        _stage_ffn_weights(layer, wg_hbm, wu_hbm, wd_hbm, wg_ref, wu_ref, wd_ref, wide, wide_sem, tall, tall_sem)
        w_scr[:, _C_SMALL:_C_QK] = jnp.zeros((D_MODEL, LANES), BF16)

        def regrouped(r0, chunk):
            for dst, src, width in _regroup_plan():
                w_scr[r0:r0 + chunk.shape[0], dst:dst + width] = chunk[:, src:src + width].astype(BF16)

        _stage_weight(w_hbm, layer, WIDE_STAGE_ROWS, wide, wide_sem, regrouped)

    @pl.when(pl.program_id(0) % tiles_per_seq == 0)
    def _():
        halo_scr[...] = jnp.zeros_like(halo_scr)

    x = _ffn_block(x_ref[...], fnw_ref, wg_ref, wu_ref, wd_ref)
    x_out_ref[...] = x
    xn = _rms_rows(x, nw_ref[...]).astype(BF16)

    def proj(lo, hi):
        return _dot(xn, w_scr[:, lo:hi])

    half = (_C_V - _C_QK) // 2
    p_xbc = proj(_C_XBC, _C_SMALL)
    p_small = proj(_C_SMALL, _C_QK)
    p_q, p_k = proj(_C_QK, _C_QK + half), proj(_C_QK + half, _C_V)
    p_mq = proj(_C_M, _C_M + MLSTM_WIDTH)
    p_mk = proj(_C_M + MLSTM_WIDTH, _C_M + 2 * MLSTM_WIDTH)
    p_mv = proj(_C_M + 2 * MLSTM_WIDTH, _C_M + 3 * MLSTM_WIDTH)
    p_mo = proj(_C_M + 3 * MLSTM_WIDTH, _C_Z)
    p_z = proj(_C_Z, _C_XBC)
    p_v = proj(_C_V, _C_END)

    qk = [a * lax.rsqrt(_group_mean_sq(a, g32_ref[...]) + NORM_EPS) * qkw_ref[:, i * half:(i + 1) * half]
          for i, a in enumerate((p_q, p_k))]
    _store_transposed_tiles(qt_ref, qk[0])
    k_ref[...] = qk[1].astype(BF16)
    _store_transposed_tiles(vt_ref, p_v)

    ext = jnp.concatenate([halo_scr[...], p_xbc], axis=0)
    halo_scr[...] = p_xbc[ROW_TILE - 8:, :]
    conv = convb_ref[...] + convw_ref[SSM_CONV - 1:SSM_CONV, :] * p_xbc
    for j in range(1, SSM_CONV):
        conv = conv + convw_ref[SSM_CONV - 1 - j:SSM_CONV - j, :] * ext[8 - j:8 - j + ROW_TILE, :]
    xa = conv * _sigmoid(conv)
    sb_ref[...] = xa[:, SSM_WIDTH:SSM_WIDTH + SSM_GROUPS * SSM_STATE].astype(BF16)
    _store_transposed_tiles(sct_ref, xa[:, SSM_WIDTH + SSM_GROUPS * SSM_STATE:])
    small_t = p_small.T + gbias_ref[...]
    dt = _softplus(small_t[_LANE_DT:_LANE_DT + SSM_HEADS])
    log_decay = dt * (-jnp.exp(alog_ref[...]))
    xs_t = xa[:, :SSM_WIDTH].T
    xdt_t = xs_t * jnp.concatenate([jnp.broadcast_to(dt[h:h + 1], (SSM_HEAD_DIM, ROW_TILE))
                                    for h in range(SSM_HEADS)], axis=0)
    for t in range(ROW_TILE // MIX_CHUNK):
        cols = slice(t * MIX_CHUNK, (t + 1) * MIX_CHUNK)
        sxt_ref[t] = xs_t[:, cols].astype(BF16)
        sxdt_ref[t] = xdt_t[:, cols].astype(BF16)
        sa_ref[t] = _scan_lanes(log_decay[:, cols], jnp.add, 0.0)
    _store_transposed_tiles(zt_ref, p_z)

    _store_transposed_tiles(mqt_ref, p_mq)
    mk_ref[...] = p_mk.astype(BF16)
    _store_transposed_tiles(mvt_ref, p_mv)
    _store_transposed_tiles(mot_ref, p_mo)
    gates = small_t[:_LANE_DT]
    for t in range(ROW_TILE // MIX_CHUNK):
        i_pre = gates[_LANE_I:_LANE_I + 8, t * MIX_CHUNK:(t + 1) * MIX_CHUNK]
        f_pre = gates[_LANE_F:_LANE_F + 8, t * MIX_CHUNK:(t + 1) * MIX_CHUNK]
        log_f = jnp.minimum(f_pre, 0.0) - jnp.log(1.0 + jnp.exp(-jnp.abs(f_pre)))
        b = _scan_lanes(log_f, jnp.add, 0.0)
        c = i_pre - b
        gt_ref[t] = jnp.concatenate([c, _scan_lanes(c, jnp.maximum, -jnp.inf), b], axis=0)


def _ffn_inproj(x, ffn_norm_w, wg, wu, wd, norm_w, w_all, qkw_row, g32, gate_bias_rows, alog_rows, conv_w, conv_b_row,
                layer, seq):
    n = x.shape[0]
    tiles = ROW_TILE // ATT_TILE

    def t_out(width, dtype=BF16):
        return (pl.BlockSpec((tiles, width, ATT_TILE), lambda i: (i, 0, 0)),
                jax.ShapeDtypeStruct((n // ATT_TILE, width, ATT_TILE), dtype))

    def r_out(width, dtype=BF16):
        return _row_spec(width), jax.ShapeDtypeStruct((n, width), dtype)

    outs = [r_out(D_MODEL, F32),
            t_out(MLSTM_WIDTH), r_out(MLSTM_WIDTH), t_out(MLSTM_WIDTH), t_out(MLSTM_WIDTH), t_out(GATE_ROWS, F32),
            t_out(SSM_WIDTH), r_out(SSM_GROUPS * SSM_STATE), t_out(SSM_GROUPS * SSM_STATE), t_out(SSM_WIDTH),
            t_out(SSM_WIDTH), t_out(SSM_HEADS, F32),
            t_out(DIFF_WIDTH), r_out(DIFF_WIDTH), t_out(DIFF_WIDTH)]
    return pl.pallas_call(
        functools.partial(_ffn_inproj_kernel, tiles_per_seq=seq // ROW_TILE, layer=layer),
        grid=(n // ROW_TILE,),
        in_specs=[_row_spec(D_MODEL), _const_spec((1, D_MODEL)), _HBM, _HBM, _HBM, _const_spec((1, D_MODEL)), _HBM,
                  _const_spec((1, 2 * DIFF_WIDTH)), _const_spec((DIFF_WIDTH, DIFF_WIDTH)),
                  _const_spec((LANES, ROW_TILE)), _const_spec((SSM_HEADS, ROW_TILE)),
                  _const_spec((SSM_CONV, SSM_CONV_DIM)), _const_spec((1, SSM_CONV_DIM))],
        out_specs=[o[0] for o in outs],
        out_shape=[o[1] for o in outs],
        scratch_shapes=[pltpu.VMEM((8, SSM_CONV_DIM), F32), pltpu.VMEM((D_MODEL, _C_END), BF16)]
        + _ffn_weight_scratch(),
        compiler_params=_params(("arbitrary",)),
        name="ffn_inproj",
    )(x, ffn_norm_w.reshape(1, D_MODEL), wg, wu, wd, norm_w.reshape(1, D_MODEL), w_all, qkw_row, g32,
      gate_bias_rows, alog_rows, conv_w, conv_b_row)


def _mlstm_kernel(qt_ref, k_ref, vt_ref, ot_ref, gt_ref, nw_ref, yt_ref, ct_scr, nt_scr, m_scr):
    @pl.when(pl.program_id(1) == 0)
    def _():
        ct_scr[...] = jnp.zeros_like(ct_scr)
        nt_scr[...] = jnp.zeros_like(nt_scr)
        m_scr[...] = jnp.zeros_like(m_scr)

    for t in range(MIX_STEP_CHUNKS):
        _mlstm_chunk(t, qt_ref, k_ref, vt_ref, ot_ref, gt_ref, nw_ref, yt_ref, ct_scr, nt_scr, m_scr)


def _mlstm_chunk(t, qt_ref, k_ref, vt_ref, ot_ref, gt_ref, nw_ref, yt_ref, ct_scr, nt_scr, m_scr):
    L = MIX_CHUNK
    H = MLSTM_HEADS
    W = MLSTM_HEAD_DIM
    qt = qt_ref[t]
    ks = k_ref[t * L:(t + 1) * L, :] * (MLSTM_HEAD_DIM ** -0.5)
    vt = vt_ref[t]
    gt = gt_ref[t]
    c, c_max, b = gt[0:8], gt[8:16], gt[16:24]
    m_prev = m_scr[...]
    big_m = jnp.maximum(m_prev, c_max)
    inter = jnp.exp(m_prev - big_m)
    floor = jnp.exp(-(b + big_m))
    b_last = jnp.broadcast_to(b[:, L - 1:L], b.shape)
    m_new = b_last + jnp.maximum(m_prev, jnp.broadcast_to(c_max[:, L - 1:L], b.shape))
    w = jnp.exp(b_last + c - m_new)
    decay = jnp.exp(b_last + m_prev - m_new)

    c2_cols = jnp.concatenate([c * LOG2E, jnp.zeros((LANES - 8, L), F32)], axis=0).T
    big_m2 = big_m * LOG2E
    causal = (lax.broadcasted_iota(jnp.int32, (L, L), 0) <= lax.broadcasted_iota(jnp.int32, (L, L), 1))
    row_head = lax.broadcasted_iota(jnp.int32, (H * W, L), 0) // W
    qk = [_dot(ks, jnp.where(row_head == h, qt, jnp.zeros_like(qt))) for h in range(H)]
    nums, dens = [], []
    for h in range(H):
        d = jnp.exp2(jnp.where(causal, c2_cols[:, h:h + 1] - big_m2[h:h + 1, :], -jnp.inf))
        s = qk[h] * d
        dens.append(jnp.sum(s, axis=0, keepdims=True))
        nums.append(_dot(vt[h * W:(h + 1) * W, :], s.astype(BF16)))
    cq = _dot(ct_scr[...].astype(BF16), qt)
    nq = _dot(nt_scr[...].astype(BF16), qt)
    outs = []
    for h in range(H):
        num = nums[h] + inter[h:h + 1] * cq[h * W:(h + 1) * W]
        den = dens[h] + inter[h:h + 1] * nq[h:h + 1]
        hh = num / jnp.maximum(jnp.abs(den), floor[h:h + 1])
        outs.append(hh * lax.rsqrt(jnp.mean(hh * hh, axis=0, keepdims=True) + NORM_EPS))
    hn = jnp.concatenate(outs, axis=0) * nw_ref[...]
    yt_ref[t] = (_sigmoid(ot_ref[t].astype(F32)) * hn).astype(BF16)

    w_full = jnp.concatenate([jnp.broadcast_to(w[h:h + 1], (W, L)) for h in range(H)], axis=0)
    g_new = _dot((vt.astype(F32) * w_full).astype(BF16), ks)
    decay_full = jnp.concatenate([jnp.broadcast_to(decay[h:h + 1, :1], (W, H * W)) for h in range(H)], axis=0)
    same_head = (lax.broadcasted_iota(jnp.int32, (H * W, H * W), 0) // W
                 == lax.broadcasted_iota(jnp.int32, (H * W, H * W), 1) // W)
    ct_scr[...] = decay_full * ct_scr[...] + jnp.where(same_head, g_new, 0.0)
    w16 = jnp.concatenate([w, jnp.zeros_like(w)], axis=0).astype(BF16)
    n_new = _dot(w16, ks)
    own_lanes = (lax.broadcasted_iota(jnp.int32, (16, H * W), 0)
                 == lax.broadcasted_iota(jnp.int32, (16, H * W), 1) // W)
    decay16 = jnp.broadcast_to(jnp.concatenate([decay[:, :1], jnp.zeros((8, 1), F32)], axis=0), (16, H * W))
    nt_scr[...] = decay16 * nt_scr[...] + jnp.where(own_lanes, n_new, 0.0)
    m_scr[...] = m_new


def _mlstm(mqt, mk, mvt, mot, gt, norm_w_rows, batch, seq):
    ns = seq // (MIX_CHUNK * MIX_STEP_CHUNKS)
    n = batch * seq
    t_spec = pl.BlockSpec((MIX_STEP_CHUNKS, MLSTM_WIDTH, MIX_CHUNK), lambda b, c: (b * ns + c, 0, 0))
    return pl.pallas_call(
        _mlstm_kernel,
        grid=(batch, ns),
        in_specs=[t_spec, pl.BlockSpec((MIX_STEP_CHUNKS * MIX_CHUNK, MLSTM_WIDTH), lambda b, c: (b * ns + c, 0)),
                  t_spec, t_spec,
                  pl.BlockSpec((MIX_STEP_CHUNKS, GATE_ROWS, MIX_CHUNK), lambda b, c: (b * ns + c, 0, 0)),
                  _const_spec((MLSTM_WIDTH, MIX_CHUNK))],
        out_specs=t_spec,
        out_shape=jax.ShapeDtypeStruct((n // MIX_CHUNK, MLSTM_WIDTH, MIX_CHUNK), BF16),
        scratch_shapes=[pltpu.VMEM((MLSTM_WIDTH, MLSTM_WIDTH), F32),
                        pltpu.VMEM((16, MLSTM_WIDTH), F32),
                        pltpu.VMEM((8, MIX_CHUNK), F32)],
        compiler_params=_params(("arbitrary", "arbitrary")),
        name="mlstm",
    )(mqt, mk, mvt, mot, gt, norm_w_rows)


def _ssd_kernel(b_ref, ct_ref, xt_ref, xdt_ref, zt_ref, a_ref, dskip_ref, nw_ref, yt_ref, st_scr):
    @pl.when(pl.program_id(1) == 0)
    def _():
        st_scr[...] = jnp.zeros_like(st_scr)

    for t in range(MIX_STEP_CHUNKS):
        _ssd_chunk(t, b_ref, ct_ref, xt_ref, xdt_ref, zt_ref, a_ref, dskip_ref, nw_ref, yt_ref, st_scr)


def _ssd_chunk(t, b_ref, ct_ref, xt_ref, xdt_ref, zt_ref, a_ref, dskip_ref, nw_ref, yt_ref, st_scr):
    L = MIX_CHUNK
    P = SSM_HEAD_DIM
    HG = SSM_HEADS // SSM_GROUPS
    GW = HG * P
    NS = SSM_STATE
    a = a_ref[t]
    a_last = jnp.broadcast_to(a[:, L - 1:L], a.shape)
    exp_a = jnp.exp(a)
    d_in = jnp.exp(a_last - a)
    chunk_decay = jnp.exp(a_last)
    a2 = a * LOG2E
    a2_cols = jnp.concatenate([a2, jnp.zeros((LANES - SSM_HEADS, L), F32)], axis=0).T
    causal = (lax.broadcasted_iota(jnp.int32, (L, L), 0) <= lax.broadcasted_iota(jnp.int32, (L, L), 1))
    bn = b_ref[t * L:(t + 1) * L, :]
    ct = ct_ref[t]
    xdt = xdt_ref[t]

    def per_head_rows(rows, grp, width):
        return jnp.concatenate([jnp.broadcast_to(rows[grp * HG + i:grp * HG + i + 1, :width], (P, width))
                                for i in range(HG)], axis=0)

    scores, y_off = [], []
    for grp in range(SSM_GROUPS):
        bg = bn[:, grp * NS:(grp + 1) * NS]
        cg = ct[grp * NS:(grp + 1) * NS, :]
        scores.append(_dot(bg, cg))
        st = st_scr[grp]
        y_off.append(_dot(st.astype(BF16), cg))
        x_in = (xdt[grp * GW:(grp + 1) * GW, :].astype(F32) * per_head_rows(d_in, grp, L)).astype(BF16)
        st_scr[grp] = per_head_rows(chunk_decay, grp, NS) * st + _dot(x_in, bg)
    for grp in range(SSM_GROUPS):
        y_diag = []
        for i in range(HG):
            h = grp * HG + i
            seg = jnp.exp2(jnp.where(causal, a2[h:h + 1, :] - a2_cols[:, h:h + 1], -jnp.inf))
            p = (scores[grp] * seg).astype(BF16)
            y_diag.append(_dot(xdt[h * P:(h + 1) * P, :], p))
        rows = slice(grp * GW, (grp + 1) * GW)
        y = (jnp.concatenate(y_diag, axis=0) + y_off[grp] * per_head_rows(exp_a, grp, L)
             + dskip_ref[rows, :] * xt_ref[t, rows, :].astype(F32))
        zg = zt_ref[t, rows, :].astype(F32)
        y = y * (zg * _sigmoid(zg))
        ms = jnp.mean(y * y, axis=0, keepdims=True)
        yt_ref[t, rows, :] = (y * lax.rsqrt(ms + NORM_EPS) * nw_ref[rows, :]).astype(BF16)


def _ssd(sb, sct, sxt, sxdt, zt, sa, dskip_rows, norm_w_rows, batch, seq):
    ns = seq // (MIX_CHUNK * MIX_STEP_CHUNKS)
    n = batch * seq

    def t_spec(width):
        return pl.BlockSpec((MIX_STEP_CHUNKS, width, MIX_CHUNK), lambda b, c: (b * ns + c, 0, 0))

    return pl.pallas_call(
        _ssd_kernel,
        grid=(batch, ns),
        in_specs=[pl.BlockSpec((MIX_STEP_CHUNKS * MIX_CHUNK, SSM_GROUPS * SSM_STATE), lambda b, c: (b * ns + c, 0)),
                  t_spec(SSM_GROUPS * SSM_STATE), t_spec(SSM_WIDTH), t_spec(SSM_WIDTH), t_spec(SSM_WIDTH),
                  t_spec(SSM_HEADS), _const_spec((SSM_WIDTH, MIX_CHUNK)), _const_spec((SSM_WIDTH, MIX_CHUNK))],
        out_specs=t_spec(SSM_WIDTH),
        out_shape=jax.ShapeDtypeStruct((n // MIX_CHUNK, SSM_WIDTH, MIX_CHUNK), BF16),
        scratch_shapes=[pltpu.VMEM((SSM_GROUPS, SSM_WIDTH // SSM_GROUPS, SSM_STATE), F32)],
        compiler_params=_params(("arbitrary", "arbitrary")),
        name="ssd",
    )(sb, sct, sxt, sxdt, zt, sa, dskip_rows, norm_w_rows)


def _t5_bias_tile(rel, relb_ref, head):
    n = jnp.maximum(rel, 0)
    max_exact = REL_BUCKETS // 2
    nf = jnp.maximum(n, 1).astype(F32)
    large = max_exact + (jnp.log(nf / max_exact) / math.log(REL_MAX_DIST / max_exact)
                         * (REL_BUCKETS - max_exact)).astype(jnp.int32)
    large = jnp.minimum(large, REL_BUCKETS - 1)
    bucket = jnp.where(n < max_exact, n, large)
    far = relb_ref[REL_BUCKETS - 1, head]
    bias = jnp.zeros(rel.shape, F32)
    for bkt in range(REL_BUCKETS - 1):
        bias = jnp.where(bucket == bkt, (relb_ref[bkt, head] - far) * LOG2E, bias)
    return jnp.where(rel >= 0, bias, NEG_BIG)


def _attn_kernel(relb_ref, qt_ref, k_ref, vt_ref, lam_ref, subw_ref, g64_ref,
                 y_ref, bias_scr, qm_scr, m_scr, acc_scr, sa_scr, sb_scr, *, lam_init):
    T = ATT_TILE
    H = DIFF_HEADS
    DV = DIFF_V_DIM
    R = ATT_STEP_TILES
    step_idx = pl.program_id(1)

    @pl.when((pl.program_id(0) == 0) & (step_idx == 0))
    def _():
        rel = lax.broadcasted_iota(jnp.int32, (T, T), 1) - lax.broadcasted_iota(jnp.int32, (T, T), 0)
        for h in range(H):
            bias_scr[0, h] = _t5_bias_tile(rel, relb_ref, h)
            bias_scr[1, h] = _t5_bias_tile(rel + T, relb_ref, h)

    comp_of_row = lax.broadcasted_iota(jnp.int32, (DIFF_WIDTH, T), 0) // DIFF_QK_DIM
    for r in range(R):
        qt = qt_ref[r]
        for hc in range(2 * H):
            qm_scr[r, hc] = jnp.where(comp_of_row == hc, qt, jnp.zeros_like(qt))
    m_scr[...] = jnp.full(m_scr.shape, NEG_BIG, F32)
    acc_scr[...] = jnp.zeros_like(acc_scr)
    ones_rows = jnp.ones((ONES_ROWS, T), BF16)

    def key_tile(j):
        return k_ref[pl.ds(pl.multiple_of(j * T, T), T), :]

    def logits_one(s_ref, kt, r, hc):
        s_ref[hc] = _dot(kt, qm_scr[r, hc])

    def consume_one(s_ref, vt, r, hc, bias_idx):
        h = hc // 2
        s = s_ref[hc]
        if bias_idx is not None:
            s = bias_scr[bias_idx, h] + s
        m_prev = m_scr[r, hc]
        m_next = jnp.maximum(m_prev, jnp.max(s, axis=0, keepdims=True))
        alpha = jnp.exp2(m_prev - m_next)
        p = jnp.exp2(s - m_next).astype(BF16)
        m_scr[r, hc] = m_next
        v_aug = jnp.concatenate([vt[h * DV:(h + 1) * DV, :], ones_rows], axis=0)
        acc_scr[r, hc] = alpha * acc_scr[r, hc] + _dot(v_aug, p)

    LEAD = 2

    def step(s_next, j_next, r_next, s_cur, j_cur, r_cur, bias_idx):
        kt = key_tile(j_next)
        vt = vt_ref[j_cur]
        for hc in range(LEAD):
            logits_one(s_next, kt, r_next, hc)
        for hc in range(2 * H):
            if hc + LEAD < 2 * H:
                logits_one(s_next, kt, r_next, hc + LEAD)
            consume_one(s_cur, vt, r_cur, hc, bias_idx)

    DIAG, NEAR = 0, 1

    def far_pairs(count, cur, oth, r):
        def pair(j):
            step(oth, j + 1, r, cur, j, r, None)
            step(cur, j + 2, r, oth, j + 1, r, None)

        def body(t, carry):
            pair(4 * t)
            pair(4 * t + 2)
            return carry

        lax.fori_loop(0, count // 2, body, 0)

        @pl.when(count % 2 == 1)
        def _():
            pair(2 * (count - 1))

    cur, oth = sa_scr, sb_scr
    kt0 = key_tile(0)
    for hc in range(2 * H):
        logits_one(cur, kt0, 0, hc)
    for r in range(R):
        d = R * step_idx + r
        if r % 2 == 1:
            far_pairs((d - 1) // 2, cur, oth, r)
            step(oth, d, r, cur, d - 1, r, NEAR)
            diag, free = oth, cur
        else:
            far_pairs(jnp.maximum(d // 2 - 1, 0), cur, oth, r)

            def near_tiles(d=d, cur=cur, oth=oth, r=r):
                step(oth, d - 1, r, cur, d - 2, r, None)
                step(cur, d, r, oth, d - 1, r, NEAR)

            if r == 0:
                pl.when(d >= 1)(near_tiles)
            else:
                near_tiles()
            diag, free = cur, oth
        if r + 1 < R:
            step(free, 0, r + 1, diag, d, r, DIAG)
            cur, oth = free, diag
        else:
            vt_last = vt_ref[d]
            for hc in range(2 * H):
                consume_one(diag, vt_last, r, hc, DIAG)

    lam_p = lam_ref[...]
    lam = (jnp.exp(jnp.sum(lam_p[0:1] * lam_p[1:2], axis=1, keepdims=True))
           - jnp.exp(jnp.sum(lam_p[2:3] * lam_p[3:4], axis=1, keepdims=True)) + lam_init)
    for r in range(R):
        def normalised(hc):
            return acc_scr[r, hc, :DV, :] / acc_scr[r, hc, DV:DV + 1, :]

        o_t = jnp.concatenate([normalised(2 * h) - lam * normalised(2 * h + 1) for h in range(H)], axis=0)
        o = o_t.T
        ms = _group_mean_sq(o, g64_ref[...])
        y_ref[r * T:(r + 1) * T, :] = (o * lax.rsqrt(ms + NORM_EPS) * subw_ref[...] * (1.0 - lam_init)).astype(BF16)


def _diff_attn(qt, kn, vt, rel_bias, lambdas, subw_row, g64, lam_init, batch, seq):
    nk = seq // ATT_TILE
    nsteps = nk // ATT_STEP_TILES
    n = batch * seq
    return pl.pallas_call(
        functools.partial(_attn_kernel, lam_init=lam_init),
        grid=(batch, nsteps),
        in_specs=[pl.BlockSpec(memory_space=pltpu.SMEM),
                  pl.BlockSpec((ATT_STEP_TILES, DIFF_WIDTH, ATT_TILE), lambda b, i: (b * nsteps + i, 0, 0)),
                  pl.BlockSpec((seq, DIFF_WIDTH), lambda b, i: (b, 0)),
                  pl.BlockSpec((nk, DIFF_WIDTH, ATT_TILE), lambda b, i: (b, 0, 0)),
                  _const_spec((4, DIFF_QK_DIM)), _const_spec((1, DIFF_WIDTH)),
                  _const_spec((DIFF_WIDTH, DIFF_WIDTH))],
        out_specs=pl.BlockSpec((ATT_STEP_TILES * ATT_TILE, DIFF_WIDTH), lambda b, i: (b * nsteps + i, 0)),
        out_shape=jax.ShapeDtypeStruct((n, DIFF_WIDTH), BF16),
        scratch_shapes=[pltpu.VMEM((2, DIFF_HEADS, ATT_TILE, ATT_TILE), F32),
                        pltpu.VMEM((ATT_STEP_TILES, 2 * DIFF_HEADS, DIFF_WIDTH, ATT_TILE), BF16),
                        pltpu.VMEM((ATT_STEP_TILES, 2 * DIFF_HEADS, 1, ATT_TILE), F32),
                        pltpu.VMEM((ATT_STEP_TILES, 2 * DIFF_HEADS, DIFF_V_DIM + ONES_ROWS, ATT_TILE), F32),
                        pltpu.VMEM((2 * DIFF_HEADS, ATT_TILE, ATT_TILE), F32),
                        pltpu.VMEM((2 * DIFF_HEADS, ATT_TILE, ATT_TILE), F32)],
        compiler_params=_params(("arbitrary", "arbitrary")),
        name="diff_attn",
    )(rel_bias, qt, kn, vt, lambdas, subw_row, g64)


def _untransposed(yt_ref):
    return jnp.concatenate([yt_ref[t].astype(F32).T for t in range(yt_ref.shape[0])], axis=0).astype(BF16)


def _outproj_ffn_kernel(x_ref, ymt_ref, yst_ref, yd_ref, w_hbm, fnw_ref, wg_hbm, wu_hbm, wd_hbm, o_ref,
                        w_ref, wg_ref, wu_ref, wd_ref, wide, wide_sem, tall, tall_sem, *, layer):
    @pl.when(pl.program_id(0) == 0)
    def _():
        def store(r0, chunk):
            w_ref[r0:r0 + chunk.shape[0], :] = chunk.astype(BF16)

        _stage_weight(w_hbm, layer, D_MODEL // 4, tall, tall_sem, store)
        _stage_ffn_weights(layer, wg_hbm, wu_hbm, wd_hbm, wg_ref, wu_ref, wd_ref, wide, wide_sem, tall, tall_sem)

    y = jnp.concatenate([_untransposed(ymt_ref), _untransposed(yst_ref), yd_ref[...]], axis=1)
    x = x_ref[...] + _dot(y, w_ref[...])
    o_ref[...] = _ffn_block(x, fnw_ref, wg_ref, wu_ref, wd_ref)


def _outproj_ffn(x, y_m, y_s, y_d, w_out, ffn_norm_w, wg, wu, wd, layer):
    n = x.shape[0]
    return pl.pallas_call(
        functools.partial(_outproj_ffn_kernel, layer=layer),
        grid=(n // ROW_TILE,),
        in_specs=[_row_spec(D_MODEL),
                  pl.BlockSpec((ROW_TILE // MIX_CHUNK, MLSTM_WIDTH, MIX_CHUNK), lambda i: (i, 0, 0)),
                  pl.BlockSpec((ROW_TILE // MIX_CHUNK, SSM_WIDTH, MIX_CHUNK), lambda i: (i, 0, 0)),
                  _row_spec(DIFF_WIDTH), _HBM, _const_spec((1, D_MODEL)), _HBM, _HBM, _HBM],
        out_specs=_row_spec(D_MODEL),
        out_shape=jax.ShapeDtypeStruct((n, D_MODEL), F32),
        scratch_shapes=[pltpu.VMEM((D_MODEL, D_MODEL), BF16)] + _ffn_weight_scratch(),
        compiler_params=_params(("arbitrary",)),
        name="outproj_ffn",
    )(x, y_m, y_s, y_d, w_out, ffn_norm_w.reshape(1, D_MODEL), wg, wu, wd)


def _small_row(values_by_lane):
    row = jnp.zeros((LANES,), F32)
    for lane, vals in values_by_lane:
        row = lax.dynamic_update_slice(row, vals.astype(F32), (lane,))
    return row.reshape(1, LANES)


def kernel(x, ffn1_norm_w, ffn1_w_gate, ffn1_w_up, ffn1_w_down, mix_norm_w, w_in, mlstm_gate_bias, mlstm_norm_w, ssm_conv_w, ssm_conv_b, ssm_dt_bias, ssm_A_log, ssm_D, ssm_norm_w, diff_q_norm_w, diff_k_norm_w, diff_lambda, diff_subln_w, rel_bias, w_out, ffn2_norm_w, ffn2_w_gate, ffn2_w_up, ffn2_w_down):
    batch, seq, d = x.shape
    assert d == D_MODEL and MIX_CHUNK == ATT_TILE and seq % ROW_TILE == 0 and ROW_TILE % MIX_CHUNK == 0
    assert seq % (MIX_CHUNK * MIX_STEP_CHUNKS) == 0 and seq % (ATT_STEP_TILES * ATT_TILE) == 0 and ATT_STEP_TILES % 2 == 0
    xf = x.reshape(batch * seq, D_MODEL)
    g32 = _block_diag(DIFF_WIDTH, DIFF_QK_DIM, 1.0 / DIFF_QK_DIM)
    g64 = _block_diag(DIFF_WIDTH, DIFF_V_DIM, 1.0 / DIFF_V_DIM)
    ffn1 = (ffn1_w_gate, ffn1_w_up, ffn1_w_down)
    ffn2 = (ffn2_w_gate, ffn2_w_up, ffn2_w_down)
    for l in range(DEPTH):
        qkw_row = jnp.concatenate([jnp.tile(diff_q_norm_w[l].reshape(-1), DIFF_HEADS) * (DIFF_QK_DIM ** -0.5 * LOG2E),
                                   jnp.tile(diff_k_norm_w[l].reshape(-1), DIFF_HEADS)]).reshape(1, 2 * DIFF_WIDTH)
        bias_row = _small_row([(_LANE_I, mlstm_gate_bias[l, 0]), (_LANE_F, mlstm_gate_bias[l, 1]),
                               (_LANE_DT, ssm_dt_bias[l])])
        (xf, mqt, mk, mvt, mot, gt, zt, sb, sct, sxt, sxdt, sa, qt, kn, vt) = _ffn_inproj(
            xf, ffn1_norm_w[l], *ffn1, mix_norm_w[l], w_in, qkw_row, g32,
            jnp.broadcast_to(bias_row.reshape(LANES, 1), (LANES, ROW_TILE)),
            jnp.broadcast_to(ssm_A_log[l][:, None], (SSM_HEADS, ROW_TILE)),
            ssm_conv_w[l], ssm_conv_b[l].reshape(1, SSM_CONV_DIM), l, seq)
        y_m = _mlstm(mqt, mk, mvt, mot, gt, jnp.broadcast_to(mlstm_norm_w[l][:, None], (MLSTM_WIDTH, MIX_CHUNK)),
                     batch, seq)
        y_s = _ssd(sb, sct, sxt, sxdt, zt, sa,
                   jnp.broadcast_to(jnp.repeat(ssm_D[l], SSM_HEAD_DIM)[:, None], (SSM_WIDTH, MIX_CHUNK)),
                   jnp.broadcast_to(ssm_norm_w[l][:, None], (SSM_WIDTH, MIX_CHUNK)), batch, seq)
        lam_init = 0.8 - 0.6 * math.exp(-0.3 * l)
        y_d = _diff_attn(qt, kn, vt, rel_bias, diff_lambda[l], jnp.tile(diff_subln_w[l], DIFF_HEADS).reshape(1, DIFF_WIDTH),
                         g64, lam_init, batch, seq)
        xf = _outproj_ffn(xf, y_m, y_s, y_d, w_out, ffn2_norm_w[l], *ffn2, l)
    return xf.reshape(batch, seq, D_MODEL)
```

```python
import functools
import math

import jax
import jax.numpy as jnp
from jax import lax
from jax.experimental import pallas as pl
from jax.experimental.pallas import tpu as pltpu

F32 = jnp.float32
BF16 = jnp.bfloat16

D_MODEL = 1024
DEPTH = 2
D_FF = 2816
MLSTM_HEADS = 4
MLSTM_HEAD_DIM = 64
MLSTM_WIDTH = MLSTM_HEADS * MLSTM_HEAD_DIM
SSM_HEADS = 8
SSM_HEAD_DIM = 64
SSM_WIDTH = SSM_HEADS * SSM_HEAD_DIM
SSM_STATE = 128
SSM_GROUPS = 2
SSM_CONV = 4
SSM_CONV_DIM = SSM_WIDTH + 2 * SSM_GROUPS * SSM_STATE
DIFF_HEADS = 4
DIFF_QK_DIM = 32
DIFF_V_DIM = 64
DIFF_WIDTH = DIFF_HEADS * DIFF_V_DIM
REL_BUCKETS = 32
REL_MAX_DIST = 128
NORM_EPS = 1e-6

LANES = 128
ROW_TILE = 512
MXU_COLS = 256
FFN_SPLITS = (0, 6 * MXU_COLS, D_FF)
MIX_CHUNK = 256
MIX_STEP_CHUNKS = 8
ATT_TILE = 256
ATT_STEP_TILES = 4
VMEM_LIMIT = 56 * 1024 * 1024
NEG_BIG = -1e30
LOG2E = math.log2(math.e)
ONES_ROWS = 16

_C_M = 0
_C_Z = _C_M + 4 * MLSTM_WIDTH
_C_XBC = _C_Z + SSM_WIDTH
_C_SMALL = _C_XBC + SSM_CONV_DIM
_C_QK = _C_SMALL + LANES
_C_V = _C_QK + 4 * DIFF_HEADS * DIFF_QK_DIM
_C_END = _C_V + DIFF_WIDTH
_LANE_I = 0
_LANE_F = 8
_LANE_DT = 16
GATE_ROWS = 24
D_IN = 4 * MLSTM_WIDTH + 2 * MLSTM_HEADS + SSM_WIDTH + SSM_CONV_DIM + SSM_HEADS + 3 * DIFF_WIDTH


def _regroup_plan():
    sizes = (4 * MLSTM_WIDTH, MLSTM_HEADS, MLSTM_HEADS, SSM_WIDTH, SSM_CONV_DIM, SSM_HEADS, 3 * DIFF_WIDTH)
    dests = (_C_M, _C_SMALL + _LANE_I, _C_SMALL + _LANE_F, _C_Z, _C_XBC, _C_SMALL + _LANE_DT, _C_QK)
    plan, src = [], 0
    for dst, width in zip(dests, sizes):
        plan.append((dst, src, width))
        src += width
    assert src == D_IN
    return plan


def _dot(a, b):
    return jnp.dot(a, b, preferred_element_type=F32)


def _sigmoid(x):
    return 1.0 / (1.0 + jnp.exp(-x))


def _softplus(x):
    return jnp.maximum(x, 0.0) + jnp.log(1.0 + jnp.exp(-jnp.abs(x)))


def _rms_rows(x, w_row):
    ms = jnp.mean(x * x, axis=-1, keepdims=True)
    return x * lax.rsqrt(ms + NORM_EPS) * w_row


def _group_mean_sq(x, gmat):
    sq = x * x
    hi = sq.astype(BF16)
    lo = (sq - hi.astype(F32)).astype(BF16)
    return _dot(hi, gmat) + _dot(lo, gmat)


def _scan_lanes(x, op, identity):
    n = x.shape[1]
    lane = lax.broadcasted_iota(jnp.int32, x.shape, 1)
    step = 1
    while step < n:
        x = op(x, jnp.where(lane >= step, pltpu.roll(x, step, axis=1), identity))
        step *= 2
    return x


def _block_diag(n, group, value):
    r = jnp.arange(n) // group
    return jnp.where(r[:, None] == r[None, :], value, 0.0).astype(BF16)


def _const_spec(shape):
    nd = len(shape)
    return pl.BlockSpec(shape, lambda *_: (0,) * nd)


def _params(sem):
    return pltpu.CompilerParams(dimension_semantics=sem, vmem_limit_bytes=VMEM_LIMIT)


def _ffn_block(x, nw_ref, wg_ref, wu_ref, wd_ref):
    xn = _rms_rows(x, nw_ref[...]).astype(BF16)
    acc = None
    for lo, hi in zip(FFN_SPLITS[:-1], FFN_SPLITS[1:]):
        assert (hi - lo) % MXU_COLS == 0
        cols = slice(lo, hi)
        g = _dot(xn, wg_ref[:, cols])
        u = _dot(xn, wu_ref[:, cols])
        h = (g * _sigmoid(g) * u).astype(BF16)
        part = _dot(h, wd_ref[cols, :])
        acc = part if acc is None else acc + part
    return x + 0.5 * acc


def _row_spec(width):
    return pl.BlockSpec((ROW_TILE, width), lambda i: (i, 0))


WIDE_STAGE_ROWS = 128
TALL_STAGE_ROWS = 352
_HBM = pl.BlockSpec(memory_space=pl.ANY)


def _stage_weight(w_hbm, layer, rows, stage, sem, store):
    n_rows, n_cols = w_hbm.shape[1], w_hbm.shape[2]
    assert n_rows % rows == 0

    def copy(c):
        return pltpu.make_async_copy(w_hbm.at[layer, pl.ds(c * rows, rows), :],
                                     stage.at[c % 2, pl.ds(0, rows), pl.ds(0, n_cols)], sem.at[c % 2])

    copy(0).start()
    for c in range(n_rows // rows):
        if c + 1 < n_rows // rows:
            copy(c + 1).start()
        copy(c).wait()
        store(c * rows, stage[c % 2, :rows, :n_cols])


def _stage_ffn_weights(layer, wg_hbm, wu_hbm, wd_hbm, wg_scr, wu_scr, wd_scr, wide, wide_sem, tall, tall_sem):
    def into(dst):
        def store(r0, chunk):
            dst[r0:r0 + chunk.shape[0], :] = chunk.astype(BF16)
        return store

    _stage_weight(wg_hbm, layer, WIDE_STAGE_ROWS, wide, wide_sem, into(wg_scr))
    _stage_weight(wu_hbm, layer, WIDE_STAGE_ROWS, wide, wide_sem, into(wu_scr))
    _stage_weight(wd_hbm, layer, TALL_STAGE_ROWS, tall, tall_sem, into(wd_scr))


def _ffn_weight_scratch():
    return [pltpu.VMEM((D_MODEL, D_FF), BF16), pltpu.VMEM((D_MODEL, D_FF), BF16), pltpu.VMEM((D_FF, D_MODEL), BF16),
            pltpu.VMEM((2, WIDE_STAGE_ROWS, D_IN), F32), pltpu.SemaphoreType.DMA((2,)),
            pltpu.VMEM((2, TALL_STAGE_ROWS, D_MODEL), F32), pltpu.SemaphoreType.DMA((2,))]


def _store_transposed_tiles(out_ref, a):
    a_t = a.T.astype(BF16)
    for t in range(ROW_TILE // ATT_TILE):
        out_ref[t] = a_t[:, t * ATT_TILE:(t + 1) * ATT_TILE]


def _ffn_inproj_kernel(x_ref, fnw_ref, wg_hbm, wu_hbm, wd_hbm, nw_ref, w_hbm, qkw_ref, g32_ref, gbias_ref,
                       alog_ref, convw_ref, convb_ref,
                       x_out_ref, mqt_ref, mk_ref, mvt_ref, mot_ref, gt_ref,
                       zt_ref, sb_ref, sct_ref, sxt_ref, sxdt_ref, sa_ref, qt_ref, k_ref, vt_ref,
                       halo_scr, w_scr, wg_ref, wu_ref, wd_ref, wide, wide_sem, tall, tall_sem,
                       *, tiles_per_seq, layer):
    @pl.when(pl.program_id(0) == 0)
    def _():
        _stage_ffn_weights(layer, wg_hbm, wu_hbm, wd_hbm, wg_ref, wu_ref, wd_ref, wide, wide_sem, tall, tall_sem)
        w_scr[:, _C_SMALL:_C_QK] = jnp.zeros((D_MODEL, LANES), BF16)

        def regrouped(r0, chunk):
            for dst, src, width in _regroup_plan():
                w_scr[r0:r0 + chunk.shape[0], dst:dst + width] = chunk[:, src:src + width].astype(BF16)

        _stage_weight(w_hbm, layer, WIDE_STAGE_ROWS, wide, wide_sem, regrouped)

    @pl.when(pl.program_id(0) % tiles_per_seq == 0)
    def _():
        halo_scr[...] = jnp.zeros_like(halo_scr)

    x = _ffn_block(x_ref[...], fnw_ref, wg_ref, wu_ref, wd_ref)
    x_out_ref[...] = x
    xn = _rms_rows(x, nw_ref[...]).astype(BF16)

    def proj(lo, hi):
        return _dot(xn, w_scr[:, lo:hi])

    half = (_C_V - _C_QK) // 2
    p_xbc = proj(_C_XBC, _C_SMALL)
    p_small = proj(_C_SMALL, _C_QK)
    p_q, p_k = proj(_C_QK, _C_QK + half), proj(_C_QK + half, _C_V)
    p_mq = proj(_C_M, _C_M + MLSTM_WIDTH)
    p_mk = proj(_C_M + MLSTM_WIDTH, _C_M + 2 * MLSTM_WIDTH)
    p_mv = proj(_C_M + 2 * MLSTM_WIDTH, _C_M + 3 * MLSTM_WIDTH)
    p_mo = proj(_C_M + 3 * MLSTM_WIDTH, _C_Z)
    p_z = proj(_C_Z, _C_XBC)
    p_v = proj(_C_V, _C_END)

    qk = [a * lax.rsqrt(_group_mean_sq(a, g32_ref[...]) + NORM_EPS) * qkw_ref[:, i * half:(i + 1) * half]
          for i, a in enumerate((p_q, p_k))]
    _store_transposed_tiles(qt_ref, qk[0])
    k_ref[...] = qk[1].astype(BF16)
    _store_transposed_tiles(vt_ref, p_v)

    ext = jnp.concatenate([halo_scr[...], p_xbc], axis=0)
    halo_scr[...] = p_xbc[ROW_TILE - 8:, :]
    conv = convb_ref[...] + convw_ref[SSM_CONV - 1:SSM_CONV, :] * p_xbc
    for j in range(1, SSM_CONV):
        conv = conv + convw_ref[SSM_CONV - 1 - j:SSM_CONV - j, :] * ext[8 - j:8 - j + ROW_TILE, :]
    xa = conv * _sigmoid(conv)
    sb_ref[...] = xa[:, SSM_WIDTH:SSM_WIDTH + SSM_GROUPS * SSM_STATE].astype(BF16)
    _store_transposed_tiles(sct_ref, xa[:, SSM_WIDTH + SSM_GROUPS * SSM_STATE:])
    small_t = p_small.T + gbias_ref[...]
    dt = _softplus(small_t[_LANE_DT:_LANE_DT + SSM_HEADS])
    log_decay = dt * (-jnp.exp(alog_ref[...]))
    xs_t = xa[:, :SSM_WIDTH].T
    xdt_t = xs_t * jnp.concatenate([jnp.broadcast_to(dt[h:h + 1], (SSM_HEAD_DIM, ROW_TILE))
                                    for h in range(SSM_HEADS)], axis=0)
    for t in range(ROW_TILE // MIX_CHUNK):
        cols = slice(t * MIX_CHUNK, (t + 1) * MIX_CHUNK)
        sxt_ref[t] = xs_t[:, cols].astype(BF16)
        sxdt_ref[t] = xdt_t[:, cols].astype(BF16)
        sa_ref[t] = _scan_lanes(log_decay[:, cols], jnp.add, 0.0)
    _store_transposed_tiles(zt_ref, p_z)

    _store_transposed_tiles(mqt_ref, p_mq)
    mk_ref[...] = p_mk.astype(BF16)
    _store_transposed_tiles(mvt_ref, p_mv)
    _store_transposed_tiles(mot_ref, p_mo)
    gates = small_t[:_LANE_DT]
    for t in range(ROW_TILE // MIX_CHUNK):
        i_pre = gates[_LANE_I:_LANE_I + 8, t * MIX_CHUNK:(t + 1) * MIX_CHUNK]
        f_pre = gates[_LANE_F:_LANE_F + 8, t * MIX_CHUNK:(t + 1) * MIX_CHUNK]
        log_f = jnp.minimum(f_pre, 0.0) - jnp.log(1.0 + jnp.exp(-jnp.abs(f_pre)))
        b = _scan_lanes(log_f, jnp.add, 0.0)
        c = i_pre - b
        gt_ref[t] = jnp.concatenate([c, _scan_lanes(c, jnp.maximum, -jnp.inf), b], axis=0)


def _ffn_inproj(x, ffn_norm_w, wg, wu, wd, norm_w, w_all, qkw_row, g32, gate_bias_rows, alog_rows, conv_w, conv_b_row,
                layer, seq):
    n = x.shape[0]
    tiles = ROW_TILE // ATT_TILE

    def t_out(width, dtype=BF16):
        return (pl.BlockSpec((tiles, width, ATT_TILE), lambda i: (i, 0, 0)),
                jax.ShapeDtypeStruct((n // ATT_TILE, width, ATT_TILE), dtype))

    def r_out(width, dtype=BF16):
        return _row_spec(width), jax.ShapeDtypeStruct((n, width), dtype)

    outs = [r_out(D_MODEL, F32),
            t_out(MLSTM_WIDTH), r_out(MLSTM_WIDTH), t_out(MLSTM_WIDTH), t_out(MLSTM_WIDTH), t_out(GATE_ROWS, F32),
            t_out(SSM_WIDTH), r_out(SSM_GROUPS * SSM_STATE), t_out(SSM_GROUPS * SSM_STATE), t_out(SSM_WIDTH),
            t_out(SSM_WIDTH), t_out(SSM_HEADS, F32),
            t_out(DIFF_WIDTH), r_out(DIFF_WIDTH), t_out(DIFF_WIDTH)]
    return pl.pallas_call(
        functools.partial(_ffn_inproj_kernel, tiles_per_seq=seq // ROW_TILE, layer=layer),
        grid=(n // ROW_TILE,),
        in_specs=[_row_spec(D_MODEL), _const_spec((1, D_MODEL)), _HBM, _HBM, _HBM, _const_spec((1, D_MODEL)), _HBM,
                  _const_spec((1, 2 * DIFF_WIDTH)), _const_spec((DIFF_WIDTH, DIFF_WIDTH)),
                  _const_spec((LANES, ROW_TILE)), _const_spec((SSM_HEADS, ROW_TILE)),
                  _const_spec((SSM_CONV, SSM_CONV_DIM)), _const_spec((1, SSM_CONV_DIM))],
        out_specs=[o[0] for o in outs],
        out_shape=[o[1] for o in outs],
        scratch_shapes=[pltpu.VMEM((8, SSM_CONV_DIM), F32), pltpu.VMEM((D_MODEL, _C_END), BF16)]
        + _ffn_weight_scratch(),
        compiler_params=_params(("arbitrary",)),
        name="ffn_inproj",
    )(x, ffn_norm_w.reshape(1, D_MODEL), wg, wu, wd, norm_w.reshape(1, D_MODEL), w_all, qkw_row, g32,
      gate_bias_rows, alog_rows, conv_w, conv_b_row)


def _mlstm_kernel(qt_ref, k_ref, vt_ref, ot_ref, gt_ref, nw_ref, yt_ref, ct_scr, nt_scr, m_scr):
    @pl.when(pl.program_id(1) == 0)
    def _():
        ct_scr[...] = jnp.zeros_like(ct_scr)
        nt_scr[...] = jnp.zeros_like(nt_scr)
        m_scr[...] = jnp.zeros_like(m_scr)

    for t in range(MIX_STEP_CHUNKS):
        _mlstm_chunk(t, qt_ref, k_ref, vt_ref, ot_ref, gt_ref, nw_ref, yt_ref, ct_scr, nt_scr, m_scr)


def _mlstm_chunk(t, qt_ref, k_ref, vt_ref, ot_ref, gt_ref, nw_ref, yt_ref, ct_scr, nt_scr, m_scr):
    L = MIX_CHUNK
    H = MLSTM_HEADS
    W = MLSTM_HEAD_DIM
    qt = qt_ref[t]
    ks = k_ref[t * L:(t + 1) * L, :] * (MLSTM_HEAD_DIM ** -0.5)
    vt = vt_ref[t]
    gt = gt_ref[t]
    c, c_max, b = gt[0:8], gt[8:16], gt[16:24]
    m_prev = m_scr[...]
    big_m = jnp.maximum(m_prev, c_max)
    inter = jnp.exp(m_prev - big_m)
    floor = jnp.exp(-(b + big_m))
    b_last = jnp.broadcast_to(b[:, L - 1:L], b.shape)
    m_new = b_last + jnp.maximum(m_prev, jnp.broadcast_to(c_max[:, L - 1:L], b.shape))
    w = jnp.exp(b_last + c - m_new)
    decay = jnp.exp(b_last + m_prev - m_new)

    c2_cols = jnp.concatenate([c * LOG2E, jnp.zeros((LANES - 8, L), F32)], axis=0).T
    big_m2 = big_m * LOG2E
    causal = (lax.broadcasted_iota(jnp.int32, (L, L), 0) <= lax.broadcasted_iota(jnp.int32, (L, L), 1))
    row_head = lax.broadcasted_iota(jnp.int32, (H * W, L), 0) // W
    qk = [_dot(ks, jnp.where(row_head == h, qt, jnp.zeros_like(qt))) for h in range(H)]
    nums, dens = [], []
    for h in range(H):
        d = jnp.exp2(jnp.where(causal, c2_cols[:, h:h + 1] - big_m2[h:h + 1, :], -jnp.inf))
        s = qk[h] * d
        dens.append(jnp.sum(s, axis=0, keepdims=True))
        nums.append(_dot(vt[h * W:(h + 1) * W, :], s.astype(BF16)))
    cq = _dot(ct_scr[...].astype(BF16), qt)
    nq = _dot(nt_scr[...].astype(BF16), qt)
    outs = []
    for h in range(H):
        num = nums[h] + inter[h:h + 1] * cq[h * W:(h + 1) * W]
        den = dens[h] + inter[h:h + 1] * nq[h:h + 1]
        hh = num / jnp.maximum(jnp.abs(den), floor[h:h + 1])
        outs.append(hh * lax.rsqrt(jnp.mean(hh * hh, axis=0, keepdims=True) + NORM_EPS))
    hn = jnp.concatenate(outs, axis=0) * nw_ref[...]
    yt_ref[t] = (_sigmoid(ot_ref[t].astype(F32)) * hn).astype(BF16)

    w_full = jnp.concatenate([jnp.broadcast_to(w[h:h + 1], (W, L)) for h in range(H)], axis=0)
    g_new = _dot((vt.astype(F32) * w_full).astype(BF16), ks)
    decay_full = jnp.concatenate([jnp.broadcast_to(decay[h:h + 1, :1], (W, H * W)) for h in range(H)], axis=0)
    same_head = (lax.broadcasted_iota(jnp.int32, (H * W, H * W), 0) // W
                 == lax.broadcasted_iota(jnp.int32, (H * W, H * W), 1) // W)
    ct_scr[...] = decay_full * ct_scr[...] + jnp.where(same_head, g_new, 0.0)
    w16 = jnp.concatenate([w, jnp.zeros_like(w)], axis=0).astype(BF16)
    n_new = _dot(w16, ks)
    own_lanes = (lax.broadcasted_iota(jnp.int32, (16, H * W), 0)
                 == lax.broadcasted_iota(jnp.int32, (16, H * W), 1) // W)
    decay16 = jnp.broadcast_to(jnp.concatenate([decay[:, :1], jnp.zeros((8, 1), F32)], axis=0), (16, H * W))
    nt_scr[...] = decay16 * nt_scr[...] + jnp.where(own_lanes, n_new, 0.0)
    m_scr[...] = m_new


def _mlstm(mqt, mk, mvt, mot, gt, norm_w_rows, batch, seq):
    ns = seq // (MIX_CHUNK * MIX_STEP_CHUNKS)
    n = batch * seq
    t_spec = pl.BlockSpec((MIX_STEP_CHUNKS, MLSTM_WIDTH, MIX_CHUNK), lambda b, c: (b * ns + c, 0, 0))
    return pl.pallas_call(
        _mlstm_kernel,
        grid=(batch, ns),
        in_specs=[t_spec, pl.BlockSpec((MIX_STEP_CHUNKS * MIX_CHUNK, MLSTM_WIDTH), lambda b, c: (b * ns + c, 0)),
                  t_spec, t_spec,
                  pl.BlockSpec((MIX_STEP_CHUNKS, GATE_ROWS, MIX_CHUNK), lambda b, c: (b * ns + c, 0, 0)),
                  _const_spec((MLSTM_WIDTH, MIX_CHUNK))],
        out_specs=t_spec,
        out_shape=jax.ShapeDtypeStruct((n // MIX_CHUNK, MLSTM_WIDTH, MIX_CHUNK), BF16),
        scratch_shapes=[pltpu.VMEM((MLSTM_WIDTH, MLSTM_WIDTH), F32),
                        pltpu.VMEM((16, MLSTM_WIDTH), F32),
                        pltpu.VMEM((8, MIX_CHUNK), F32)],
        compiler_params=_params(("arbitrary", "arbitrary")),
        name="mlstm",
    )(mqt, mk, mvt, mot, gt, norm_w_rows)


def _ssd_kernel(b_ref, ct_ref, xt_ref, xdt_ref, zt_ref, a_ref, dskip_ref, nw_ref, yt_ref, st_scr):
    @pl.when(pl.program_id(1) == 0)
    def _():
        st_scr[...] = jnp.zeros_like(st_scr)

    for t in range(MIX_STEP_CHUNKS):
        _ssd_chunk(t, b_ref, ct_ref, xt_ref, xdt_ref, zt_ref, a_ref, dskip_ref, nw_ref, yt_ref, st_scr)


def _ssd_chunk(t, b_ref, ct_ref, xt_ref, xdt_ref, zt_ref, a_ref, dskip_ref, nw_ref, yt_ref, st_scr):
    L = MIX_CHUNK
    P = SSM_HEAD_DIM
    HG = SSM_HEADS // SSM_GROUPS
    GW = HG * P
    NS = SSM_STATE
    a = a_ref[t]
    a_last = jnp.broadcast_to(a[:, L - 1:L], a.shape)
    exp_a = jnp.exp(a)
    d_in = jnp.exp(a_last - a)
    chunk_decay = jnp.exp(a_last)
    a2 = a * LOG2E
    a2_cols = jnp.concatenate([a2, jnp.zeros((LANES - SSM_HEADS, L), F32)], axis=0).T
    causal = (lax.broadcasted_iota(jnp.int32, (L, L), 0) <= lax.broadcasted_iota(jnp.int32, (L, L), 1))
    bn = b_ref[t * L:(t + 1) * L, :]
    ct = ct_ref[t]
    xdt = xdt_ref[t]

    def per_head_rows(rows, grp, width):
        return jnp.concatenate([jnp.broadcast_to(rows[grp * HG + i:grp * HG + i + 1, :width], (P, width))
                                for i in range(HG)], axis=0)

    scores, y_off = [], []
    for grp in range(SSM_GROUPS):
        bg = bn[:, grp * NS:(grp + 1) * NS]
        cg = ct[grp * NS:(grp + 1) * NS, :]
        scores.append(_dot(bg, cg))
        st = st_scr[grp]
        y_off.append(_dot(st.astype(BF16), cg))
        x_in = (xdt[grp * GW:(grp + 1) * GW, :].astype(F32) * per_head_rows(d_in, grp, L)).astype(BF16)
        st_scr[grp] = per_head_rows(chunk_decay, grp, NS) * st + _dot(x_in, bg)
    for grp in range(SSM_GROUPS):
        y_diag = []
        for i in range(HG):
            h = grp * HG + i
            seg = jnp.exp2(jnp.where(causal, a2[h:h + 1, :] - a2_cols[:, h:h + 1], -jnp.inf))
            p = (scores[grp] * seg).astype(BF16)
            y_diag.append(_dot(xdt[h * P:(h + 1) * P, :], p))
        rows = slice(grp * GW, (grp + 1) * GW)
        y = (jnp.concatenate(y_diag, axis=0) + y_off[grp] * per_head_rows(exp_a, grp, L)
             + dskip_ref[rows, :] * xt_ref[t, rows, :].astype(F32))
        zg = zt_ref[t, rows, :].astype(F32)
        y = y * (zg * _sigmoid(zg))
        ms = jnp.mean(y * y, axis=0, keepdims=True)
        yt_ref[t, rows, :] = (y * lax.rsqrt(ms + NORM_EPS) * nw_ref[rows, :]).astype(BF16)


def _ssd(sb, sct, sxt, sxdt, zt, sa, dskip_rows, norm_w_rows, batch, seq):
    ns = seq // (MIX_CHUNK * MIX_STEP_CHUNKS)
    n = batch * seq

    def t_spec(width):
        return pl.BlockSpec((MIX_STEP_CHUNKS, width, MIX_CHUNK), lambda b, c: (b * ns + c, 0, 0))

    return pl.pallas_call(
        _ssd_kernel,
        grid=(batch, ns),
        in_specs=[pl.BlockSpec((MIX_STEP_CHUNKS * MIX_CHUNK, SSM_GROUPS * SSM_STATE), lambda b, c: (b * ns + c, 0)),
                  t_spec(SSM_GROUPS * SSM_STATE), t_spec(SSM_WIDTH), t_spec(SSM_WIDTH), t_spec(SSM_WIDTH),
                  t_spec(SSM_HEADS), _const_spec((SSM_WIDTH, MIX_CHUNK)), _const_spec((SSM_WIDTH, MIX_CHUNK))],
        out_specs=t_spec(SSM_WIDTH),
        out_shape=jax.ShapeDtypeStruct((n // MIX_CHUNK, SSM_WIDTH, MIX_CHUNK), BF16),
        scratch_shapes=[pltpu.VMEM((SSM_GROUPS, SSM_WIDTH // SSM_GROUPS, SSM_STATE), F32)],
        compiler_params=_params(("arbitrary", "arbitrary")),
        name="ssd",
    )(sb, sct, sxt, sxdt, zt, sa, dskip_rows, norm_w_rows)


def _t5_bias_tile(rel, relb_ref, head):
    n = jnp.maximum(rel, 0)
    max_exact = REL_BUCKETS // 2
    nf = jnp.maximum(n, 1).astype(F32)
    large = max_exact + (jnp.log(nf / max_exact) / math.log(REL_MAX_DIST / max_exact)
                         * (REL_BUCKETS - max_exact)).astype(jnp.int32)
    large = jnp.minimum(large, REL_BUCKETS - 1)
    bucket = jnp.where(n < max_exact, n, large)
    far = relb_ref[REL_BUCKETS - 1, head]
    bias = jnp.zeros(rel.shape, F32)
    for bkt in range(REL_BUCKETS - 1):
        bias = jnp.where(bucket == bkt, (relb_ref[bkt, head] - far) * LOG2E, bias)
    return jnp.where(rel >= 0, bias, NEG_BIG)


def _attn_kernel(relb_ref, qt_ref, k_ref, vt_ref, lam_ref, subw_ref, g64_ref,
                 y_ref, bias_scr, qm_scr, m_scr, acc_scr, sa_scr, sb_scr, *, lam_init):
    T = ATT_TILE
    H = DIFF_HEADS
    DV = DIFF_V_DIM
    R = ATT_STEP_TILES
    step_idx = pl.program_id(1)

    @pl.when((pl.program_id(0) == 0) & (step_idx == 0))
    def _():
        rel = lax.broadcasted_iota(jnp.int32, (T, T), 1) - lax.broadcasted_iota(jnp.int32, (T, T), 0)
        for h in range(H):
            bias_scr[0, h] = _t5_bias_tile(rel, relb_ref, h)
            bias_scr[1, h] = _t5_bias_tile(rel + T, relb_ref, h)

    comp_of_row = lax.broadcasted_iota(jnp.int32, (DIFF_WIDTH, T), 0) // DIFF_QK_DIM
    for r in range(R):
        qt = qt_ref[r]
        for hc in range(2 * H):
            qm_scr[r, hc] = jnp.where(comp_of_row == hc, qt, jnp.zeros_like(qt))
    m_scr[...] = jnp.full(m_scr.shape, NEG_BIG, F32)
    acc_scr[...] = jnp.zeros_like(acc_scr)
    ones_rows = jnp.ones((ONES_ROWS, T), BF16)

    def key_tile(j):
        return k_ref[pl.ds(pl.multiple_of(j * T, T), T), :]

    def logits_one(s_ref, kt, r, hc):
        s_ref[hc] = _dot(kt, qm_scr[r, hc])

    def consume_one(s_ref, vt, r, hc, bias_idx):
        h = hc // 2
        s = s_ref[hc]
        if bias_idx is not None:
            s = bias_scr[bias_idx, h] + s
        s = s.astype(BF16)
        m_prev = m_scr[r, hc]
        m_next = jnp.maximum(m_prev, jnp.max(s, axis=0, keepdims=True).astype(F32))
        alpha = jnp.exp2(m_prev - m_next)
        p = jnp.exp2(s - m_next.astype(BF16))
        m_scr[r, hc] = m_next
        v_aug = jnp.concatenate([vt[h * DV:(h + 1) * DV, :], ones_rows], axis=0)
        acc_scr[r, hc] = alpha * acc_scr[r, hc] + _dot(v_aug, p)

    LEAD = 2

    def step(s_next, j_next, r_next, s_cur, j_cur, r_cur, bias_idx):
        kt = key_tile(j_next)
        vt = vt_ref[j_cur]
        for hc in range(LEAD):
            logits_one(s_next, kt, r_next, hc)
        for hc in range(2 * H):
            if hc + LEAD < 2 * H:
                logits_one(s_next, kt, r_next, hc + LEAD)
            consume_one(s_cur, vt, r_cur, hc, bias_idx)

    DIAG, NEAR = 0, 1

    def far_pairs(count, cur, oth, r):
        def pair(j):
            step(oth, j + 1, r, cur, j, r, None)
            step(cur, j + 2, r, oth, j + 1, r, None)

        def body(t, carry):
            pair(4 * t)
            pair(4 * t + 2)
            return carry

        lax.fori_loop(0, count // 2, body, 0)

        @pl.when(count % 2 == 1)
        def _():
            pair(2 * (count - 1))

    cur, oth = sa_scr, sb_scr
    kt0 = key_tile(0)
    for hc in range(2 * H):
        logits_one(cur, kt0, 0, hc)
    for r in range(R):
        d = R * step_idx + r
        if r % 2 == 1:
            far_pairs((d - 1) // 2, cur, oth, r)
            step(oth, d, r, cur, d - 1, r, NEAR)
            diag, free = oth, cur
        else:
            far_pairs(jnp.maximum(d // 2 - 1, 0), cur, oth, r)

            def near_tiles(d=d, cur=cur, oth=oth, r=r):
                step(oth, d - 1, r, cur, d - 2, r, None)
                step(cur, d, r, oth, d - 1, r, NEAR)

            if r == 0:
                pl.when(d >= 1)(near_tiles)
            else:
                near_tiles()
            diag, free = cur, oth
        if r + 1 < R:
            step(free, 0, r + 1, diag, d, r, DIAG)
            cur, oth = free, diag
        else:
            vt_last = vt_ref[d]
            for hc in range(2 * H):
                consume_one(diag, vt_last, r, hc, DIAG)

    lam_p = lam_ref[...]
    lam = (jnp.exp(jnp.sum(lam_p[0:1] * lam_p[1:2], axis=1, keepdims=True))
           - jnp.exp(jnp.sum(lam_p[2:3] * lam_p[3:4], axis=1, keepdims=True)) + lam_init)
    for r in range(R):
        def normalised(hc):
            return acc_scr[r, hc, :DV, :] / acc_scr[r, hc, DV:DV + 1, :]

        o_t = jnp.concatenate([normalised(2 * h) - lam * normalised(2 * h + 1) for h in range(H)], axis=0)
        o = o_t.T
        ms = _group_mean_sq(o, g64_ref[...])
        y_ref[r * T:(r + 1) * T, :] = (o * lax.rsqrt(ms + NORM_EPS) * subw_ref[...] * (1.0 - lam_init)).astype(BF16)


def _diff_attn(qt, kn, vt, rel_bias, lambdas, subw_row, g64, lam_init, batch, seq):
    nk = seq // ATT_TILE
    nsteps = nk // ATT_STEP_TILES
    n = batch * seq
    return pl.pallas_call(
        functools.partial(_attn_kernel, lam_init=lam_init),
        grid=(batch, nsteps),
        in_specs=[pl.BlockSpec(memory_space=pltpu.SMEM),
                  pl.BlockSpec((ATT_STEP_TILES, DIFF_WIDTH, ATT_TILE), lambda b, i: (b * nsteps + i, 0, 0)),
                  pl.BlockSpec((seq, DIFF_WIDTH), lambda b, i: (b, 0)),
                  pl.BlockSpec((nk, DIFF_WIDTH, ATT_TILE), lambda b, i: (b, 0, 0)),
                  _const_spec((4, DIFF_QK_DIM)), _const_spec((1, DIFF_WIDTH)),
                  _const_spec((DIFF_WIDTH, DIFF_WIDTH))],
        out_specs=pl.BlockSpec((ATT_STEP_TILES * ATT_TILE, DIFF_WIDTH), lambda b, i: (b * nsteps + i, 0)),
        out_shape=jax.ShapeDtypeStruct((n, DIFF_WIDTH), BF16),
        scratch_shapes=[pltpu.VMEM((2, DIFF_HEADS, ATT_TILE, ATT_TILE), F32),
                        pltpu.VMEM((ATT_STEP_TILES, 2 * DIFF_HEADS, DIFF_WIDTH, ATT_TILE), BF16),
                        pltpu.VMEM((ATT_STEP_TILES, 2 * DIFF_HEADS, 1, ATT_TILE), F32),
                        pltpu.VMEM((ATT_STEP_TILES, 2 * DIFF_HEADS, DIFF_V_DIM + ONES_ROWS, ATT_TILE), F32),
                        pltpu.VMEM((2 * DIFF_HEADS, ATT_TILE, ATT_TILE), F32),
                        pltpu.VMEM((2 * DIFF_HEADS, ATT_TILE, ATT_TILE), F32)],
        compiler_params=_params(("arbitrary", "arbitrary")),
        name="diff_attn",
    )(rel_bias, qt, kn, vt, lambdas, subw_row, g64)


def _untransposed(yt_ref):
    return jnp.concatenate([yt_ref[t].astype(F32).T for t in range(yt_ref.shape[0])], axis=0).astype(BF16)


def _outproj_ffn_kernel(x_ref, ymt_ref, yst_ref, yd_ref, w_hbm, fnw_ref, wg_hbm, wu_hbm, wd_hbm, o_ref,
                        w_ref, wg_ref, wu_ref, wd_ref, wide, wide_sem, tall, tall_sem, *, layer):
    @pl.when(pl.program_id(0) == 0)
    def _():
        def store(r0, chunk):
            w_ref[r0:r0 + chunk.shape[0], :] = chunk.astype(BF16)

        _stage_weight(w_hbm, layer, D_MODEL // 4, tall, tall_sem, store)
        _stage_ffn_weights(layer, wg_hbm, wu_hbm, wd_hbm, wg_ref, wu_ref, wd_ref, wide, wide_sem, tall, tall_sem)

    y = jnp.concatenate([_untransposed(ymt_ref), _untransposed(yst_ref), yd_ref[...]], axis=1)
    x = x_ref[...] + _dot(y, w_ref[...])
    o_ref[...] = _ffn_block(x, fnw_ref, wg_ref, wu_ref, wd_ref)


def _outproj_ffn(x, y_m, y_s, y_d, w_out, ffn_norm_w, wg, wu, wd, layer):
    n = x.shape[0]
    return pl.pallas_call(
        functools.partial(_outproj_ffn_kernel, layer=layer),
        grid=(n // ROW_TILE,),
        in_specs=[_row_spec(D_MODEL),
                  pl.BlockSpec((ROW_TILE // MIX_CHUNK, MLSTM_WIDTH, MIX_CHUNK), lambda i: (i, 0, 0)),
                  pl.BlockSpec((ROW_TILE // MIX_CHUNK, SSM_WIDTH, MIX_CHUNK), lambda i: (i, 0, 0)),
                  _row_spec(DIFF_WIDTH), _HBM, _const_spec((1, D_MODEL)), _HBM, _HBM, _HBM],
        out_specs=_row_spec(D_MODEL),
        out_shape=jax.ShapeDtypeStruct((n, D_MODEL), F32),
        scratch_shapes=[pltpu.VMEM((D_MODEL, D_MODEL), BF16)] + _ffn_weight_scratch(),
        compiler_params=_params(("arbitrary",)),
        name="outproj_ffn",
    )(x, y_m, y_s, y_d, w_out, ffn_norm_w.reshape(1, D_MODEL), wg, wu, wd)


def _small_row(values_by_lane):
    row = jnp.zeros((LANES,), F32)
    for lane, vals in values_by_lane:
        row = lax.dynamic_update_slice(row, vals.astype(F32), (lane,))
    return row.reshape(1, LANES)


def kernel(x, ffn1_norm_w, ffn1_w_gate, ffn1_w_up, ffn1_w_down, mix_norm_w, w_in, mlstm_gate_bias, mlstm_norm_w, ssm_conv_w, ssm_conv_b, ssm_dt_bias, ssm_A_log, ssm_D, ssm_norm_w, diff_q_norm_w, diff_k_norm_w, diff_lambda, diff_subln_w, rel_bias, w_out, ffn2_norm_w, ffn2_w_gate, ffn2_w_up, ffn2_w_down):
    batch, seq, d = x.shape
    assert d == D_MODEL and MIX_CHUNK == ATT_TILE and seq % ROW_TILE == 0 and ROW_TILE % MIX_CHUNK == 0
    assert seq % (MIX_CHUNK * MIX_STEP_CHUNKS) == 0 and seq % (ATT_STEP_TILES * ATT_TILE) == 0 and ATT_STEP_TILES % 2 == 0
    xf = x.reshape(batch * seq, D_MODEL)
    g32 = _block_diag(DIFF_WIDTH, DIFF_QK_DIM, 1.0 / DIFF_QK_DIM)
    g64 = _block_diag(DIFF_WIDTH, DIFF_V_DIM, 1.0 / DIFF_V_DIM)
    ffn1 = (ffn1_w_gate, ffn1_w_up, ffn1_w_down)
    ffn2 = (ffn2_w_gate, ffn2_w_up, ffn2_w_down)
    for l in range(DEPTH):
        qkw_row = jnp.concatenate([jnp.tile(diff_q_norm_w[l].reshape(-1), DIFF_HEADS) * (DIFF_QK_DIM ** -0.5 * LOG2E),
                                   jnp.tile(diff_k_norm_w[l].reshape(-1), DIFF_HEADS)]).reshape(1, 2 * DIFF_WIDTH)
        bias_row = _small_row([(_LANE_I, mlstm_gate_bias[l, 0]), (_LANE_F, mlstm_gate_bias[l, 1]),
                               (_LANE_DT, ssm_dt_bias[l])])
        (xf, mqt, mk, mvt, mot, gt, zt, sb, sct, sxt, sxdt, sa, qt, kn, vt) = _ffn_inproj(
            xf, ffn1_norm_w[l], *ffn1, mix_norm_w[l], w_in, qkw_row, g32,
            jnp.broadcast_to(bias_row.reshape(LANES, 1), (LANES, ROW_TILE)),
            jnp.broadcast_to(ssm_A_log[l][:, None], (SSM_HEADS, ROW_TILE)),
            ssm_conv_w[l], ssm_conv_b[l].reshape(1, SSM_CONV_DIM), l, seq)
        y_m = _mlstm(mqt, mk, mvt, mot, gt, jnp.broadcast_to(mlstm_norm_w[l][:, None], (MLSTM_WIDTH, MIX_CHUNK)),
                     batch, seq)
        y_s = _ssd(sb, sct, sxt, sxdt, zt, sa,
                   jnp.broadcast_to(jnp.repeat(ssm_D[l], SSM_HEAD_DIM)[:, None], (SSM_WIDTH, MIX_CHUNK)),
                   jnp.broadcast_to(ssm_norm_w[l][:, None], (SSM_WIDTH, MIX_CHUNK)), batch, seq)
        lam_init = 0.8 - 0.6 * math.exp(-0.3 * l)
        y_d = _diff_attn(qt, kn, vt, rel_bias, diff_lambda[l], jnp.tile(diff_subln_w[l], DIFF_HEADS).reshape(1, DIFF_WIDTH),
                         g64, lam_init, batch, seq)
        xf = _outproj_ffn(xf, y_m, y_s, y_d, w_out, ffn2_norm_w[l], *ffn2, l)
    return xf.reshape(batch, seq, D_MODEL)
```

```python
import functools
import math

import jax
import jax.numpy as jnp
from jax import lax
from jax.experimental import pallas as pl
from jax.experimental.pallas import tpu as pltpu

F32 = jnp.float32
BF16 = jnp.bfloat16

D_MODEL = 1024
DEPTH = 2
D_FF = 2816
MLSTM_HEADS = 4
MLSTM_HEAD_DIM = 64
MLSTM_WIDTH = MLSTM_HEADS * MLSTM_HEAD_DIM
SSM_HEADS = 8
SSM_HEAD_DIM = 64
SSM_WIDTH = SSM_HEADS * SSM_HEAD_DIM
SSM_STATE = 128
SSM_GROUPS = 2
SSM_CONV = 4
SSM_CONV_DIM = SSM_WIDTH + 2 * SSM_GROUPS * SSM_STATE
DIFF_HEADS = 4
DIFF_QK_DIM = 32
DIFF_V_DIM = 64
DIFF_WIDTH = DIFF_HEADS * DIFF_V_DIM
REL_BUCKETS = 32
REL_MAX_DIST = 128
NORM_EPS = 1e-6

LANES = 128
ROW_TILE = 512
MXU_COLS = 256
FFN_SPLITS = (0, 6 * MXU_COLS, D_FF)
MIX_CHUNK = 256
MIX_STEP_CHUNKS = 8
ATT_TILE = 256
ATT_STEP_TILES = 4
VMEM_LIMIT = 56 * 1024 * 1024
NEG_BIG = -1e30
LOG2E = math.log2(math.e)
ONES_ROWS = 16

_C_M = 0
_C_Z = _C_M + 4 * MLSTM_WIDTH
_C_XBC = _C_Z + SSM_WIDTH
_C_SMALL = _C_XBC + SSM_CONV_DIM
_C_QK = _C_SMALL + LANES
_C_V = _C_QK + 4 * DIFF_HEADS * DIFF_QK_DIM
_C_END = _C_V + DIFF_WIDTH
_LANE_I = 0
_LANE_F = 8
_LANE_DT = 16
GATE_ROWS = 24
D_IN = 4 * MLSTM_WIDTH + 2 * MLSTM_HEADS + SSM_WIDTH + SSM_CONV_DIM + SSM_HEADS + 3 * DIFF_WIDTH


def _regroup_plan():
    sizes = (4 * MLSTM_WIDTH, MLSTM_HEADS, MLSTM_HEADS, SSM_WIDTH, SSM_CONV_DIM, SSM_HEADS, 3 * DIFF_WIDTH)
    dests = (_C_M, _C_SMALL + _LANE_I, _C_SMALL + _LANE_F, _C_Z, _C_XBC, _C_SMALL + _LANE_DT, _C_QK)
    plan, src = [], 0
    for dst, width in zip(dests, sizes):
        plan.append((dst, src, width))
        src += width
    assert src == D_IN
    return plan


def _dot(a, b):
    return jnp.dot(a, b, preferred_element_type=F32)


def _sigmoid(x):
    return 1.0 / (1.0 + jnp.exp(-x))


def _softplus(x):
    return jnp.maximum(x, 0.0) + jnp.log(1.0 + jnp.exp(-jnp.abs(x)))


def _rms_rows(x, w_row):
    ms = jnp.mean(x * x, axis=-1, keepdims=True)
    return x * lax.rsqrt(ms + NORM_EPS) * w_row


def _group_mean_sq(x, gmat):
    sq = x * x
    hi = sq.astype(BF16)
    lo = (sq - hi.astype(F32)).astype(BF16)
    return _dot(hi, gmat) + _dot(lo, gmat)


def _scan_lanes(x, op, identity):
    n = x.shape[1]
    lane = lax.broadcasted_iota(jnp.int32, x.shape, 1)
    step = 1
    while step < n:
        x = op(x, jnp.where(lane >= step, pltpu.roll(x, step, axis=1), identity))
        step *= 2
    return x


def _block_diag(n, group, value):
    r = jnp.arange(n) // group
    return jnp.where(r[:, None] == r[None, :], value, 0.0).astype(BF16)


def _const_spec(shape):
    nd = len(shape)
    return pl.BlockSpec(shape, lambda *_: (0,) * nd)


def _params(sem):
    return pltpu.CompilerParams(dimension_semantics=sem, vmem_limit_bytes=VMEM_LIMIT)


def _ffn_block(x, nw_ref, wg_ref, wu_ref, wd_ref):
    xn = _rms_rows(x, nw_ref[...]).astype(BF16)
    acc = None
    for lo, hi in zip(FFN_SPLITS[:-1], FFN_SPLITS[1:]):
        assert (hi - lo) % MXU_COLS == 0
        cols = slice(lo, hi)
        g = _dot(xn, wg_ref[:, cols])
        u = _dot(xn, wu_ref[:, cols])
        h = (g * _sigmoid(g) * u).astype(BF16)
        part = _dot(h, wd_ref[cols, :])
        acc = part if acc is None else acc + part
    return x + 0.5 * acc


def _row_spec(width):
    return pl.BlockSpec((ROW_TILE, width), lambda i: (i, 0))


WIDE_STAGE_ROWS = 128
TALL_STAGE_ROWS = 352
_HBM = pl.BlockSpec(memory_space=pl.ANY)


def _stage_weight(w_hbm, layer, rows, stage, sem, store):
    n_rows, n_cols = w_hbm.shape[1], w_hbm.shape[2]
    assert n_rows % rows == 0

    def copy(c):
        return pltpu.make_async_copy(w_hbm.at[layer, pl.ds(c * rows, rows), :],
                                     stage.at[c % 2, pl.ds(0, rows), pl.ds(0, n_cols)], sem.at[c % 2])

    copy(0).start()
    for c in range(n_rows // rows):
        if c + 1 < n_rows // rows:
            copy(c + 1).start()
        copy(c).wait()
        store(c * rows, stage[c % 2, :rows, :n_cols])


def _stage_ffn_weights(layer, wg_hbm, wu_hbm, wd_hbm, wg_scr, wu_scr, wd_scr, wide, wide_sem, tall, tall_sem):
    def into(dst):
        def store(r0, chunk):
            dst[r0:r0 + chunk.shape[0], :] = chunk.astype(BF16)
        return store

    _stage_weight(wg_hbm, layer, WIDE_STAGE_ROWS, wide, wide_sem, into(wg_scr))
    _stage_weight(wu_hbm, layer, WIDE_STAGE_ROWS, wide, wide_sem, into(wu_scr))
    _stage_weight(wd_hbm, layer, TALL_STAGE_ROWS, tall, tall_sem, into(wd_scr))


def _ffn_weight_scratch():
    return [pltpu.VMEM((D_MODEL, D_FF), BF16), pltpu.VMEM((D_MODEL, D_FF), BF16), pltpu.VMEM((D_FF, D_MODEL), BF16),
            pltpu.VMEM((2, WIDE_STAGE_ROWS, D_IN), F32), pltpu.SemaphoreType.DMA((2,)),
            pltpu.VMEM((2, TALL_STAGE_ROWS, D_MODEL), F32), pltpu.SemaphoreType.DMA((2,))]


def _store_transposed_tiles(out_ref, a):
    a_t = a.T.astype(BF16)
    for t in range(ROW_TILE // ATT_TILE):
        out_ref[t] = a_t[:, t * ATT_TILE:(t + 1) * ATT_TILE]


def _ffn_inproj_kernel(x_ref, fnw_ref, wg_hbm, wu_hbm, wd_hbm, nw_ref, w_hbm, qkw_ref, g32_ref, gbias_ref,
                       alog_ref, convw_ref, convb_ref,
                       x_out_ref, mqt_ref, mk_ref, mvt_ref, mot_ref, gt_ref,
                       zt_ref, sb_ref, sct_ref, sxt_ref, sxdt_ref, sa_ref, qt_ref, k_ref, vt_ref,
                       halo_scr, w_scr, wg_ref, wu_ref, wd_ref, wide, wide_sem, tall, tall_sem,
                       *, tiles_per_seq, layer):
    @pl.when(pl.program_id(0) == 0)
    def _():
        _stage_ffn_weights(layer, wg_hbm, wu_hbm, wd_hbm, wg_ref, wu_ref, wd_ref, wide, wide_sem, tall, tall_sem)
        w_scr[:, _C_SMALL:_C_QK] = jnp.zeros((D_MODEL, LANES), BF16)

        def regrouped(r0, chunk):
            for dst, src, width in _regroup_plan():
                w_scr[r0:r0 + chunk.shape[0], dst:dst + width] = chunk[:, src:src + width].astype(BF16)

        _stage_weight(w_hbm, layer, WIDE_STAGE_ROWS, wide, wide_sem, regrouped)

    @pl.when(pl.program_id(0) % tiles_per_seq == 0)
    def _():
        halo_scr[...] = jnp.zeros_like(halo_scr)

    x = _ffn_block(x_ref[...], fnw_ref, wg_ref, wu_ref, wd_ref)
    x_out_ref[...] = x
    xn = _rms_rows(x, nw_ref[...]).astype(BF16)

    def proj(lo, hi):
        return _dot(xn, w_scr[:, lo:hi])

    half = (_C_V - _C_QK) // 2
    p_xbc = proj(_C_XBC, _C_SMALL)
    p_small = proj(_C_SMALL, _C_QK)
    p_q, p_k = proj(_C_QK, _C_QK + half), proj(_C_QK + half, _C_V)
    p_mq = proj(_C_M, _C_M + MLSTM_WIDTH)
    p_mk = proj(_C_M + MLSTM_WIDTH, _C_M + 2 * MLSTM_WIDTH)
    p_mv = proj(_C_M + 2 * MLSTM_WIDTH, _C_M + 3 * MLSTM_WIDTH)
    p_mo = proj(_C_M + 3 * MLSTM_WIDTH, _C_Z)
    p_z = proj(_C_Z, _C_XBC)
    p_v = proj(_C_V, _C_END)

    qk = [a * lax.rsqrt(_group_mean_sq(a, g32_ref[...]) + NORM_EPS) * qkw_ref[:, i * half:(i + 1) * half]
          for i, a in enumerate((p_q, p_k))]
    _store_transposed_tiles(qt_ref, qk[0])
    k_ref[...] = qk[1].astype(BF16)
    _store_transposed_tiles(vt_ref, p_v)

    ext = jnp.concatenate([halo_scr[...], p_xbc], axis=0)
    halo_scr[...] = p_xbc[ROW_TILE - 8:, :]
    conv = convb_ref[...] + convw_ref[SSM_CONV - 1:SSM_CONV, :] * p_xbc
    for j in range(1, SSM_CONV):
        conv = conv + convw_ref[SSM_CONV - 1 - j:SSM_CONV - j, :] * ext[8 - j:8 - j + ROW_TILE, :]
    xa = conv * _sigmoid(conv)
    sb_ref[...] = xa[:, SSM_WIDTH:SSM_WIDTH + SSM_GROUPS * SSM_STATE].astype(BF16)
    _store_transposed_tiles(sct_ref, xa[:, SSM_WIDTH + SSM_GROUPS * SSM_STATE:])
    small_t = p_small.T + gbias_ref[...]
    dt = _softplus(small_t[_LANE_DT:_LANE_DT + SSM_HEADS])
    log_decay = dt * (-jnp.exp(alog_ref[...]))
    xs_t = xa[:, :SSM_WIDTH].T
    xdt_t = xs_t * jnp.concatenate([jnp.broadcast_to(dt[h:h + 1], (SSM_HEAD_DIM, ROW_TILE))
                                    for h in range(SSM_HEADS)], axis=0)
    for t in range(ROW_TILE // MIX_CHUNK):
        cols = slice(t * MIX_CHUNK, (t + 1) * MIX_CHUNK)
        sxt_ref[t] = xs_t[:, cols].astype(BF16)
        sxdt_ref[t] = xdt_t[:, cols].astype(BF16)
        sa_ref[t] = _scan_lanes(log_decay[:, cols], jnp.add, 0.0)
    _store_transposed_tiles(zt_ref, p_z)

    _store_transposed_tiles(mqt_ref, p_mq)
    mk_ref[...] = p_mk.astype(BF16)
    _store_transposed_tiles(mvt_ref, p_mv)
    _store_transposed_tiles(mot_ref, p_mo)
    gates = small_t[:_LANE_DT]
    for t in range(ROW_TILE // MIX_CHUNK):
        i_pre = gates[_LANE_I:_LANE_I + 8, t * MIX_CHUNK:(t + 1) * MIX_CHUNK]
        f_pre = gates[_LANE_F:_LANE_F + 8, t * MIX_CHUNK:(t + 1) * MIX_CHUNK]
        log_f = jnp.minimum(f_pre, 0.0) - jnp.log(1.0 + jnp.exp(-jnp.abs(f_pre)))
        b = _scan_lanes(log_f, jnp.add, 0.0)
        c = i_pre - b
        gt_ref[t] = jnp.concatenate([c, _scan_lanes(c, jnp.maximum, -jnp.inf), b], axis=0)


def _ffn_inproj(x, ffn_norm_w, wg, wu, wd, norm_w, w_all, qkw_row, g32, gate_bias_rows, alog_rows, conv_w, conv_b_row,
                layer, seq):
    n = x.shape[0]
    tiles = ROW_TILE // ATT_TILE

    def t_out(width, dtype=BF16):
        return (pl.BlockSpec((tiles, width, ATT_TILE), lambda i: (i, 0, 0)),
                jax.ShapeDtypeStruct((n // ATT_TILE, width, ATT_TILE), dtype))

    def r_out(width, dtype=BF16):
        return _row_spec(width), jax.ShapeDtypeStruct((n, width), dtype)

    outs = [r_out(D_MODEL, F32),
            t_out(MLSTM_WIDTH), r_out(MLSTM_WIDTH), t_out(MLSTM_WIDTH), t_out(MLSTM_WIDTH), t_out(GATE_ROWS, F32),
            t_out(SSM_WIDTH), r_out(SSM_GROUPS * SSM_STATE), t_out(SSM_GROUPS * SSM_STATE), t_out(SSM_WIDTH),
            t_out(SSM_WIDTH), t_out(SSM_HEADS, F32),
            t_out(DIFF_WIDTH), r_out(DIFF_WIDTH), t_out(DIFF_WIDTH)]
    return pl.pallas_call(
        functools.partial(_ffn_inproj_kernel, tiles_per_seq=seq // ROW_TILE, layer=layer),
        grid=(n // ROW_TILE,),
        in_specs=[_row_spec(D_MODEL), _const_spec((1, D_MODEL)), _HBM, _HBM, _HBM, _const_spec((1, D_MODEL)), _HBM,
                  _const_spec((1, 2 * DIFF_WIDTH)), _const_spec((DIFF_WIDTH, DIFF_WIDTH)),
                  _const_spec((LANES, ROW_TILE)), _const_spec((SSM_HEADS, ROW_TILE)),
                  _const_spec((SSM_CONV, SSM_CONV_DIM)), _const_spec((1, SSM_CONV_DIM))],
        out_specs=[o[0] for o in outs],
        out_shape=[o[1] for o in outs],
        scratch_shapes=[pltpu.VMEM((8, SSM_CONV_DIM), F32), pltpu.VMEM((D_MODEL, _C_END), BF16)]
        + _ffn_weight_scratch(),
        compiler_params=_params(("arbitrary",)),
        name="ffn_inproj",
    )(x, ffn_norm_w.reshape(1, D_MODEL), wg, wu, wd, norm_w.reshape(1, D_MODEL), w_all, qkw_row, g32,
      gate_bias_rows, alog_rows, conv_w, conv_b_row)


def _mlstm_kernel(qt_ref, k_ref, vt_ref, ot_ref, gt_ref, nw_ref, yt_ref, ct_scr, nt_scr, m_scr):
    @pl.when(pl.program_id(1) == 0)
    def _():
        ct_scr[...] = jnp.zeros_like(ct_scr)
        nt_scr[...] = jnp.zeros_like(nt_scr)
        m_scr[...] = jnp.zeros_like(m_scr)

    for t in range(MIX_STEP_CHUNKS):
        _mlstm_chunk(t, qt_ref, k_ref, vt_ref, ot_ref, gt_ref, nw_ref, yt_ref, ct_scr, nt_scr, m_scr)


def _mlstm_chunk(t, qt_ref, k_ref, vt_ref, ot_ref, gt_ref, nw_ref, yt_ref, ct_scr, nt_scr, m_scr):
    L = MIX_CHUNK
    H = MLSTM_HEADS
    W = MLSTM_HEAD_DIM
    qt = qt_ref[t]
    ks = k_ref[t * L:(t + 1) * L, :] * (MLSTM_HEAD_DIM ** -0.5)
    vt = vt_ref[t]
    gt = gt_ref[t]
    c, c_max, b = gt[0:8], gt[8:16], gt[16:24]
    m_prev = m_scr[...]
    big_m = jnp.maximum(m_prev, c_max)
    inter = jnp.exp(m_prev - big_m)
    floor = jnp.exp(-(b + big_m))
    b_last = jnp.broadcast_to(b[:, L - 1:L], b.shape)
    m_new = b_last + jnp.maximum(m_prev, jnp.broadcast_to(c_max[:, L - 1:L], b.shape))
    w = jnp.exp(b_last + c - m_new)
    decay = jnp.exp(b_last + m_prev - m_new)

    c2_cols = jnp.concatenate([c * LOG2E, jnp.zeros((LANES - 8, L), F32)], axis=0).T
    big_m2 = big_m * LOG2E
    causal = (lax.broadcasted_iota(jnp.int32, (L, L), 0) <= lax.broadcasted_iota(jnp.int32, (L, L), 1))
    row_head = lax.broadcasted_iota(jnp.int32, (H * W, L), 0) // W
    qk = [_dot(ks, jnp.where(row_head == h, qt, jnp.zeros_like(qt))) for h in range(H)]
    nums, dens = [], []
    for h in range(H):
        d = jnp.exp2(jnp.where(causal, c2_cols[:, h:h + 1] - big_m2[h:h + 1, :], -jnp.inf))
        s = qk[h] * d
        dens.append(jnp.sum(s, axis=0, keepdims=True))
        nums.append(_dot(vt[h * W:(h + 1) * W, :], s.astype(BF16)))
    cq = _dot(ct_scr[...].astype(BF16), qt)
    nq = _dot(nt_scr[...].astype(BF16), qt)
    outs = []
    for h in range(H):
        num = nums[h] + inter[h:h + 1] * cq[h * W:(h + 1) * W]
        den = dens[h] + inter[h:h + 1] * nq[h:h + 1]
        hh = num / jnp.maximum(jnp.abs(den), floor[h:h + 1])
        outs.append(hh * lax.rsqrt(jnp.mean(hh * hh, axis=0, keepdims=True) + NORM_EPS))
    hn = jnp.concatenate(outs, axis=0) * nw_ref[...]
    yt_ref[t] = (_sigmoid(ot_ref[t].astype(F32)) * hn).astype(BF16)

    w_full = jnp.concatenate([jnp.broadcast_to(w[h:h + 1], (W, L)) for h in range(H)], axis=0)
    g_new = _dot((vt.astype(F32) * w_full).astype(BF16), ks)
    decay_full = jnp.concatenate([jnp.broadcast_to(decay[h:h + 1, :1], (W, H * W)) for h in range(H)], axis=0)
    same_head = (lax.broadcasted_iota(jnp.int32, (H * W, H * W), 0) // W
                 == lax.broadcasted_iota(jnp.int32, (H * W, H * W), 1) // W)
    ct_scr[...] = decay_full * ct_scr[...] + jnp.where(same_head, g_new, 0.0)
    w16 = jnp.concatenate([w, jnp.zeros_like(w)], axis=0).astype(BF16)
    n_new = _dot(w16, ks)
    own_lanes = (lax.broadcasted_iota(jnp.int32, (16, H * W), 0)
                 == lax.broadcasted_iota(jnp.int32, (16, H * W), 1) // W)
    decay16 = jnp.broadcast_to(jnp.concatenate([decay[:, :1], jnp.zeros((8, 1), F32)], axis=0), (16, H * W))
    nt_scr[...] = decay16 * nt_scr[...] + jnp.where(own_lanes, n_new, 0.0)
    m_scr[...] = m_new


def _mlstm(mqt, mk, mvt, mot, gt, norm_w_rows, batch, seq):
    ns = seq // (MIX_CHUNK * MIX_STEP_CHUNKS)
    n = batch * seq
    t_spec = pl.BlockSpec((MIX_STEP_CHUNKS, MLSTM_WIDTH, MIX_CHUNK), lambda b, c: (b * ns + c, 0, 0))
    return pl.pallas_call(
        _mlstm_kernel,
        grid=(batch, ns),
        in_specs=[t_spec, pl.BlockSpec((MIX_STEP_CHUNKS * MIX_CHUNK, MLSTM_WIDTH), lambda b, c: (b * ns + c, 0)),
                  t_spec, t_spec,
                  pl.BlockSpec((MIX_STEP_CHUNKS, GATE_ROWS, MIX_CHUNK), lambda b, c: (b * ns + c, 0, 0)),
                  _const_spec((MLSTM_WIDTH, MIX_CHUNK))],
        out_specs=t_spec,
        out_shape=jax.ShapeDtypeStruct((n // MIX_CHUNK, MLSTM_WIDTH, MIX_CHUNK), BF16),
        scratch_shapes=[pltpu.VMEM((MLSTM_WIDTH, MLSTM_WIDTH), F32),
                        pltpu.VMEM((16, MLSTM_WIDTH), F32),
                        pltpu.VMEM((8, MIX_CHUNK), F32)],
        compiler_params=_params(("arbitrary", "arbitrary")),
        name="mlstm",
    )(mqt, mk, mvt, mot, gt, norm_w_rows)


def _ssd_kernel(b_ref, ct_ref, xt_ref, xdt_ref, zt_ref, a_ref, dskip_ref, nw_ref, yt_ref, st_scr):
    @pl.when(pl.program_id(1) == 0)
    def _():
        st_scr[...] = jnp.zeros_like(st_scr)

    for t in range(MIX_STEP_CHUNKS):
        _ssd_chunk(t, b_ref, ct_ref, xt_ref, xdt_ref, zt_ref, a_ref, dskip_ref, nw_ref, yt_ref, st_scr)


def _ssd_chunk(t, b_ref, ct_ref, xt_ref, xdt_ref, zt_ref, a_ref, dskip_ref, nw_ref, yt_ref, st_scr):
    L = MIX_CHUNK
    P = SSM_HEAD_DIM
    HG = SSM_HEADS // SSM_GROUPS
    GW = HG * P
    NS = SSM_STATE
    a = a_ref[t]
    a_last = jnp.broadcast_to(a[:, L - 1:L], a.shape)
    exp_a = jnp.exp(a)
    d_in = jnp.exp(a_last - a)
    chunk_decay = jnp.exp(a_last)
    a2 = a * LOG2E
    a2_cols = jnp.concatenate([a2, jnp.zeros((LANES - SSM_HEADS, L), F32)], axis=0).T
    causal = (lax.broadcasted_iota(jnp.int32, (L, L), 0) <= lax.broadcasted_iota(jnp.int32, (L, L), 1))
    bn = b_ref[t * L:(t + 1) * L, :]
    ct = ct_ref[t]
    xdt = xdt_ref[t]

    def per_head_rows(rows, grp, width):
        return jnp.concatenate([jnp.broadcast_to(rows[grp * HG + i:grp * HG + i + 1, :width], (P, width))
                                for i in range(HG)], axis=0)

    scores, y_off = [], []
    for grp in range(SSM_GROUPS):
        bg = bn[:, grp * NS:(grp + 1) * NS]
        cg = ct[grp * NS:(grp + 1) * NS, :]
        scores.append(_dot(bg, cg).astype(BF16))
        st = st_scr[grp]
        y_off.append(_dot(st.astype(BF16), cg))
        x_in = (xdt[grp * GW:(grp + 1) * GW, :].astype(F32) * per_head_rows(d_in, grp, L)).astype(BF16)
        st_scr[grp] = per_head_rows(chunk_decay, grp, NS) * st + _dot(x_in, bg)
    for grp in range(SSM_GROUPS):
        y_diag = []
        for i in range(HG):
            h = grp * HG + i
            seg = jnp.exp2(jnp.where(causal, a2[h:h + 1, :] - a2_cols[:, h:h + 1], -jnp.inf).astype(BF16))
            p = scores[grp] * seg
            y_diag.append(_dot(xdt[h * P:(h + 1) * P, :], p))
        rows = slice(grp * GW, (grp + 1) * GW)
        y = (jnp.concatenate(y_diag, axis=0) + y_off[grp] * per_head_rows(exp_a, grp, L)
             + dskip_ref[rows, :] * xt_ref[t, rows, :].astype(F32))
        zg = zt_ref[t, rows, :].astype(F32)
        y = y * (zg * _sigmoid(zg))
        ms = jnp.mean(y * y, axis=0, keepdims=True)
        yt_ref[t, rows, :] = (y * lax.rsqrt(ms + NORM_EPS) * nw_ref[rows, :]).astype(BF16)


def _ssd(sb, sct, sxt, sxdt, zt, sa, dskip_rows, norm_w_rows, batch, seq):
    ns = seq // (MIX_CHUNK * MIX_STEP_CHUNKS)
    n = batch * seq

    def t_spec(width):
        return pl.BlockSpec((MIX_STEP_CHUNKS, width, MIX_CHUNK), lambda b, c: (b * ns + c, 0, 0))

    return pl.pallas_call(
        _ssd_kernel,
        grid=(batch, ns),
        in_specs=[pl.BlockSpec((MIX_STEP_CHUNKS * MIX_CHUNK, SSM_GROUPS * SSM_STATE), lambda b, c: (b * ns + c, 0)),
                  t_spec(SSM_GROUPS * SSM_STATE), t_spec(SSM_WIDTH), t_spec(SSM_WIDTH), t_spec(SSM_WIDTH),
                  t_spec(SSM_HEADS), _const_spec((SSM_WIDTH, MIX_CHUNK)), _const_spec((SSM_WIDTH, MIX_CHUNK))],
        out_specs=t_spec(SSM_WIDTH),
        out_shape=jax.ShapeDtypeStruct((n // MIX_CHUNK, SSM_WIDTH, MIX_CHUNK), BF16),
        scratch_shapes=[pltpu.VMEM((SSM_GROUPS, SSM_WIDTH // SSM_GROUPS, SSM_STATE), F32)],
        compiler_params=_params(("arbitrary", "arbitrary")),
        name="ssd",
    )(sb, sct, sxt, sxdt, zt, sa, dskip_rows, norm_w_rows)


def _t5_bias_tile(rel, relb_ref, head):
    n = jnp.maximum(rel, 0)
    max_exact = REL_BUCKETS // 2
    nf = jnp.maximum(n, 1).astype(F32)
    large = max_exact + (jnp.log(nf / max_exact) / math.log(REL_MAX_DIST / max_exact)
                         * (REL_BUCKETS - max_exact)).astype(jnp.int32)
    large = jnp.minimum(large, REL_BUCKETS - 1)
    bucket = jnp.where(n < max_exact, n, large)
    far = relb_ref[REL_BUCKETS - 1, head]
    bias = jnp.zeros(rel.shape, F32)
    for bkt in range(REL_BUCKETS - 1):
        bias = jnp.where(bucket == bkt, (relb_ref[bkt, head] - far) * LOG2E, bias)
    return jnp.where(rel >= 0, bias, NEG_BIG)


def _attn_kernel(relb_ref, qt_ref, k_ref, vt_ref, lam_ref, subw_ref, g64_ref,
                 y_ref, bias_scr, qm_scr, m_scr, acc_scr, sa_scr, sb_scr, *, lam_init):
    T = ATT_TILE
    H = DIFF_HEADS
    DV = DIFF_V_DIM
    R = ATT_STEP_TILES
    step_idx = pl.program_id(1)

    @pl.when((pl.program_id(0) == 0) & (step_idx == 0))
    def _():
        rel = lax.broadcasted_iota(jnp.int32, (T, T), 1) - lax.broadcasted_iota(jnp.int32, (T, T), 0)
        for h in range(H):
            bias_scr[0, h] = _t5_bias_tile(rel, relb_ref, h)
            bias_scr[1, h] = _t5_bias_tile(rel + T, relb_ref, h)

    comp_of_row = lax.broadcasted_iota(jnp.int32, (DIFF_WIDTH, T), 0) // DIFF_QK_DIM
    for r in range(R):
        qt = qt_ref[r]
        for hc in range(2 * H):
            qm_scr[r, hc] = jnp.where(comp_of_row == hc, qt, jnp.zeros_like(qt))
    m_scr[...] = jnp.full(m_scr.shape, NEG_BIG, F32)
    acc_scr[...] = jnp.zeros_like(acc_scr)
    ones_rows = jnp.ones((ONES_ROWS, T), BF16)

    def key_tile(j):
        return k_ref[pl.ds(pl.multiple_of(j * T, T), T), :]

    def logits_one(s_ref, kt, r, hc):
        s_ref[hc] = _dot(kt, qm_scr[r, hc])

    def consume_one(s_ref, vt, r, hc, bias_idx):
        h = hc // 2
        s = s_ref[hc]
        if bias_idx is not None:
            s = bias_scr[bias_idx, h] + s
        s = s.astype(BF16)
        m_prev = m_scr[r, hc]
        m_next = jnp.maximum(m_prev, jnp.max(s, axis=0, keepdims=True).astype(F32))
        alpha = jnp.exp2(m_prev - m_next)
        p = jnp.exp2(s - m_next.astype(BF16))
        m_scr[r, hc] = m_next
        v_aug = jnp.concatenate([vt[h * DV:(h + 1) * DV, :], ones_rows], axis=0)
        acc_scr[r, hc] = alpha * acc_scr[r, hc] + _dot(v_aug, p)

    LEAD = 2

    def step(s_next, j_next, r_next, s_cur, j_cur, r_cur, bias_idx):
        kt = key_tile(j_next)
        vt = vt_ref[j_cur]
        for hc in range(LEAD):
            logits_one(s_next, kt, r_next, hc)
        for hc in range(2 * H):
            if hc + LEAD < 2 * H:
                logits_one(s_next, kt, r_next, hc + LEAD)
            consume_one(s_cur, vt, r_cur, hc, bias_idx)

    DIAG, NEAR = 0, 1

    def far_pairs(count, cur, oth, r):
        def pair(j):
            step(oth, j + 1, r, cur, j, r, None)
            step(cur, j + 2, r, oth, j + 1, r, None)

        def body(t, carry):
            pair(4 * t)
            pair(4 * t + 2)
            return carry

        lax.fori_loop(0, count // 2, body, 0)

        @pl.when(count % 2 == 1)
        def _():
            pair(2 * (count - 1))

    cur, oth = sa_scr, sb_scr
    kt0 = key_tile(0)
    for hc in range(2 * H):
        logits_one(cur, kt0, 0, hc)
    for r in range(R):
        d = R * step_idx + r
        if r % 2 == 1:
            far_pairs((d - 1) // 2, cur, oth, r)
            step(oth, d, r, cur, d - 1, r, NEAR)
            diag, free = oth, cur
        else:
            far_pairs(jnp.maximum(d // 2 - 1, 0), cur, oth, r)

            def near_tiles(d=d, cur=cur, oth=oth, r=r):
                step(oth, d - 1, r, cur, d - 2, r, None)
                step(cur, d, r, oth, d - 1, r, NEAR)

            if r == 0:
                pl.when(d >= 1)(near_tiles)
            else:
                near_tiles()
            diag, free = cur, oth
        if r + 1 < R:
            step(free, 0, r + 1, diag, d, r, DIAG)
            cur, oth = free, diag
        else:
            vt_last = vt_ref[d]
            for hc in range(2 * H):
                consume_one(diag, vt_last, r, hc, DIAG)

    lam_p = lam_ref[...]
    lam = (jnp.exp(jnp.sum(lam_p[0:1] * lam_p[1:2], axis=1, keepdims=True))
           - jnp.exp(jnp.sum(lam_p[2:3] * lam_p[3:4], axis=1, keepdims=True)) + lam_init)
    for r in range(R):
        def normalised(hc):
            return acc_scr[r, hc, :DV, :] / acc_scr[r, hc, DV:DV + 1, :]

        o_t = jnp.concatenate([normalised(2 * h) - lam * normalised(2 * h + 1) for h in range(H)], axis=0)
        o = o_t.T
        ms = _group_mean_sq(o, g64_ref[...])
        y_ref[r * T:(r + 1) * T, :] = (o * lax.rsqrt(ms + NORM_EPS) * subw_ref[...] * (1.0 - lam_init)).astype(BF16)


def _diff_attn(qt, kn, vt, rel_bias, lambdas, subw_row, g64, lam_init, batch, seq):
    nk = seq // ATT_TILE
    nsteps = nk // ATT_STEP_TILES
    n = batch * seq
    return pl.pallas_call(
        functools.partial(_attn_kernel, lam_init=lam_init),
        grid=(batch, nsteps),
        in_specs=[pl.BlockSpec(memory_space=pltpu.SMEM),
                  pl.BlockSpec((ATT_STEP_TILES, DIFF_WIDTH, ATT_TILE), lambda b, i: (b * nsteps + i, 0, 0)),
                  pl.BlockSpec((seq, DIFF_WIDTH), lambda b, i: (b, 0)),
                  pl.BlockSpec((nk, DIFF_WIDTH, ATT_TILE), lambda b, i: (b, 0, 0)),
                  _const_spec((4, DIFF_QK_DIM)), _const_spec((1, DIFF_WIDTH)),
                  _const_spec((DIFF_WIDTH, DIFF_WIDTH))],
        out_specs=pl.BlockSpec((ATT_STEP_TILES * ATT_TILE, DIFF_WIDTH), lambda b, i: (b * nsteps + i, 0)),
        out_shape=jax.ShapeDtypeStruct((n, DIFF_WIDTH), BF16),
        scratch_shapes=[pltpu.VMEM((2, DIFF_HEADS, ATT_TILE, ATT_TILE), F32),
                        pltpu.VMEM((ATT_STEP_TILES, 2 * DIFF_HEADS, DIFF_WIDTH, ATT_TILE), BF16),
                        pltpu.VMEM((ATT_STEP_TILES, 2 * DIFF_HEADS, 1, ATT_TILE), F32),
                        pltpu.VMEM((ATT_STEP_TILES, 2 * DIFF_HEADS, DIFF_V_DIM + ONES_ROWS, ATT_TILE), F32),
                        pltpu.VMEM((2 * DIFF_HEADS, ATT_TILE, ATT_TILE), F32),
                        pltpu.VMEM((2 * DIFF_HEADS, ATT_TILE, ATT_TILE), F32)],
        compiler_params=_params(("arbitrary", "arbitrary")),
        name="diff_attn",
    )(rel_bias, qt, kn, vt, lambdas, subw_row, g64)


def _untransposed(yt_ref):
    return jnp.concatenate([yt_ref[t].astype(F32).T for t in range(yt_ref.shape[0])], axis=0).astype(BF16)


def _outproj_ffn_kernel(x_ref, ymt_ref, yst_ref, yd_ref, w_hbm, fnw_ref, wg_hbm, wu_hbm, wd_hbm, o_ref,
                        w_ref, wg_ref, wu_ref, wd_ref, wide, wide_sem, tall, tall_sem, *, layer):
    @pl.when(pl.program_id(0) == 0)
    def _():
        def store(r0, chunk):
            w_ref[r0:r0 + chunk.shape[0], :] = chunk.astype(BF16)

        _stage_weight(w_hbm, layer, D_MODEL // 4, tall, tall_sem, store)
        _stage_ffn_weights(layer, wg_hbm, wu_hbm, wd_hbm, wg_ref, wu_ref, wd_ref, wide, wide_sem, tall, tall_sem)

    y = jnp.concatenate([_untransposed(ymt_ref), _untransposed(yst_ref), yd_ref[...]], axis=1)
    x = x_ref[...] + _dot(y, w_ref[...])
    o_ref[...] = _ffn_block(x, fnw_ref, wg_ref, wu_ref, wd_ref)


def _outproj_ffn(x, y_m, y_s, y_d, w_out, ffn_norm_w, wg, wu, wd, layer):
    n = x.shape[0]
    return pl.pallas_call(
        functools.partial(_outproj_ffn_kernel, layer=layer),
        grid=(n // ROW_TILE,),
        in_specs=[_row_spec(D_MODEL),
                  pl.BlockSpec((ROW_TILE // MIX_CHUNK, MLSTM_WIDTH, MIX_CHUNK), lambda i: (i, 0, 0)),
                  pl.BlockSpec((ROW_TILE // MIX_CHUNK, SSM_WIDTH, MIX_CHUNK), lambda i: (i, 0, 0)),
                  _row_spec(DIFF_WIDTH), _HBM, _const_spec((1, D_MODEL)), _HBM, _HBM, _HBM],
        out_specs=_row_spec(D_MODEL),
        out_shape=jax.ShapeDtypeStruct((n, D_MODEL), F32),
        scratch_shapes=[pltpu.VMEM((D_MODEL, D_MODEL), BF16)] + _ffn_weight_scratch(),
        compiler_params=_params(("arbitrary",)),
        name="outproj_ffn",
    )(x, y_m, y_s, y_d, w_out, ffn_norm_w.reshape(1, D_MODEL), wg, wu, wd)


def _small_row(values_by_lane):
    row = jnp.zeros((LANES,), F32)
    for lane, vals in values_by_lane:
        row = lax.dynamic_update_slice(row, vals.astype(F32), (lane,))
    return row.reshape(1, LANES)


def kernel(x, ffn1_norm_w, ffn1_w_gate, ffn1_w_up, ffn1_w_down, mix_norm_w, w_in, mlstm_gate_bias, mlstm_norm_w, ssm_conv_w, ssm_conv_b, ssm_dt_bias, ssm_A_log, ssm_D, ssm_norm_w, diff_q_norm_w, diff_k_norm_w, diff_lambda, diff_subln_w, rel_bias, w_out, ffn2_norm_w, ffn2_w_gate, ffn2_w_up, ffn2_w_down):
    batch, seq, d = x.shape
    assert d == D_MODEL and MIX_CHUNK == ATT_TILE and seq % ROW_TILE == 0 and ROW_TILE % MIX_CHUNK == 0
    assert seq % (MIX_CHUNK * MIX_STEP_CHUNKS) == 0 and seq % (ATT_STEP_TILES * ATT_TILE) == 0 and ATT_STEP_TILES % 2 == 0
    xf = x.reshape(batch * seq, D_MODEL)
    g32 = _block_diag(DIFF_WIDTH, DIFF_QK_DIM, 1.0 / DIFF_QK_DIM)
    g64 = _block_diag(DIFF_WIDTH, DIFF_V_DIM, 1.0 / DIFF_V_DIM)
    ffn1 = (ffn1_w_gate, ffn1_w_up, ffn1_w_down)
    ffn2 = (ffn2_w_gate, ffn2_w_up, ffn2_w_down)
    for l in range(DEPTH):
        qkw_row = jnp.concatenate([jnp.tile(diff_q_norm_w[l].reshape(-1), DIFF_HEADS) * (DIFF_QK_DIM ** -0.5 * LOG2E),
                                   jnp.tile(diff_k_norm_w[l].reshape(-1), DIFF_HEADS)]).reshape(1, 2 * DIFF_WIDTH)
        bias_row = _small_row([(_LANE_I, mlstm_gate_bias[l, 0]), (_LANE_F, mlstm_gate_bias[l, 1]),
                               (_LANE_DT, ssm_dt_bias[l])])
        (xf, mqt, mk, mvt, mot, gt, zt, sb, sct, sxt, sxdt, sa, qt, kn, vt) = _ffn_inproj(
            xf, ffn1_norm_w[l], *ffn1, mix_norm_w[l], w_in, qkw_row, g32,
            jnp.broadcast_to(bias_row.reshape(LANES, 1), (LANES, ROW_TILE)),
            jnp.broadcast_to(ssm_A_log[l][:, None], (SSM_HEADS, ROW_TILE)),
            ssm_conv_w[l], ssm_conv_b[l].reshape(1, SSM_CONV_DIM), l, seq)
        y_m = _mlstm(mqt, mk, mvt, mot, gt, jnp.broadcast_to(mlstm_norm_w[l][:, None], (MLSTM_WIDTH, MIX_CHUNK)),
                     batch, seq)
        y_s = _ssd(sb, sct, sxt, sxdt, zt, sa,
                   jnp.broadcast_to(jnp.repeat(ssm_D[l], SSM_HEAD_DIM)[:, None], (SSM_WIDTH, MIX_CHUNK)),
                   jnp.broadcast_to(ssm_norm_w[l][:, None], (SSM_WIDTH, MIX_CHUNK)), batch, seq)
        lam_init = 0.8 - 0.6 * math.exp(-0.3 * l)
        y_d = _diff_attn(qt, kn, vt, rel_bias, diff_lambda[l], jnp.tile(diff_subln_w[l], DIFF_HEADS).reshape(1, DIFF_WIDTH),
                         g64, lam_init, batch, seq)
        xf = _outproj_ffn(xf, y_m, y_s, y_d, w_out, ffn2_norm_w[l], *ffn2, l)
    return xf.reshape(batch, seq, D_MODEL)
```

```python
import functools
import math

import jax
import jax.numpy as jnp
from jax import lax
from jax.experimental import pallas as pl
from jax.experimental.pallas import tpu as pltpu

F32 = jnp.float32
BF16 = jnp.bfloat16

D_MODEL = 1024
DEPTH = 2
D_FF = 2816
MLSTM_HEADS = 4
MLSTM_HEAD_DIM = 64
MLSTM_WIDTH = MLSTM_HEADS * MLSTM_HEAD_DIM
SSM_HEADS = 8
SSM_HEAD_DIM = 64
SSM_WIDTH = SSM_HEADS * SSM_HEAD_DIM
SSM_STATE = 128
SSM_GROUPS = 2
SSM_CONV = 4
SSM_CONV_DIM = SSM_WIDTH + 2 * SSM_GROUPS * SSM_STATE
DIFF_HEADS = 4
DIFF_QK_DIM = 32
DIFF_V_DIM = 64
DIFF_WIDTH = DIFF_HEADS * DIFF_V_DIM
REL_BUCKETS = 32
REL_MAX_DIST = 128
NORM_EPS = 1e-6

LANES = 128
ROW_TILE = 512
MXU_COLS = 256
FFN_SPLITS = (0, 6 * MXU_COLS, D_FF)
MIX_CHUNK = 256
MIX_STEP_CHUNKS = 8
ATT_TILE = 256
ATT_STEP_TILES = 4
VMEM_LIMIT = 56 * 1024 * 1024
NEG_BIG = -1e30
LOG2E = math.log2(math.e)
ONES_ROWS = 16

_C_M = 0
_C_Z = _C_M + 4 * MLSTM_WIDTH
_C_XBC = _C_Z + SSM_WIDTH
_C_SMALL = _C_XBC + SSM_CONV_DIM
_C_QK = _C_SMALL + LANES
_C_V = _C_QK + 4 * DIFF_HEADS * DIFF_QK_DIM
_C_END = _C_V + DIFF_WIDTH
_LANE_I = 0
_LANE_F = 8
_LANE_DT = 16
GATE_ROWS = 24
D_IN = 4 * MLSTM_WIDTH + 2 * MLSTM_HEADS + SSM_WIDTH + SSM_CONV_DIM + SSM_HEADS + 3 * DIFF_WIDTH


def _regroup_plan():
    sizes = (4 * MLSTM_WIDTH, MLSTM_HEADS, MLSTM_HEADS, SSM_WIDTH, SSM_CONV_DIM, SSM_HEADS, 3 * DIFF_WIDTH)
    dests = (_C_M, _C_SMALL + _LANE_I, _C_SMALL + _LANE_F, _C_Z, _C_XBC, _C_SMALL + _LANE_DT, _C_QK)
    plan, src = [], 0
    for dst, width in zip(dests, sizes):
        plan.append((dst, src, width))
        src += width
    assert src == D_IN
    return plan


def _dot(a, b):
    return jnp.dot(a, b, preferred_element_type=F32)


def _sigmoid(x):
    return 1.0 / (1.0 + jnp.exp(-x))


def _softplus(x):
    return jnp.maximum(x, 0.0) + jnp.log(1.0 + jnp.exp(-jnp.abs(x)))


def _rms_rows(x, w_row):
    ms = jnp.mean(x * x, axis=-1, keepdims=True)
    return x * lax.rsqrt(ms + NORM_EPS) * w_row


def _group_mean_sq(x, gmat):
    sq = x * x
    hi = sq.astype(BF16)
    lo = (sq - hi.astype(F32)).astype(BF16)
    return _dot(hi, gmat) + _dot(lo, gmat)


def _scan_lanes(x, op, identity):
    n = x.shape[1]
    lane = lax.broadcasted_iota(jnp.int32, x.shape, 1)
    step = 1
    while step < n:
        x = op(x, jnp.where(lane >= step, pltpu.roll(x, step, axis=1), identity))
        step *= 2
    return x


def _block_diag(n, group, value):
    r = jnp.arange(n) // group
    return jnp.where(r[:, None] == r[None, :], value, 0.0).astype(BF16)


def _const_spec(shape):
    nd = len(shape)
    return pl.BlockSpec(shape, lambda *_: (0,) * nd)


def _params(sem):
    return pltpu.CompilerParams(dimension_semantics=sem, vmem_limit_bytes=VMEM_LIMIT)


def _ffn_block(x, nw_ref, wg_ref, wu_ref, wd_ref):
    xn = _rms_rows(x, nw_ref[...]).astype(BF16)
    acc = None
    for lo, hi in zip(FFN_SPLITS[:-1], FFN_SPLITS[1:]):
        assert (hi - lo) % MXU_COLS == 0
        cols = slice(lo, hi)
        g = _dot(xn, wg_ref[:, cols])
        u = _dot(xn, wu_ref[:, cols])
        h = (g * _sigmoid(g) * u).astype(BF16)
        part = _dot(h, wd_ref[cols, :])
        acc = part if acc is None else acc + part
    return x + 0.5 * acc


def _row_spec(width):
    return pl.BlockSpec((ROW_TILE, width), lambda i: (i, 0))


WIDE_STAGE_ROWS = 128
TALL_STAGE_ROWS = 352
_HBM = pl.BlockSpec(memory_space=pl.ANY)


def _stage_weight(w_hbm, layer, rows, stage, sem, store):
    n_rows, n_cols = w_hbm.shape[1], w_hbm.shape[2]
    assert n_rows % rows == 0

    def copy(c):
        return pltpu.make_async_copy(w_hbm.at[layer, pl.ds(c * rows, rows), :],
                                     stage.at[c % 2, pl.ds(0, rows), pl.ds(0, n_cols)], sem.at[c % 2])

    copy(0).start()
    for c in range(n_rows // rows):
        if c + 1 < n_rows // rows:
            copy(c + 1).start()
        copy(c).wait()
        store(c * rows, stage[c % 2, :rows, :n_cols])


def _stage_ffn_weights(layer, wg_hbm, wu_hbm, wd_hbm, wg_scr, wu_scr, wd_scr, wide, wide_sem, tall, tall_sem):
    def into(dst):
        def store(r0, chunk):
            dst[r0:r0 + chunk.shape[0], :] = chunk.astype(BF16)
        return store

    _stage_weight(wg_hbm, layer, WIDE_STAGE_ROWS, wide, wide_sem, into(wg_scr))
    _stage_weight(wu_hbm, layer, WIDE_STAGE_ROWS, wide, wide_sem, into(wu_scr))
    _stage_weight(wd_hbm, layer, TALL_STAGE_ROWS, tall, tall_sem, into(wd_scr))


def _ffn_weight_scratch():
    return [pltpu.VMEM((D_MODEL, D_FF), BF16), pltpu.VMEM((D_MODEL, D_FF), BF16), pltpu.VMEM((D_FF, D_MODEL), BF16),
            pltpu.VMEM((2, WIDE_STAGE_ROWS, D_IN), F32), pltpu.SemaphoreType.DMA((2,)),
            pltpu.VMEM((2, TALL_STAGE_ROWS, D_MODEL), F32), pltpu.SemaphoreType.DMA((2,))]


def _store_transposed_tiles(out_ref, a):
    a_t = a.T.astype(BF16)
    for t in range(ROW_TILE // ATT_TILE):
        out_ref[t] = a_t[:, t * ATT_TILE:(t + 1) * ATT_TILE]


def _ffn_inproj_kernel(x_ref, fnw_ref, wg_hbm, wu_hbm, wd_hbm, nw_ref, w_hbm, qkw_ref, g32_ref, gbias_ref,
                       alog_ref, convw_ref, convb_ref,
                       x_out_ref, mqt_ref, mk_ref, mvt_ref, mot_ref, gt_ref,
                       zt_ref, sb_ref, sct_ref, sxt_ref, sxdt_ref, sa_ref, qt_ref, k_ref, vt_ref,
                       halo_scr, w_scr, wg_ref, wu_ref, wd_ref, wide, wide_sem, tall, tall_sem,
                       *, tiles_per_seq, layer):
    @pl.when(pl.program_id(0) == 0)
    def _():
        _stage_ffn_weights(layer, wg_hbm, wu_hbm, wd_hbm, wg_ref, wu_ref, wd_ref, wide, wide_sem, tall, tall_sem)
        w_scr[:, _C_SMALL:_C_QK] = jnp.zeros((D_MODEL, LANES), BF16)

        def regrouped(r0, chunk):
            for dst, src, width in _regroup_plan():
                w_scr[r0:r0 + chunk.shape[0], dst:dst + width] = chunk[:, src:src + width].astype(BF16)

        _stage_weight(w_hbm, layer, WIDE_STAGE_ROWS, wide, wide_sem, regrouped)

    @pl.when(pl.program_id(0) % tiles_per_seq == 0)
    def _():
        halo_scr[...] = jnp.zeros_like(halo_scr)

    x = _ffn_block(x_ref[...], fnw_ref, wg_ref, wu_ref, wd_ref)
    x_out_ref[...] = x
    xn = _rms_rows(x, nw_ref[...]).astype(BF16)

    def proj(lo, hi):
        return _dot(xn, w_scr[:, lo:hi])

    half = (_C_V - _C_QK) // 2
    p_xbc = proj(_C_XBC, _C_SMALL)
    p_small = proj(_C_SMALL, _C_QK)
    p_q, p_k = proj(_C_QK, _C_QK + half), proj(_C_QK + half, _C_V)
    p_mq = proj(_C_M, _C_M + MLSTM_WIDTH)
    p_mk = proj(_C_M + MLSTM_WIDTH, _C_M + 2 * MLSTM_WIDTH)
    p_mv = proj(_C_M + 2 * MLSTM_WIDTH, _C_M + 3 * MLSTM_WIDTH)
    p_mo = proj(_C_M + 3 * MLSTM_WIDTH, _C_Z)
    p_z = proj(_C_Z, _C_XBC)
    p_v = proj(_C_V, _C_END)

    qk = [a * lax.rsqrt(_group_mean_sq(a, g32_ref[...]) + NORM_EPS) * qkw_ref[:, i * half:(i + 1) * half]
          for i, a in enumerate((p_q, p_k))]
    _store_transposed_tiles(qt_ref, qk[0])
    k_ref[...] = qk[1].astype(BF16)
    _store_transposed_tiles(vt_ref, p_v)

    ext = jnp.concatenate([halo_scr[...], p_xbc], axis=0)
    halo_scr[...] = p_xbc[ROW_TILE - 8:, :]
    conv = convb_ref[...] + convw_ref[SSM_CONV - 1:SSM_CONV, :] * p_xbc
    for j in range(1, SSM_CONV):
        conv = conv + convw_ref[SSM_CONV - 1 - j:SSM_CONV - j, :] * ext[8 - j:8 - j + ROW_TILE, :]
    xa = conv * _sigmoid(conv)
    sb_ref[...] = xa[:, SSM_WIDTH:SSM_WIDTH + SSM_GROUPS * SSM_STATE].astype(BF16)
    _store_transposed_tiles(sct_ref, xa[:, SSM_WIDTH + SSM_GROUPS * SSM_STATE:])
    small_t = p_small.T + gbias_ref[...]
    dt = _softplus(small_t[_LANE_DT:_LANE_DT + SSM_HEADS])
    log_decay = dt * (-jnp.exp(alog_ref[...]))
    xs_t = xa[:, :SSM_WIDTH].T
    xdt_t = xs_t * jnp.concatenate([jnp.broadcast_to(dt[h:h + 1], (SSM_HEAD_DIM, ROW_TILE))
                                    for h in range(SSM_HEADS)], axis=0)
    for t in range(ROW_TILE // MIX_CHUNK):
        cols = slice(t * MIX_CHUNK, (t + 1) * MIX_CHUNK)
        sxt_ref[t] = xs_t[:, cols].astype(BF16)
        sxdt_ref[t] = xdt_t[:, cols].astype(BF16)
        sa_ref[t] = _scan_lanes(log_decay[:, cols], jnp.add, 0.0)
    _store_transposed_tiles(zt_ref, p_z)

    _store_transposed_tiles(mqt_ref, p_mq)
    mk_ref[...] = p_mk.astype(BF16)
    _store_transposed_tiles(mvt_ref, p_mv)
    _store_transposed_tiles(mot_ref, p_mo)
    gates = small_t[:_LANE_DT]
    for t in range(ROW_TILE // MIX_CHUNK):
        i_pre = gates[_LANE_I:_LANE_I + 8, t * MIX_CHUNK:(t + 1) * MIX_CHUNK]
        f_pre = gates[_LANE_F:_LANE_F + 8, t * MIX_CHUNK:(t + 1) * MIX_CHUNK]
        log_f = jnp.minimum(f_pre, 0.0) - jnp.log(1.0 + jnp.exp(-jnp.abs(f_pre)))
        b = _scan_lanes(log_f, jnp.add, 0.0)
        c = i_pre - b
        gt_ref[t] = jnp.concatenate([c, _scan_lanes(c, jnp.maximum, -jnp.inf), b], axis=0)


def _ffn_inproj(x, ffn_norm_w, wg, wu, wd, norm_w, w_all, qkw_row, g32, gate_bias_rows, alog_rows, conv_w, conv_b_row,
                layer, seq):
    n = x.shape[0]
    tiles = ROW_TILE // ATT_TILE

    def t_out(width, dtype=BF16):
        return (pl.BlockSpec((tiles, width, ATT_TILE), lambda i: (i, 0, 0)),
                jax.ShapeDtypeStruct((n // ATT_TILE, width, ATT_TILE), dtype))

    def r_out(width, dtype=BF16):
        return _row_spec(width), jax.ShapeDtypeStruct((n, width), dtype)

    outs = [r_out(D_MODEL, F32),
            t_out(MLSTM_WIDTH), r_out(MLSTM_WIDTH), t_out(MLSTM_WIDTH), t_out(MLSTM_WIDTH), t_out(GATE_ROWS, F32),
            t_out(SSM_WIDTH), r_out(SSM_GROUPS * SSM_STATE), t_out(SSM_GROUPS * SSM_STATE), t_out(SSM_WIDTH),
            t_out(SSM_WIDTH), t_out(SSM_HEADS, F32),
            t_out(DIFF_WIDTH), r_out(DIFF_WIDTH), t_out(DIFF_WIDTH)]
    return pl.pallas_call(
        functools.partial(_ffn_inproj_kernel, tiles_per_seq=seq // ROW_TILE, layer=layer),
        grid=(n // ROW_TILE,),
        in_specs=[_row_spec(D_MODEL), _const_spec((1, D_MODEL)), _HBM, _HBM, _HBM, _const_spec((1, D_MODEL)), _HBM,
                  _const_spec((1, 2 * DIFF_WIDTH)), _const_spec((DIFF_WIDTH, DIFF_WIDTH)),
                  _const_spec((LANES, ROW_TILE)), _const_spec((SSM_HEADS, ROW_TILE)),
                  _const_spec((SSM_CONV, SSM_CONV_DIM)), _const_spec((1, SSM_CONV_DIM))],
        out_specs=[o[0] for o in outs],
        out_shape=[o[1] for o in outs],
        scratch_shapes=[pltpu.VMEM((8, SSM_CONV_DIM), F32), pltpu.VMEM((D_MODEL, _C_END), BF16)]
        + _ffn_weight_scratch(),
        compiler_params=_params(("arbitrary",)),
        name="ffn_inproj",
    )(x, ffn_norm_w.reshape(1, D_MODEL), wg, wu, wd, norm_w.reshape(1, D_MODEL), w_all, qkw_row, g32,
      gate_bias_rows, alog_rows, conv_w, conv_b_row)


def _mlstm_kernel(qt_ref, k_ref, vt_ref, ot_ref, gt_ref, nw_ref, yt_ref, ct_scr, nt_scr, m_scr):
    @pl.when(pl.program_id(1) == 0)
    def _():
        ct_scr[...] = jnp.zeros_like(ct_scr)
        nt_scr[...] = jnp.zeros_like(nt_scr)
        m_scr[...] = jnp.zeros_like(m_scr)

    for t in range(MIX_STEP_CHUNKS):
        _mlstm_chunk(t, qt_ref, k_ref, vt_ref, ot_ref, gt_ref, nw_ref, yt_ref, ct_scr, nt_scr, m_scr)


def _mlstm_chunk(t, qt_ref, k_ref, vt_ref, ot_ref, gt_ref, nw_ref, yt_ref, ct_scr, nt_scr, m_scr):
    L = MIX_CHUNK
    H = MLSTM_HEADS
    W = MLSTM_HEAD_DIM
    qt = qt_ref[t]
    ks = k_ref[t * L:(t + 1) * L, :] * (MLSTM_HEAD_DIM ** -0.5)
    vt = vt_ref[t]
    gt = gt_ref[t]
    c, c_max, b = gt[0:8], gt[8:16], gt[16:24]
    m_prev = m_scr[...]
    big_m = jnp.maximum(m_prev, c_max)
    inter = jnp.exp(m_prev - big_m)
    floor = jnp.exp(-(b + big_m))
    b_last = jnp.broadcast_to(b[:, L - 1:L], b.shape)
    m_new = b_last + jnp.maximum(m_prev, jnp.broadcast_to(c_max[:, L - 1:L], b.shape))
    w = jnp.exp(b_last + c - m_new)
    decay = jnp.exp(b_last + m_prev - m_new)

    c2_cols = jnp.concatenate([c * LOG2E, jnp.zeros((LANES - 8, L), F32)], axis=0).T
    big_m2 = big_m * LOG2E
    causal = (lax.broadcasted_iota(jnp.int32, (L, L), 0) <= lax.broadcasted_iota(jnp.int32, (L, L), 1))
    row_head = lax.broadcasted_iota(jnp.int32, (H * W, L), 0) // W
    qk = [_dot(ks, jnp.where(row_head == h, qt, jnp.zeros_like(qt))) for h in range(H)]
    nums, dens = [], []
    for h in range(H):
        d = jnp.exp2(jnp.where(causal, c2_cols[:, h:h + 1] - big_m2[h:h + 1, :], -jnp.inf))
        s = qk[h] * d
        dens.append(jnp.sum(s, axis=0, keepdims=True))
        nums.append(_dot(vt[h * W:(h + 1) * W, :], s.astype(BF16)))
    cq = _dot(ct_scr[...].astype(BF16), qt)
    nq = _dot(nt_scr[...].astype(BF16), qt)
    outs = []
    for h in range(H):
        num = nums[h] + inter[h:h + 1] * cq[h * W:(h + 1) * W]
        den = dens[h] + inter[h:h + 1] * nq[h:h + 1]
        hh = num / jnp.maximum(jnp.abs(den), floor[h:h + 1])
        outs.append(hh * lax.rsqrt(jnp.mean(hh * hh, axis=0, keepdims=True) + NORM_EPS))
    hn = jnp.concatenate(outs, axis=0) * nw_ref[...]
    yt_ref[t] = (_sigmoid(ot_ref[t].astype(F32)) * hn).astype(BF16)

    w_full = jnp.concatenate([jnp.broadcast_to(w[h:h + 1], (W, L)) for h in range(H)], axis=0)
    g_new = _dot((vt.astype(F32) * w_full).astype(BF16), ks)
    decay_full = jnp.concatenate([jnp.broadcast_to(decay[h:h + 1, :1], (W, H * W)) for h in range(H)], axis=0)
    same_head = (lax.broadcasted_iota(jnp.int32, (H * W, H * W), 0) // W
                 == lax.broadcasted_iota(jnp.int32, (H * W, H * W), 1) // W)
    ct_scr[...] = decay_full * ct_scr[...] + jnp.where(same_head, g_new, 0.0)
    w16 = jnp.concatenate([w, jnp.zeros_like(w)], axis=0).astype(BF16)
    n_new = _dot(w16, ks)
    own_lanes = (lax.broadcasted_iota(jnp.int32, (16, H * W), 0)
                 == lax.broadcasted_iota(jnp.int32, (16, H * W), 1) // W)
    decay16 = jnp.broadcast_to(jnp.concatenate([decay[:, :1], jnp.zeros((8, 1), F32)], axis=0), (16, H * W))
    nt_scr[...] = decay16 * nt_scr[...] + jnp.where(own_lanes, n_new, 0.0)
    m_scr[...] = m_new


def _mlstm(mqt, mk, mvt, mot, gt, norm_w_rows, batch, seq):
    ns = seq // (MIX_CHUNK * MIX_STEP_CHUNKS)
    n = batch * seq
    t_spec = pl.BlockSpec((MIX_STEP_CHUNKS, MLSTM_WIDTH, MIX_CHUNK), lambda b, c: (b * ns + c, 0, 0))
    return pl.pallas_call(
        _mlstm_kernel,
        grid=(batch, ns),
        in_specs=[t_spec, pl.BlockSpec((MIX_STEP_CHUNKS * MIX_CHUNK, MLSTM_WIDTH), lambda b, c: (b * ns + c, 0)),
                  t_spec, t_spec,
                  pl.BlockSpec((MIX_STEP_CHUNKS, GATE_ROWS, MIX_CHUNK), lambda b, c: (b * ns + c, 0, 0)),
                  _const_spec((MLSTM_WIDTH, MIX_CHUNK))],
        out_specs=t_spec,
        out_shape=jax.ShapeDtypeStruct((n // MIX_CHUNK, MLSTM_WIDTH, MIX_CHUNK), BF16),
        scratch_shapes=[pltpu.VMEM((MLSTM_WIDTH, MLSTM_WIDTH), F32),
                        pltpu.VMEM((16, MLSTM_WIDTH), F32),
                        pltpu.VMEM((8, MIX_CHUNK), F32)],
        compiler_params=_params(("arbitrary", "arbitrary")),
        name="mlstm",
    )(mqt, mk, mvt, mot, gt, norm_w_rows)


def _ssd_kernel(b_ref, ct_ref, xt_ref, xdt_ref, zt_ref, a_ref, dskip_ref, nw_ref, yt_ref, st_scr):
    @pl.when(pl.program_id(1) == 0)
    def _():
        st_scr[...] = jnp.zeros_like(st_scr)

    for t in range(MIX_STEP_CHUNKS):
        _ssd_chunk(t, b_ref, ct_ref, xt_ref, xdt_ref, zt_ref, a_ref, dskip_ref, nw_ref, yt_ref, st_scr)


def _ssd_chunk(t, b_ref, ct_ref, xt_ref, xdt_ref, zt_ref, a_ref, dskip_ref, nw_ref, yt_ref, st_scr):
    L = MIX_CHUNK
    P = SSM_HEAD_DIM
    HG = SSM_HEADS // SSM_GROUPS
    GW = HG * P
    NS = SSM_STATE
    a = a_ref[t]
    a_last = jnp.broadcast_to(a[:, L - 1:L], a.shape)
    exp_a = jnp.exp(a)
    d_in = jnp.exp(a_last - a)
    chunk_decay = jnp.exp(a_last)
    a2 = a * LOG2E
    a2_cols = jnp.concatenate([a2, jnp.zeros((LANES - SSM_HEADS, L), F32)], axis=0).T
    causal = (lax.broadcasted_iota(jnp.int32, (L, L), 0) <= lax.broadcasted_iota(jnp.int32, (L, L), 1))
    bn = b_ref[t * L:(t + 1) * L, :]
    ct = ct_ref[t]
    xdt = xdt_ref[t]

    def per_head_rows(rows, grp, width):
        return jnp.concatenate([jnp.broadcast_to(rows[grp * HG + i:grp * HG + i + 1, :width], (P, width))
                                for i in range(HG)], axis=0)

    scores, y_off = [], []
    for grp in range(SSM_GROUPS):
        bg = bn[:, grp * NS:(grp + 1) * NS]
        cg = ct[grp * NS:(grp + 1) * NS, :]
        scores.append(_dot(bg, cg))
        st = st_scr[grp]
        y_off.append(_dot(st.astype(BF16), cg))
        x_in = (xdt[grp * GW:(grp + 1) * GW, :].astype(F32) * per_head_rows(d_in, grp, L)).astype(BF16)
        st_scr[grp] = per_head_rows(chunk_decay, grp, NS) * st + _dot(x_in, bg)
    for grp in range(SSM_GROUPS):
        y_diag = []
        for i in range(HG):
            h = grp * HG + i
            seg = jnp.exp2(jnp.where(causal, a2[h:h + 1, :] - a2_cols[:, h:h + 1], -jnp.inf))
            p = (scores[grp] * seg).astype(BF16)
            y_diag.append(_dot(xdt[h * P:(h + 1) * P, :], p))
        rows = slice(grp * GW, (grp + 1) * GW)
        y = (jnp.concatenate(y_diag, axis=0) + y_off[grp] * per_head_rows(exp_a, grp, L)
             + dskip_ref[rows, :] * xt_ref[t, rows, :].astype(F32))
        zg = zt_ref[t, rows, :].astype(F32)
        y = y * (zg * _sigmoid(zg))
        ms = jnp.mean(y * y, axis=0, keepdims=True)
        yt_ref[t, rows, :] = (y * lax.rsqrt(ms + NORM_EPS) * nw_ref[rows, :]).astype(BF16)


def _ssd(sb, sct, sxt, sxdt, zt, sa, dskip_rows, norm_w_rows, batch, seq):
    ns = seq // (MIX_CHUNK * MIX_STEP_CHUNKS)
    n = batch * seq

    def t_spec(width):
        return pl.BlockSpec((MIX_STEP_CHUNKS, width, MIX_CHUNK), lambda b, c: (b * ns + c, 0, 0))

    return pl.pallas_call(
        _ssd_kernel,
        grid=(batch, ns),
        in_specs=[pl.BlockSpec((MIX_STEP_CHUNKS * MIX_CHUNK, SSM_GROUPS * SSM_STATE), lambda b, c: (b * ns + c, 0)),
                  t_spec(SSM_GROUPS * SSM_STATE), t_spec(SSM_WIDTH), t_spec(SSM_WIDTH), t_spec(SSM_WIDTH),
                  t_spec(SSM_HEADS), _const_spec((SSM_WIDTH, MIX_CHUNK)), _const_spec((SSM_WIDTH, MIX_CHUNK))],
        out_specs=t_spec(SSM_WIDTH),
        out_shape=jax.ShapeDtypeStruct((n // MIX_CHUNK, SSM_WIDTH, MIX_CHUNK), BF16),
        scratch_shapes=[pltpu.VMEM((SSM_GROUPS, SSM_WIDTH // SSM_GROUPS, SSM_STATE), F32)],
        compiler_params=_params(("arbitrary", "arbitrary")),
        name="ssd",
    )(sb, sct, sxt, sxdt, zt, sa, dskip_rows, norm_w_rows)


def _t5_bias_tile(rel, relb_ref, head):
    n = jnp.maximum(rel, 0)
    max_exact = REL_BUCKETS // 2
    nf = jnp.maximum(n, 1).astype(F32)
    large = max_exact + (jnp.log(nf / max_exact) / math.log(REL_MAX_DIST / max_exact)
                         * (REL_BUCKETS - max_exact)).astype(jnp.int32)
    large = jnp.minimum(large, REL_BUCKETS - 1)
    bucket = jnp.where(n < max_exact, n, large)
    far = relb_ref[REL_BUCKETS - 1, head]
    bias = jnp.zeros(rel.shape, F32)
    for bkt in range(REL_BUCKETS - 1):
        bias = jnp.where(bucket == bkt, (relb_ref[bkt, head] - far) * LOG2E, bias)
    return jnp.where(rel >= 0, bias, NEG_BIG)


def _attn_kernel(relb_ref, qt_ref, k_ref, vt_ref, lam_ref, subw_ref, g64_ref,
                 y_ref, bias_scr, qm_scr, m_scr, acc_scr, sa_scr, sb_scr, *, lam_init):
    T = ATT_TILE
    H = DIFF_HEADS
    DV = DIFF_V_DIM
    R = ATT_STEP_TILES
    step_idx = pl.program_id(1)

    @pl.when((pl.program_id(0) == 0) & (step_idx == 0))
    def _():
        rel = lax.broadcasted_iota(jnp.int32, (T, T), 1) - lax.broadcasted_iota(jnp.int32, (T, T), 0)
        for h in range(H):
            bias_scr[0, h] = _t5_bias_tile(rel, relb_ref, h)
            bias_scr[1, h] = _t5_bias_tile(rel + T, relb_ref, h)

    comp_of_row = lax.broadcasted_iota(jnp.int32, (DIFF_WIDTH, T), 0) // DIFF_QK_DIM
    for r in range(R):
        qt = qt_ref[r]
        for hc in range(2 * H):
            qm_scr[r, hc] = jnp.where(comp_of_row == hc, qt, jnp.zeros_like(qt))
    m_scr[...] = jnp.full(m_scr.shape, NEG_BIG, F32)
    acc_scr[...] = jnp.zeros_like(acc_scr)
    ones_rows = jnp.ones((ONES_ROWS, T), BF16)

    def key_tile(j):
        return k_ref[pl.ds(pl.multiple_of(j * T, T), T), :]

    def logits_one(s_ref, kt, r, hc):
        s_ref[hc] = _dot(kt, qm_scr[r, hc])

    def consume_one(s_ref, vt, r, hc, bias_idx):
        h = hc // 2
        s = s_ref[hc]
        if bias_idx is not None:
            s = bias_scr[bias_idx, h] + s
        s = s.astype(BF16)
        m_prev = m_scr[r, hc]
        m_next = jnp.maximum(m_prev, jnp.max(s, axis=0, keepdims=True).astype(F32))
        alpha = jnp.exp2(m_prev - m_next)
        p = jnp.exp2(s - m_next.astype(BF16))
        m_scr[r, hc] = m_next
        v_aug = jnp.concatenate([vt[h * DV:(h + 1) * DV, :], ones_rows], axis=0)
        acc_scr[r, hc] = alpha * acc_scr[r, hc] + _dot(v_aug, p)

    LEAD = 2

    def step(s_next, j_next, r_next, s_cur, j_cur, r_cur, bias_idx):
        kt = key_tile(j_next)
        vt = vt_ref[j_cur]
        for hc in range(LEAD):
            logits_one(s_next, kt, r_next, hc)
        for hc in range(2 * H):
            if hc + LEAD < 2 * H:
                logits_one(s_next, kt, r_next, hc + LEAD)
            consume_one(s_cur, vt, r_cur, hc, bias_idx)

    DIAG, NEAR = 0, 1

    def far_pairs(count, cur, oth, r):
        def pair(j):
            step(oth, j + 1, r, cur, j, r, None)
            step(cur, j + 2, r, oth, j + 1, r, None)

        def body(t, carry):
            pair(4 * t)
            pair(4 * t + 2)
            return carry

        lax.fori_loop(0, count // 2, body, 0)

        @pl.when(count % 2 == 1)
        def _():
            pair(2 * (count - 1))

    cur, oth = sa_scr, sb_scr
    kt0 = key_tile(0)
    for hc in range(2 * H):
        logits_one(cur, kt0, 0, hc)
    for r in range(R):
        d = R * step_idx + r
        if r % 2 == 1:
            far_pairs((d - 1) // 2, cur, oth, r)
            step(oth, d, r, cur, d - 1, r, NEAR)
            diag, free = oth, cur
        else:
            far_pairs(jnp.maximum(d // 2 - 1, 0), cur, oth, r)

            def near_tiles(d=d, cur=cur, oth=oth, r=r):
                step(oth, d - 1, r, cur, d - 2, r, None)
                step(cur, d, r, oth, d - 1, r, NEAR)

            if r == 0:
                pl.when(d >= 1)(near_tiles)
            else:
                near_tiles()
            diag, free = cur, oth
        if r + 1 < R:
            step(free, 0, r + 1, diag, d, r, DIAG)
            cur, oth = free, diag
        else:
            vt_last = vt_ref[d]
            for hc in range(2 * H):
                consume_one(diag, vt_last, r, hc, DIAG)

    lam_p = lam_ref[...]
    lam = (jnp.exp(jnp.sum(lam_p[0:1] * lam_p[1:2], axis=1, keepdims=True))
           - jnp.exp(jnp.sum(lam_p[2:3] * lam_p[3:4], axis=1, keepdims=True)) + lam_init)
    for r in range(R):
        def normalised(hc):
            return acc_scr[r, hc, :DV, :] / acc_scr[r, hc, DV:DV + 1, :]

        o_t = jnp.concatenate([normalised(2 * h) - lam * normalised(2 * h + 1) for h in range(H)], axis=0)
        o = o_t.T
        ms = _group_mean_sq(o, g64_ref[...])
        y_ref[r * T:(r + 1) * T, :] = (o * lax.rsqrt(ms + NORM_EPS) * subw_ref[...] * (1.0 - lam_init)).astype(BF16)


def _diff_attn(qt, kn, vt, rel_bias, lambdas, subw_row, g64, lam_init, batch, seq):
    nk = seq // ATT_TILE
    nsteps = nk // ATT_STEP_TILES
    n = batch * seq
    return pl.pallas_call(
        functools.partial(_attn_kernel, lam_init=lam_init),
        grid=(batch, nsteps),
        in_specs=[pl.BlockSpec(memory_space=pltpu.SMEM),
                  pl.BlockSpec((ATT_STEP_TILES, DIFF_WIDTH, ATT_TILE), lambda b, i: (b * nsteps + i, 0, 0)),
                  pl.BlockSpec((seq, DIFF_WIDTH), lambda b, i: (b, 0)),
                  pl.BlockSpec((nk, DIFF_WIDTH, ATT_TILE), lambda b, i: (b, 0, 0)),
                  _const_spec((4, DIFF_QK_DIM)), _const_spec((1, DIFF_WIDTH)),
                  _const_spec((DIFF_WIDTH, DIFF_WIDTH))],
        out_specs=pl.BlockSpec((ATT_STEP_TILES * ATT_TILE, DIFF_WIDTH), lambda b, i: (b * nsteps + i, 0)),
        out_shape=jax.ShapeDtypeStruct((n, DIFF_WIDTH), BF16),
        scratch_shapes=[pltpu.VMEM((2, DIFF_HEADS, ATT_TILE, ATT_TILE), F32),
                        pltpu.VMEM((ATT_STEP_TILES, 2 * DIFF_HEADS, DIFF_WIDTH, ATT_TILE), BF16),
                        pltpu.VMEM((ATT_STEP_TILES, 2 * DIFF_HEADS, 1, ATT_TILE), F32),
                        pltpu.VMEM((ATT_STEP_TILES, 2 * DIFF_HEADS, DIFF_V_DIM + ONES_ROWS, ATT_TILE), F32),
                        pltpu.VMEM((2 * DIFF_HEADS, ATT_TILE, ATT_TILE), F32),
                        pltpu.VMEM((2 * DIFF_HEADS, ATT_TILE, ATT_TILE), F32)],
        compiler_params=_params(("arbitrary", "arbitrary")),
        name="diff_attn",
    )(rel_bias, qt, kn, vt, lambdas, subw_row, g64)


def _untransposed(yt_ref):
    return jnp.concatenate([yt_ref[t].astype(F32).T for t in range(yt_ref.shape[0])], axis=0).astype(BF16)


def _outproj_ffn_kernel(x_ref, ymt_ref, yst_ref, yd_ref, w_hbm, fnw_ref, wg_hbm, wu_hbm, wd_hbm, o_ref,
                        w_ref, wg_ref, wu_ref, wd_ref, wide, wide_sem, tall, tall_sem, *, layer):
    @pl.when(pl.program_id(0) == 0)
    def _():
        def store(r0, chunk):
            w_ref[r0:r0 + chunk.shape[0], :] = chunk.astype(BF16)

        _stage_weight(w_hbm, layer, D_MODEL // 4, tall, tall_sem, store)
        _stage_ffn_weights(layer, wg_hbm, wu_hbm, wd_hbm, wg_ref, wu_ref, wd_ref, wide, wide_sem, tall, tall_sem)

    y = jnp.concatenate([_untransposed(ymt_ref), _untransposed(yst_ref), yd_ref[...]], axis=1)
    x = x_ref[...] + _dot(y, w_ref[...])
    o_ref[...] = _ffn_block(x, fnw_ref, wg_ref, wu_ref, wd_ref)


def _outproj_ffn(x, y_m, y_s, y_d, w_out, ffn_norm_w, wg, wu, wd, layer):
    n = x.shape[0]
    return pl.pallas_call(
        functools.partial(_outproj_ffn_kernel, layer=layer),
        grid=(n // ROW_TILE,),
        in_specs=[_row_spec(D_MODEL),
                  pl.BlockSpec((ROW_TILE // MIX_CHUNK, MLSTM_WIDTH, MIX_CHUNK), lambda i: (i, 0, 0)),
                  pl.BlockSpec((ROW_TILE // MIX_CHUNK, SSM_WIDTH, MIX_CHUNK), lambda i: (i, 0, 0)),
                  _row_spec(DIFF_WIDTH), _HBM, _const_spec((1, D_MODEL)), _HBM, _HBM, _HBM],
        out_specs=_row_spec(D_MODEL),
        out_shape=jax.ShapeDtypeStruct((n, D_MODEL), F32),
        scratch_shapes=[pltpu.VMEM((D_MODEL, D_MODEL), BF16)] + _ffn_weight_scratch(),
        compiler_params=_params(("arbitrary",)),
        name="outproj_ffn",
    )(x, y_m, y_s, y_d, w_out, ffn_norm_w.reshape(1, D_MODEL), wg, wu, wd)


def _small_row(values_by_lane):
    row = jnp.zeros((LANES,), F32)
    for lane, vals in values_by_lane:
        row = lax.dynamic_update_slice(row, vals.astype(F32), (lane,))
    return row.reshape(1, LANES)


def kernel(x, ffn1_norm_w, ffn1_w_gate, ffn1_w_up, ffn1_w_down, mix_norm_w, w_in, mlstm_gate_bias, mlstm_norm_w, ssm_conv_w, ssm_conv_b, ssm_dt_bias, ssm_A_log, ssm_D, ssm_norm_w, diff_q_norm_w, diff_k_norm_w, diff_lambda, diff_subln_w, rel_bias, w_out, ffn2_norm_w, ffn2_w_gate, ffn2_w_up, ffn2_w_down):
    batch, seq, d = x.shape
    assert d == D_MODEL and MIX_CHUNK == ATT_TILE and seq % ROW_TILE == 0 and ROW_TILE % MIX_CHUNK == 0
    assert seq % (MIX_CHUNK * MIX_STEP_CHUNKS) == 0 and seq % (ATT_STEP_TILES * ATT_TILE) == 0 and ATT_STEP_TILES % 2 == 0
    xf = x.reshape(batch * seq, D_MODEL)
    g32 = _block_diag(DIFF_WIDTH, DIFF_QK_DIM, 1.0 / DIFF_QK_DIM)
    g64 = _block_diag(DIFF_WIDTH, DIFF_V_DIM, 1.0 / DIFF_V_DIM)
    ffn1 = (ffn1_w_gate, ffn1_w_up, ffn1_w_down)
    ffn2 = (ffn2_w_gate, ffn2_w_up, ffn2_w_down)
    for l in range(DEPTH):
        qkw_row = jnp.concatenate([jnp.tile(diff_q_norm_w[l].reshape(-1), DIFF_HEADS) * (DIFF_QK_DIM ** -0.5 * LOG2E),
                                   jnp.tile(diff_k_norm_w[l].reshape(-1), DIFF_HEADS)]).reshape(1, 2 * DIFF_WIDTH)
        bias_row = _small_row([(_LANE_I, mlstm_gate_bias[l, 0]), (_LANE_F, mlstm_gate_bias[l, 1]),
                               (_LANE_DT, ssm_dt_bias[l])])
        (xf, mqt, mk, mvt, mot, gt, zt, sb, sct, sxt, sxdt, sa, qt, kn, vt) = _ffn_inproj(
            xf, ffn1_norm_w[l], *ffn1, mix_norm_w[l], w_in, qkw_row, g32,
            jnp.broadcast_to(bias_row.reshape(LANES, 1), (LANES, ROW_TILE)),
            jnp.broadcast_to(ssm_A_log[l][:, None], (SSM_HEADS, ROW_TILE)),
            ssm_conv_w[l], ssm_conv_b[l].reshape(1, SSM_CONV_DIM), l, seq)
        y_m = _mlstm(mqt, mk, mvt, mot, gt, jnp.broadcast_to(mlstm_norm_w[l][:, None], (MLSTM_WIDTH, MIX_CHUNK)),
                     batch, seq)
        y_s = _ssd(sb, sct, sxt, sxdt, zt, sa,
                   jnp.broadcast_to(jnp.repeat(ssm_D[l], SSM_HEAD_DIM)[:, None], (SSM_WIDTH, MIX_CHUNK)),
                   jnp.broadcast_to(ssm_norm_w[l][:, None], (SSM_WIDTH, MIX_CHUNK)), batch, seq)
        lam_init = 0.8 - 0.6 * math.exp(-0.3 * l)
        y_d = _diff_attn(qt, kn, vt, rel_bias, diff_lambda[l], jnp.tile(diff_subln_w[l], DIFF_HEADS).reshape(1, DIFF_WIDTH),
                         g64, lam_init, batch, seq)
        xf = _outproj_ffn(xf, y_m, y_s, y_d, w_out, ffn2_norm_w[l], *ffn2, l)
    return xf.reshape(batch, seq, D_MODEL)
```

```python
import functools
import math

import jax
import jax.numpy as jnp
from jax import lax
from jax.experimental import pallas as pl
from jax.experimental.pallas import tpu as pltpu

F32 = jnp.float32
BF16 = jnp.bfloat16

D_MODEL = 1024
DEPTH = 2
D_FF = 2816
MLSTM_HEADS = 4
MLSTM_HEAD_DIM = 64
MLSTM_WIDTH = MLSTM_HEADS * MLSTM_HEAD_DIM
SSM_HEADS = 8
SSM_HEAD_DIM = 64
SSM_WIDTH = SSM_HEADS * SSM_HEAD_DIM
SSM_STATE = 128
SSM_GROUPS = 2
SSM_CONV = 4
SSM_CONV_DIM = SSM_WIDTH + 2 * SSM_GROUPS * SSM_STATE
DIFF_HEADS = 4
DIFF_QK_DIM = 32
DIFF_V_DIM = 64
DIFF_WIDTH = DIFF_HEADS * DIFF_V_DIM
REL_BUCKETS = 32
REL_MAX_DIST = 128
NORM_EPS = 1e-6

LANES = 128
ROW_TILE = 512
MXU_COLS = 256
CONV_COLS = 256
FFN_SPLITS = (0, 6 * MXU_COLS, D_FF)
MIX_CHUNK = 256
MIX_STEP_CHUNKS = 8
ATT_TILE = 256
ATT_STEP_TILES = 4
VMEM_LIMIT = 56 * 1024 * 1024
NEG_BIG = -1e30
LOG2E = math.log2(math.e)
ONES_ROWS = 16

_C_M = 0
_C_Z = _C_M + 4 * MLSTM_WIDTH
_C_XBC = _C_Z + SSM_WIDTH
_C_SMALL = _C_XBC + SSM_CONV_DIM
_C_QK = _C_SMALL + LANES
_C_V = _C_QK + 4 * DIFF_HEADS * DIFF_QK_DIM
_C_END = _C_V + DIFF_WIDTH
_LANE_I = 0
_LANE_F = 8
_LANE_DT = 16
GATE_ROWS = 24
D_IN = 4 * MLSTM_WIDTH + 2 * MLSTM_HEADS + SSM_WIDTH + SSM_CONV_DIM + SSM_HEADS + 3 * DIFF_WIDTH


def _regroup_plan():
    sizes = (4 * MLSTM_WIDTH, MLSTM_HEADS, MLSTM_HEADS, SSM_WIDTH, SSM_CONV_DIM, SSM_HEADS, 3 * DIFF_WIDTH)
    dests = (_C_M, _C_SMALL + _LANE_I, _C_SMALL + _LANE_F, _C_Z, _C_XBC, _C_SMALL + _LANE_DT, _C_QK)
    plan, src = [], 0
    for dst, width in zip(dests, sizes):
        plan.append((dst, src, width))
        src += width
    assert src == D_IN
    return plan


def _dot(a, b):
    return jnp.dot(a, b, preferred_element_type=F32)


def _sigmoid(x):
    return 1.0 / (1.0 + jnp.exp(-x))


def _softplus(x):
    return jnp.maximum(x, 0.0) + jnp.log(1.0 + jnp.exp(-jnp.abs(x)))


def _rms_rows(x, w_row):
    ms = jnp.mean(x * x, axis=-1, keepdims=True)
    return x * lax.rsqrt(ms + NORM_EPS) * w_row


def _group_mean_sq(x, gmat):
    sq = x * x
    hi = sq.astype(BF16)
    lo = (sq - hi.astype(F32)).astype(BF16)
    return _dot(hi, gmat) + _dot(lo, gmat)


def _scan_lanes(x, op, identity):
    n = x.shape[1]
    lane = lax.broadcasted_iota(jnp.int32, x.shape, 1)
    step = 1
    while step < n:
        x = op(x, jnp.where(lane >= step, pltpu.roll(x, step, axis=1), identity))
        step *= 2
    return x


def _block_diag(n, group, value):
    r = jnp.arange(n) // group
    return jnp.where(r[:, None] == r[None, :], value, 0.0).astype(BF16)


def _const_spec(shape):
    nd = len(shape)
    return pl.BlockSpec(shape, lambda *_: (0,) * nd)


def _params(sem):
    return pltpu.CompilerParams(dimension_semantics=sem, vmem_limit_bytes=VMEM_LIMIT)


def _ffn_block(x, nw_ref, wg_ref, wu_ref, wd_ref):
    xn = _rms_rows(x, nw_ref[...]).astype(BF16)
    acc = None
    for lo, hi in zip(FFN_SPLITS[:-1], FFN_SPLITS[1:]):
        assert (hi - lo) % MXU_COLS == 0
        cols = slice(lo, hi)
        g = _dot(xn, wg_ref[:, cols])
        u = _dot(xn, wu_ref[:, cols])
        h = (g * _sigmoid(g) * u).astype(BF16)
        part = _dot(h, wd_ref[cols, :])
        acc = part if acc is None else acc + part
    return x + 0.5 * acc


def _row_spec(width):
    return pl.BlockSpec((ROW_TILE, width), lambda i: (i, 0))


WIDE_STAGE_ROWS = 128
TALL_STAGE_ROWS = 352
_HBM = pl.BlockSpec(memory_space=pl.ANY)


def _stage_weight(w_hbm, layer, rows, stage, sem, store):
    n_rows, n_cols = w_hbm.shape[1], w_hbm.shape[2]
    assert n_rows % rows == 0

    def copy(c):
        return pltpu.make_async_copy(w_hbm.at[layer, pl.ds(c * rows, rows), :],
                                     stage.at[c % 2, pl.ds(0, rows), pl.ds(0, n_cols)], sem.at[c % 2])

    copy(0).start()
    for c in range(n_rows // rows):
        if c + 1 < n_rows // rows:
            copy(c + 1).start()
        copy(c).wait()
        store(c * rows, stage[c % 2, :rows, :n_cols])


def _stage_ffn_weights(layer, wg_hbm, wu_hbm, wd_hbm, wg_scr, wu_scr, wd_scr, wide, wide_sem, tall, tall_sem):
    def into(dst):
        def store(r0, chunk):
            dst[r0:r0 + chunk.shape[0], :] = chunk.astype(BF16)
        return store

    _stage_weight(wg_hbm, layer, WIDE_STAGE_ROWS, wide, wide_sem, into(wg_scr))
    _stage_weight(wu_hbm, layer, WIDE_STAGE_ROWS, wide, wide_sem, into(wu_scr))
    _stage_weight(wd_hbm, layer, TALL_STAGE_ROWS, tall, tall_sem, into(wd_scr))


def _ffn_weight_scratch():
    return [pltpu.VMEM((D_MODEL, D_FF), BF16), pltpu.VMEM((D_MODEL, D_FF), BF16), pltpu.VMEM((D_FF, D_MODEL), BF16),
            pltpu.VMEM((2, WIDE_STAGE_ROWS, D_IN), F32), pltpu.SemaphoreType.DMA((2,)),
            pltpu.VMEM((2, TALL_STAGE_ROWS, D_MODEL), F32), pltpu.SemaphoreType.DMA((2,))]


def _store_transposed_tiles(out_ref, a):
    a_t = a.T.astype(BF16)
    for t in range(ROW_TILE // ATT_TILE):
        out_ref[t] = a_t[:, t * ATT_TILE:(t + 1) * ATT_TILE]


def _ffn_inproj_kernel(x_ref, fnw_ref, wg_hbm, wu_hbm, wd_hbm, nw_ref, w_hbm, qkw_ref, g32_ref, gbias_ref,
                       alog_ref, convw_ref, convb_ref,
                       x_out_ref, mqt_ref, mk_ref, mvt_ref, mot_ref, gt_ref,
                       zt_ref, sb_ref, sct_ref, sxt_ref, sxdt_ref, sa_ref, qt_ref, k_ref, vt_ref,
                       halo_scr, w_scr, wg_ref, wu_ref, wd_ref, wide, wide_sem, tall, tall_sem,
                       *, tiles_per_seq, layer):
    @pl.when(pl.program_id(0) == 0)
    def _():
        _stage_ffn_weights(layer, wg_hbm, wu_hbm, wd_hbm, wg_ref, wu_ref, wd_ref, wide, wide_sem, tall, tall_sem)
        w_scr[:, _C_SMALL:_C_QK] = jnp.zeros((D_MODEL, LANES), BF16)

        def regrouped(r0, chunk):
            for dst, src, width in _regroup_plan():
                w_scr[r0:r0 + chunk.shape[0], dst:dst + width] = chunk[:, src:src + width].astype(BF16)

        _stage_weight(w_hbm, layer, WIDE_STAGE_ROWS, wide, wide_sem, regrouped)

    @pl.when(pl.program_id(0) % tiles_per_seq == 0)
    def _():
        halo_scr[...] = jnp.zeros_like(halo_scr)

    x = _ffn_block(x_ref[...], fnw_ref, wg_ref, wu_ref, wd_ref)
    x_out_ref[...] = x
    xn = _rms_rows(x, nw_ref[...]).astype(BF16)

    def proj(lo, hi):
        return _dot(xn, w_scr[:, lo:hi])

    half = (_C_V - _C_QK) // 2
    p_xbc = [proj(_C_XBC + c * CONV_COLS, _C_XBC + (c + 1) * CONV_COLS) for c in range(SSM_CONV_DIM // CONV_COLS)]
    p_small = proj(_C_SMALL, _C_QK)
    p_q, p_k = proj(_C_QK, _C_QK + half), proj(_C_QK + half, _C_V)
    p_mq = proj(_C_M, _C_M + MLSTM_WIDTH)
    p_mk = proj(_C_M + MLSTM_WIDTH, _C_M + 2 * MLSTM_WIDTH)
    p_mv = proj(_C_M + 2 * MLSTM_WIDTH, _C_M + 3 * MLSTM_WIDTH)
    p_mo = proj(_C_M + 3 * MLSTM_WIDTH, _C_Z)
    p_z = proj(_C_Z, _C_XBC)
    p_v = proj(_C_V, _C_END)

    qk = [a * lax.rsqrt(_group_mean_sq(a, g32_ref[...]) + NORM_EPS) * qkw_ref[:, i * half:(i + 1) * half]
          for i, a in enumerate((p_q, p_k))]
    _store_transposed_tiles(qt_ref, qk[0])
    k_ref[...] = qk[1].astype(BF16)
    _store_transposed_tiles(vt_ref, p_v)

    small_t = p_small.T + gbias_ref[...]
    dt = _softplus(small_t[_LANE_DT:_LANE_DT + SSM_HEADS])
    log_decay = dt * (-jnp.exp(alog_ref[...]))
    for t in range(ROW_TILE // MIX_CHUNK):
        cols = slice(t * MIX_CHUNK, (t + 1) * MIX_CHUNK)
        sa_ref[t] = _scan_lanes(log_decay[:, cols], jnp.add, 0.0)
    heads_per_piece = CONV_COLS // SSM_HEAD_DIM
    for c, xp in enumerate(p_xbc):
        cs = slice(c * CONV_COLS, (c + 1) * CONV_COLS)
        ext = jnp.concatenate([halo_scr[:, cs], xp], axis=0)
        halo_scr[:, cs] = xp[ROW_TILE - 8:, :]
        conv = convb_ref[:, cs] + convw_ref[SSM_CONV - 1:SSM_CONV, cs] * xp
        for j in range(1, SSM_CONV):
            conv = conv + convw_ref[SSM_CONV - 1 - j:SSM_CONV - j, cs] * ext[8 - j:8 - j + ROW_TILE, :]
        xa = conv * _sigmoid(conv)
        if cs.stop <= SSM_WIDTH:
            xs_t = xa.T
            xdt_t = xs_t * jnp.concatenate(
                [jnp.broadcast_to(dt[h:h + 1], (SSM_HEAD_DIM, ROW_TILE))
                 for h in range(c * heads_per_piece, (c + 1) * heads_per_piece)], axis=0)
            for t in range(ROW_TILE // MIX_CHUNK):
                cols = slice(t * MIX_CHUNK, (t + 1) * MIX_CHUNK)
                sxt_ref[t, cs, :] = xs_t[:, cols].astype(BF16)
                sxdt_ref[t, cs, :] = xdt_t[:, cols].astype(BF16)
        elif cs.stop <= SSM_WIDTH + SSM_GROUPS * SSM_STATE:
            sb_ref[:, cs.start - SSM_WIDTH:cs.stop - SSM_WIDTH] = xa.astype(BF16)
        else:
            lo = cs.start - SSM_WIDTH - SSM_GROUPS * SSM_STATE
            xa_t = xa.T.astype(BF16)
            for t in range(ROW_TILE // MIX_CHUNK):
                sct_ref[t, lo:lo + CONV_COLS, :] = xa_t[:, t * MIX_CHUNK:(t + 1) * MIX_CHUNK]
    _store_transposed_tiles(zt_ref, p_z)

    _store_transposed_tiles(mqt_ref, p_mq)
    mk_ref[...] = p_mk.astype(BF16)
    _store_transposed_tiles(mvt_ref, p_mv)
    _store_transposed_tiles(mot_ref, p_mo)
    gates = small_t[:_LANE_DT]
    for t in range(ROW_TILE // MIX_CHUNK):
        i_pre = gates[_LANE_I:_LANE_I + 8, t * MIX_CHUNK:(t + 1) * MIX_CHUNK]
        f_pre = gates[_LANE_F:_LANE_F + 8, t * MIX_CHUNK:(t + 1) * MIX_CHUNK]
        log_f = jnp.minimum(f_pre, 0.0) - jnp.log(1.0 + jnp.exp(-jnp.abs(f_pre)))
        b = _scan_lanes(log_f, jnp.add, 0.0)
        c = i_pre - b
        gt_ref[t] = jnp.concatenate([c, _scan_lanes(c, jnp.maximum, -jnp.inf), b], axis=0)


def _ffn_inproj(x, ffn_norm_w, wg, wu, wd, norm_w, w_all, qkw_row, g32, gate_bias_rows, alog_rows, conv_w, conv_b_row,
                layer, seq):
    n = x.shape[0]
    tiles = ROW_TILE // ATT_TILE

    def t_out(width, dtype=BF16):
        return (pl.BlockSpec((tiles, width, ATT_TILE), lambda i: (i, 0, 0)),
                jax.ShapeDtypeStruct((n // ATT_TILE, width, ATT_TILE), dtype))

    def r_out(width, dtype=BF16):
        return _row_spec(width), jax.ShapeDtypeStruct((n, width), dtype)

    outs = [r_out(D_MODEL, F32),
            t_out(MLSTM_WIDTH), r_out(MLSTM_WIDTH), t_out(MLSTM_WIDTH), t_out(MLSTM_WIDTH), t_out(GATE_ROWS, F32),
            t_out(SSM_WIDTH), r_out(SSM_GROUPS * SSM_STATE), t_out(SSM_GROUPS * SSM_STATE), t_out(SSM_WIDTH),
            t_out(SSM_WIDTH), t_out(SSM_HEADS, F32),
            t_out(DIFF_WIDTH), r_out(DIFF_WIDTH), t_out(DIFF_WIDTH)]
    return pl.pallas_call(
        functools.partial(_ffn_inproj_kernel, tiles_per_seq=seq // ROW_TILE, layer=layer),
        grid=(n // ROW_TILE,),
        in_specs=[_row_spec(D_MODEL), _const_spec((1, D_MODEL)), _HBM, _HBM, _HBM, _const_spec((1, D_MODEL)), _HBM,
                  _const_spec((1, 2 * DIFF_WIDTH)), _const_spec((DIFF_WIDTH, DIFF_WIDTH)),
                  _const_spec((LANES, ROW_TILE)), _const_spec((SSM_HEADS, ROW_TILE)),
                  _const_spec((SSM_CONV, SSM_CONV_DIM)), _const_spec((1, SSM_CONV_DIM))],
        out_specs=[o[0] for o in outs],
        out_shape=[o[1] for o in outs],
        scratch_shapes=[pltpu.VMEM((8, SSM_CONV_DIM), F32), pltpu.VMEM((D_MODEL, _C_END), BF16)]
        + _ffn_weight_scratch(),
        compiler_params=_params(("arbitrary",)),
        name="ffn_inproj",
    )(x, ffn_norm_w.reshape(1, D_MODEL), wg, wu, wd, norm_w.reshape(1, D_MODEL), w_all, qkw_row, g32,
      gate_bias_rows, alog_rows, conv_w, conv_b_row)


def _mlstm_kernel(qt_ref, k_ref, vt_ref, ot_ref, gt_ref, nw_ref, yt_ref, ct_scr, nt_scr, m_scr):
    @pl.when(pl.program_id(1) == 0)
    def _():
        ct_scr[...] = jnp.zeros_like(ct_scr)
        nt_scr[...] = jnp.zeros_like(nt_scr)
        m_scr[...] = jnp.zeros_like(m_scr)

    for t in range(MIX_STEP_CHUNKS):
        _mlstm_chunk(t, qt_ref, k_ref, vt_ref, ot_ref, gt_ref, nw_ref, yt_ref, ct_scr, nt_scr, m_scr)


def _mlstm_chunk(t, qt_ref, k_ref, vt_ref, ot_ref, gt_ref, nw_ref, yt_ref, ct_scr, nt_scr, m_scr):
    L = MIX_CHUNK
    H = MLSTM_HEADS
    W = MLSTM_HEAD_DIM
    qt = qt_ref[t]
    ks = k_ref[t * L:(t + 1) * L, :] * (MLSTM_HEAD_DIM ** -0.5)
    vt = vt_ref[t]
    gt = gt_ref[t]
    c, c_max, b = gt[0:8], gt[8:16], gt[16:24]
    m_prev = m_scr[...]
    big_m = jnp.maximum(m_prev, c_max)
    inter = jnp.exp(m_prev - big_m)
    floor = jnp.exp(-(b + big_m))
    b_last = jnp.broadcast_to(b[:, L - 1:L], b.shape)
    m_new = b_last + jnp.maximum(m_prev, jnp.broadcast_to(c_max[:, L - 1:L], b.shape))
    w = jnp.exp(b_last + c - m_new)
    decay = jnp.exp(b_last + m_prev - m_new)

    c2_cols = jnp.concatenate([c * LOG2E, jnp.zeros((LANES - 8, L), F32)], axis=0).T
    big_m2 = big_m * LOG2E
    causal = (lax.broadcasted_iota(jnp.int32, (L, L), 0) <= lax.broadcasted_iota(jnp.int32, (L, L), 1))
    row_head = lax.broadcasted_iota(jnp.int32, (H * W, L), 0) // W
    qk = [_dot(ks, jnp.where(row_head == h, qt, jnp.zeros_like(qt))) for h in range(H)]
    nums, dens = [], []
    half = L // 2
    for h in range(H):
        def weighted(rows, cols, h=h):
            d = jnp.exp2(jnp.where(causal[rows, cols], c2_cols[rows, h:h + 1] - big_m2[h:h + 1, cols], -jnp.inf))
            return qk[h][rows, cols] * d

        s_top = weighted(slice(0, half), slice(0, L))
        s_low = weighted(slice(half, L), slice(half, L))
        dens.append(jnp.sum(s_top, axis=0, keepdims=True)
                    + jnp.concatenate([jnp.zeros((1, half), F32), jnp.sum(s_low, axis=0, keepdims=True)], axis=1))
        p = jnp.concatenate([s_top.astype(BF16),
                             jnp.concatenate([jnp.zeros((half, half), BF16), s_low.astype(BF16)], axis=1)], axis=0)
        nums.append(_dot(vt[h * W:(h + 1) * W, :], p))
    cq = _dot(ct_scr[...].astype(BF16), qt)
    nq = _dot(nt_scr[...].astype(BF16), qt)
    outs = []
    for h in range(H):
        num = nums[h] + inter[h:h + 1] * cq[h * W:(h + 1) * W]
        den = dens[h] + inter[h:h + 1] * nq[h:h + 1]
        hh = num / jnp.maximum(jnp.abs(den), floor[h:h + 1])
        outs.append(hh * lax.rsqrt(jnp.mean(hh * hh, axis=0, keepdims=True) + NORM_EPS))
    hn = jnp.concatenate(outs, axis=0) * nw_ref[...]
    yt_ref[t] = (_sigmoid(ot_ref[t].astype(F32)) * hn).astype(BF16)

    w_full = jnp.concatenate([jnp.broadcast_to(w[h:h + 1], (W, L)) for h in range(H)], axis=0)
    g_new = _dot((vt.astype(F32) * w_full).astype(BF16), ks)
    decay_full = jnp.concatenate([jnp.broadcast_to(decay[h:h + 1, :1], (W, H * W)) for h in range(H)], axis=0)
    same_head = (lax.broadcasted_iota(jnp.int32, (H * W, H * W), 0) // W
                 == lax.broadcasted_iota(jnp.int32, (H * W, H * W), 1) // W)
    ct_scr[...] = decay_full * ct_scr[...] + jnp.where(same_head, g_new, 0.0)
    w16 = jnp.concatenate([w, jnp.zeros_like(w)], axis=0).astype(BF16)
    n_new = _dot(w16, ks)
    own_lanes = (lax.broadcasted_iota(jnp.int32, (16, H * W), 0)
                 == lax.broadcasted_iota(jnp.int32, (16, H * W), 1) // W)
    decay16 = jnp.broadcast_to(jnp.concatenate([decay[:, :1], jnp.zeros((8, 1), F32)], axis=0), (16, H * W))
    nt_scr[...] = decay16 * nt_scr[...] + jnp.where(own_lanes, n_new, 0.0)
    m_scr[...] = m_new


def _mlstm(mqt, mk, mvt, mot, gt, norm_w_rows, batch, seq):
    ns = seq // (MIX_CHUNK * MIX_STEP_CHUNKS)
    n = batch * seq
    t_spec = pl.BlockSpec((MIX_STEP_CHUNKS, MLSTM_WIDTH, MIX_CHUNK), lambda b, c: (b * ns + c, 0, 0))
    return pl.pallas_call(
        _mlstm_kernel,
        grid=(batch, ns),
        in_specs=[t_spec, pl.BlockSpec((MIX_STEP_CHUNKS * MIX_CHUNK, MLSTM_WIDTH), lambda b, c: (b * ns + c, 0)),
                  t_spec, t_spec,
                  pl.BlockSpec((MIX_STEP_CHUNKS, GATE_ROWS, MIX_CHUNK), lambda b, c: (b * ns + c, 0, 0)),
                  _const_spec((MLSTM_WIDTH, MIX_CHUNK))],
        out_specs=t_spec,
        out_shape=jax.ShapeDtypeStruct((n // MIX_CHUNK, MLSTM_WIDTH, MIX_CHUNK), BF16),
        scratch_shapes=[pltpu.VMEM((MLSTM_WIDTH, MLSTM_WIDTH), F32),
                        pltpu.VMEM((16, MLSTM_WIDTH), F32),
                        pltpu.VMEM((8, MIX_CHUNK), F32)],
        compiler_params=_params(("arbitrary", "arbitrary")),
        name="mlstm",
    )(mqt, mk, mvt, mot, gt, norm_w_rows)


def _ssd_kernel(b_ref, ct_ref, xt_ref, xdt_ref, zt_ref, a_ref, dskip_ref, nw_ref, yt_ref, st_scr):
    @pl.when(pl.program_id(1) == 0)
    def _():
        st_scr[...] = jnp.zeros_like(st_scr)

    for t in range(MIX_STEP_CHUNKS):
        _ssd_chunk(t, b_ref, ct_ref, xt_ref, xdt_ref, zt_ref, a_ref, dskip_ref, nw_ref, yt_ref, st_scr)


def _ssd_chunk(t, b_ref, ct_ref, xt_ref, xdt_ref, zt_ref, a_ref, dskip_ref, nw_ref, yt_ref, st_scr):
    L = MIX_CHUNK
    P = SSM_HEAD_DIM
    HG = SSM_HEADS // SSM_GROUPS
    GW = HG * P
    NS = SSM_STATE
    a = a_ref[t]
    a_last = jnp.broadcast_to(a[:, L - 1:L], a.shape)
    exp_a = jnp.exp(a)
    d_in = jnp.exp(a_last - a)
    chunk_decay = jnp.exp(a_last)
    a2 = a * LOG2E
    a2_cols = jnp.concatenate([a2, jnp.zeros((LANES - SSM_HEADS, L), F32)], axis=0).T
    causal = (lax.broadcasted_iota(jnp.int32, (L, L), 0) <= lax.broadcasted_iota(jnp.int32, (L, L), 1))
    bn = b_ref[t * L:(t + 1) * L, :]
    ct = ct_ref[t]
    xdt = xdt_ref[t]

    def per_head_rows(rows, grp, width):
        return jnp.concatenate([jnp.broadcast_to(rows[grp * HG + i:grp * HG + i + 1, :width], (P, width))
                                for i in range(HG)], axis=0)

    scores, y_off = [], []
    for grp in range(SSM_GROUPS):
        bg = bn[:, grp * NS:(grp + 1) * NS]
        cg = ct[grp * NS:(grp + 1) * NS, :]
        scores.append(_dot(bg, cg))
        st = st_scr[grp]
        y_off.append(_dot(st.astype(BF16), cg))
        x_in = (xdt[grp * GW:(grp + 1) * GW, :].astype(F32) * per_head_rows(d_in, grp, L)).astype(BF16)
        st_scr[grp] = per_head_rows(chunk_decay, grp, NS) * st + _dot(x_in, bg)
    for grp in range(SSM_GROUPS):
        y_diag = []
        for i in range(HG):
            h = grp * HG + i
            def weighted(rows, cols, h=h, grp=grp):
                seg = jnp.exp2(jnp.where(causal[rows, cols], a2[h:h + 1, cols] - a2_cols[rows, h:h + 1], -jnp.inf))
                return (scores[grp][rows, cols] * seg).astype(BF16)

            half = L // 2
            p = jnp.concatenate(
                [weighted(slice(0, half), slice(0, L)),
                 jnp.concatenate([jnp.zeros((half, half), BF16), weighted(slice(half, L), slice(half, L))], axis=1)],
                axis=0)
            y_diag.append(_dot(xdt[h * P:(h + 1) * P, :], p))
        rows = slice(grp * GW, (grp + 1) * GW)
        y = (jnp.concatenate(y_diag, axis=0) + y_off[grp] * per_head_rows(exp_a, grp, L)
             + dskip_ref[rows, :] * xt_ref[t, rows, :].astype(F32))
        zg = zt_ref[t, rows, :].astype(F32)
        y = y * (zg * _sigmoid(zg))
        ms = jnp.mean(y * y, axis=0, keepdims=True)
        yt_ref[t, rows, :] = (y * lax.rsqrt(ms + NORM_EPS) * nw_ref[rows, :]).astype(BF16)


def _ssd(sb, sct, sxt, sxdt, zt, sa, dskip_rows, norm_w_rows, batch, seq):
    ns = seq // (MIX_CHUNK * MIX_STEP_CHUNKS)
    n = batch * seq

    def t_spec(width):
        return pl.BlockSpec((MIX_STEP_CHUNKS, width, MIX_CHUNK), lambda b, c: (b * ns + c, 0, 0))

    return pl.pallas_call(
        _ssd_kernel,
        grid=(batch, ns),
        in_specs=[pl.BlockSpec((MIX_STEP_CHUNKS * MIX_CHUNK, SSM_GROUPS * SSM_STATE), lambda b, c: (b * ns + c, 0)),
                  t_spec(SSM_GROUPS * SSM_STATE), t_spec(SSM_WIDTH), t_spec(SSM_WIDTH), t_spec(SSM_WIDTH),
                  t_spec(SSM_HEADS), _const_spec((SSM_WIDTH, MIX_CHUNK)), _const_spec((SSM_WIDTH, MIX_CHUNK))],
        out_specs=t_spec(SSM_WIDTH),
        out_shape=jax.ShapeDtypeStruct((n // MIX_CHUNK, SSM_WIDTH, MIX_CHUNK), BF16),
        scratch_shapes=[pltpu.VMEM((SSM_GROUPS, SSM_WIDTH // SSM_GROUPS, SSM_STATE), F32)],
        compiler_params=_params(("arbitrary", "arbitrary")),
        name="ssd",
    )(sb, sct, sxt, sxdt, zt, sa, dskip_rows, norm_w_rows)


def _t5_bias_tile(rel, relb_ref, head):
    n = jnp.maximum(rel, 0)
    max_exact = REL_BUCKETS // 2
    nf = jnp.maximum(n, 1).astype(F32)
    large = max_exact + (jnp.log(nf / max_exact) / math.log(REL_MAX_DIST / max_exact)
                         * (REL_BUCKETS - max_exact)).astype(jnp.int32)
    large = jnp.minimum(large, REL_BUCKETS - 1)
    bucket = jnp.where(n < max_exact, n, large)
    far = relb_ref[REL_BUCKETS - 1, head]
    bias = jnp.zeros(rel.shape, F32)
    for bkt in range(REL_BUCKETS - 1):
        bias = jnp.where(bucket == bkt, (relb_ref[bkt, head] - far) * LOG2E, bias)
    return jnp.where(rel >= 0, bias, NEG_BIG)


def _attn_kernel(relb_ref, qt_ref, k_ref, vt_ref, lam_ref, subw_ref, g64_ref,
                 y_ref, bias_scr, qm_scr, m_scr, acc_scr, sa_scr, sb_scr, *, lam_init):
    T = ATT_TILE
    H = DIFF_HEADS
    DV = DIFF_V_DIM
    R = ATT_STEP_TILES
    step_idx = pl.program_id(1)

    @pl.when((pl.program_id(0) == 0) & (step_idx == 0))
    def _():
        rel = lax.broadcasted_iota(jnp.int32, (T, T), 1) - lax.broadcasted_iota(jnp.int32, (T, T), 0)
        for h in range(H):
            bias_scr[0, h] = _t5_bias_tile(rel, relb_ref, h)
            bias_scr[1, h] = _t5_bias_tile(rel + T, relb_ref, h)

    comp_of_row = lax.broadcasted_iota(jnp.int32, (DIFF_WIDTH, T), 0) // DIFF_QK_DIM
    for r in range(R):
        qt = qt_ref[r]
        for hc in range(2 * H):
            qm_scr[r, hc] = jnp.where(comp_of_row == hc, qt, jnp.zeros_like(qt))
    m_scr[...] = jnp.full(m_scr.shape, NEG_BIG, F32)
    acc_scr[...] = jnp.zeros_like(acc_scr)
    ones_rows = jnp.ones((ONES_ROWS, T), BF16)

    def key_tile(j):
        return k_ref[pl.ds(pl.multiple_of(j * T, T), T), :]

    def logits_one(s_ref, kt, r, hc):
        s_ref[hc] = _dot(kt, qm_scr[r, hc])

    def consume_one(s_ref, vt, r, hc, bias_idx):
        h = hc // 2
        s = s_ref[hc]
        if bias_idx is not None:
            s = bias_scr[bias_idx, h] + s
        s = s.astype(BF16)
        m_prev = m_scr[r, hc]
        m_next = jnp.maximum(m_prev, jnp.max(s, axis=0, keepdims=True).astype(F32))
        alpha = jnp.exp2(m_prev - m_next)
        p = jnp.exp2(s - m_next.astype(BF16))
        m_scr[r, hc] = m_next
        v_aug = jnp.concatenate([vt[h * DV:(h + 1) * DV, :], ones_rows], axis=0)
        acc_scr[r, hc] = alpha * acc_scr[r, hc] + _dot(v_aug, p)

    LEAD = 2

    def step(s_next, j_next, r_next, s_cur, j_cur, r_cur, bias_idx):
        kt = key_tile(j_next)
        vt = vt_ref[j_cur]
        for hc in range(LEAD):
            logits_one(s_next, kt, r_next, hc)
        for hc in range(2 * H):
            if hc + LEAD < 2 * H:
                logits_one(s_next, kt, r_next, hc + LEAD)
            consume_one(s_cur, vt, r_cur, hc, bias_idx)

    DIAG, NEAR = 0, 1

    def far_pairs(count, cur, oth, r):
        def pair(j):
            step(oth, j + 1, r, cur, j, r, None)
            step(cur, j + 2, r, oth, j + 1, r, None)

        def body(t, carry):
            pair(4 * t)
            pair(4 * t + 2)
            return carry

        lax.fori_loop(0, count // 2, body, 0)

        @pl.when(count % 2 == 1)
        def _():
            pair(2 * (count - 1))

    cur, oth = sa_scr, sb_scr
    kt0 = key_tile(0)
    for hc in range(2 * H):
        logits_one(cur, kt0, 0, hc)
    for r in range(R):
        d = R * step_idx + r
        if r % 2 == 1:
            far_pairs((d - 1) // 2, cur, oth, r)
            step(oth, d, r, cur, d - 1, r, NEAR)
            diag, free = oth, cur
        else:
            far_pairs(jnp.maximum(d // 2 - 1, 0), cur, oth, r)

            def near_tiles(d=d, cur=cur, oth=oth, r=r):
                step(oth, d - 1, r, cur, d - 2, r, None)
                step(cur, d, r, oth, d - 1, r, NEAR)

            if r == 0:
                pl.when(d >= 1)(near_tiles)
            else:
                near_tiles()
            diag, free = cur, oth
        if r + 1 < R:
            step(free, 0, r + 1, diag, d, r, DIAG)
            cur, oth = free, diag
        else:
            vt_last = vt_ref[d]
            for hc in range(2 * H):
                consume_one(diag, vt_last, r, hc, DIAG)

    lam_p = lam_ref[...]
    lam = (jnp.exp(jnp.sum(lam_p[0:1] * lam_p[1:2], axis=1, keepdims=True))
           - jnp.exp(jnp.sum(lam_p[2:3] * lam_p[3:4], axis=1, keepdims=True)) + lam_init)
    for r in range(R):
        def normalised(hc):
            return acc_scr[r, hc, :DV, :] / acc_scr[r, hc, DV:DV + 1, :]

        o_t = jnp.concatenate([normalised(2 * h) - lam * normalised(2 * h + 1) for h in range(H)], axis=0)
        o = o_t.T
        ms = _group_mean_sq(o, g64_ref[...])
        y_ref[r * T:(r + 1) * T, :] = (o * lax.rsqrt(ms + NORM_EPS) * subw_ref[...] * (1.0 - lam_init)).astype(BF16)


def _diff_attn(qt, kn, vt, rel_bias, lambdas, subw_row, g64, lam_init, batch, seq):
    nk = seq // ATT_TILE
    nsteps = nk // ATT_STEP_TILES
    n = batch * seq
    return pl.pallas_call(
        functools.partial(_attn_kernel, lam_init=lam_init),
        grid=(batch, nsteps),
        in_specs=[pl.BlockSpec(memory_space=pltpu.SMEM),
                  pl.BlockSpec((ATT_STEP_TILES, DIFF_WIDTH, ATT_TILE), lambda b, i: (b * nsteps + i, 0, 0)),
                  pl.BlockSpec((seq, DIFF_WIDTH), lambda b, i: (b, 0)),
                  pl.BlockSpec((nk, DIFF_WIDTH, ATT_TILE), lambda b, i: (b, 0, 0)),
                  _const_spec((4, DIFF_QK_DIM)), _const_spec((1, DIFF_WIDTH)),
                  _const_spec((DIFF_WIDTH, DIFF_WIDTH))],
        out_specs=pl.BlockSpec((ATT_STEP_TILES * ATT_TILE, DIFF_WIDTH), lambda b, i: (b * nsteps + i, 0)),
        out_shape=jax.ShapeDtypeStruct((n, DIFF_WIDTH), BF16),
        scratch_shapes=[pltpu.VMEM((2, DIFF_HEADS, ATT_TILE, ATT_TILE), F32),
                        pltpu.VMEM((ATT_STEP_TILES, 2 * DIFF_HEADS, DIFF_WIDTH, ATT_TILE), BF16),
                        pltpu.VMEM((ATT_STEP_TILES, 2 * DIFF_HEADS, 1, ATT_TILE), F32),
                        pltpu.VMEM((ATT_STEP_TILES, 2 * DIFF_HEADS, DIFF_V_DIM + ONES_ROWS, ATT_TILE), F32),
                        pltpu.VMEM((2 * DIFF_HEADS, ATT_TILE, ATT_TILE), F32),
                        pltpu.VMEM((2 * DIFF_HEADS, ATT_TILE, ATT_TILE), F32)],
        compiler_params=_params(("arbitrary", "arbitrary")),
        name="diff_attn",
    )(rel_bias, qt, kn, vt, lambdas, subw_row, g64)


def _untransposed(yt_ref):
    return jnp.concatenate([yt_ref[t].astype(F32).T for t in range(yt_ref.shape[0])], axis=0).astype(BF16)


def _outproj_ffn_kernel(x_ref, ymt_ref, yst_ref, yd_ref, w_hbm, fnw_ref, wg_hbm, wu_hbm, wd_hbm, o_ref,
                        w_ref, wg_ref, wu_ref, wd_ref, wide, wide_sem, tall, tall_sem, *, layer):
    @pl.when(pl.program_id(0) == 0)
    def _():
        def store(r0, chunk):
            w_ref[r0:r0 + chunk.shape[0], :] = chunk.astype(BF16)

        _stage_weight(w_hbm, layer, D_MODEL // 4, tall, tall_sem, store)
        _stage_ffn_weights(layer, wg_hbm, wu_hbm, wd_hbm, wg_ref, wu_ref, wd_ref, wide, wide_sem, tall, tall_sem)

    y = jnp.concatenate([_untransposed(ymt_ref), _untransposed(yst_ref), yd_ref[...]], axis=1)
    x = x_ref[...] + _dot(y, w_ref[...])
    o_ref[...] = _ffn_block(x, fnw_ref, wg_ref, wu_ref, wd_ref)


def _outproj_ffn(x, y_m, y_s, y_d, w_out, ffn_norm_w, wg, wu, wd, layer):
    n = x.shape[0]
    return pl.pallas_call(
        functools.partial(_outproj_ffn_kernel, layer=layer),
        grid=(n // ROW_TILE,),
        in_specs=[_row_spec(D_MODEL),
                  pl.BlockSpec((ROW_TILE // MIX_CHUNK, MLSTM_WIDTH, MIX_CHUNK), lambda i: (i, 0, 0)),
                  pl.BlockSpec((ROW_TILE // MIX_CHUNK, SSM_WIDTH, MIX_CHUNK), lambda i: (i, 0, 0)),
                  _row_spec(DIFF_WIDTH), _HBM, _const_spec((1, D_MODEL)), _HBM, _HBM, _HBM],
        out_specs=_row_spec(D_MODEL),
        out_shape=jax.ShapeDtypeStruct((n, D_MODEL), F32),
        scratch_shapes=[pltpu.VMEM((D_MODEL, D_MODEL), BF16)] + _ffn_weight_scratch(),
        compiler_params=_params(("arbitrary",)),
        name="outproj_ffn",
    )(x, y_m, y_s, y_d, w_out, ffn_norm_w.reshape(1, D_MODEL), wg, wu, wd)


def _small_row(values_by_lane):
    row = jnp.zeros((LANES,), F32)
    for lane, vals in values_by_lane:
        row = lax.dynamic_update_slice(row, vals.astype(F32), (lane,))
    return row.reshape(1, LANES)


def kernel(x, ffn1_norm_w, ffn1_w_gate, ffn1_w_up, ffn1_w_down, mix_norm_w, w_in, mlstm_gate_bias, mlstm_norm_w, ssm_conv_w, ssm_conv_b, ssm_dt_bias, ssm_A_log, ssm_D, ssm_norm_w, diff_q_norm_w, diff_k_norm_w, diff_lambda, diff_subln_w, rel_bias, w_out, ffn2_norm_w, ffn2_w_gate, ffn2_w_up, ffn2_w_down):
    batch, seq, d = x.shape
    assert d == D_MODEL and MIX_CHUNK == ATT_TILE and seq % ROW_TILE == 0 and ROW_TILE % MIX_CHUNK == 0
    assert seq % (MIX_CHUNK * MIX_STEP_CHUNKS) == 0 and seq % (ATT_STEP_TILES * ATT_TILE) == 0 and ATT_STEP_TILES % 2 == 0
    xf = x.reshape(batch * seq, D_MODEL)
    g32 = _block_diag(DIFF_WIDTH, DIFF_QK_DIM, 1.0 / DIFF_QK_DIM)
    g64 = _block_diag(DIFF_WIDTH, DIFF_V_DIM, 1.0 / DIFF_V_DIM)
    ffn1 = (ffn1_w_gate, ffn1_w_up, ffn1_w_down)
    ffn2 = (ffn2_w_gate, ffn2_w_up, ffn2_w_down)
    for l in range(DEPTH):
        qkw_row = jnp.concatenate([jnp.tile(diff_q_norm_w[l].reshape(-1), DIFF_HEADS) * (DIFF_QK_DIM ** -0.5 * LOG2E),
                                   jnp.tile(diff_k_norm_w[l].reshape(-1), DIFF_HEADS)]).reshape(1, 2 * DIFF_WIDTH)
        bias_row = _small_row([(_LANE_I, mlstm_gate_bias[l, 0]), (_LANE_F, mlstm_gate_bias[l, 1]),
                               (_LANE_DT, ssm_dt_bias[l])])
        (xf, mqt, mk, mvt, mot, gt, zt, sb, sct, sxt, sxdt, sa, qt, kn, vt) = _ffn_inproj(
            xf, ffn1_norm_w[l], *ffn1, mix_norm_w[l], w_in, qkw_row, g32,
            jnp.broadcast_to(bias_row.reshape(LANES, 1), (LANES, ROW_TILE)),
            jnp.broadcast_to(ssm_A_log[l][:, None], (SSM_HEADS, ROW_TILE)),
            ssm_conv_w[l], ssm_conv_b[l].reshape(1, SSM_CONV_DIM), l, seq)
        y_m = _mlstm(mqt, mk, mvt, mot, gt, jnp.broadcast_to(mlstm_norm_w[l][:, None], (MLSTM_WIDTH, MIX_CHUNK)),
                     batch, seq)
        y_s = _ssd(sb, sct, sxt, sxdt, zt, sa,
                   jnp.broadcast_to(jnp.repeat(ssm_D[l], SSM_HEAD_DIM)[:, None], (SSM_WIDTH, MIX_CHUNK)),
                   jnp.broadcast_to(ssm_norm_w[l][:, None], (SSM_WIDTH, MIX_CHUNK)), batch, seq)
        lam_init = 0.8 - 0.6 * math.exp(-0.3 * l)
        y_d = _diff_attn(qt, kn, vt, rel_bias, diff_lambda[l], jnp.tile(diff_subln_w[l], DIFF_HEADS).reshape(1, DIFF_WIDTH),
                         g64, lam_init, batch, seq)
        xf = _outproj_ffn(xf, y_m, y_s, y_d, w_out, ffn2_norm_w[l], *ffn2, l)
    return xf.reshape(batch, seq, D_MODEL)
```

```python
import functools
import math

import jax
import jax.numpy as jnp
from jax import lax
from jax.experimental import pallas as pl
from jax.experimental.pallas import tpu as pltpu

F32 = jnp.float32
BF16 = jnp.bfloat16

D_MODEL = 1024
DEPTH = 2
D_FF = 2816
MLSTM_HEADS = 4
MLSTM_HEAD_DIM = 64
MLSTM_WIDTH = MLSTM_HEADS * MLSTM_HEAD_DIM
SSM_HEADS = 8
SSM_HEAD_DIM = 64
SSM_WIDTH = SSM_HEADS * SSM_HEAD_DIM
SSM_STATE = 128
SSM_GROUPS = 2
SSM_CONV = 4
SSM_CONV_DIM = SSM_WIDTH + 2 * SSM_GROUPS * SSM_STATE
DIFF_HEADS = 4
DIFF_QK_DIM = 32
DIFF_V_DIM = 64
DIFF_WIDTH = DIFF_HEADS * DIFF_V_DIM
REL_BUCKETS = 32
REL_MAX_DIST = 128
NORM_EPS = 1e-6

LANES = 128
ROW_TILE = 512
MXU_COLS = 256
CONV_COLS = 256
FFN_SPLITS = (0, 6 * MXU_COLS, D_FF)
MIX_CHUNK = 256
MIX_STEP_CHUNKS = 8
ATT_TILE = 256
ATT_STEP_TILES = 4
VMEM_LIMIT = 56 * 1024 * 1024
NEG_BIG = -1e30
LOG2E = math.log2(math.e)
ONES_ROWS = 16

_C_M = 0
_C_Z = _C_M + 4 * MLSTM_WIDTH
_C_XBC = _C_Z + SSM_WIDTH
_C_SMALL = _C_XBC + SSM_CONV_DIM
_C_QK = _C_SMALL + LANES
_C_V = _C_QK + 4 * DIFF_HEADS * DIFF_QK_DIM
_C_END = _C_V + DIFF_WIDTH
_LANE_I = 0
_LANE_F = 8
_LANE_DT = 16
GATE_ROWS = 24
D_IN = 4 * MLSTM_WIDTH + 2 * MLSTM_HEADS + SSM_WIDTH + SSM_CONV_DIM + SSM_HEADS + 3 * DIFF_WIDTH


def _regroup_plan():
    sizes = (4 * MLSTM_WIDTH, MLSTM_HEADS, MLSTM_HEADS, SSM_WIDTH, SSM_CONV_DIM, SSM_HEADS, 3 * DIFF_WIDTH)
    dests = (_C_M, _C_SMALL + _LANE_I, _C_SMALL + _LANE_F, _C_Z, _C_XBC, _C_SMALL + _LANE_DT, _C_QK)
    plan, src = [], 0
    for dst, width in zip(dests, sizes):
        plan.append((dst, src, width))
        src += width
    assert src == D_IN
    return plan


def _dot(a, b):
    return jnp.dot(a, b, preferred_element_type=F32)


def _sigmoid(x):
    return 1.0 / (1.0 + jnp.exp(-x))


def _softplus(x):
    return jnp.maximum(x, 0.0) + jnp.log(1.0 + jnp.exp(-jnp.abs(x)))


def _rms_rows(x, w_row):
    ms = jnp.mean(x * x, axis=-1, keepdims=True)
    return x * lax.rsqrt(ms + NORM_EPS) * w_row


def _group_mean_sq(x, gmat):
    sq = x * x
    hi = sq.astype(BF16)
    lo = (sq - hi.astype(F32)).astype(BF16)
    return _dot(hi, gmat) + _dot(lo, gmat)


def _scan_lanes(x, op, identity):
    n = x.shape[1]
    lane = lax.broadcasted_iota(jnp.int32, x.shape, 1)
    step = 1
    while step < n:
        x = op(x, jnp.where(lane >= step, pltpu.roll(x, step, axis=1), identity))
        step *= 2
    return x


def _block_diag(n, group, value):
    r = jnp.arange(n) // group
    return jnp.where(r[:, None] == r[None, :], value, 0.0).astype(BF16)


def _const_spec(shape):
    nd = len(shape)
    return pl.BlockSpec(shape, lambda *_: (0,) * nd)


def _params(sem):
    return pltpu.CompilerParams(dimension_semantics=sem, vmem_limit_bytes=VMEM_LIMIT)


def _ffn_block(x, nw_ref, wg_ref, wu_ref, wd_ref):
    xn = _rms_rows(x, nw_ref[...]).astype(BF16)
    acc = None
    for lo, hi in zip(FFN_SPLITS[:-1], FFN_SPLITS[1:]):
        assert (hi - lo) % MXU_COLS == 0
        cols = slice(lo, hi)
        g = _dot(xn, wg_ref[:, cols])
        u = _dot(xn, wu_ref[:, cols])
        h = (g * _sigmoid(g) * u).astype(BF16)
        part = _dot(h, wd_ref[cols, :])
        acc = part if acc is None else acc + part
    return x + 0.5 * acc


def _row_spec(width):
    return pl.BlockSpec((ROW_TILE, width), lambda i: (i, 0))


WIDE_STAGE_ROWS = 128
TALL_STAGE_ROWS = 352
_HBM = pl.BlockSpec(memory_space=pl.ANY)


def _stage_weight(w_hbm, layer, rows, stage, sem, store):
    n_rows, n_cols = w_hbm.shape[1], w_hbm.shape[2]
    assert n_rows % rows == 0

    def copy(c):
        return pltpu.make_async_copy(w_hbm.at[layer, pl.ds(c * rows, rows), :],
                                     stage.at[c % 2, pl.ds(0, rows), pl.ds(0, n_cols)], sem.at[c % 2])

    copy(0).start()
    for c in range(n_rows // rows):
        if c + 1 < n_rows // rows:
            copy(c + 1).start()
        copy(c).wait()
        store(c * rows, stage[c % 2, :rows, :n_cols])


def _stage_ffn_weights(layer, wg_hbm, wu_hbm, wd_hbm, wg_scr, wu_scr, wd_scr, wide, wide_sem, tall, tall_sem):
    def into(dst):
        def store(r0, chunk):
            dst[r0:r0 + chunk.shape[0], :] = chunk.astype(BF16)
        return store

    _stage_weight(wg_hbm, layer, WIDE_STAGE_ROWS, wide, wide_sem, into(wg_scr))
    _stage_weight(wu_hbm, layer, WIDE_STAGE_ROWS, wide, wide_sem, into(wu_scr))
    _stage_weight(wd_hbm, layer, TALL_STAGE_ROWS, tall, tall_sem, into(wd_scr))


def _ffn_weight_scratch():
    return [pltpu.VMEM((D_MODEL, D_FF), BF16), pltpu.VMEM((D_MODEL, D_FF), BF16), pltpu.VMEM((D_FF, D_MODEL), BF16),
            pltpu.VMEM((2, WIDE_STAGE_ROWS, D_IN), F32), pltpu.SemaphoreType.DMA((2,)),
            pltpu.VMEM((2, TALL_STAGE_ROWS, D_MODEL), F32), pltpu.SemaphoreType.DMA((2,))]


def _store_transposed_tiles(out_ref, a):
    a_t = a.T.astype(BF16)
    for t in range(ROW_TILE // ATT_TILE):
        out_ref[t] = a_t[:, t * ATT_TILE:(t + 1) * ATT_TILE]


def _ffn_inproj_kernel(x_ref, fnw_ref, wg_hbm, wu_hbm, wd_hbm, nw_ref, w_hbm, qkw_ref, g32_ref, gbias_ref,
                       alog_ref, convw_ref, convb_ref,
                       x_out_ref, mqt_ref, mk_ref, mvt_ref, mot_ref, gt_ref,
                       zt_ref, sb_ref, sct_ref, sxt_ref, sxdt_ref, sa_ref, qt_ref, k_ref, vt_ref,
                       halo_scr, w_scr, wg_ref, wu_ref, wd_ref, wide, wide_sem, tall, tall_sem,
                       *, tiles_per_seq, layer):
    @pl.when(pl.program_id(0) == 0)
    def _():
        _stage_ffn_weights(layer, wg_hbm, wu_hbm, wd_hbm, wg_ref, wu_ref, wd_ref, wide, wide_sem, tall, tall_sem)
        w_scr[:, _C_SMALL:_C_QK] = jnp.zeros((D_MODEL, LANES), BF16)

        def regrouped(r0, chunk):
            for dst, src, width in _regroup_plan():
                w_scr[r0:r0 + chunk.shape[0], dst:dst + width] = chunk[:, src:src + width].astype(BF16)

        _stage_weight(w_hbm, layer, WIDE_STAGE_ROWS, wide, wide_sem, regrouped)

    @pl.when(pl.program_id(0) % tiles_per_seq == 0)
    def _():
        halo_scr[...] = jnp.zeros_like(halo_scr)

    x = _ffn_block(x_ref[...], fnw_ref, wg_ref, wu_ref, wd_ref)
    x_out_ref[...] = x
    xn = _rms_rows(x, nw_ref[...]).astype(BF16)

    def proj(lo, hi):
        return _dot(xn, w_scr[:, lo:hi])

    half = (_C_V - _C_QK) // 2
    p_xbc = [proj(_C_XBC + c * CONV_COLS, _C_XBC + (c + 1) * CONV_COLS) for c in range(SSM_CONV_DIM // CONV_COLS)]
    p_small = proj(_C_SMALL, _C_QK)
    p_q, p_k = proj(_C_QK, _C_QK + half), proj(_C_QK + half, _C_V)
    p_mq = proj(_C_M, _C_M + MLSTM_WIDTH)
    p_mk = proj(_C_M + MLSTM_WIDTH, _C_M + 2 * MLSTM_WIDTH)
    p_mv = proj(_C_M + 2 * MLSTM_WIDTH, _C_M + 3 * MLSTM_WIDTH)
    p_mo = proj(_C_M + 3 * MLSTM_WIDTH, _C_Z)
    p_z = proj(_C_Z, _C_XBC)
    p_v = proj(_C_V, _C_END)

    qk = [a * lax.rsqrt(_group_mean_sq(a, g32_ref[...]) + NORM_EPS) * qkw_ref[:, i * half:(i + 1) * half]
          for i, a in enumerate((p_q, p_k))]
    _store_transposed_tiles(qt_ref, qk[0])
    k_ref[...] = qk[1].astype(BF16)
    _store_transposed_tiles(vt_ref, p_v)

    small_t = p_small.T + gbias_ref[...]
    dt = _softplus(small_t[_LANE_DT:_LANE_DT + SSM_HEADS])
    log_decay = dt * (-jnp.exp(alog_ref[...]))
    for t in range(ROW_TILE // MIX_CHUNK):
        cols = slice(t * MIX_CHUNK, (t + 1) * MIX_CHUNK)
        sa_ref[t] = _scan_lanes(log_decay[:, cols], jnp.add, 0.0)
    heads_per_piece = CONV_COLS // SSM_HEAD_DIM
    for c, xp in enumerate(p_xbc):
        cs = slice(c * CONV_COLS, (c + 1) * CONV_COLS)
        ext = jnp.concatenate([halo_scr[:, cs], xp], axis=0)
        halo_scr[:, cs] = xp[ROW_TILE - 8:, :]
        conv = convb_ref[:, cs] + convw_ref[SSM_CONV - 1:SSM_CONV, cs] * xp
        for j in range(1, SSM_CONV):
            conv = conv + convw_ref[SSM_CONV - 1 - j:SSM_CONV - j, cs] * ext[8 - j:8 - j + ROW_TILE, :]
        xa = conv * _sigmoid(conv)
        if cs.stop <= SSM_WIDTH:
            xs_t = xa.T
            xdt_t = xs_t * jnp.concatenate(
                [jnp.broadcast_to(dt[h:h + 1], (SSM_HEAD_DIM, ROW_TILE))
                 for h in range(c * heads_per_piece, (c + 1) * heads_per_piece)], axis=0)
            for t in range(ROW_TILE // MIX_CHUNK):
                cols = slice(t * MIX_CHUNK, (t + 1) * MIX_CHUNK)
                sxt_ref[t, cs, :] = xs_t[:, cols].astype(BF16)
                sxdt_ref[t, cs, :] = xdt_t[:, cols].astype(BF16)
        elif cs.stop <= SSM_WIDTH + SSM_GROUPS * SSM_STATE:
            sb_ref[:, cs.start - SSM_WIDTH:cs.stop - SSM_WIDTH] = xa.astype(BF16)
        else:
            lo = cs.start - SSM_WIDTH - SSM_GROUPS * SSM_STATE
            xa_t = xa.T.astype(BF16)
            for t in range(ROW_TILE // MIX_CHUNK):
                sct_ref[t, lo:lo + CONV_COLS, :] = xa_t[:, t * MIX_CHUNK:(t + 1) * MIX_CHUNK]
    _store_transposed_tiles(zt_ref, p_z)

    _store_transposed_tiles(mqt_ref, p_mq)
    mk_ref[...] = p_mk.astype(BF16)
    _store_transposed_tiles(mvt_ref, p_mv)
    _store_transposed_tiles(mot_ref, p_mo)
    gates = small_t[:_LANE_DT]
    for t in range(ROW_TILE // MIX_CHUNK):
        i_pre = gates[_LANE_I:_LANE_I + 8, t * MIX_CHUNK:(t + 1) * MIX_CHUNK]
        f_pre = gates[_LANE_F:_LANE_F + 8, t * MIX_CHUNK:(t + 1) * MIX_CHUNK]
        log_f = jnp.minimum(f_pre, 0.0) - jnp.log(1.0 + jnp.exp(-jnp.abs(f_pre)))
        b = _scan_lanes(log_f, jnp.add, 0.0)
        c = i_pre - b
        gt_ref[t] = jnp.concatenate([c, _scan_lanes(c, jnp.maximum, -jnp.inf), b], axis=0)


def _ffn_inproj(x, ffn_norm_w, wg, wu, wd, norm_w, w_all, qkw_row, g32, gate_bias_rows, alog_rows, conv_w, conv_b_row,
                layer, seq):
    n = x.shape[0]
    tiles = ROW_TILE // ATT_TILE

    def t_out(width, dtype=BF16):
        return (pl.BlockSpec((tiles, width, ATT_TILE), lambda i: (i, 0, 0)),
                jax.ShapeDtypeStruct((n // ATT_TILE, width, ATT_TILE), dtype))

    def r_out(width, dtype=BF16):
        return _row_spec(width), jax.ShapeDtypeStruct((n, width), dtype)

    outs = [r_out(D_MODEL, F32),
            t_out(MLSTM_WIDTH), r_out(MLSTM_WIDTH), t_out(MLSTM_WIDTH), t_out(MLSTM_WIDTH), t_out(GATE_ROWS, F32),
            t_out(SSM_WIDTH), r_out(SSM_GROUPS * SSM_STATE), t_out(SSM_GROUPS * SSM_STATE), t_out(SSM_WIDTH),
            t_out(SSM_WIDTH), t_out(SSM_HEADS, F32),
            t_out(DIFF_WIDTH), r_out(DIFF_WIDTH), t_out(DIFF_WIDTH)]
    return pl.pallas_call(
        functools.partial(_ffn_inproj_kernel, tiles_per_seq=seq // ROW_TILE, layer=layer),
        grid=(n // ROW_TILE,),
        in_specs=[_row_spec(D_MODEL), _const_spec((1, D_MODEL)), _HBM, _HBM, _HBM, _const_spec((1, D_MODEL)), _HBM,
                  _const_spec((1, 2 * DIFF_WIDTH)), _const_spec((DIFF_WIDTH, DIFF_WIDTH)),
                  _const_spec((LANES, ROW_TILE)), _const_spec((SSM_HEADS, ROW_TILE)),
                  _const_spec((SSM_CONV, SSM_CONV_DIM)), _const_spec((1, SSM_CONV_DIM))],
        out_specs=[o[0] for o in outs],
        out_shape=[o[1] for o in outs],
        scratch_shapes=[pltpu.VMEM((8, SSM_CONV_DIM), F32), pltpu.VMEM((D_MODEL, _C_END), BF16)]
        + _ffn_weight_scratch(),
        compiler_params=_params(("arbitrary",)),
        name="ffn_inproj",
    )(x, ffn_norm_w.reshape(1, D_MODEL), wg, wu, wd, norm_w.reshape(1, D_MODEL), w_all, qkw_row, g32,
      gate_bias_rows, alog_rows, conv_w, conv_b_row)


def _mlstm_kernel(qt_ref, k_ref, vt_ref, ot_ref, gt_ref, nw_ref, yt_ref, ct_scr, nt_scr, m_scr):
    @pl.when(pl.program_id(1) == 0)
    def _():
        ct_scr[...] = jnp.zeros_like(ct_scr)
        nt_scr[...] = jnp.zeros_like(nt_scr)
        m_scr[...] = jnp.zeros_like(m_scr)

    for t in range(MIX_STEP_CHUNKS):
        _mlstm_chunk(t, qt_ref, k_ref, vt_ref, ot_ref, gt_ref, nw_ref, yt_ref, ct_scr, nt_scr, m_scr)


def _mlstm_chunk(t, qt_ref, k_ref, vt_ref, ot_ref, gt_ref, nw_ref, yt_ref, ct_scr, nt_scr, m_scr):
    L = MIX_CHUNK
    H = MLSTM_HEADS
    W = MLSTM_HEAD_DIM
    qt = qt_ref[t]
    ks = k_ref[t * L:(t + 1) * L, :] * (MLSTM_HEAD_DIM ** -0.5)
    vt = vt_ref[t]
    gt = gt_ref[t]
    c, c_max, b = gt[0:8], gt[8:16], gt[16:24]
    m_prev = m_scr[...]
    big_m = jnp.maximum(m_prev, c_max)
    inter = jnp.exp(m_prev - big_m)
    floor = jnp.exp(-(b + big_m))
    b_last = jnp.broadcast_to(b[:, L - 1:L], b.shape)
    m_new = b_last + jnp.maximum(m_prev, jnp.broadcast_to(c_max[:, L - 1:L], b.shape))
    w = jnp.exp(b_last + c - m_new)
    decay = jnp.exp(b_last + m_prev - m_new)

    c2_cols = jnp.concatenate([c * LOG2E, jnp.zeros((LANES - 8, L), F32)], axis=0).T
    big_m2 = big_m * LOG2E
    causal = (lax.broadcasted_iota(jnp.int32, (L, L), 0) <= lax.broadcasted_iota(jnp.int32, (L, L), 1))
    row_head = lax.broadcasted_iota(jnp.int32, (H * W, L), 0) // W
    qk = [_dot(ks, jnp.where(row_head == h, qt, jnp.zeros_like(qt))) for h in range(H)]
    nums, dens = [], []
    half = L // 2
    for h in range(H):
        def weighted(rows, cols, h=h):
            d = jnp.exp2(jnp.where(causal[rows, cols], c2_cols[rows, h:h + 1] - big_m2[h:h + 1, cols], -jnp.inf))
            return qk[h][rows, cols] * d

        s_top = weighted(slice(0, half), slice(0, L))
        s_low = weighted(slice(half, L), slice(half, L))
        dens.append(jnp.sum(s_top, axis=0, keepdims=True)
                    + jnp.concatenate([jnp.zeros((1, half), F32), jnp.sum(s_low, axis=0, keepdims=True)], axis=1))
        p = jnp.concatenate([s_top.astype(BF16),
                             jnp.concatenate([jnp.zeros((half, half), BF16), s_low.astype(BF16)], axis=1)], axis=0)
        nums.append(_dot(vt[h * W:(h + 1) * W, :], p))
    cq = _dot(ct_scr[...].astype(BF16), qt)
    nq = _dot(nt_scr[...].astype(BF16), qt)
    outs = []
    for h in range(H):
        num = nums[h] + inter[h:h + 1] * cq[h * W:(h + 1) * W]
        den = dens[h] + inter[h:h + 1] * nq[h:h + 1]
        hh = num / jnp.maximum(jnp.abs(den), floor[h:h + 1])
        outs.append(hh * lax.rsqrt(jnp.mean(hh * hh, axis=0, keepdims=True) + NORM_EPS))
    hn = jnp.concatenate(outs, axis=0) * nw_ref[...]
    yt_ref[t] = (_sigmoid(ot_ref[t].astype(F32)) * hn).astype(BF16)

    w_full = jnp.concatenate([jnp.broadcast_to(w[h:h + 1], (W, L)) for h in range(H)], axis=0)
    g_new = _dot((vt.astype(F32) * w_full).astype(BF16), ks)
    decay_full = jnp.concatenate([jnp.broadcast_to(decay[h:h + 1, :1], (W, H * W)) for h in range(H)], axis=0)
    same_head = (lax.broadcasted_iota(jnp.int32, (H * W, H * W), 0) // W
                 == lax.broadcasted_iota(jnp.int32, (H * W, H * W), 1) // W)
    ct_scr[...] = decay_full * ct_scr[...] + jnp.where(same_head, g_new, 0.0)
    w16 = jnp.concatenate([w, jnp.zeros_like(w)], axis=0).astype(BF16)
    n_new = _dot(w16, ks)
    own_lanes = (lax.broadcasted_iota(jnp.int32, (16, H * W), 0)
                 == lax.broadcasted_iota(jnp.int32, (16, H * W), 1) // W)
    decay16 = jnp.broadcast_to(jnp.concatenate([decay[:, :1], jnp.zeros((8, 1), F32)], axis=0), (16, H * W))
    nt_scr[...] = decay16 * nt_scr[...] + jnp.where(own_lanes, n_new, 0.0)
    m_scr[...] = m_new


def _mlstm(mqt, mk, mvt, mot, gt, norm_w_rows, batch, seq):
    ns = seq // (MIX_CHUNK * MIX_STEP_CHUNKS)
    n = batch * seq
    t_spec = pl.BlockSpec((MIX_STEP_CHUNKS, MLSTM_WIDTH, MIX_CHUNK), lambda b, c: (b * ns + c, 0, 0))
    return pl.pallas_call(
        _mlstm_kernel,
        grid=(batch, ns),
        in_specs=[t_spec, pl.BlockSpec((MIX_STEP_CHUNKS * MIX_CHUNK, MLSTM_WIDTH), lambda b, c: (b * ns + c, 0)),
                  t_spec, t_spec,
                  pl.BlockSpec((MIX_STEP_CHUNKS, GATE_ROWS, MIX_CHUNK), lambda b, c: (b * ns + c, 0, 0)),
                  _const_spec((MLSTM_WIDTH, MIX_CHUNK))],
        out_specs=t_spec,
        out_shape=jax.ShapeDtypeStruct((n // MIX_CHUNK, MLSTM_WIDTH, MIX_CHUNK), BF16),
        scratch_shapes=[pltpu.VMEM((MLSTM_WIDTH, MLSTM_WIDTH), F32),
                        pltpu.VMEM((16, MLSTM_WIDTH), F32),
                        pltpu.VMEM((8, MIX_CHUNK), F32)],
        compiler_params=_params(("arbitrary", "arbitrary")),
        name="mlstm",
    )(mqt, mk, mvt, mot, gt, norm_w_rows)


def _ssd_kernel(b_ref, ct_ref, xt_ref, xdt_ref, zt_ref, a_ref, dskip_ref, nw_ref, yt_ref, st_scr):
    @pl.when(pl.program_id(1) == 0)
    def _():
        st_scr[...] = jnp.zeros_like(st_scr)

    for t in range(MIX_STEP_CHUNKS):
        _ssd_chunk(t, b_ref, ct_ref, xt_ref, xdt_ref, zt_ref, a_ref, dskip_ref, nw_ref, yt_ref, st_scr)


def _ssd_chunk(t, b_ref, ct_ref, xt_ref, xdt_ref, zt_ref, a_ref, dskip_ref, nw_ref, yt_ref, st_scr):
    L = MIX_CHUNK
    P = SSM_HEAD_DIM
    HG = SSM_HEADS // SSM_GROUPS
    GW = HG * P
    NS = SSM_STATE
    a = a_ref[t]
    a_last = jnp.broadcast_to(a[:, L - 1:L], a.shape)
    exp_a = jnp.exp(a)
    d_in = jnp.exp(a_last - a)
    chunk_decay = jnp.exp(a_last)
    a2 = a * LOG2E
    a2_cols = jnp.concatenate([a2, jnp.zeros((LANES - SSM_HEADS, L), F32)], axis=0).T
    causal = (lax.broadcasted_iota(jnp.int32, (L, L), 0) <= lax.broadcasted_iota(jnp.int32, (L, L), 1))
    bn = b_ref[t * L:(t + 1) * L, :]
    ct = ct_ref[t]
    xdt = xdt_ref[t]

    def per_head_rows(rows, grp, width):
        return jnp.concatenate([jnp.broadcast_to(rows[grp * HG + i:grp * HG + i + 1, :width], (P, width))
                                for i in range(HG)], axis=0)

    scores, y_off = [], []
    for grp in range(SSM_GROUPS):
        bg = bn[:, grp * NS:(grp + 1) * NS]
        cg = ct[grp * NS:(grp + 1) * NS, :]
        scores.append(_dot(bg, cg))
        st = st_scr[grp]
        y_off.append(_dot(st.astype(BF16), cg))
        x_in = (xdt[grp * GW:(grp + 1) * GW, :].astype(F32) * per_head_rows(d_in, grp, L)).astype(BF16)
        st_scr[grp] = per_head_rows(chunk_decay, grp, NS) * st + _dot(x_in, bg)
    for grp in range(SSM_GROUPS):
        y_diag = []
        for i in range(HG):
            h = grp * HG + i
            def weighted(rows, cols, h=h, grp=grp):
                seg = jnp.exp2(jnp.where(causal[rows, cols], a2[h:h + 1, cols] - a2_cols[rows, h:h + 1], -jnp.inf))
                return (scores[grp][rows, cols] * seg).astype(BF16)

            half = L // 2
            p = jnp.concatenate(
                [weighted(slice(0, half), slice(0, L)),
                 jnp.concatenate([jnp.zeros((half, half), BF16), weighted(slice(half, L), slice(half, L))], axis=1)],
                axis=0)
            y_diag.append(_dot(xdt[h * P:(h + 1) * P, :], p))
        rows = slice(grp * GW, (grp + 1) * GW)
        y = (jnp.concatenate(y_diag, axis=0) + y_off[grp] * per_head_rows(exp_a, grp, L)
             + dskip_ref[rows, :] * xt_ref[t, rows, :].astype(F32))
        zg = zt_ref[t, rows, :].astype(F32)
        y = y * (zg * _sigmoid(zg))
        ms = jnp.mean(y * y, axis=0, keepdims=True)
        yt_ref[t, rows, :] = (y * lax.rsqrt(ms + NORM_EPS) * nw_ref[rows, :]).astype(BF16)


def _ssd(sb, sct, sxt, sxdt, zt, sa, dskip_rows, norm_w_rows, batch, seq):
    ns = seq // (MIX_CHUNK * MIX_STEP_CHUNKS)
    n = batch * seq

    def t_spec(width):
        return pl.BlockSpec((MIX_STEP_CHUNKS, width, MIX_CHUNK), lambda b, c: (b * ns + c, 0, 0))

    return pl.pallas_call(
        _ssd_kernel,
        grid=(batch, ns),
        in_specs=[pl.BlockSpec((MIX_STEP_CHUNKS * MIX_CHUNK, SSM_GROUPS * SSM_STATE), lambda b, c: (b * ns + c, 0)),
                  t_spec(SSM_GROUPS * SSM_STATE), t_spec(SSM_WIDTH), t_spec(SSM_WIDTH), t_spec(SSM_WIDTH),
                  t_spec(SSM_HEADS), _const_spec((SSM_WIDTH, MIX_CHUNK)), _const_spec((SSM_WIDTH, MIX_CHUNK))],
        out_specs=t_spec(SSM_WIDTH),
        out_shape=jax.ShapeDtypeStruct((n // MIX_CHUNK, SSM_WIDTH, MIX_CHUNK), BF16),
        scratch_shapes=[pltpu.VMEM((SSM_GROUPS, SSM_WIDTH // SSM_GROUPS, SSM_STATE), F32)],
        compiler_params=_params(("arbitrary", "arbitrary")),
        name="ssd",
    )(sb, sct, sxt, sxdt, zt, sa, dskip_rows, norm_w_rows)


def _t5_bias_tile(rel, relb_ref, head):
    n = jnp.maximum(rel, 0)
    max_exact = REL_BUCKETS // 2
    nf = jnp.maximum(n, 1).astype(F32)
    large = max_exact + (jnp.log(nf / max_exact) / math.log(REL_MAX_DIST / max_exact)
                         * (REL_BUCKETS - max_exact)).astype(jnp.int32)
    large = jnp.minimum(large, REL_BUCKETS - 1)
    bucket = jnp.where(n < max_exact, n, large)
    far = relb_ref[REL_BUCKETS - 1, head]
    bias = jnp.zeros(rel.shape, F32)
    for bkt in range(REL_BUCKETS - 1):
        bias = jnp.where(bucket == bkt, (relb_ref[bkt, head] - far) * LOG2E, bias)
    return jnp.where(rel >= 0, bias, NEG_BIG)


def _attn_kernel(relb_ref, qt_ref, k_ref, vt_ref, lam_ref, subw_ref, g64_ref,
                 y_ref, bias_scr, qm_scr, m_scr, acc_scr, sa_scr, sb_scr, *, lam_init):
    T = ATT_TILE
    H = DIFF_HEADS
    DV = DIFF_V_DIM
    R = ATT_STEP_TILES
    step_idx = pl.program_id(1)

    @pl.when((pl.program_id(0) == 0) & (step_idx == 0))
    def _():
        rel = lax.broadcasted_iota(jnp.int32, (T, T), 1) - lax.broadcasted_iota(jnp.int32, (T, T), 0)
        for h in range(H):
            bias_scr[0, h] = _t5_bias_tile(rel, relb_ref, h)
            bias_scr[1, h] = _t5_bias_tile(rel + T, relb_ref, h)

    comp_of_row = lax.broadcasted_iota(jnp.int32, (DIFF_WIDTH, T), 0) // DIFF_QK_DIM
    for r in range(R):
        qt = qt_ref[r]
        for hc in range(2 * H):
            qm_scr[r, hc] = jnp.where(comp_of_row == hc, qt, jnp.zeros_like(qt))
    m_scr[...] = jnp.full(m_scr.shape, NEG_BIG, F32)
    acc_scr[...] = jnp.zeros_like(acc_scr)
    ones_rows = jnp.ones((ONES_ROWS, T), BF16)

    def key_tile(j):
        return k_ref[pl.ds(pl.multiple_of(j * T, T), T), :]

    def logits_one(s_ref, kt, r, hc):
        s_ref[hc] = _dot(kt, qm_scr[r, hc])

    def consume_one(s_ref, vt, r, hc, bias_idx):
        h = hc // 2
        m_prev = m_scr[r, hc]
        if bias_idx == DIAG:
            half = T // 2
            top = (bias_scr[DIAG, h, :half, :] + s_ref[hc, :half, :]).astype(BF16)
            low = (bias_scr[DIAG, h, half:, half:] + s_ref[hc, half:, half:]).astype(BF16)
            m_low = jnp.concatenate([jnp.full((1, half), NEG_BIG, BF16), jnp.max(low, axis=0, keepdims=True)], axis=1)
            m_cur = jnp.maximum(jnp.max(top, axis=0, keepdims=True), m_low)
            m_next = jnp.maximum(m_prev, m_cur.astype(F32))
            m_b = m_next.astype(BF16)
            p = jnp.concatenate(
                [jnp.exp2(top - m_b),
                 jnp.concatenate([jnp.zeros((half, half), BF16), jnp.exp2(low - m_b[:, half:])], axis=1)], axis=0)
        else:
            s = s_ref[hc]
            if bias_idx is not None:
                s = bias_scr[bias_idx, h] + s
            s = s.astype(BF16)
            m_next = jnp.maximum(m_prev, jnp.max(s, axis=0, keepdims=True).astype(F32))
            p = jnp.exp2(s - m_next.astype(BF16))
        alpha = jnp.exp2(m_prev - m_next)
        m_scr[r, hc] = m_next
        v_aug = jnp.concatenate([vt[h * DV:(h + 1) * DV, :], ones_rows], axis=0)
        acc_scr[r, hc] = alpha * acc_scr[r, hc] + _dot(v_aug, p)

    LEAD = 2

    def step(s_next, j_next, r_next, s_cur, j_cur, r_cur, bias_idx):
        kt = key_tile(j_next)
        vt = vt_ref[j_cur]
        for hc in range(LEAD):
            logits_one(s_next, kt, r_next, hc)
        for hc in range(2 * H):
            if hc + LEAD < 2 * H:
                logits_one(s_next, kt, r_next, hc + LEAD)
            consume_one(s_cur, vt, r_cur, hc, bias_idx)

    DIAG, NEAR = 0, 1

    def far_pairs(count, cur, oth, r):
        def pair(j):
            step(oth, j + 1, r, cur, j, r, None)
            step(cur, j + 2, r, oth, j + 1, r, None)

        def body(t, carry):
            pair(4 * t)
            pair(4 * t + 2)
            return carry

        lax.fori_loop(0, count // 2, body, 0)

        @pl.when(count % 2 == 1)
        def _():
            pair(2 * (count - 1))

    cur, oth = sa_scr, sb_scr
    kt0 = key_tile(0)
    for hc in range(2 * H):
        logits_one(cur, kt0, 0, hc)
    for r in range(R):
        d = R * step_idx + r
        if r % 2 == 1:
            far_pairs((d - 1) // 2, cur, oth, r)
            step(oth, d, r, cur, d - 1, r, NEAR)
            diag, free = oth, cur
        else:
            far_pairs(jnp.maximum(d // 2 - 1, 0), cur, oth, r)

            def near_tiles(d=d, cur=cur, oth=oth, r=r):
                step(oth, d - 1, r, cur, d - 2, r, None)
                step(cur, d, r, oth, d - 1, r, NEAR)

            if r == 0:
                pl.when(d >= 1)(near_tiles)
            else:
                near_tiles()
            diag, free = cur, oth
        if r + 1 < R:
            step(free, 0, r + 1, diag, d, r, DIAG)
            cur, oth = free, diag
        else:
            vt_last = vt_ref[d]
            for hc in range(2 * H):
                consume_one(diag, vt_last, r, hc, DIAG)

    lam_p = lam_ref[...]
    lam = (jnp.exp(jnp.sum(lam_p[0:1] * lam_p[1:2], axis=1, keepdims=True))
           - jnp.exp(jnp.sum(lam_p[2:3] * lam_p[3:4], axis=1, keepdims=True)) + lam_init)
    for r in range(R):
        def normalised(hc):
            return acc_scr[r, hc, :DV, :] / acc_scr[r, hc, DV:DV + 1, :]

        o_t = jnp.concatenate([normalised(2 * h) - lam * normalised(2 * h + 1) for h in range(H)], axis=0)
        o = o_t.T
        ms = _group_mean_sq(o, g64_ref[...])
        y_ref[r * T:(r + 1) * T, :] = (o * lax.rsqrt(ms + NORM_EPS) * subw_ref[...] * (1.0 - lam_init)).astype(BF16)


def _diff_attn(qt, kn, vt, rel_bias, lambdas, subw_row, g64, lam_init, batch, seq):
    nk = seq // ATT_TILE
    nsteps = nk // ATT_STEP_TILES
    n = batch * seq
    return pl.pallas_call(
        functools.partial(_attn_kernel, lam_init=lam_init),
        grid=(batch, nsteps),
        in_specs=[pl.BlockSpec(memory_space=pltpu.SMEM),
                  pl.BlockSpec((ATT_STEP_TILES, DIFF_WIDTH, ATT_TILE), lambda b, i: (b * nsteps + i, 0, 0)),
                  pl.BlockSpec((seq, DIFF_WIDTH), lambda b, i: (b, 0)),
                  pl.BlockSpec((nk, DIFF_WIDTH, ATT_TILE), lambda b, i: (b, 0, 0)),
                  _const_spec((4, DIFF_QK_DIM)), _const_spec((1, DIFF_WIDTH)),
                  _const_spec((DIFF_WIDTH, DIFF_WIDTH))],
        out_specs=pl.BlockSpec((ATT_STEP_TILES * ATT_TILE, DIFF_WIDTH), lambda b, i: (b * nsteps + i, 0)),
        out_shape=jax.ShapeDtypeStruct((n, DIFF_WIDTH), BF16),
        scratch_shapes=[pltpu.VMEM((2, DIFF_HEADS, ATT_TILE, ATT_TILE), F32),
                        pltpu.VMEM((ATT_STEP_TILES, 2 * DIFF_HEADS, DIFF_WIDTH, ATT_TILE), BF16),
                        pltpu.VMEM((ATT_STEP_TILES, 2 * DIFF_HEADS, 1, ATT_TILE), F32),
                        pltpu.VMEM((ATT_STEP_TILES, 2 * DIFF_HEADS, DIFF_V_DIM + ONES_ROWS, ATT_TILE), F32),
                        pltpu.VMEM((2 * DIFF_HEADS, ATT_TILE, ATT_TILE), F32),
                        pltpu.VMEM((2 * DIFF_HEADS, ATT_TILE, ATT_TILE), F32)],
        compiler_params=_params(("arbitrary", "arbitrary")),
        name="diff_attn",
    )(rel_bias, qt, kn, vt, lambdas, subw_row, g64)


def _untransposed(yt_ref):
    return jnp.concatenate([yt_ref[t].astype(F32).T for t in range(yt_ref.shape[0])], axis=0).astype(BF16)


def _outproj_ffn_kernel(x_ref, ymt_ref, yst_ref, yd_ref, w_hbm, fnw_ref, wg_hbm, wu_hbm, wd_hbm, o_ref,
                        w_ref, wg_ref, wu_ref, wd_ref, wide, wide_sem, tall, tall_sem, *, layer):
    @pl.when(pl.program_id(0) == 0)
    def _():
        def store(r0, chunk):
            w_ref[r0:r0 + chunk.shape[0], :] = chunk.astype(BF16)

        _stage_weight(w_hbm, layer, D_MODEL // 4, tall, tall_sem, store)
        _stage_ffn_weights(layer, wg_hbm, wu_hbm, wd_hbm, wg_ref, wu_ref, wd_ref, wide, wide_sem, tall, tall_sem)

    y = jnp.concatenate([_untransposed(ymt_ref), _untransposed(yst_ref), yd_ref[...]], axis=1)
    x = x_ref[...] + _dot(y, w_ref[...])
    o_ref[...] = _ffn_block(x, fnw_ref, wg_ref, wu_ref, wd_ref)


def _outproj_ffn(x, y_m, y_s, y_d, w_out, ffn_norm_w, wg, wu, wd, layer):
    n = x.shape[0]
    return pl.pallas_call(
        functools.partial(_outproj_ffn_kernel, layer=layer),
        grid=(n // ROW_TILE,),
        in_specs=[_row_spec(D_MODEL),
                  pl.BlockSpec((ROW_TILE // MIX_CHUNK, MLSTM_WIDTH, MIX_CHUNK), lambda i: (i, 0, 0)),
                  pl.BlockSpec((ROW_TILE // MIX_CHUNK, SSM_WIDTH, MIX_CHUNK), lambda i: (i, 0, 0)),
                  _row_spec(DIFF_WIDTH), _HBM, _const_spec((1, D_MODEL)), _HBM, _HBM, _HBM],
        out_specs=_row_spec(D_MODEL),
        out_shape=jax.ShapeDtypeStruct((n, D_MODEL), F32),
        scratch_shapes=[pltpu.VMEM((D_MODEL, D_MODEL), BF16)] + _ffn_weight_scratch(),
        compiler_params=_params(("arbitrary",)),
        name="outproj_ffn",
    )(x, y_m, y_s, y_d, w_out, ffn_norm_w.reshape(1, D_MODEL), wg, wu, wd)


def _small_row(values_by_lane):
    row = jnp.zeros((LANES,), F32)
    for lane, vals in values_by_lane:
        row = lax.dynamic_update_slice(row, vals.astype(F32), (lane,))
    return row.reshape(1, LANES)


def kernel(x, ffn1_norm_w, ffn1_w_gate, ffn1_w_up, ffn1_w_down, mix_norm_w, w_in, mlstm_gate_bias, mlstm_norm_w, ssm_conv_w, ssm_conv_b, ssm_dt_bias, ssm_A_log, ssm_D, ssm_norm_w, diff_q_norm_w, diff_k_norm_w, diff_lambda, diff_subln_w, rel_bias, w_out, ffn2_norm_w, ffn2_w_gate, ffn2_w_up, ffn2_w_down):
    batch, seq, d = x.shape
    assert d == D_MODEL and MIX_CHUNK == ATT_TILE and seq % ROW_TILE == 0 and ROW_TILE % MIX_CHUNK == 0
    assert seq % (MIX_CHUNK * MIX_STEP_CHUNKS) == 0 and seq % (ATT_STEP_TILES * ATT_TILE) == 0 and ATT_STEP_TILES % 2 == 0
    xf = x.reshape(batch * seq, D_MODEL)
    g32 = _block_diag(DIFF_WIDTH, DIFF_QK_DIM, 1.0 / DIFF_QK_DIM)
    g64 = _block_diag(DIFF_WIDTH, DIFF_V_DIM, 1.0 / DIFF_V_DIM)
    ffn1 = (ffn1_w_gate, ffn1_w_up, ffn1_w_down)
    ffn2 = (ffn2_w_gate, ffn2_w_up, ffn2_w_down)
    for l in range(DEPTH):
        qkw_row = jnp.concatenate([jnp.tile(diff_q_norm_w[l].reshape(-1), DIFF_HEADS) * (DIFF_QK_DIM ** -0.5 * LOG2E),
                                   jnp.tile(diff_k_norm_w[l].reshape(-1), DIFF_HEADS)]).reshape(1, 2 * DIFF_WIDTH)
        bias_row = _small_row([(_LANE_I, mlstm_gate_bias[l, 0]), (_LANE_F, mlstm_gate_bias[l, 1]),
                               (_LANE_DT, ssm_dt_bias[l])])
        (xf, mqt, mk, mvt, mot, gt, zt, sb, sct, sxt, sxdt, sa, qt, kn, vt) = _ffn_inproj(
            xf, ffn1_norm_w[l], *ffn1, mix_norm_w[l], w_in, qkw_row, g32,
            jnp.broadcast_to(bias_row.reshape(LANES, 1), (LANES, ROW_TILE)),
            jnp.broadcast_to(ssm_A_log[l][:, None], (SSM_HEADS, ROW_TILE)),
            ssm_conv_w[l], ssm_conv_b[l].reshape(1, SSM_CONV_DIM), l, seq)
        y_m = _mlstm(mqt, mk, mvt, mot, gt, jnp.broadcast_to(mlstm_norm_w[l][:, None], (MLSTM_WIDTH, MIX_CHUNK)),
                     batch, seq)
        y_s = _ssd(sb, sct, sxt, sxdt, zt, sa,
                   jnp.broadcast_to(jnp.repeat(ssm_D[l], SSM_HEAD_DIM)[:, None], (SSM_WIDTH, MIX_CHUNK)),
                   jnp.broadcast_to(ssm_norm_w[l][:, None], (SSM_WIDTH, MIX_CHUNK)), batch, seq)
        lam_init = 0.8 - 0.6 * math.exp(-0.3 * l)
        y_d = _diff_attn(qt, kn, vt, rel_bias, diff_lambda[l], jnp.tile(diff_subln_w[l], DIFF_HEADS).reshape(1, DIFF_WIDTH),
                         g64, lam_init, batch, seq)
        xf = _outproj_ffn(xf, y_m, y_s, y_d, w_out, ffn2_norm_w[l], *ffn2, l)
    return xf.reshape(batch, seq, D_MODEL)
```
